```python
import math
import jax, jax.numpy as jnp
from jax import lax
import numpy as np

D_MODEL = 1024
BATCH = 8
SEQ = 4096
DEPTH = 1

A_HEADS = 8
A_GROUPS = 2
A_HPG = A_HEADS // A_GROUPS
HEAD_DIM = 64
A_WIDTH = A_HEADS * HEAD_DIM
KV_WIDTH = A_GROUPS * HEAD_DIM
CMP_BLOCK = 32
CMP_STRIDE = 16
CMP_HIDDEN = 128
SLC_BLOCK = 64
SLC_TOP_N = 16
WINDOW = 512
Q_BLOCK = 128
ROPE_THETA = 10000.0
FORCE_SCORE = 1e30
B_HEADS = 4
B_KDIM = 128
B_VDIM = 128
B_FWIDTH = B_HEADS * B_KDIM
B_WIDTH = B_HEADS * B_VDIM
CHUNK = 64
NORM_EPS = 1e-5

IN_SPLITS = (A_WIDTH,
             6 * KV_WIDTH,
             A_HEADS * 3,
             A_WIDTH,
             B_FWIDTH,
             B_FWIDTH,
             B_WIDTH,
             B_WIDTH,
             D_MODEL,
             D_MODEL)
IN_WIDTH = sum(IN_SPLITS)
SPLIT_POINTS = tuple(int(v) for v in np.cumsum(IN_SPLITS)[:-1])

kernel_name = "nsa_hgrn2_gated_parallel_deepnorm"


def layer_norm(x, g, b):
    xf = x.astype(jnp.float32)
    mu = jnp.mean(xf, axis=-1, keepdims=True)
    var = jnp.mean(jnp.square(xf - mu), axis=-1, keepdims=True)
    y = (xf - mu) * lax.rsqrt(var + NORM_EPS) * g.astype(jnp.float32) + b.astype(jnp.float32)
    return y.astype(x.dtype)


def rope(x, pos):
    inv = ROPE_THETA ** (-jnp.arange(0, HEAD_DIM, 2, dtype=jnp.float32) / HEAD_DIM)
    ang = pos.astype(jnp.float32)[:, None] * inv[None, :]
    cos = jnp.concatenate([jnp.cos(ang), jnp.cos(ang)], axis=-1)
    sin = jnp.concatenate([jnp.sin(ang), jnp.sin(ang)], axis=-1)
    xf = x.astype(jnp.float32)
    x1, x2 = jnp.split(xf, 2, axis=-1)
    rot = jnp.concatenate([-x2, x1], axis=-1)
    return (xf * cos + rot * sin).astype(x.dtype)


def masked_softmax(s, mask):
    s = jnp.where(mask, s.astype(jnp.float32), -jnp.inf)
    m = jnp.max(s, axis=-1, keepdims=True)
    m = jnp.where(jnp.isfinite(m), m, 0.0)
    p = jnp.exp(s - m)
    d = jnp.sum(p, axis=-1, keepdims=True)
    return p / jnp.where(d > 0, d, 1.0)


def compress(kv, pe, w1, w2):
    B, G, S, D = kv.shape
    r = CMP_BLOCK // CMP_STRIDE
    n = S // CMP_STRIDE - r + 1
    chunks = kv.reshape(B, G, S // CMP_STRIDE, CMP_STRIDE, D)
    blocks = jnp.concatenate([chunks[:, :, j:j + n] for j in range(r)], axis=3)
    h = (blocks + pe).reshape(B, G, n, CMP_BLOCK * D)
    return jax.nn.silu(h @ w1) @ w2


def nsa_attention(q, k_cmp, v_cmp, k_slc, v_slc, k_win, v_win, gates,
                  pe_k, w_k1, w_k2, pe_v, w_v1, w_v2):
    B, S, _ = q.shape
    G, HPG, D = A_GROUPS, A_HPG, HEAD_DIM
    pos = jnp.arange(S)
    q = q.reshape(B, S, G, HPG, D).transpose(0, 2, 3, 1, 4)
    kv_t = lambda a: a.reshape(B, S, G, D).transpose(0, 2, 1, 3)
    k_cmp, v_cmp, k_slc, v_slc, k_win, v_win = (kv_t(a) for a in (k_cmp, v_cmp, k_slc, v_slc, k_win, v_win))
    q_rope = rope(q, pos)
    k_slc = rope(k_slc, pos)
    k_win = rope(k_win, pos)
    kc = compress(k_cmp, pe_k, w_k1, w_k2)
    vc = compress(v_cmp, pe_v, w_v1, w_v2)
    n_cmp = kc.shape[2]
    n_slc = S // SLC_BLOCK
    n_sel = min(SLC_TOP_N, n_slc)
    cmp_start = jnp.arange(n_cmp) * CMP_STRIDE
    cmp_end = cmp_start + CMP_BLOCK - 1
    slc_start = jnp.arange(n_slc) * SLC_BLOCK
    overlap = ((cmp_start[:, None] < slc_start[None, :] + SLC_BLOCK)
               & (cmp_start[:, None] + CMP_BLOCK > slc_start[None, :])).astype(jnp.float32)
    ks_blocks = k_slc.reshape(B, G, n_slc, SLC_BLOCK, D)
    vs_blocks = v_slc.reshape(B, G, n_slc, SLC_BLOCK, D)
    kw_pad = jnp.pad(k_win, ((0, 0), (0, 0), (WINDOW, 0), (0, 0)))
    vw_pad = jnp.pad(v_win, ((0, 0), (0, 0), (WINDOW, 0), (0, 0)))
    gates = jax.nn.sigmoid(gates).reshape(B, S, G, HPG, 3).transpose(0, 2, 3, 1, 4)
    nq = S // Q_BLOCK

    def to_blocks(a):
        a = a.reshape(B, G, HPG, nq, Q_BLOCK, *a.shape[4:])
        return jnp.moveaxis(a, 3, 0)

    scale = HEAD_DIM ** -0.5
    bi = jnp.arange(B)[:, None, None, None]
    gi = jnp.arange(G)[None, :, None, None]
    j_idx = jnp.arange(n_slc)

    def block_fn(args):
        qr, qn, gt, blk = args
        s0 = blk * Q_BLOCK
        t = s0 + jnp.arange(Q_BLOCK)
        sc = jnp.einsum('bghqd,bgnd->bghqn', qn, kc) * scale
        pc = masked_softmax(sc, cmp_end[None, :] <= t[:, None])
        o_cmp = jnp.einsum('bghqn,bgnd->bghqd', pc.astype(vc.dtype), vc)
        imp = jnp.einsum('bghqn,nj->bgqj', pc, overlap)
        cur = (t // SLC_BLOCK)[:, None]
        forced = (j_idx[None, :] == 0) | (j_idx[None, :] == cur) | (j_idx[None, :] == cur - 1)
        imp = jnp.where(j_idx[None, :] > cur, -jnp.inf, jnp.where(forced, FORCE_SCORE, imp))
        top_s, top_i = lax.top_k(imp, n_sel)
        ksel = ks_blocks[bi, gi, top_i]
        vsel = vs_blocks[bi, gi, top_i]
        kpos = top_i[..., None] * SLC_BLOCK + jnp.arange(SLC_BLOCK)
        msel = jnp.isfinite(top_s)[..., None] & (kpos <= t[:, None, None])
        L = n_sel * SLC_BLOCK
        ksel = ksel.reshape(B, G, Q_BLOCK, L, D)
        vsel = vsel.reshape(B, G, Q_BLOCK, L, D)
        msel = msel.reshape(B, G, Q_BLOCK, L)
        ss = jnp.einsum('bghqd,bgqld->bghql', qr, ksel) * scale
        ps = masked_softmax(ss, msel[:, :, None])
        o_slc = jnp.einsum('bghql,bgqld->bghqd', ps.astype(vsel.dtype), vsel)
        kw = lax.dynamic_slice_in_dim(kw_pad, s0, WINDOW + Q_BLOCK, axis=2)
        vw = lax.dynamic_slice_in_dim(vw_pad, s0, WINDOW + Q_BLOCK, axis=2)
        kp = s0 - WINDOW + jnp.arange(WINDOW + Q_BLOCK)
        mw = (kp[None, :] <= t[:, None]) & (kp[None, :] > t[:, None] - WINDOW) & (kp[None, :] >= 0)
        sw = jnp.einsum('bghqd,bgkd->bghqk', qr, kw) * scale
        pw = masked_softmax(sw, mw)
        o_win = jnp.einsum('bghqk,bgkd->bghqd', pw.astype(vw.dtype), vw)
        return gt[..., 0:1] * o_cmp + gt[..., 1:2] * o_slc + gt[..., 2:3] * o_win

    out = lax.map(block_fn, (to_blocks(q_rope), to_blocks(q), to_blocks(gates), jnp.arange(nq)))
    return out.transpose(1, 0, 4, 2, 3, 5).reshape(B, S, A_WIDTH)


def hgrn2(q, f, i, lb):
    B, S, _ = q.shape
    H, DK, DV, C = B_HEADS, B_KDIM, B_VDIM, CHUNK
    n = S // C
    qf = jax.nn.silu(q.astype(jnp.float32))
    fg = lb + (1.0 - lb) * jax.nn.sigmoid(f.astype(jnp.float32))
    logf = jnp.log(fg)
    k = 1.0 - fg
    to_c = lambda a, d: a.reshape(B, n, C, H, d).transpose(0, 3, 1, 2, 4)
    qf, logf, k = to_c(qf, DK), to_c(logf, DK), to_c(k, DK)
    v = to_c(i.astype(jnp.float32), DV)
    b = jnp.cumsum(logf, axis=3)
    b_last = b[..., -1:, :]
    qe = qf * jnp.exp(b)
    ke = k * jnp.exp(-b)
    kd = k * jnp.exp(b_last - b)
    dl = jnp.exp(b_last[..., 0, :])
    causal = jnp.tril(jnp.ones((C, C), dtype=bool))
    attn = jnp.where(causal, jnp.einsum('bhncd,bhnsd->bhncs', qe, ke), 0.0)
    o_intra = jnp.einsum('bhncs,bhnse->bhnce', attn, v)

    def step(state, inp):
        qe_n, kd_n, v_n, dl_n = inp
        o = jnp.einsum('bhcd,bhde->bhce', qe_n, state)
        state = dl_n[..., None] * state + jnp.einsum('bhcd,bhce->bhde', kd_n, v_n)
        return state, o

    xs = (jnp.moveaxis(qe, 2, 0), jnp.moveaxis(kd, 2, 0), jnp.moveaxis(v, 2, 0), jnp.moveaxis(dl, 2, 0))
    s0 = jnp.zeros((B, H, DK, DV), jnp.float32)
    _, o_inter = lax.scan(step, s0, xs)
    o = o_intra + jnp.moveaxis(o_inter, 0, 2)
    return o.transpose(0, 2, 3, 1, 4).reshape(B, S, H, DV)


def setup_inputs(seed: int = 0) -> dict:
    key = jax.random.key(seed)
    ks = jax.random.split(key, 20)
    beta = (8 * DEPTH) ** -0.25
    nrm = lambda k, shape, s: jax.random.normal(k, shape, jnp.float32) * s
    return {
        "x": nrm(ks[0], (BATCH, SEQ, D_MODEL), 1.0),
        "w_in": nrm(ks[1], (DEPTH, D_MODEL, IN_WIDTH), D_MODEL ** -0.5),
        "b_in": nrm(ks[2], (DEPTH, IN_WIDTH), 0.01),
        "pe_cmp_k": nrm(ks[3], (DEPTH, CMP_BLOCK, HEAD_DIM), 0.1),
        "w_cmp_k1": nrm(ks[4], (DEPTH, CMP_BLOCK * HEAD_DIM, CMP_HIDDEN), (CMP_BLOCK * HEAD_DIM) ** -0.5),
        "w_cmp_k2": nrm(ks[5], (DEPTH, CMP_HIDDEN, HEAD_DIM), CMP_HIDDEN ** -0.5),
        "pe_cmp_v": nrm(ks[6], (DEPTH, CMP_BLOCK, HEAD_DIM), 0.1),
        "w_cmp_v1": nrm(ks[7], (DEPTH, CMP_BLOCK * HEAD_DIM, CMP_HIDDEN), (CMP_BLOCK * HEAD_DIM) ** -0.5),
        "w_cmp_v2": nrm(ks[8], (DEPTH, CMP_HIDDEN, HEAD_DIM), CMP_HIDDEN ** -0.5),
        "hgrn_lb_logits": nrm(ks[9], (DEPTH + 1, B_FWIDTH), 0.1),
        "hgrn_norm_g": 1.0 + nrm(ks[10], (DEPTH, B_WIDTH), 0.01),
        "w_branch_a": nrm(ks[11], (DEPTH, A_WIDTH, D_MODEL), beta * A_WIDTH ** -0.5),
        "w_branch_b": nrm(ks[12], (DEPTH, B_WIDTH, D_MODEL), beta * B_WIDTH ** -0.5),
        "w_out": nrm(ks[13], (DEPTH, D_MODEL, D_MODEL), beta * D_MODEL ** -0.5),
        "ln_g": 1.0 + nrm(ks[14], (DEPTH, D_MODEL), 0.01),
        "ln_b": nrm(ks[15], (DEPTH, D_MODEL), 0.01),
    }


def reference(x, w_in, b_in, pe_cmp_k, w_cmp_k1, w_cmp_k2, pe_cmp_v, w_cmp_v1, w_cmp_v2,
              hgrn_lb_logits, hgrn_norm_g, w_branch_a, w_branch_b, w_out, ln_g, ln_b):
    alpha = (2 * DEPTH) ** 0.25
    lb_all = jnp.cumsum(jax.nn.softmax(hgrn_lb_logits.astype(jnp.float32), axis=0), axis=0)
    B, S, _ = x.shape
    for l in range(DEPTH):
        h = x @ w_in[l] + b_in[l]
        (q_a, kv_a, g_nsa, z_a, q_b, f_b, i_b, z_b, gm_a, gm_b) = jnp.split(h, SPLIT_POINTS, axis=-1)
        k_cmp, v_cmp, k_slc, v_slc, k_win, v_win = jnp.split(kv_a, 6, axis=-1)
        o_a = nsa_attention(q_a, k_cmp, v_cmp, k_slc, v_slc, k_win, v_win, g_nsa,
                            pe_cmp_k[l], w_cmp_k1[l], w_cmp_k2[l], pe_cmp_v[l], w_cmp_v1[l], w_cmp_v2[l])
        o_a = o_a * jax.nn.silu(z_a)
        o_b = hgrn2(q_b, f_b, i_b, lb_all[l])
        o_b = o_b * lax.rsqrt(jnp.mean(jnp.square(o_b), axis=-1, keepdims=True) + NORM_EPS)
        o_b = (o_b.reshape(B, S, B_WIDTH) * hgrn_norm_g[l].astype(jnp.float32)).astype(x.dtype)
        o_b = o_b * jax.nn.silu(z_b)
        y = jax.nn.sigmoid(gm_a) * (o_a @ w_branch_a[l]) + jax.nn.sigmoid(gm_b) * (o_b @ w_branch_b[l])
        x = layer_norm(alpha * x + y @ w_out[l], ln_g[l], ln_b[l])
    return x
```

```python
import functools
import math

import numpy as np
import jax
import jax.numpy as jnp
from jax import lax
from jax.experimental import pallas as pl
from jax.experimental.pallas import tpu as pltpu

D_MODEL = 1024
DEPTH = 1
A_HEADS = 8
A_GROUPS = 2
A_HPG = A_HEADS // A_GROUPS
HEAD_DIM = 64
A_WIDTH = A_HEADS * HEAD_DIM
KV_WIDTH = A_GROUPS * HEAD_DIM
CMP_BLOCK = 32
CMP_STRIDE = 16
CMP_HIDDEN = 128
SLC_BLOCK = 64
SLC_TOP_N = 16
WINDOW = 512
ROPE_THETA = 10000.0
FORCE_SCORE = 1e30
B_HEADS = 4
B_KDIM = 128
B_VDIM = 128
B_FWIDTH = B_HEADS * B_KDIM
B_WIDTH = B_HEADS * B_VDIM
CHUNK = 64
NORM_EPS = 1e-5

IN_SPLITS = (A_WIDTH, 6 * KV_WIDTH, A_HEADS * 3, A_WIDTH, B_FWIDTH, B_FWIDTH,
             B_WIDTH, B_WIDTH, D_MODEL, D_MODEL)
_OFF = tuple(int(v) for v in np.cumsum((0,) + IN_SPLITS))

V7X_VMEM_LIMIT_BYTES = 56 * 1024 * 1024
PROJ_ROWS = 512
Q_TILE = 128
KV_CHUNK = 512
WIN_KEYS = WINDOW + Q_TILE
HG_ROWS = 256
OUT_ROWS = 512
MASKED = -1e30
GATE_ROWS = 16

F32 = jnp.float32
BF16 = jnp.bfloat16


def _dot(a, b):
    return jnp.dot(a, b, preferred_element_type=F32)


def _dot_nt(a, b):
    return lax.dot_general(a, b, (((1,), (1,)), ((), ())), preferred_element_type=F32)


def _sigmoid(x):
    return 1.0 / (1.0 + jnp.exp(-x))


def _silu(x):
    return x * _sigmoid(x)


def _proj_kernel(x_ref, wn_ref, bn_ref, wt_ref, bt_ref, cost_ref, sint_ref,
                 cos_ref, sina_ref, sinb_ref,
                 qr_ref, qn_ref, vts_ref, vtw_ref, gt_ref, kc_ref, vc_ref,
                 ks_ref, kw_ref, sza_ref, qb_ref, fb_ref, ib_ref, szb_ref):
    xb = x_ref[0].astype(BF16)
    scale = HEAD_DIM ** -0.5

    ht = _dot_nt(wt_ref[...], xb) + bt_ref[...]
    cost = cost_ref[...]
    sint = sint_ref[...]
    half = HEAD_DIM // 2
    for h in range(A_HEADS):
        blk = ht[h * HEAD_DIM:(h + 1) * HEAD_DIM]
        rot = jnp.concatenate([-blk[half:], blk[:half]], axis=0)
        qr_ref[0, h * HEAD_DIM:(h + 1) * HEAD_DIM, :] = ((blk * cost + rot * sint) * scale).astype(BF16)
        qn_ref[0, h * HEAD_DIM:(h + 1) * HEAD_DIM, :] = (blk * scale).astype(BF16)
    o = A_WIDTH
    vts_ref[0] = ht[o:o + KV_WIDTH].astype(BF16)
    vtw_ref[0] = ht[o + KV_WIDTH:o + 2 * KV_WIDTH].astype(BF16)
    gt_ref[0] = _sigmoid(ht[o + 2 * KV_WIDTH:o + 2 * KV_WIDTH + 2 * GATE_ROWS])

    def cols(lo, hi):
        return _dot(xb, wn_ref[:, lo:hi]) + bn_ref[:, lo:hi]

    kv = cols(0, 4 * KV_WIDTH)
    kc_ref[0] = kv[:, 0:KV_WIDTH].astype(BF16)
    vc_ref[0] = kv[:, KV_WIDTH:2 * KV_WIDTH].astype(BF16)
    cos = cos_ref[...]
    sina = sina_ref[...]
    sinb = sinb_ref[...]

    def rope_rows(k):
        return (k * cos + pltpu.roll(k, 128 - half, axis=1) * sina + pltpu.roll(k, half, axis=1) * sinb)

    ks_ref[0] = rope_rows(kv[:, 2 * KV_WIDTH:3 * KV_WIDTH]).astype(BF16)
    kw_ref[0] = rope_rows(kv[:, 3 * KV_WIDTH:4 * KV_WIDTH]).astype(BF16)
    o = 4 * KV_WIDTH
    sza_ref[0] = _silu(cols(o, o + A_WIDTH)).astype(BF16)
    o += A_WIDTH
    qb_ref[0] = cols(o, o + B_FWIDTH).astype(BF16)
    o += B_FWIDTH
    fb_ref[0] = cols(o, o + B_FWIDTH)
    o += B_FWIDTH
    ib_ref[0] = cols(o, o + B_WIDTH).astype(BF16)
    o += B_WIDTH
    szb_ref[0] = _silu(cols(o, o + B_WIDTH)).astype(BF16)


def _project(x, wn, bn, wt, bt, cost, sint, cos, sina, sinb):
    B, S, D = x.shape
    T = PROJ_ROWS
    n_t = wt.shape[0]
    n_n = wn.shape[1]
    full = lambda shape: pl.BlockSpec(shape, lambda b, s: (0,) * len(shape))
    row_out = lambda w: pl.BlockSpec((1, T, w), lambda b, s: (b, s, 0))
    col_out = lambda r: pl.BlockSpec((1, r, T), lambda b, s: (b, 0, s))
    sds = jax.ShapeDtypeStruct
    out_shape = (
        sds((B, A_WIDTH, S), BF16), sds((B, A_WIDTH, S), BF16),
        sds((B, KV_WIDTH, S), BF16), sds((B, KV_WIDTH, S), BF16),
        sds((B, 2 * GATE_ROWS, S), F32),
        sds((B, S, KV_WIDTH), BF16), sds((B, S, KV_WIDTH), BF16),
        sds((B, S, KV_WIDTH), BF16), sds((B, S, KV_WIDTH), BF16),
        sds((B, S, A_WIDTH), BF16),
        sds((B, S, B_FWIDTH), BF16), sds((B, S, B_FWIDTH), F32),
        sds((B, S, B_WIDTH), BF16), sds((B, S, B_WIDTH), BF16),
    )
    out_specs = (
        col_out(A_WIDTH), col_out(A_WIDTH), col_out(KV_WIDTH), col_out(KV_WIDTH),
        col_out(2 * GATE_ROWS),
        row_out(KV_WIDTH), row_out(KV_WIDTH), row_out(KV_WIDTH), row_out(KV_WIDTH),
        row_out(A_WIDTH), row_out(B_FWIDTH), row_out(B_FWIDTH), row_out(B_WIDTH), row_out(B_WIDTH),
    )
    in_specs = [
        pl.BlockSpec((1, T, D), lambda b, s: (b, s, 0)),
        full((D, n_n)), full((1, n_n)), full((n_t, D)), full((n_t, 1)),
        pl.BlockSpec((HEAD_DIM, T), lambda b, s: (0, s)),
        pl.BlockSpec((HEAD_DIM, T), lambda b, s: (0, s)),
        pl.BlockSpec((T, KV_WIDTH), lambda b, s: (s, 0)),
        pl.BlockSpec((T, KV_WIDTH), lambda b, s: (s, 0)),
        pl.BlockSpec((T, KV_WIDTH), lambda b, s: (s, 0)),
    ]
    return pl.pallas_call(
        _proj_kernel, out_shape=out_shape, grid=(B, S // T),
        in_specs=in_specs, out_specs=out_specs,
        compiler_params=pltpu.CompilerParams(
            dimension_semantics=("parallel", "parallel"),
            vmem_limit_bytes=V7X_VMEM_LIMIT_BYTES),
        name="in_proj",
    )(x, wn, bn, wt, bt, cost, sint, cos, sina, sinb)


def _compress_kernel(ck_ref, cv_ref, w1k_ref, w1kf_ref, pek_ref, w2k_ref,
                     w1v_ref, w1vf_ref, pev_ref, w2vt_ref, kc_ref, vct_ref):
    nc = ck_ref.shape[2]

    def hidden(c_ref, w1_ref, w1f_ref, pe_ref):
        a = _dot(c_ref[0, 0], w1_ref[...])
        lo = a[:, :CMP_HIDDEN]
        hi = pltpu.roll(a[:, CMP_HIDDEN:], nc - 1, axis=0)
        pe_term = jnp.sum(w1f_ref[...] * pe_ref[...], axis=0, keepdims=True)
        return _silu(lo + hi + pe_term).astype(BF16)

    hk = hidden(ck_ref, w1k_ref, w1kf_ref, pek_ref)
    kc_ref[0, 0] = _dot(hk, w2k_ref[...]).astype(BF16)
    hv = hidden(cv_ref, w1v_ref, w1vf_ref, pev_ref)
    vct_ref[0, 0] = _dot_nt(w2vt_ref[...], hv).astype(BF16)


def _compress(ck, cv, w1k, w1kf, pek, w2k, w1v, w1vf, pev, w2vt):
    B, G, NC, W = ck.shape
    full = lambda a: pl.BlockSpec(a.shape, lambda b, g: (0,) * a.ndim)
    blk = pl.BlockSpec((1, 1, NC, W), lambda b, g: (b, g, 0, 0))
    return pl.pallas_call(
        _compress_kernel,
        out_shape=(jax.ShapeDtypeStruct((B, G, NC, HEAD_DIM), BF16),
                   jax.ShapeDtypeStruct((B, G, HEAD_DIM, NC), BF16)),
        grid=(B, G),
        in_specs=[blk, blk, full(w1k), full(w1kf), full(pek), full(w2k),
                  full(w1v), full(w1vf), full(pev), full(w2vt)],
        out_specs=(pl.BlockSpec((1, 1, NC, HEAD_DIM), lambda b, g: (b, g, 0, 0)),
                   pl.BlockSpec((1, 1, HEAD_DIM, NC), lambda b, g: (b, g, 0, 0))),
        compiler_params=pltpu.CompilerParams(
            dimension_semantics=("parallel", "parallel"),
            vmem_limit_bytes=V7X_VMEM_LIMIT_BYTES),
        name="kv_compress",
    )(ck, cv, w1k, w1kf, pek, w2k, w1v, w1vf, pev, w2vt)


def _nsa_kernel(qr_ref, qn_ref, gt_ref, kc_ref, vct_ref, ks_ref, kw_ref,
                vts_ref, vtw_ref, ovl_ref, o_ref, sel_ref, *, seq, n_sel):
    g = pl.program_id(1)
    qi = pl.program_id(2)
    s0 = qi * Q_TILE
    nc = kc_ref.shape[2]
    n_slc = seq // SLC_BLOCK
    lanes = A_HPG * Q_TILE

    def stack_heads(ref):
        return jnp.concatenate(
            [ref[0, h * HEAD_DIM:(h + 1) * HEAD_DIM, :] for h in range(A_HPG)], axis=1)

    def per_head(a):
        return jnp.concatenate([a] * A_HPG, axis=1)

    qn = stack_heads(qn_ref)
    qr = stack_heads(qr_ref).astype(F32)
    own = lax.broadcasted_iota(jnp.int32, (KV_WIDTH, lanes), 0) // HEAD_DIM == g
    qrz = jnp.where(own, jnp.concatenate([qr, qr], axis=0), 0.0).astype(BF16)
    t_row = s0 + lax.broadcasted_iota(jnp.int32, (1, Q_TILE), 1)
    t_all = s0 + (lax.broadcasted_iota(jnp.int32, (1, lanes), 1) & (Q_TILE - 1))

    sc = _dot(kc_ref[0, 0], qn)
    n_idx = lax.broadcasted_iota(jnp.int32, (nc, lanes), 0)
    cvalid = n_idx * CMP_STRIDE + (CMP_BLOCK - 1) <= t_all
    sc = jnp.where(cvalid, sc, MASKED)
    mc = jnp.max(sc, axis=0, keepdims=True)
    pc = jnp.where(cvalid, jnp.exp(sc - mc), 0.0)
    lc = jnp.sum(pc, axis=0, keepdims=True)
    pc = pc * (1.0 / jnp.where(lc > 0, lc, 1.0))
    o_cmp = _dot(vct_ref[0, 0], pc.astype(BF16))

    ps = pc[:, 0:Q_TILE]
    for h in range(1, A_HPG):
        ps = ps + pc[:, h * Q_TILE:(h + 1) * Q_TILE]
    p_hi = ps.astype(BF16)
    r1 = ps - p_hi.astype(F32)
    p_mid = r1.astype(BF16)
    p_lo = (r1 - p_mid.astype(F32)).astype(BF16)
    ovl = ovl_ref[...]
    imp = _dot(ovl, p_hi) + _dot(ovl, p_mid) + _dot(ovl, p_lo)

    j_idx = lax.broadcasted_iota(jnp.int32, (n_slc, Q_TILE), 0)
    cur = t_row // SLC_BLOCK
    forced = (j_idx == 0) | (j_idx == cur) | (j_idx == cur - 1)
    val = jnp.where(j_idx > cur, -jnp.inf, jnp.where(forced, FORCE_SCORE, imp))
    rank = jnp.zeros((n_slc, Q_TILE), F32)
    for k in range(n_slc):
        row = val[k:k + 1, :]
        beats = (row > val) | ((row == val) & (j_idx > k))
        rank = rank + jnp.where(beats, 1.0, 0.0)
    sel_ref[...] = jnp.where((rank < n_sel) & (j_idx <= cur), 1.0, 0.0)

    blocks_per_chunk = KV_CHUNK // SLC_BLOCK
    r_idx = lax.broadcasted_iota(jnp.int32, (KV_CHUNK, lanes), 0)

    def slc_step(c, carry):
        m, l, acc = carry
        k0 = pl.multiple_of(c * KV_CHUNK, KV_CHUNK)
        s = _dot(ks_ref[0, pl.ds(k0, KV_CHUNK), :], qrz)
        selc = sel_ref[pl.ds(pl.multiple_of(c * blocks_per_chunk, blocks_per_chunk), blocks_per_chunk), :]
        selc = jnp.concatenate([jnp.broadcast_to(selc[j:j + 1, :], (SLC_BLOCK, Q_TILE))
                                for j in range(blocks_per_chunk)], axis=0)
        valid = (per_head(selc) > 0.5) & (r_idx + k0 <= t_all)
        s = jnp.where(valid, s, MASKED)
        m_new = jnp.maximum(m, jnp.max(s, axis=0, keepdims=True))
        alpha = jnp.exp(m - m_new)
        p = jnp.exp(s - m_new)
        l = alpha * l + jnp.sum(p, axis=0, keepdims=True)
        acc = alpha * acc + _dot(vts_ref[0, :, pl.ds(k0, KV_CHUNK)], p.astype(BF16))
        return m_new, l, acc

    n_chunks = (s0 + Q_TILE + KV_CHUNK - 1) // KV_CHUNK
    m0 = jnp.full((1, lanes), MASKED, F32)
    l0 = jnp.zeros((1, lanes), F32)
    a0 = jnp.zeros((HEAD_DIM, lanes), F32)
    _, ls, acc_s = lax.fori_loop(0, n_chunks, slc_step, (m0, l0, a0))
    o_slc = acc_s * (1.0 / ls)

    w0 = pl.multiple_of(jnp.clip(s0 - WINDOW, 0, seq - WIN_KEYS), Q_TILE)
    sw = _dot(kw_ref[0, pl.ds(w0, WIN_KEYS), :], qrz)
    kpos = w0 + lax.broadcasted_iota(jnp.int32, (WIN_KEYS, lanes), 0)
    wvalid = (kpos <= t_all) & (kpos > t_all - WINDOW)
    sw = jnp.where(wvalid, sw, MASKED)
    mw = jnp.max(sw, axis=0, keepdims=True)
    pw = jnp.where(wvalid, jnp.exp(sw - mw), 0.0)
    lw = jnp.sum(pw, axis=0, keepdims=True)
    o_win = _dot(vtw_ref[0, :, pl.ds(w0, WIN_KEYS)], pw.astype(BF16)) * (1.0 / lw)

    gt = gt_ref[0]

    def gate_row(branch):
        return jnp.concatenate(
            [gt[branch * A_HPG + h:branch * A_HPG + h + 1, :] for h in range(A_HPG)], axis=1)

    ot = gate_row(0) * o_cmp + gate_row(1) * o_slc + gate_row(2) * o_win
    halves = []
    for pair in range(A_HPG // 2):
        two = jnp.concatenate([ot[:, (2 * pair) * Q_TILE:(2 * pair + 1) * Q_TILE],
                               ot[:, (2 * pair + 1) * Q_TILE:(2 * pair + 2) * Q_TILE]], axis=0)
        halves.append(two.T)
    o_ref[0] = jnp.concatenate(halves, axis=1).astype(o_ref.dtype)


def _nsa(qr, qn, gt, kc, vct, ks, kw, vts, vtw, ovl):
    B, _, S = qr.shape
    NC = kc.shape[2]
    n_slc = S // SLC_BLOCK
    n_sel = min(SLC_TOP_N, n_slc)
    gw = A_HPG * HEAD_DIM
    kernel = functools.partial(_nsa_kernel, seq=S, n_sel=n_sel)
    in_specs = [
        pl.BlockSpec((1, gw, Q_TILE), lambda b, g, q: (b, g, q)),
        pl.BlockSpec((1, gw, Q_TILE), lambda b, g, q: (b, g, q)),
        pl.BlockSpec((1, GATE_ROWS, Q_TILE), lambda b, g, q: (b, g, q)),
        pl.BlockSpec((1, 1, NC, HEAD_DIM), lambda b, g, q: (b, g, 0, 0)),
        pl.BlockSpec((1, 1, HEAD_DIM, NC), lambda b, g, q: (b, g, 0, 0)),
        pl.BlockSpec((1, S, KV_WIDTH), lambda b, g, q: (b, 0, 0)),
        pl.BlockSpec((1, S, KV_WIDTH), lambda b, g, q: (b, 0, 0)),
        pl.BlockSpec((1, HEAD_DIM, S), lambda b, g, q: (b, g, 0)),
        pl.BlockSpec((1, HEAD_DIM, S), lambda b, g, q: (b, g, 0)),
        pl.BlockSpec((n_slc, NC), lambda b, g, q: (0, 0)),
    ]
    return pl.pallas_call(
        kernel,
        out_shape=jax.ShapeDtypeStruct((B, S, A_WIDTH), BF16),
        grid=(B, A_GROUPS, S // Q_TILE),
        in_specs=in_specs,
        out_specs=pl.BlockSpec((1, Q_TILE, gw), lambda b, g, q: (b, q, g)),
        scratch_shapes=[pltpu.VMEM((n_slc, Q_TILE), F32)],
        compiler_params=pltpu.CompilerParams(
            dimension_semantics=("parallel", "parallel", "arbitrary"),
            vmem_limit_bytes=V7X_VMEM_LIMIT_BYTES),
        name="nsa_attention",
    )(qr, qn, gt, kc, vct, ks, kw, vts, vtw, ovl)


def _hgrn_kernel(q_ref, f_ref, i_ref, sz_ref, lbl_ref, g_ref, tri_ref, o_ref,
                 qe_s, kv_s, dl_s, oi_s, *, layer):
    S = q_ref.shape[1]
    n_grp = S // HG_ROWS
    cpg = HG_ROWS // CHUNK

    lg = lbl_ref[...]
    e = jnp.exp(lg - jnp.max(lg, axis=0, keepdims=True))
    lb = jnp.sum(e[0:layer + 1], axis=0, keepdims=True) / jnp.sum(e, axis=0, keepdims=True)

    tri = tri_ref[...]
    ri = lax.broadcasted_iota(jnp.int32, (HG_ROWS, HG_ROWS), 0)
    ci = lax.broadcasted_iota(jnp.int32, (HG_ROWS, HG_ROWS), 1)
    causal = (ri // CHUNK == ci // CHUNK) & (ci <= ri)

    def intra(gi, _):
        r0 = pl.multiple_of(gi * HG_ROWS, HG_ROWS)
        rows = pl.ds(r0, HG_ROWS)
        fg = lb + (1.0 - lb) * _sigmoid(f_ref[0, rows, :])
        logf = jnp.log(fg)
        kk = 1.0 - fg
        l_hi = logf.astype(BF16)
        r1 = logf - l_hi.astype(F32)
        l_mid = r1.astype(BF16)
        l_lo = (r1 - l_mid.astype(F32)).astype(BF16)
        b = _dot(tri, l_hi) + _dot(tri, l_mid) + _dot(tri, l_lo)
        b_last = jnp.concatenate(
            [jnp.broadcast_to(b[(c + 1) * CHUNK - 1:(c + 1) * CHUNK, :], (CHUNK, B_KDIM))
             for c in range(cpg)], axis=0)
        qe = (_silu(q_ref[0, rows, :].astype(F32)) * jnp.exp(b)).astype(BF16)
        ke = (kk * jnp.exp(-b)).astype(BF16)
        kd = (kk * jnp.exp(b_last - b)).astype(BF16)
        v = i_ref[0, rows, :]
        attn = jnp.where(causal, _dot_nt(qe, ke), 0.0)
        oi_s[rows, :] = _dot(attn.astype(BF16), v)
        qe_s[rows, :] = qe
        for c in range(cpg):
            sl = slice(c * CHUNK, (c + 1) * CHUNK)
            n = gi * cpg + c
            kv_s[n] = lax.dot_general(v[sl], kd[sl], (((0,), (0,)), ((), ())),
                                      preferred_element_type=F32)
            dl_s[n] = jnp.exp(b_last[c * CHUNK:c * CHUNK + 8, :])
        return 0

    lax.fori_loop(0, n_grp, intra, 0)

    def scan(n, state):
        upd = kv_s[n]
        kv_s[n] = state
        return dl_s[n][0:1, :] * state + upd

    lax.fori_loop(0, S // CHUNK, scan, jnp.zeros((B_VDIM, B_KDIM), F32))

    gain = g_ref[...]

    def inter(gi, _):
        r0 = pl.multiple_of(gi * HG_ROWS, HG_ROWS)
        for c in range(cpg):
            rows = pl.ds(r0 + c * CHUNK, CHUNK)
            n = gi * cpg + c
            o = oi_s[rows, :] + _dot_nt(qe_s[rows, :], kv_s[n].astype(BF16))
            o = o * lax.rsqrt(jnp.mean(o * o, axis=-1, keepdims=True) + NORM_EPS)
            o_ref[0, rows, :] = (o * gain * sz_ref[0, rows, :].astype(F32)).astype(o_ref.dtype)
        return 0

    lax.fori_loop(0, n_grp, inter, 0)


def _hgrn(qb, fb, ib, szb, lb_logits, gain, tri, layer):
    B, S, _ = qb.shape
    n_ch = S // CHUNK
    tok = lambda: pl.BlockSpec((1, S, B_KDIM), lambda b, h: (b, 0, h))
    return pl.pallas_call(
        functools.partial(_hgrn_kernel, layer=layer),
        out_shape=jax.ShapeDtypeStruct((B, S, B_WIDTH), BF16),
        grid=(B, B_HEADS),
        in_specs=[tok(), tok(), tok(), tok(),
                  pl.BlockSpec((lb_logits.shape[0], B_KDIM), lambda b, h: (0, h)),
                  pl.BlockSpec((1, B_VDIM), lambda b, h: (0, h)),
                  pl.BlockSpec((HG_ROWS, HG_ROWS), lambda b, h: (0, 0))],
        out_specs=tok(),
        scratch_shapes=[pltpu.VMEM((S, B_KDIM), BF16),
                        pltpu.VMEM((n_ch, B_VDIM, B_KDIM), F32),
                        pltpu.VMEM((n_ch, 8, B_KDIM), F32),
                        pltpu.VMEM((S, B_VDIM), F32)],
        compiler_params=pltpu.CompilerParams(
            dimension_semantics=("parallel", "parallel"),
            vmem_limit_bytes=V7X_VMEM_LIMIT_BYTES),
        name="hgrn2",
    )(qb, fb, ib, szb, lb_logits, gain, tri)


def _out_kernel(x_ref, oa_ref, sza_ref, ob_ref, wgm_ref, bgm_ref, wa_ref, wb_ref,
                wo_ref, lng_ref, lnb_ref, o_ref, *, alpha):
    x = x_ref[...]
    sg = _sigmoid(_dot(x.astype(BF16), wgm_ref[...]) + bgm_ref[...])
    a = (oa_ref[...].astype(F32) * sza_ref[...].astype(F32)).astype(BF16)
    y = sg[:, :D_MODEL] * _dot(a, wa_ref[...]) + sg[:, D_MODEL:] * _dot(ob_ref[...], wb_ref[...])
    r = alpha * x + _dot(y.astype(BF16), wo_ref[...])
    mu = jnp.mean(r, axis=-1, keepdims=True)
    d = r - mu
    var = jnp.mean(d * d, axis=-1, keepdims=True)
    o_ref[...] = d * lax.rsqrt(var + NORM_EPS) * lng_ref[...] + lnb_ref[...]


def _merge_out(x2, oa, sza, ob, wgm, bgm, wa, wb, wo, lng, lnb, alpha):
    N, D = x2.shape
    T = OUT_ROWS
    full = lambda a: pl.BlockSpec(a.shape, lambda i: (0,) * a.ndim)
    rows = lambda w: pl.BlockSpec((T, w), lambda i: (i, 0))
    return pl.pallas_call(
        functools.partial(_out_kernel, alpha=alpha),
        out_shape=jax.ShapeDtypeStruct((N, D), x2.dtype),
        grid=(N // T,),
        in_specs=[rows(D), rows(A_WIDTH), rows(A_WIDTH), rows(B_WIDTH),
                  full(wgm), full(bgm), full(wa), full(wb), full(wo), full(lng), full(lnb)],
        out_specs=rows(D),
        compiler_params=pltpu.CompilerParams(
            dimension_semantics=("parallel",),
            vmem_limit_bytes=V7X_VMEM_LIMIT_BYTES),
        name="merge_out",
    )(x2, oa, sza, ob, wgm, bgm, wa, wb, wo, lng, lnb)


def _rope_tables(S):
    inv = ROPE_THETA ** (-jnp.arange(0, HEAD_DIM, 2, dtype=F32) / HEAD_DIM)
    ang = jnp.arange(S, dtype=F32)[:, None] * inv[None, :]
    cos = jnp.concatenate([jnp.cos(ang), jnp.cos(ang)], axis=-1)
    sin = jnp.concatenate([jnp.sin(ang), jnp.sin(ang)], axis=-1)
    first = (jnp.arange(HEAD_DIM) < HEAD_DIM // 2)[None, :]
    sina = jnp.where(first, -sin, 0.0)
    sinb = jnp.where(first, 0.0, sin)
    tile = lambda a: jnp.concatenate([a] * A_GROUPS, axis=-1)
    return cos.T, sin.T, tile(cos), tile(sina), tile(sinb)


def _overlap_t(S):
    n_cmp = S // CMP_STRIDE
    n_slc = S // SLC_BLOCK
    cs = np.arange(n_cmp)[None, :] * CMP_STRIDE
    ss = np.arange(n_slc)[:, None] * SLC_BLOCK
    ov = (cs < ss + SLC_BLOCK) & (cs + CMP_BLOCK > ss) & (np.arange(n_cmp)[None, :] < n_cmp - 1)
    return jnp.asarray(ov, dtype=BF16)


def _block_tri():
    r = np.arange(HG_ROWS)
    return jnp.asarray((r[:, None] // CHUNK == r[None, :] // CHUNK) & (r[None, :] <= r[:, None]), dtype=BF16)


def _layer(x, l, w_in, b_in, pe_k, w_k1, w_k2, pe_v, w_v1, w_v2, lb_logits, norm_g,
           w_a, w_b, w_o, ln_g, ln_b):
    B, S, D = x.shape
    alpha = (2 * DEPTH) ** 0.25
    o = _OFF
    wsl = lambda i: w_in[:, o[i]:o[i + 1]]
    bsl = lambda i: b_in[o[i]:o[i + 1]]
    kvw, kvb = wsl(1), bsl(1)
    kv_w = lambda j: kvw[:, j * KV_WIDTH:(j + 1) * KV_WIDTH]
    kv_b = lambda j: kvb[j * KV_WIDTH:(j + 1) * KV_WIDTH]
    gw, gb = wsl(2), bsl(2)
    gidx = np.zeros((A_GROUPS, GATE_ROWS), np.int32)
    gmask = np.zeros((A_GROUPS, GATE_ROWS), np.float32)
    for g in range(A_GROUPS):
        for br in range(3):
            for h in range(A_HPG):
                gidx[g, br * A_HPG + h] = (g * A_HPG + h) * 3 + br
                gmask[g, br * A_HPG + h] = 1.0
    gidx, gmask = gidx.reshape(-1), gmask.reshape(-1)
    gw_t = gw[:, gidx] * gmask[None, :]
    gb_t = gb[gidx] * gmask

    wt = jnp.concatenate([wsl(0), kv_w(3), kv_w(5), gw_t], axis=1).T.astype(BF16)
    bt = jnp.concatenate([bsl(0), kv_b(3), kv_b(5), gb_t])[:, None]
    wn = jnp.concatenate([kv_w(0), kv_w(1), kv_w(2), kv_w(4), wsl(3), wsl(4), wsl(5), wsl(6), wsl(7)],
                         axis=1).astype(BF16)
    bn = jnp.concatenate([kv_b(0), kv_b(1), kv_b(2), kv_b(4), bsl(3), bsl(4), bsl(5), bsl(6), bsl(7)])[None, :]
    cost, sint, cos, sina, sinb = _rope_tables(S)

    (qr, qn, vts, vtw, gt, kcmp, vcmp, ks, kw, sza, qb, fb, ib, szb) = _project(
        x, wn, bn, wt, bt, cost, sint, cos, sina, sinb)

    NC = S // CMP_STRIDE

    def pieces(a):
        a = a.reshape(B, NC, CMP_STRIDE, A_GROUPS, HEAD_DIM).transpose(0, 3, 1, 2, 4)
        return a.reshape(B, A_GROUPS, NC, CMP_STRIDE * HEAD_DIM)

    half = CMP_STRIDE * HEAD_DIM
    w1cat = lambda w1: jnp.concatenate([w1[:half], w1[half:]], axis=1).astype(BF16)
    kc, vct = _compress(pieces(kcmp), pieces(vcmp),
                        w1cat(w_k1), w_k1, pe_k.reshape(-1, 1), w_k2.astype(BF16),
                        w1cat(w_v1), w_v1, pe_v.reshape(-1, 1), w_v2.T.astype(BF16))

    oa = _nsa(qr, qn, gt, kc, vct, ks, kw, vts, vtw, _overlap_t(S))
    ob = _hgrn(qb, fb, ib, szb, lb_logits, norm_g[None, :], _block_tri(), l)

    wgm = jnp.concatenate([wsl(8), wsl(9)], axis=1).astype(BF16)
    bgm = jnp.concatenate([bsl(8), bsl(9)])[None, :]
    out = _merge_out(x.reshape(B * S, D), oa.reshape(B * S, A_WIDTH), sza.reshape(B * S, A_WIDTH),
                     ob.reshape(B * S, B_WIDTH), wgm, bgm, w_a.astype(BF16), w_b.astype(BF16),
                     w_o.astype(BF16), ln_g[None, :], ln_b[None, :], alpha)
    return out.reshape(B, S, D)


@jax.jit
def kernel(x, w_in, b_in, pe_cmp_k, w_cmp_k1, w_cmp_k2, pe_cmp_v, w_cmp_v1, w_cmp_v2,
           hgrn_lb_logits, hgrn_norm_g, w_branch_a, w_branch_b, w_out, ln_g, ln_b):
    B, S, D = x.shape
    assert D == D_MODEL and S % KV_CHUNK == 0 and S % PROJ_ROWS == 0 and S >= WIN_KEYS
    assert (B * S) % OUT_ROWS == 0 and S % HG_ROWS == 0
    for l in range(DEPTH):
        x = _layer(x, l, w_in[l], b_in[l], pe_cmp_k[l], w_cmp_k1[l], w_cmp_k2[l],
                   pe_cmp_v[l], w_cmp_v1[l], w_cmp_v2[l], hgrn_lb_logits, hgrn_norm_g[l],
                   w_branch_a[l], w_branch_b[l], w_out[l], ln_g[l], ln_b[l])
    return x
```

```python
import functools
import math

import numpy as np
import jax
import jax.numpy as jnp
from jax import lax
from jax.experimental import pallas as pl
from jax.experimental.pallas import tpu as pltpu

D_MODEL = 1024
DEPTH = 1
A_HEADS = 8
A_GROUPS = 2
A_HPG = A_HEADS // A_GROUPS
HEAD_DIM = 64
A_WIDTH = A_HEADS * HEAD_DIM
KV_WIDTH = A_GROUPS * HEAD_DIM
CMP_BLOCK = 32
CMP_STRIDE = 16
CMP_HIDDEN = 128
SLC_BLOCK = 64
SLC_TOP_N = 16
WINDOW = 512
ROPE_THETA = 10000.0
FORCE_SCORE = 1e30
B_HEADS = 4
B_KDIM = 128
B_VDIM = 128
B_FWIDTH = B_HEADS * B_KDIM
B_WIDTH = B_HEADS * B_VDIM
CHUNK = 64
NORM_EPS = 1e-5

IN_SPLITS = (A_WIDTH, 6 * KV_WIDTH, A_HEADS * 3, A_WIDTH, B_FWIDTH, B_FWIDTH,
             B_WIDTH, B_WIDTH, D_MODEL, D_MODEL)
_OFF = tuple(int(v) for v in np.cumsum((0,) + IN_SPLITS))

V7X_VMEM_LIMIT_BYTES = 56 * 1024 * 1024
PROJ_ROWS = 512
Q_TILE = 128
KV_CHUNK = 512
WIN_KEYS = WINDOW + Q_TILE
HG_ROWS = 256
OUT_ROWS = 512
MASKED = -1e30
LOG2E = math.log2(math.e)
GATE_ROWS = 16

F32 = jnp.float32
BF16 = jnp.bfloat16


def _dot(a, b):
    return jnp.dot(a, b, preferred_element_type=F32)


def _dot_nt(a, b):
    return lax.dot_general(a, b, (((1,), (1,)), ((), ())), preferred_element_type=F32)


def _sigmoid(x):
    return 1.0 / (1.0 + jnp.exp(-x))


def _silu(x):
    return x * _sigmoid(x)


def _proj_kernel(x_ref, wn_ref, bn_ref, wt_ref, bt_ref, cost_ref, sint_ref,
                 cos_ref, sina_ref, sinb_ref,
                 qr_ref, qn_ref, vts_ref, vtw_ref, gt_ref, kc_ref, vc_ref,
                 ks_ref, kw_ref, sza_ref, qb_ref, fb_ref, ib_ref, szb_ref):
    xb = x_ref[0].astype(BF16)
    scale = HEAD_DIM ** -0.5 * LOG2E

    ht = _dot_nt(wt_ref[...], xb) + bt_ref[...]
    cost = cost_ref[...]
    sint = sint_ref[...]
    half = HEAD_DIM // 2
    for h in range(A_HEADS):
        blk = ht[h * HEAD_DIM:(h + 1) * HEAD_DIM]
        rot = jnp.concatenate([-blk[half:], blk[:half]], axis=0)
        qr_ref[0, h * HEAD_DIM:(h + 1) * HEAD_DIM, :] = ((blk * cost + rot * sint) * scale).astype(BF16)
        qn_ref[0, h * HEAD_DIM:(h + 1) * HEAD_DIM, :] = (blk * scale).astype(BF16)
    o = A_WIDTH
    vts_ref[0] = ht[o:o + KV_WIDTH].astype(BF16)
    vtw_ref[0] = ht[o + KV_WIDTH:o + 2 * KV_WIDTH].astype(BF16)
    gt_ref[0] = _sigmoid(ht[o + 2 * KV_WIDTH:o + 2 * KV_WIDTH + 2 * GATE_ROWS])

    def cols(lo, hi):
        return _dot(xb, wn_ref[:, lo:hi]) + bn_ref[:, lo:hi]

    kv = cols(0, 4 * KV_WIDTH)
    kc_ref[0] = kv[:, 0:KV_WIDTH].astype(BF16)
    vc_ref[0] = kv[:, KV_WIDTH:2 * KV_WIDTH].astype(BF16)
    cos = cos_ref[...]
    sina = sina_ref[...]
    sinb = sinb_ref[...]

    def rope_rows(k):
        return (k * cos + pltpu.roll(k, 128 - half, axis=1) * sina + pltpu.roll(k, half, axis=1) * sinb)

    ks = rope_rows(kv[:, 2 * KV_WIDTH:3 * KV_WIDTH])
    rows = ks.shape[0]
    lane = lax.broadcasted_iota(jnp.int32, (rows, KV_WIDTH), 1)
    pos = pl.program_id(1) * rows + lax.broadcasted_iota(jnp.int32, (rows, KV_WIDTH), 0)
    blocks_per_chunk = KV_CHUNK // SLC_BLOCK
    ind = jnp.where(lane == HEAD_DIM + (pos // SLC_BLOCK) % blocks_per_chunk, 1.0, 0.0)
    ks_ref[0, 0] = (jnp.where(lane < HEAD_DIM, ks, 0.0) + ind).astype(BF16)
    ks_ref[0, 1] = (jnp.where(lane < HEAD_DIM, pltpu.roll(ks, HEAD_DIM, axis=1), 0.0) + ind).astype(BF16)
    kw_ref[0] = rope_rows(kv[:, 3 * KV_WIDTH:4 * KV_WIDTH]).astype(BF16)
    o = 4 * KV_WIDTH
    sza_ref[0] = _silu(cols(o, o + A_WIDTH)).astype(BF16)
    o += A_WIDTH
    qb_ref[0] = cols(o, o + B_FWIDTH).astype(BF16)
    o += B_FWIDTH
    fb_ref[0] = cols(o, o + B_FWIDTH)
    o += B_FWIDTH
    ib_ref[0] = cols(o, o + B_WIDTH).astype(BF16)
    o += B_WIDTH
    szb_ref[0] = _silu(cols(o, o + B_WIDTH)).astype(BF16)


def _project(x, wn, bn, wt, bt, cost, sint, cos, sina, sinb):
    B, S, D = x.shape
    T = PROJ_ROWS
    n_t = wt.shape[0]
    n_n = wn.shape[1]
    full = lambda shape: pl.BlockSpec(shape, lambda b, s: (0,) * len(shape))
    row_out = lambda w: pl.BlockSpec((1, T, w), lambda b, s: (b, s, 0))
    col_out = lambda r: pl.BlockSpec((1, r, T), lambda b, s: (b, 0, s))
    sds = jax.ShapeDtypeStruct
    out_shape = (
        sds((B, A_WIDTH, S), BF16), sds((B, A_WIDTH, S), BF16),
        sds((B, KV_WIDTH, S), BF16), sds((B, KV_WIDTH, S), BF16),
        sds((B, 2 * GATE_ROWS, S), F32),
        sds((B, S, KV_WIDTH), BF16), sds((B, S, KV_WIDTH), BF16),
        sds((B, A_GROUPS, S, KV_WIDTH), BF16), sds((B, S, KV_WIDTH), BF16),
        sds((B, S, A_WIDTH), BF16),
        sds((B, S, B_FWIDTH), BF16), sds((B, S, B_FWIDTH), F32),
        sds((B, S, B_WIDTH), BF16), sds((B, S, B_WIDTH), BF16),
    )
    out_specs = (
        col_out(A_WIDTH), col_out(A_WIDTH), col_out(KV_WIDTH), col_out(KV_WIDTH),
        col_out(2 * GATE_ROWS),
        row_out(KV_WIDTH), row_out(KV_WIDTH),
        pl.BlockSpec((1, A_GROUPS, T, KV_WIDTH), lambda b, s: (b, 0, s, 0)), row_out(KV_WIDTH),
        row_out(A_WIDTH), row_out(B_FWIDTH), row_out(B_FWIDTH), row_out(B_WIDTH), row_out(B_WIDTH),
    )
    in_specs = [
        pl.BlockSpec((1, T, D), lambda b, s: (b, s, 0)),
        full((D, n_n)), full((1, n_n)), full((n_t, D)), full((n_t, 1)),
        pl.BlockSpec((HEAD_DIM, T), lambda b, s: (0, s)),
        pl.BlockSpec((HEAD_DIM, T), lambda b, s: (0, s)),
        pl.BlockSpec((T, KV_WIDTH), lambda b, s: (s, 0)),
        pl.BlockSpec((T, KV_WIDTH), lambda b, s: (s, 0)),
        pl.BlockSpec((T, KV_WIDTH), lambda b, s: (s, 0)),
    ]
    return pl.pallas_call(
        _proj_kernel, out_shape=out_shape, grid=(B, S // T),
        in_specs=in_specs, out_specs=out_specs,
        compiler_params=pltpu.CompilerParams(
            dimension_semantics=("parallel", "parallel"),
            vmem_limit_bytes=V7X_VMEM_LIMIT_BYTES),
        name="in_proj",
    )(x, wn, bn, wt, bt, cost, sint, cos, sina, sinb)


def _compress_kernel(ck_ref, cv_ref, w1k_ref, w1kf_ref, pek_ref, w2k_ref,
                     w1v_ref, w1vf_ref, pev_ref, w2vt_ref, kc_ref, vct_ref):
    nc = ck_ref.shape[2]

    def hidden(c_ref, w1_ref, w1f_ref, pe_ref):
        a = _dot(c_ref[0, 0], w1_ref[...])
        lo = a[:, :CMP_HIDDEN]
        hi = pltpu.roll(a[:, CMP_HIDDEN:], nc - 1, axis=0)
        pe_term = jnp.sum(w1f_ref[...] * pe_ref[...], axis=0, keepdims=True)
        return _silu(lo + hi + pe_term).astype(BF16)

    hk = hidden(ck_ref, w1k_ref, w1kf_ref, pek_ref)
    kc_ref[0, 0] = _dot(hk, w2k_ref[...]).astype(BF16)
    hv = hidden(cv_ref, w1v_ref, w1vf_ref, pev_ref)
    vct_ref[0, 0] = _dot_nt(w2vt_ref[...], hv).astype(BF16)


def _compress(ck, cv, w1k, w1kf, pek, w2k, w1v, w1vf, pev, w2vt):
    B, G, NC, W = ck.shape
    full = lambda a: pl.BlockSpec(a.shape, lambda b, g: (0,) * a.ndim)
    blk = pl.BlockSpec((1, 1, NC, W), lambda b, g: (b, g, 0, 0))
    return pl.pallas_call(
        _compress_kernel,
        out_shape=(jax.ShapeDtypeStruct((B, G, NC, HEAD_DIM), BF16),
                   jax.ShapeDtypeStruct((B, G, HEAD_DIM, NC), BF16)),
        grid=(B, G),
        in_specs=[blk, blk, full(w1k), full(w1kf), full(pek), full(w2k),
                  full(w1v), full(w1vf), full(pev), full(w2vt)],
        out_specs=(pl.BlockSpec((1, 1, NC, HEAD_DIM), lambda b, g: (b, g, 0, 0)),
                   pl.BlockSpec((1, 1, HEAD_DIM, NC), lambda b, g: (b, g, 0, 0))),
        compiler_params=pltpu.CompilerParams(
            dimension_semantics=("parallel", "parallel"),
            vmem_limit_bytes=V7X_VMEM_LIMIT_BYTES),
        name="kv_compress",
    )(ck, cv, w1k, w1kf, pek, w2k, w1v, w1vf, pev, w2vt)


def _nsa_kernel(qr_ref, qn_ref, gt_ref, kc_ref, vct_ref, ks_ref, kw_ref,
                vts_ref, vtw_ref, ovl_ref, o_ref,
                val_s, rank_s, bias_s, wq_s, m_s, l_s, acc_s, ocw_s, sbuf_s, *, seq, n_sel):
    g = pl.program_id(1)
    nc = kc_ref.shape[2]
    n_slc = seq // SLC_BLOCK
    n_q = seq // Q_TILE
    n_ck = seq // KV_CHUNK
    lanes = A_HPG * Q_TILE
    bpc = KV_CHUNK // SLC_BLOCK
    tpc = KV_CHUNK // Q_TILE
    n_rb = n_slc // 8
    win_tiles = WINDOW // Q_TILE

    def stack_heads(ref, q0):
        return jnp.concatenate(
            [ref[0, h * HEAD_DIM:(h + 1) * HEAD_DIM, pl.ds(q0, Q_TILE)] for h in range(A_HPG)], axis=1)

    def per_head(a):
        return jnp.concatenate([a] * A_HPG, axis=1)

    def gate_row(gt, branch):
        return jnp.concatenate(
            [gt[branch * A_HPG + h:branch * A_HPG + h + 1, :] for h in range(A_HPG)], axis=1)

    lane_q = lax.broadcasted_iota(jnp.int32, (1, lanes), 1) & (Q_TILE - 1)
    own = lax.broadcasted_iota(jnp.int32, (KV_WIDTH, lanes), 0) // HEAD_DIM == g
    r_sq = lax.broadcasted_iota(jnp.int32, (Q_TILE, lanes), 0)
    causal_sq = r_sq <= lane_q
    lower_sq = r_sq > lane_q

    def prep(qi, _):
        s0 = pl.multiple_of(qi * Q_TILE, Q_TILE)
        t_all = s0 + lane_q
        qn = stack_heads(qn_ref, s0)
        qr = stack_heads(qr_ref, s0)
        wq_s[qi] = qr
        qr32 = qr.astype(F32)
        qrz = jnp.where(own, jnp.concatenate([qr32, qr32], axis=0), 0.0).astype(BF16)

        sc = _dot(kc_ref[0, 0], qn)
        n_idx = lax.broadcasted_iota(jnp.int32, (nc, lanes), 0)
        cvalid = n_idx * CMP_STRIDE + (CMP_BLOCK - 1) <= t_all
        sc = jnp.where(cvalid, sc, MASKED)
        mc = jnp.max(sc, axis=0, keepdims=True)
        pc = jnp.where(cvalid, jnp.exp2(sc - mc), 0.0)
        lc = jnp.sum(pc, axis=0, keepdims=True)
        pc = pc * (1.0 / jnp.where(lc > 0, lc, 1.0))
        o_cmp = _dot(vct_ref[0, 0], pc.astype(BF16))

        ps = pc[:, 0:Q_TILE]
        for h in range(1, A_HPG):
            ps = ps + pc[:, h * Q_TILE:(h + 1) * Q_TILE]
        p_hi = ps.astype(BF16)
        r1 = ps - p_hi.astype(F32)
        p_mid = r1.astype(BF16)
        p_lo = (r1 - p_mid.astype(F32)).astype(BF16)
        ovl = ovl_ref[...]
        imp = _dot(ovl, p_hi) + _dot(ovl, p_mid) + _dot(ovl, p_lo)

        t_row = s0 + lax.broadcasted_iota(jnp.int32, (1, Q_TILE), 1)
        j_idx = lax.broadcasted_iota(jnp.int32, (n_slc, Q_TILE), 0)
        cur = t_row // SLC_BLOCK
        forced = (j_idx == 0) | (j_idx == cur) | (j_idx == cur - 1)
        val_s[...] = jnp.where(j_idx > cur, -1.0, jnp.where(forced, FORCE_SCORE, imp))
        rank_s[...] = jnp.zeros((n_slc, Q_TILE), F32)
        last_rb = (2 * qi + 1) // 8
        need_rank = 2 * qi + 2 > n_sel
        jl = lax.broadcasted_iota(jnp.int32, (8, Q_TILE), 0)
        for kb in range(n_rb):
            @pl.when(jnp.logical_and(need_rank, kb <= last_rb))
            def _(kb=kb):
                v = val_s[...]
                rows_k = v[8 * kb:8 * kb + 8]
                for jb in range(n_rb):
                    vj = v[8 * jb:8 * jb + 8]
                    cnt = jnp.zeros((8, Q_TILE), F32)
                    for kl in range(8):
                        row = rows_k[kl:kl + 1, :]
                        if jb < kb:
                            beats = row > vj
                        elif jb > kb:
                            beats = row >= vj
                        else:
                            beats = (row > vj) | ((row == vj) & (jl > kl))
                        cnt = cnt + jnp.where(beats, 1.0, 0.0)
                    rank_s[8 * jb:8 * jb + 8, :] += cnt
        sel = (rank_s[...] < n_sel) & (j_idx <= cur)
        bias = per_head(jnp.where(sel, 0.0, MASKED))
        pad = jnp.zeros((bpc, lanes), F32)
        for c in range(n_ck):
            bias_s[qi, c] = jnp.concatenate([bias[c * bpc:(c + 1) * bpc], pad], axis=0).astype(BF16)

        gt = gt_ref[0, :, pl.ds(s0, Q_TILE)]
        go_cmp = gate_row(gt, 0) * o_cmp
        g_win = gate_row(gt, 2)

        @pl.when(qi >= win_tiles)
        def _():
            w0 = pl.multiple_of(s0 - WINDOW, Q_TILE)
            sw = _dot(kw_ref[0, pl.ds(w0, WIN_KEYS), :], qrz)
            sw = jnp.concatenate([jnp.where(lower_sq, sw[:Q_TILE], MASKED), sw[Q_TILE:WINDOW],
                                  jnp.where(causal_sq, sw[WINDOW:], MASKED)], axis=0)
            mw = jnp.max(sw, axis=0, keepdims=True)
            pw = jnp.exp2(sw - mw)
            lw = jnp.sum(pw, axis=0, keepdims=True)
            o_win = _dot(vtw_ref[0, :, pl.ds(w0, WIN_KEYS)], pw.astype(BF16)) * (1.0 / lw)
            ocw_s[qi] = go_cmp + g_win * o_win

        @pl.when(qi < win_tiles)
        def _():
            sw = _dot(kw_ref[0, 0:WIN_KEYS, :], qrz)
            kpos = lax.broadcasted_iota(jnp.int32, (WIN_KEYS, lanes), 0)
            wvalid = (kpos <= t_all) & (kpos > t_all - WINDOW)
            sw = jnp.where(wvalid, sw, MASKED)
            mw = jnp.max(sw, axis=0, keepdims=True)
            pw = jnp.exp2(sw - mw)
            lw = jnp.sum(pw, axis=0, keepdims=True)
            o_win = _dot(vtw_ref[0, :, 0:WIN_KEYS], pw.astype(BF16)) * (1.0 / lw)
            ocw_s[qi] = go_cmp + g_win * o_win

        m_s[qi] = jnp.full((8, lanes), MASKED, F32)
        l_s[qi] = jnp.zeros((8, lanes), F32)
        acc_s[qi] = jnp.zeros((HEAD_DIM, lanes), F32)
        return 0

    lax.fori_loop(0, n_q, prep, 0)

    wq_pad = jnp.zeros((KV_WIDTH - HEAD_DIM - 2 * bpc, lanes), BF16)

    def scores(c, qi, diag):
        rows = KV_CHUNK if diag is None else (diag + 1) * Q_TILE
        k0 = pl.multiple_of(c * KV_CHUNK, KV_CHUNK)
        wq = jnp.concatenate([wq_s[qi], bias_s[qi, c], wq_pad], axis=0)
        s = _dot(ks_ref[0, 0, pl.ds(k0, rows), :], wq)
        if diag is not None:
            head = [s[:rows - Q_TILE]] if diag > 0 else []
            s = jnp.concatenate(head + [jnp.where(causal_sq, s[rows - Q_TILE:], MASKED)], axis=0)
        return s

    def absorb(c, qi, s):
        rows = s.shape[0]
        k0 = pl.multiple_of(c * KV_CHUNK, KV_CHUNK)
        m_old = m_s[qi][0:1]
        m_new = jnp.maximum(m_old, jnp.max(s, axis=0, keepdims=True))
        alpha = jnp.exp2(m_old - m_new)
        p = jnp.exp2(s - m_new)
        l_s[qi] = alpha * l_s[qi] + jnp.sum(p, axis=0, keepdims=True)
        m_s[qi] = jnp.broadcast_to(m_new, (8, lanes))
        acc_s[qi] = alpha * acc_s[qi] + _dot(vts_ref[0, :, pl.ds(k0, rows)], p.astype(BF16))

    def chunk(c, _):
        q_diag = c * tpc
        q_first = q_diag + tpc
        last = n_q - 1
        s_prev = scores(c, q_diag, 0)
        for r in range(1, tpc):
            s_next = scores(c, q_diag + r, r)
            absorb(c, q_diag + r - 1, s_prev)
            s_prev = s_next
        sbuf_s[0] = scores(c, jnp.minimum(q_first, last), None)
        absorb(c, q_diag + tpc - 1, s_prev)

        def pair(i, _):
            q0 = q_first + 2 * i
            sbuf_s[1] = scores(c, q0 + 1, None)
            absorb(c, q0, sbuf_s[0])
            sbuf_s[0] = scores(c, jnp.minimum(q0 + 2, last), None)
            absorb(c, q0 + 1, sbuf_s[1])
            return 0

        lax.fori_loop(0, (n_q - q_first) // 2, pair, 0)
        return 0

    lax.fori_loop(0, n_ck, chunk, 0)

    def finish(qi, _):
        s0 = pl.multiple_of(qi * Q_TILE, Q_TILE)
        gt = gt_ref[0, :, pl.ds(s0, Q_TILE)]
        ot = ocw_s[qi] + gate_row(gt, 1) * acc_s[qi] * (1.0 / l_s[qi][0:1])
        halves = []
        for pr in range(A_HPG // 2):
            two = jnp.concatenate([ot[:, (2 * pr) * Q_TILE:(2 * pr + 1) * Q_TILE],
                                   ot[:, (2 * pr + 1) * Q_TILE:(2 * pr + 2) * Q_TILE]], axis=0)
            halves.append(two.T)
        o_ref[0, pl.ds(s0, Q_TILE), :] = jnp.concatenate(halves, axis=1).astype(o_ref.dtype)
        return 0

    lax.fori_loop(0, n_q, finish, 0)


def _nsa(qr, qn, gt, kc, vct, ks, kw, vts, vtw, ovl):
    B, _, S = qr.shape
    NC = kc.shape[2]
    n_slc = S // SLC_BLOCK
    n_sel = min(SLC_TOP_N, n_slc)
    n_q = S // Q_TILE
    n_ck = S // KV_CHUNK
    gw = A_HPG * HEAD_DIM
    lanes = A_HPG * Q_TILE
    kernel = functools.partial(_nsa_kernel, seq=S, n_sel=n_sel)
    in_specs = [
        pl.BlockSpec((1, gw, S), lambda b, g: (b, g, 0)),
        pl.BlockSpec((1, gw, S), lambda b, g: (b, g, 0)),
        pl.BlockSpec((1, GATE_ROWS, S), lambda b, g: (b, g, 0)),
        pl.BlockSpec((1, 1, NC, HEAD_DIM), lambda b, g: (b, g, 0, 0)),
        pl.BlockSpec((1, 1, HEAD_DIM, NC), lambda b, g: (b, g, 0, 0)),
        pl.BlockSpec((1, 1, S, KV_WIDTH), lambda b, g: (b, g, 0, 0)),
        pl.BlockSpec((1, S, KV_WIDTH), lambda b, g: (b, 0, 0)),
        pl.BlockSpec((1, HEAD_DIM, S), lambda b, g: (b, g, 0)),
        pl.BlockSpec((1, HEAD_DIM, S), lambda b, g: (b, g, 0)),
        pl.BlockSpec((n_slc, NC), lambda b, g: (0, 0)),
    ]
    scratch = [
        pltpu.VMEM((n_slc, Q_TILE), F32),
        pltpu.VMEM((n_slc, Q_TILE), F32),
        pltpu.VMEM((n_q, n_ck, 2 * (KV_CHUNK // SLC_BLOCK), lanes), BF16),
        pltpu.VMEM((n_q, HEAD_DIM, lanes), BF16),
        pltpu.VMEM((n_q, 8, lanes), F32),
        pltpu.VMEM((n_q, 8, lanes), F32),
        pltpu.VMEM((n_q, HEAD_DIM, lanes), F32),
        pltpu.VMEM((n_q, HEAD_DIM, lanes), F32),
        pltpu.VMEM((2, KV_CHUNK, lanes), F32),
    ]
    return pl.pallas_call(
        kernel,
        out_shape=jax.ShapeDtypeStruct((B, S, A_WIDTH), BF16),
        grid=(B, A_GROUPS),
        in_specs=in_specs,
        out_specs=pl.BlockSpec((1, S, gw), lambda b, g: (b, 0, g)),
        scratch_shapes=scratch,
        compiler_params=pltpu.CompilerParams(
            dimension_semantics=("parallel", "parallel"),
            vmem_limit_bytes=V7X_VMEM_LIMIT_BYTES),
        name="nsa_attention",
    )(qr, qn, gt, kc, vct, ks, kw, vts, vtw, ovl)


def _hgrn_kernel(q_ref, f_ref, i_ref, sz_ref, lbl_ref, g_ref, tri_ref, o_ref,
                 qe_s, kv_s, dl_s, oi_s, *, layer):
    S = q_ref.shape[1]
    n_grp = S // HG_ROWS
    cpg = HG_ROWS // CHUNK

    lg = lbl_ref[...]
    e = jnp.exp(lg - jnp.max(lg, axis=0, keepdims=True))
    lb = jnp.sum(e[0:layer + 1], axis=0, keepdims=True) / jnp.sum(e, axis=0, keepdims=True)

    tri = tri_ref[...]
    ri = lax.broadcasted_iota(jnp.int32, (HG_ROWS, HG_ROWS), 0)
    ci = lax.broadcasted_iota(jnp.int32, (HG_ROWS, HG_ROWS), 1)
    causal = (ri // CHUNK == ci // CHUNK) & (ci <= ri)

    def intra(gi, _):
        r0 = pl.multiple_of(gi * HG_ROWS, HG_ROWS)
        rows = pl.ds(r0, HG_ROWS)
        fg = lb + (1.0 - lb) * _sigmoid(f_ref[0, rows, :])
        logf = jnp.log(fg)
        kk = 1.0 - fg
        l_hi = logf.astype(BF16)
        r1 = logf - l_hi.astype(F32)
        l_mid = r1.astype(BF16)
        l_lo = (r1 - l_mid.astype(F32)).astype(BF16)
        b = _dot(tri, l_hi) + _dot(tri, l_mid) + _dot(tri, l_lo)
        b_last = jnp.concatenate(
            [jnp.broadcast_to(b[(c + 1) * CHUNK - 1:(c + 1) * CHUNK, :], (CHUNK, B_KDIM))
             for c in range(cpg)], axis=0)
        qe = (_silu(q_ref[0, rows, :].astype(F32)) * jnp.exp(b)).astype(BF16)
        ke = (kk * jnp.exp(-b)).astype(BF16)
        kd = (kk * jnp.exp(b_last - b)).astype(BF16)
        v = i_ref[0, rows, :]
        attn = jnp.where(causal, _dot_nt(qe, ke), 0.0)
        oi_s[rows, :] = _dot(attn.astype(BF16), v)
        qe_s[rows, :] = qe
        for c in range(cpg):
            sl = slice(c * CHUNK, (c + 1) * CHUNK)
            n = gi * cpg + c
            kv_s[n] = lax.dot_general(v[sl], kd[sl], (((0,), (0,)), ((), ())),
                                      preferred_element_type=F32)
            dl_s[n] = jnp.exp(b_last[c * CHUNK:c * CHUNK + 8, :])
        return 0

    lax.fori_loop(0, n_grp, intra, 0)

    def scan(n, state):
        upd = kv_s[n]
        kv_s[n] = state
        return dl_s[n][0:1, :] * state + upd

    lax.fori_loop(0, S // CHUNK, scan, jnp.zeros((B_VDIM, B_KDIM), F32))

    gain = g_ref[...]

    def inter(gi, _):
        r0 = pl.multiple_of(gi * HG_ROWS, HG_ROWS)
        for c in range(cpg):
            rows = pl.ds(r0 + c * CHUNK, CHUNK)
            n = gi * cpg + c
            o = oi_s[rows, :] + _dot_nt(qe_s[rows, :], kv_s[n].astype(BF16))
            o = o * lax.rsqrt(jnp.mean(o * o, axis=-1, keepdims=True) + NORM_EPS)
            o_ref[0, rows, :] = (o * gain * sz_ref[0, rows, :].astype(F32)).astype(o_ref.dtype)
        return 0

    lax.fori_loop(0, n_grp, inter, 0)


def _hgrn(qb, fb, ib, szb, lb_logits, gain, tri, layer):
    B, S, _ = qb.shape
    n_ch = S // CHUNK
    tok = lambda: pl.BlockSpec((1, S, B_KDIM), lambda b, h: (b, 0, h))
    return pl.pallas_call(
        functools.partial(_hgrn_kernel, layer=layer),
        out_shape=jax.ShapeDtypeStruct((B, S, B_WIDTH), BF16),
        grid=(B, B_HEADS),
        in_specs=[tok(), tok(), tok(), tok(),
                  pl.BlockSpec((lb_logits.shape[0], B_KDIM), lambda b, h: (0, h)),
                  pl.BlockSpec((1, B_VDIM), lambda b, h: (0, h)),
                  pl.BlockSpec((HG_ROWS, HG_ROWS), lambda b, h: (0, 0))],
        out_specs=tok(),
        scratch_shapes=[pltpu.VMEM((S, B_KDIM), BF16),
                        pltpu.VMEM((n_ch, B_VDIM, B_KDIM), F32),
                        pltpu.VMEM((n_ch, 8, B_KDIM), F32),
                        pltpu.VMEM((S, B_VDIM), F32)],
        compiler_params=pltpu.CompilerParams(
            dimension_semantics=("parallel", "parallel"),
            vmem_limit_bytes=V7X_VMEM_LIMIT_BYTES),
        name="hgrn2",
    )(qb, fb, ib, szb, lb_logits, gain, tri)


def _out_kernel(x_ref, oa_ref, sza_ref, ob_ref, wgm_ref, bgm_ref, wa_ref, wb_ref,
                wo_ref, lng_ref, lnb_ref, o_ref, *, alpha):
    x = x_ref[...]
    sg = _sigmoid(_dot(x.astype(BF16), wgm_ref[...]) + bgm_ref[...])
    a = (oa_ref[...].astype(F32) * sza_ref[...].astype(F32)).astype(BF16)
    y = sg[:, :D_MODEL] * _dot(a, wa_ref[...]) + sg[:, D_MODEL:] * _dot(ob_ref[...], wb_ref[...])
    r = alpha * x + _dot(y.astype(BF16), wo_ref[...])
    mu = jnp.mean(r, axis=-1, keepdims=True)
    d = r - mu
    var = jnp.mean(d * d, axis=-1, keepdims=True)
    o_ref[...] = d * lax.rsqrt(var + NORM_EPS) * lng_ref[...] + lnb_ref[...]


def _merge_out(x2, oa, sza, ob, wgm, bgm, wa, wb, wo, lng, lnb, alpha):
    N, D = x2.shape
    T = OUT_ROWS
    full = lambda a: pl.BlockSpec(a.shape, lambda i: (0,) * a.ndim)
    rows = lambda w: pl.BlockSpec((T, w), lambda i: (i, 0))
    return pl.pallas_call(
        functools.partial(_out_kernel, alpha=alpha),
        out_shape=jax.ShapeDtypeStruct((N, D), x2.dtype),
        grid=(N // T,),
        in_specs=[rows(D), rows(A_WIDTH), rows(A_WIDTH), rows(B_WIDTH),
                  full(wgm), full(bgm), full(wa), full(wb), full(wo), full(lng), full(lnb)],
        out_specs=rows(D),
        compiler_params=pltpu.CompilerParams(
            dimension_semantics=("parallel",),
            vmem_limit_bytes=V7X_VMEM_LIMIT_BYTES),
        name="merge_out",
    )(x2, oa, sza, ob, wgm, bgm, wa, wb, wo, lng, lnb)


def _rope_tables(S):
    inv = ROPE_THETA ** (-jnp.arange(0, HEAD_DIM, 2, dtype=F32) / HEAD_DIM)
    ang = jnp.arange(S, dtype=F32)[:, None] * inv[None, :]
    cos = jnp.concatenate([jnp.cos(ang), jnp.cos(ang)], axis=-1)
    sin = jnp.concatenate([jnp.sin(ang), jnp.sin(ang)], axis=-1)
    first = (jnp.arange(HEAD_DIM) < HEAD_DIM // 2)[None, :]
    sina = jnp.where(first, -sin, 0.0)
    sinb = jnp.where(first, 0.0, sin)
    tile = lambda a: jnp.concatenate([a] * A_GROUPS, axis=-1)
    return cos.T, sin.T, tile(cos), tile(sina), tile(sinb)


def _overlap_t(S):
    n_cmp = S // CMP_STRIDE
    n_slc = S // SLC_BLOCK
    cs = np.arange(n_cmp)[None, :] * CMP_STRIDE
    ss = np.arange(n_slc)[:, None] * SLC_BLOCK
    ov = (cs < ss + SLC_BLOCK) & (cs + CMP_BLOCK > ss) & (np.arange(n_cmp)[None, :] < n_cmp - 1)
    return jnp.asarray(ov, dtype=BF16)


def _block_tri():
    r = np.arange(HG_ROWS)
    return jnp.asarray((r[:, None] // CHUNK == r[None, :] // CHUNK) & (r[None, :] <= r[:, None]), dtype=BF16)


def _layer(x, l, w_in, b_in, pe_k, w_k1, w_k2, pe_v, w_v1, w_v2, lb_logits, norm_g,
           w_a, w_b, w_o, ln_g, ln_b):
    B, S, D = x.shape
    alpha = (2 * DEPTH) ** 0.25
    o = _OFF
    wsl = lambda i: w_in[:, o[i]:o[i + 1]]
    bsl = lambda i: b_in[o[i]:o[i + 1]]
    kvw, kvb = wsl(1), bsl(1)
    kv_w = lambda j: kvw[:, j * KV_WIDTH:(j + 1) * KV_WIDTH]
    kv_b = lambda j: kvb[j * KV_WIDTH:(j + 1) * KV_WIDTH]
    gw, gb = wsl(2), bsl(2)
    gidx = np.zeros((A_GROUPS, GATE_ROWS), np.int32)
    gmask = np.zeros((A_GROUPS, GATE_ROWS), np.float32)
    for g in range(A_GROUPS):
        for br in range(3):
            for h in range(A_HPG):
                gidx[g, br * A_HPG + h] = (g * A_HPG + h) * 3 + br
                gmask[g, br * A_HPG + h] = 1.0
    gidx, gmask = gidx.reshape(-1), gmask.reshape(-1)
    gw_t = gw[:, gidx] * gmask[None, :]
    gb_t = gb[gidx] * gmask

    wt = jnp.concatenate([wsl(0), kv_w(3), kv_w(5), gw_t], axis=1).T.astype(BF16)
    bt = jnp.concatenate([bsl(0), kv_b(3), kv_b(5), gb_t])[:, None]
    wn = jnp.concatenate([kv_w(0), kv_w(1), kv_w(2), kv_w(4), wsl(3), wsl(4), wsl(5), wsl(6), wsl(7)],
                         axis=1).astype(BF16)
    bn = jnp.concatenate([kv_b(0), kv_b(1), kv_b(2), kv_b(4), bsl(3), bsl(4), bsl(5), bsl(6), bsl(7)])[None, :]
    cost, sint, cos, sina, sinb = _rope_tables(S)

    (qr, qn, vts, vtw, gt, kcmp, vcmp, ks, kw, sza, qb, fb, ib, szb) = _project(
        x, wn, bn, wt, bt, cost, sint, cos, sina, sinb)

    NC = S // CMP_STRIDE

    def pieces(a):
        a = a.reshape(B, NC, CMP_STRIDE, A_GROUPS, HEAD_DIM).transpose(0, 3, 1, 2, 4)
        return a.reshape(B, A_GROUPS, NC, CMP_STRIDE * HEAD_DIM)

    half = CMP_STRIDE * HEAD_DIM
    w1cat = lambda w1: jnp.concatenate([w1[:half], w1[half:]], axis=1).astype(BF16)
    kc, vct = _compress(pieces(kcmp), pieces(vcmp),
                        w1cat(w_k1), w_k1, pe_k.reshape(-1, 1), w_k2.astype(BF16),
                        w1cat(w_v1), w_v1, pe_v.reshape(-1, 1), w_v2.T.astype(BF16))

    oa = _nsa(qr, qn, gt, kc, vct, ks, kw, vts, vtw, _overlap_t(S))
    ob = _hgrn(qb, fb, ib, szb, lb_logits, norm_g[None, :], _block_tri(), l)

    wgm = jnp.concatenate([wsl(8), wsl(9)], axis=1).astype(BF16)
    bgm = jnp.concatenate([bsl(8), bsl(9)])[None, :]
    out = _merge_out(x.reshape(B * S, D), oa.reshape(B * S, A_WIDTH), sza.reshape(B * S, A_WIDTH),
                     ob.reshape(B * S, B_WIDTH), wgm, bgm, w_a.astype(BF16), w_b.astype(BF16),
                     w_o.astype(BF16), ln_g[None, :], ln_b[None, :], alpha)
    return out.reshape(B, S, D)


@jax.jit
def kernel(x, w_in, b_in, pe_cmp_k, w_cmp_k1, w_cmp_k2, pe_cmp_v, w_cmp_v1, w_cmp_v2,
           hgrn_lb_logits, hgrn_norm_g, w_branch_a, w_branch_b, w_out, ln_g, ln_b):
    B, S, D = x.shape
    assert D == D_MODEL and S % KV_CHUNK == 0 and S % PROJ_ROWS == 0 and S >= WIN_KEYS
    assert (B * S) % OUT_ROWS == 0 and S % HG_ROWS == 0 and (S // SLC_BLOCK) % 8 == 0
    for l in range(DEPTH):
        x = _layer(x, l, w_in[l], b_in[l], pe_cmp_k[l], w_cmp_k1[l], w_cmp_k2[l],
                   pe_cmp_v[l], w_cmp_v1[l], w_cmp_v2[l], hgrn_lb_logits, hgrn_norm_g[l],
                   w_branch_a[l], w_branch_b[l], w_out[l], ln_g[l], ln_b[l])
    return x
```

```python
import functools
import math

import numpy as np
import jax
import jax.numpy as jnp
from jax import lax
from jax.experimental import pallas as pl
from jax.experimental.pallas import tpu as pltpu

D_MODEL = 1024
DEPTH = 1
A_HEADS = 8
A_GROUPS = 2
A_HPG = A_HEADS // A_GROUPS
HEAD_DIM = 64
A_WIDTH = A_HEADS * HEAD_DIM
KV_WIDTH = A_GROUPS * HEAD_DIM
CMP_BLOCK = 32
CMP_STRIDE = 16
CMP_HIDDEN = 128
SLC_BLOCK = 64
SLC_TOP_N = 16
WINDOW = 512
ROPE_THETA = 10000.0
FORCE_SCORE = 1e30
B_HEADS = 4
B_KDIM = 128
B_VDIM = 128
B_FWIDTH = B_HEADS * B_KDIM
B_WIDTH = B_HEADS * B_VDIM
CHUNK = 64
NORM_EPS = 1e-5

IN_SPLITS = (A_WIDTH, 6 * KV_WIDTH, A_HEADS * 3, A_WIDTH, B_FWIDTH, B_FWIDTH,
             B_WIDTH, B_WIDTH, D_MODEL, D_MODEL)
_OFF = tuple(int(v) for v in np.cumsum((0,) + IN_SPLITS))

V7X_VMEM_LIMIT_BYTES = 56 * 1024 * 1024
PROJ_ROWS = 512
Q_TILE = 128
KV_CHUNK = 512
SLC_UNROLL = 4
WIN_KEYS = WINDOW + Q_TILE
HG_ROWS = 256
HG_GROUPS = 2
OUT_ROWS = 512
MASKED = -1e30
LOG2E = math.log2(math.e)
GATE_ROWS = 16

F32 = jnp.float32
BF16 = jnp.bfloat16


def _dot(a, b):
    return jnp.dot(a, b, preferred_element_type=F32)


def _dot_nt(a, b):
    return lax.dot_general(a, b, (((1,), (1,)), ((), ())), preferred_element_type=F32)


def _sigmoid(x):
    return 1.0 / (1.0 + jnp.exp(-x))


def _silu(x):
    return x * _sigmoid(x)


def _proj_kernel(x_ref, wn_ref, bn_ref, wt_ref, bt_ref, cost_ref, sint_ref,
                 cos_ref, sina_ref, sinb_ref,
                 qr_ref, qn_ref, vts_ref, vtw_ref, gt_ref, kc_ref, vc_ref,
                 ks_ref, kw_ref, sza_ref, qb_ref, fb_ref, ib_ref, szb_ref, cmp_s):
    xb = x_ref[0].astype(BF16)
    scale = HEAD_DIM ** -0.5 * LOG2E

    ht = _dot_nt(wt_ref[...], xb) + bt_ref[...]
    cost = cost_ref[...]
    sint = sint_ref[...]
    half = HEAD_DIM // 2
    for h in range(A_HEADS):
        blk = ht[h * HEAD_DIM:(h + 1) * HEAD_DIM]
        rot = jnp.concatenate([-blk[half:], blk[:half]], axis=0)
        qr_ref[0, h * HEAD_DIM:(h + 1) * HEAD_DIM, :] = ((blk * cost + rot * sint) * scale).astype(BF16)
        qn_ref[0, h * HEAD_DIM:(h + 1) * HEAD_DIM, :] = (blk * scale).astype(BF16)
    o = A_WIDTH
    vts_ref[0] = ht[o:o + KV_WIDTH].astype(BF16)
    vtw_ref[0] = ht[o + KV_WIDTH:o + 2 * KV_WIDTH].astype(BF16)
    gt_ref[0] = _sigmoid(ht[o + 2 * KV_WIDTH:o + 2 * KV_WIDTH + 2 * GATE_ROWS])

    def cols(lo, hi):
        return _dot(xb, wn_ref[:, lo:hi]) + bn_ref[:, lo:hi]

    kv = cols(0, 4 * KV_WIDTH)
    cmp_s[0] = kv[:, 0:KV_WIDTH]
    cmp_s[1] = kv[:, KV_WIDTH:2 * KV_WIDTH]
    pieces = cmp_s.shape[1] // CMP_STRIDE
    for t in range(CMP_STRIDE):
        kc_ref[0, t] = cmp_s[0, pl.ds(t, pieces, stride=CMP_STRIDE), :].astype(BF16)
        vc_ref[0, t] = cmp_s[1, pl.ds(t, pieces, stride=CMP_STRIDE), :].astype(BF16)
    cos = cos_ref[...]
    sina = sina_ref[...]
    sinb = sinb_ref[...]

    def rope_rows(k):
        return (k * cos + pltpu.roll(k, 128 - half, axis=1) * sina + pltpu.roll(k, half, axis=1) * sinb)

    ks = rope_rows(kv[:, 2 * KV_WIDTH:3 * KV_WIDTH])
    rows = ks.shape[0]
    lane = lax.broadcasted_iota(jnp.int32, (rows, KV_WIDTH), 1)
    pos = pl.program_id(1) * rows + lax.broadcasted_iota(jnp.int32, (rows, KV_WIDTH), 0)
    blocks_per_chunk = KV_CHUNK // SLC_BLOCK
    ind = jnp.where(lane == HEAD_DIM + (pos // SLC_BLOCK) % blocks_per_chunk, 1.0, 0.0)
    ks_ref[0, 0] = (jnp.where(lane < HEAD_DIM, ks, 0.0) + ind).astype(BF16)
    ks_ref[0, 1] = (jnp.where(lane < HEAD_DIM, pltpu.roll(ks, HEAD_DIM, axis=1), 0.0) + ind).astype(BF16)
    kw = rope_rows(kv[:, 3 * KV_WIDTH:4 * KV_WIDTH])
    kw_ref[0, 0] = jnp.where(lane < HEAD_DIM, kw, 0.0).astype(BF16)
    kw_ref[0, 1] = jnp.where(lane < HEAD_DIM, pltpu.roll(kw, HEAD_DIM, axis=1), 0.0).astype(BF16)
    o = 4 * KV_WIDTH
    sza_ref[0] = _silu(cols(o, o + A_WIDTH)).astype(BF16)
    o += A_WIDTH
    qb_ref[0] = cols(o, o + B_FWIDTH).astype(BF16)
    o += B_FWIDTH
    fb_ref[0] = cols(o, o + B_FWIDTH)
    o += B_FWIDTH
    ib_ref[0] = cols(o, o + B_WIDTH).astype(BF16)
    o += B_WIDTH
    szb_ref[0] = _silu(cols(o, o + B_WIDTH)).astype(BF16)


def _project(x, wn, bn, wt, bt, cost, sint, cos, sina, sinb):
    B, S, D = x.shape
    T = PROJ_ROWS
    n_t = wt.shape[0]
    n_n = wn.shape[1]
    full = lambda shape: pl.BlockSpec(shape, lambda b, s: (0,) * len(shape))
    row_out = lambda w: pl.BlockSpec((1, T, w), lambda b, s: (b, s, 0))
    col_out = lambda r: pl.BlockSpec((1, r, T), lambda b, s: (b, 0, s))
    sds = jax.ShapeDtypeStruct
    out_shape = (
        sds((B, A_WIDTH, S), BF16), sds((B, A_WIDTH, S), BF16),
        sds((B, KV_WIDTH, S), BF16), sds((B, KV_WIDTH, S), BF16),
        sds((B, 2 * GATE_ROWS, S), F32),
        sds((B, CMP_STRIDE, S // CMP_STRIDE, KV_WIDTH), BF16),
        sds((B, CMP_STRIDE, S // CMP_STRIDE, KV_WIDTH), BF16),
        sds((B, A_GROUPS, S, KV_WIDTH), BF16), sds((B, A_GROUPS, S, KV_WIDTH), BF16),
        sds((B, S, A_WIDTH), BF16),
        sds((B, S, B_FWIDTH), BF16), sds((B, S, B_FWIDTH), F32),
        sds((B, S, B_WIDTH), BF16), sds((B, S, B_WIDTH), BF16),
    )
    out_specs = (
        col_out(A_WIDTH), col_out(A_WIDTH), col_out(KV_WIDTH), col_out(KV_WIDTH),
        col_out(2 * GATE_ROWS),
        pl.BlockSpec((1, CMP_STRIDE, T // CMP_STRIDE, KV_WIDTH), lambda b, s: (b, 0, s, 0)),
        pl.BlockSpec((1, CMP_STRIDE, T // CMP_STRIDE, KV_WIDTH), lambda b, s: (b, 0, s, 0)),
        pl.BlockSpec((1, A_GROUPS, T, KV_WIDTH), lambda b, s: (b, 0, s, 0)),
        pl.BlockSpec((1, A_GROUPS, T, KV_WIDTH), lambda b, s: (b, 0, s, 0)),
        row_out(A_WIDTH), row_out(B_FWIDTH), row_out(B_FWIDTH), row_out(B_WIDTH), row_out(B_WIDTH),
    )
    in_specs = [
        pl.BlockSpec((1, T, D), lambda b, s: (b, s, 0)),
        full((D, n_n)), full((1, n_n)), full((n_t, D)), full((n_t, 1)),
        pl.BlockSpec((HEAD_DIM, T), lambda b, s: (0, s)),
        pl.BlockSpec((HEAD_DIM, T), lambda b, s: (0, s)),
        pl.BlockSpec((T, KV_WIDTH), lambda b, s: (s, 0)),
        pl.BlockSpec((T, KV_WIDTH), lambda b, s: (s, 0)),
        pl.BlockSpec((T, KV_WIDTH), lambda b, s: (s, 0)),
    ]
    return pl.pallas_call(
        _proj_kernel, out_shape=out_shape, grid=(B, S // T),
        in_specs=in_specs, out_specs=out_specs,
        scratch_shapes=[pltpu.VMEM((2, T, KV_WIDTH), F32)],
        compiler_params=pltpu.CompilerParams(
            dimension_semantics=("parallel", "parallel"),
            vmem_limit_bytes=V7X_VMEM_LIMIT_BYTES),
        name="in_proj",
    )(x, wn, bn, wt, bt, cost, sint, cos, sina, sinb)


def _compress_kernel(ck_ref, cv_ref, w1k_ref, w1kf_ref, pek_ref, w2k_ref,
                     w1v_ref, w1vf_ref, pev_ref, w2vt_ref, kc_ref, vct_ref):
    nc = ck_ref.shape[2]
    hid = CMP_HIDDEN

    def hidden(c_ref, w1_ref, w1f_ref, pe_ref):
        a = _dot(c_ref[0, 0], w1_ref[0])
        for t in range(1, CMP_STRIDE):
            a = a + _dot(c_ref[0, t], w1_ref[t])
        pe_term = jnp.sum(w1f_ref[...] * pe_ref[...], axis=0, keepdims=True)
        out = []
        for g in range(A_GROUPS):
            lo = a[:, 2 * g * hid:(2 * g + 1) * hid]
            hi = pltpu.roll(a[:, (2 * g + 1) * hid:(2 * g + 2) * hid], nc - 1, axis=0)
            out.append(_silu(lo + hi + pe_term).astype(BF16))
        return out

    hk = hidden(ck_ref, w1k_ref, w1kf_ref, pek_ref)
    hv = hidden(cv_ref, w1v_ref, w1vf_ref, pev_ref)
    for g in range(A_GROUPS):
        kc_ref[0, g] = _dot(hk[g], w2k_ref[...]).astype(BF16)
        vct_ref[0, g] = _dot_nt(w2vt_ref[...], hv[g]).astype(BF16)


def _compress(ck, cv, w1k, w1kf, pek, w2k, w1v, w1vf, pev, w2vt):
    B, P, NC, W = ck.shape
    full = lambda a: pl.BlockSpec(a.shape, lambda b: (0,) * a.ndim)
    blk = pl.BlockSpec((1, P, NC, W), lambda b: (b, 0, 0, 0))
    return pl.pallas_call(
        _compress_kernel,
        out_shape=(jax.ShapeDtypeStruct((B, A_GROUPS, NC, HEAD_DIM), BF16),
                   jax.ShapeDtypeStruct((B, A_GROUPS, HEAD_DIM, NC), BF16)),
        grid=(B,),
        in_specs=[blk, blk, full(w1k), full(w1kf), full(pek), full(w2k),
                  full(w1v), full(w1vf), full(pev), full(w2vt)],
        out_specs=(pl.BlockSpec((1, A_GROUPS, NC, HEAD_DIM), lambda b: (b, 0, 0, 0)),
                   pl.BlockSpec((1, A_GROUPS, HEAD_DIM, NC), lambda b: (b, 0, 0, 0))),
        compiler_params=pltpu.CompilerParams(
            dimension_semantics=("parallel",),
            vmem_limit_bytes=V7X_VMEM_LIMIT_BYTES),
        name="kv_compress",
    )(ck, cv, w1k, w1kf, pek, w2k, w1v, w1vf, pev, w2vt)


def _nsa_kernel(qr_ref, qn_ref, gt_ref, kc_ref, vct_ref, ks_ref, kw_ref,
                vts_ref, vtw_ref, ovl_ref, o_ref,
                bias_s, wq_s, m_s, l_s, acc_s, ocw_s, sbuf_s, scb_s, swb_s, *, seq, n_sel):
    nc = kc_ref.shape[2]
    n_slc = seq // SLC_BLOCK
    n_q = seq // Q_TILE
    n_ck = seq // KV_CHUNK
    lanes = A_HPG * Q_TILE
    bpc = KV_CHUNK // SLC_BLOCK
    tpc = KV_CHUNK // Q_TILE
    win_tiles = WINDOW // Q_TILE
    last = n_q - 1

    def stack_heads(ref, q0):
        return jnp.concatenate(
            [ref[0, h * HEAD_DIM:(h + 1) * HEAD_DIM, pl.ds(q0, Q_TILE)] for h in range(A_HPG)], axis=1)

    def per_head(a):
        return jnp.concatenate([a] * A_HPG, axis=1)

    def gate_row(gt, branch):
        return jnp.concatenate(
            [gt[branch * A_HPG + h:branch * A_HPG + h + 1, :] for h in range(A_HPG)], axis=1)

    def tile_start(qi):
        return pl.multiple_of(qi * Q_TILE, Q_TILE)

    lane_q = lax.broadcasted_iota(jnp.int32, (1, lanes), 1) & (Q_TILE - 1)
    r_sq = lax.broadcasted_iota(jnp.int32, (Q_TILE, lanes), 0)
    causal_sq = r_sq <= lane_q
    lower_sq = r_sq > lane_q
    q_pad = jnp.zeros((KV_WIDTH - HEAD_DIM, lanes), BF16)

    def cmp_keys(rb):
        return min(nc, -(-(rb * KV_CHUNK // CMP_STRIDE) // 128) * 128)

    def cmp_scores(qi, rb):
        return _dot(kc_ref[0, 0, 0:cmp_keys(rb), :], stack_heads(qn_ref, tile_start(qi)))

    def select_tile(qi, rb, sc):
        nk = cmp_keys(rb)
        s0 = tile_start(qi)
        t_row = s0 + lax.broadcasted_iota(jnp.int32, (1, Q_TILE), 1)
        wq_s[qi] = stack_heads(qr_ref, s0)

        n_end = lax.broadcasted_iota(jnp.int32, (nk, Q_TILE), 0) * CMP_STRIDE + (CMP_BLOCK - 1)
        sc = sc + per_head(jnp.where(n_end <= t_row, 0.0, MASKED))
        mc = jnp.max(sc, axis=0, keepdims=True)
        pc = jnp.exp2(sc - mc)
        lc = jnp.sum(pc, axis=0, keepdims=True)
        pc = pc * jnp.where(s0 + lane_q >= CMP_BLOCK - 1, 1.0 / lc, 0.0)
        o_cmp = _dot(vct_ref[0, 0, :, 0:nk], pc.astype(BF16))

        ps = pc[:, 0:Q_TILE]
        for h in range(1, A_HPG):
            ps = ps + pc[:, h * Q_TILE:(h + 1) * Q_TILE]
        p_hi = ps.astype(BF16)
        r1 = ps - p_hi.astype(F32)
        p_mid = r1.astype(BF16)
        p_lo = (r1 - p_mid.astype(F32)).astype(BF16)
        ovl = ovl_ref[0:8 * rb, 0:nk]
        imp = _dot(ovl, p_hi) + _dot(ovl, p_mid) + _dot(ovl, p_lo)

        j_idx = lax.broadcasted_iota(jnp.int32, (8 * rb, Q_TILE), 0)
        cur = t_row // SLC_BLOCK
        sel = j_idx <= cur
        if 8 * rb > n_sel:
            forced = (j_idx == 0) | (j_idx == cur) | (j_idx == cur - 1)
            val = jnp.where(j_idx > cur, -1.0, jnp.where(forced, FORCE_SCORE, imp))
            rows = [val[8 * j:8 * j + 8] for j in range(rb)]
            cnt = [jnp.zeros((8, Q_TILE), F32) for _ in range(rb)]
            jl = lax.broadcasted_iota(jnp.int32, (8, Q_TILE), 0)
            for kb in range(rb):
                for kl in range(8):
                    row = rows[kb][kl:kl + 1, :]
                    for jb in range(rb):
                        if jb < kb:
                            beats = row > rows[jb]
                        elif jb > kb:
                            beats = row >= rows[jb]
                        else:
                            beats = (row > rows[jb]) | ((row == rows[jb]) & (jl > kl))
                        cnt[jb] = cnt[jb] + jnp.where(beats, 1.0, 0.0)
            sel = sel & (jnp.concatenate(cnt, axis=0) < n_sel)
        bias = per_head(jnp.where(sel, 0.0, MASKED))
        pad = jnp.zeros((bpc, lanes), F32)
        for c in range(rb):
            bias_s[qi, c] = jnp.concatenate([bias[c * bpc:(c + 1) * bpc], pad], axis=0).astype(BF16)

        gt = gt_ref[0, :, pl.ds(s0, Q_TILE)]
        ocw_s[qi] = gate_row(gt, 0) * o_cmp
        m_s[qi] = jnp.full((8, lanes), MASKED, F32)
        l_s[qi] = jnp.zeros((8, lanes), F32)
        acc_s[qi] = jnp.zeros((HEAD_DIM, lanes), F32)

    for r in range(n_ck):
        rb = r + 1
        nk = cmp_keys(rb)
        scb_s[0, 0:nk, :] = cmp_scores(r * tpc, rb)

        def select_pair(i, _, rb=rb, nk=nk, base=r * tpc):
            q0 = base + 2 * i
            scb_s[1, 0:nk, :] = cmp_scores(q0 + 1, rb)
            select_tile(q0, rb, scb_s[0, 0:nk, :])
            scb_s[0, 0:nk, :] = cmp_scores(jnp.minimum(q0 + 2, base + tpc - 1), rb)
            select_tile(q0 + 1, rb, scb_s[1, 0:nk, :])
            return 0

        lax.fori_loop(0, tpc // 2, select_pair, 0)

    def win_finish(qi, sw, w0):
        mw = jnp.max(sw, axis=0, keepdims=True)
        pw = jnp.exp2(sw - mw)
        lw = jnp.sum(pw, axis=0, keepdims=True)
        o_win = _dot(vtw_ref[0, :, pl.ds(w0, WIN_KEYS)], pw.astype(BF16)) * (1.0 / lw)
        gt = gt_ref[0, :, pl.ds(tile_start(qi), Q_TILE)]
        ocw_s[qi] = ocw_s[qi] + gate_row(gt, 2) * o_win

    def win_head(qi, _):
        wq = jnp.concatenate([wq_s[qi], q_pad], axis=0)
        sw = _dot(kw_ref[0, 0, 0:WIN_KEYS, :], wq)
        t_all = tile_start(qi) + lane_q
        kpos = lax.broadcasted_iota(jnp.int32, (WIN_KEYS, lanes), 0)
        sw = jnp.where((kpos <= t_all) & (kpos > t_all - WINDOW), sw, MASKED)
        win_finish(qi, sw, 0)
        return 0

    lax.fori_loop(0, min(win_tiles, n_q), win_head, 0)

    def win_scores(qi):
        w0 = pl.multiple_of(tile_start(qi) - WINDOW, Q_TILE)
        wq = jnp.concatenate([wq_s[qi], q_pad], axis=0)
        return _dot(kw_ref[0, 0, pl.ds(w0, WIN_KEYS), :], wq)

    def win_tile(qi, sw):
        sw = jnp.concatenate([jnp.where(lower_sq, sw[:Q_TILE], MASKED), sw[Q_TILE:WINDOW],
                              jnp.where(causal_sq, sw[WINDOW:], MASKED)], axis=0)
        win_finish(qi, sw, pl.multiple_of(tile_start(qi) - WINDOW, Q_TILE))

    def win_pair(i, _):
        q0 = win_tiles + 2 * i
        swb_s[1] = win_scores(q0 + 1)
        win_tile(q0, swb_s[0])
        swb_s[0] = win_scores(jnp.minimum(q0 + 2, last))
        win_tile(q0 + 1, swb_s[1])
        return 0

    if n_q > win_tiles:
        swb_s[0] = win_scores(win_tiles)
        lax.fori_loop(0, (n_q - win_tiles) // 2, win_pair, 0)

    wq_pad = jnp.zeros((KV_WIDTH - HEAD_DIM - 2 * bpc, lanes), BF16)

    def scores(c, qi, diag):
        rows = KV_CHUNK if diag is None else (diag + 1) * Q_TILE
        k0 = pl.multiple_of(c * KV_CHUNK, KV_CHUNK)
        wq = jnp.concatenate([wq_s[qi], bias_s[qi, c], wq_pad], axis=0)
        s = _dot(ks_ref[0, 0, pl.ds(k0, rows), :], wq)
        if diag is not None:
            head = [s[:rows - Q_TILE]] if diag > 0 else []
            s = jnp.concatenate(head + [jnp.where(causal_sq, s[rows - Q_TILE:], MASKED)], axis=0)
        return s

    def absorb(c, qi, s):
        rows = s.shape[0]
        k0 = pl.multiple_of(c * KV_CHUNK, KV_CHUNK)
        m_old = m_s[qi][0:1]
        m_new = jnp.maximum(m_old, jnp.max(s, axis=0, keepdims=True))
        alpha = jnp.exp2(m_old - m_new)
        p = jnp.exp2(s - m_new)
        l_s[qi] = alpha * l_s[qi] + jnp.sum(p, axis=0, keepdims=True)
        m_s[qi] = jnp.broadcast_to(m_new, (8, lanes))
        acc_s[qi] = alpha * acc_s[qi] + _dot(vts_ref[0, :, pl.ds(k0, rows)], p.astype(BF16))

    def chunk(c, _):
        q_diag = c * tpc
        q_first = q_diag + tpc
        s_prev = scores(c, q_diag, 0)
        for r in range(1, tpc):
            s_next = scores(c, q_diag + r, r)
            absorb(c, q_diag + r - 1, s_prev)
            s_prev = s_next
        sbuf_s[0] = scores(c, jnp.minimum(q_first, last), None)
        absorb(c, q_diag + tpc - 1, s_prev)

        def quad(i, _):
            q0 = q_first + SLC_UNROLL * i
            for u in range(SLC_UNROLL):
                sbuf_s[(u + 1) & 1] = scores(c, jnp.minimum(q0 + u + 1, last), None)
                absorb(c, q0 + u, sbuf_s[u & 1])
            return 0

        lax.fori_loop(0, (n_q - q_first) // SLC_UNROLL, quad, 0)
        return 0

    lax.fori_loop(0, n_ck, chunk, 0)

    def finish(qi, _):
        s0 = tile_start(qi)
        gt = gt_ref[0, :, pl.ds(s0, Q_TILE)]
        ot = ocw_s[qi] + gate_row(gt, 1) * acc_s[qi] * (1.0 / l_s[qi][0:1])
        halves = []
        for pr in range(A_HPG // 2):
            two = jnp.concatenate([ot[:, (2 * pr) * Q_TILE:(2 * pr + 1) * Q_TILE],
                                   ot[:, (2 * pr + 1) * Q_TILE:(2 * pr + 2) * Q_TILE]], axis=0)
            halves.append(two.T)
        o_ref[0, pl.ds(s0, Q_TILE), :] = jnp.concatenate(halves, axis=1).astype(o_ref.dtype)
        return 0

    lax.fori_loop(0, n_q, finish, 0)


def _nsa(qr, qn, gt, kc, vct, ks, kw, vts, vtw, ovl):
    B, _, S = qr.shape
    NC = kc.shape[2]
    n_slc = S // SLC_BLOCK
    n_sel = min(SLC_TOP_N, n_slc)
    n_q = S // Q_TILE
    n_ck = S // KV_CHUNK
    gw = A_HPG * HEAD_DIM
    lanes = A_HPG * Q_TILE
    kernel = functools.partial(_nsa_kernel, seq=S, n_sel=n_sel)
    per_group = lambda rows, cols: pl.BlockSpec((1, 1, rows, cols), lambda b, g: (b, g, 0, 0))
    in_specs = [
        pl.BlockSpec((1, gw, S), lambda b, g: (b, g, 0)),
        pl.BlockSpec((1, gw, S), lambda b, g: (b, g, 0)),
        pl.BlockSpec((1, GATE_ROWS, S), lambda b, g: (b, g, 0)),
        per_group(NC, HEAD_DIM), per_group(HEAD_DIM, NC),
        per_group(S, KV_WIDTH), per_group(S, KV_WIDTH),
        pl.BlockSpec((1, HEAD_DIM, S), lambda b, g: (b, g, 0)),
        pl.BlockSpec((1, HEAD_DIM, S), lambda b, g: (b, g, 0)),
        pl.BlockSpec((n_slc, NC), lambda b, g: (0, 0)),
    ]
    scratch = [
        pltpu.VMEM((n_q, n_ck, 2 * (KV_CHUNK // SLC_BLOCK), lanes), BF16),
        pltpu.VMEM((n_q, HEAD_DIM, lanes), BF16),
        pltpu.VMEM((n_q, 8, lanes), F32),
        pltpu.VMEM((n_q, 8, lanes), F32),
        pltpu.VMEM((n_q, HEAD_DIM, lanes), F32),
        pltpu.VMEM((n_q, HEAD_DIM, lanes), F32),
        pltpu.VMEM((2, KV_CHUNK, lanes), F32),
        pltpu.VMEM((2, NC, lanes), F32),
        pltpu.VMEM((2, WIN_KEYS, lanes), F32),
    ]
    return pl.pallas_call(
        kernel,
        out_shape=jax.ShapeDtypeStruct((B, S, A_WIDTH), BF16),
        grid=(B, A_GROUPS),
        in_specs=in_specs,
        out_specs=pl.BlockSpec((1, S, gw), lambda b, g: (b, 0, g)),
        scratch_shapes=scratch,
        compiler_params=pltpu.CompilerParams(
            dimension_semantics=("parallel", "parallel"),
            vmem_limit_bytes=V7X_VMEM_LIMIT_BYTES),
        name="nsa_attention",
    )(qr, qn, gt, kc, vct, ks, kw, vts, vtw, ovl)


def _hgrn_kernel(q_ref, f_ref, i_ref, sz_ref, lbl_ref, g_ref, tri_ref, o_ref,
                 qe_s, kv_s, dl_s, oi_s, *, layer):
    S = q_ref.shape[1]
    n_grp = S // HG_ROWS
    cpg = HG_ROWS // CHUNK

    lg = lbl_ref[...]
    e = jnp.exp(lg - jnp.max(lg, axis=0, keepdims=True))
    lb = jnp.sum(e[0:layer + 1], axis=0, keepdims=True) / jnp.sum(e, axis=0, keepdims=True)

    tri = tri_ref[...]
    ri = lax.broadcasted_iota(jnp.int32, (HG_ROWS, HG_ROWS), 0)
    ci = lax.broadcasted_iota(jnp.int32, (HG_ROWS, HG_ROWS), 1)
    causal = (ri // CHUNK == ci // CHUNK) & (ci <= ri)

    def stage_decay(gi):
        rows = pl.ds(pl.multiple_of(gi * HG_ROWS, HG_ROWS), HG_ROWS)
        fg = lb + (1.0 - lb) * _sigmoid(f_ref[0, rows, :])
        logf = jnp.log(fg)
        l_hi = logf.astype(BF16)
        r1 = logf - l_hi.astype(F32)
        l_mid = r1.astype(BF16)
        l_lo = (r1 - l_mid.astype(F32)).astype(BF16)
        b = _dot(tri, l_hi) + _dot(tri, l_mid) + _dot(tri, l_lo)
        return rows, 1.0 - fg, b

    def stage_scores(rows, kk, b):
        b_last = jnp.concatenate(
            [jnp.broadcast_to(b[(c + 1) * CHUNK - 1:(c + 1) * CHUNK, :], (CHUNK, B_KDIM))
             for c in range(cpg)], axis=0)
        qe = (_silu(q_ref[0, rows, :].astype(F32)) * jnp.exp(b)).astype(BF16)
        ke = (kk * jnp.exp(-b)).astype(BF16)
        kd = (kk * jnp.exp(b_last - b)).astype(BF16)
        attn = jnp.where(causal, _dot_nt(qe, ke), 0.0).astype(BF16)
        return qe, kd, attn, b_last

    def stage_outputs(gi, rows, qe, kd, attn, b_last):
        v = i_ref[0, rows, :]
        oi_s[rows, :] = _dot(attn, v)
        qe_s[rows, :] = qe
        for c in range(cpg):
            sl = slice(c * CHUNK, (c + 1) * CHUNK)
            n = gi * cpg + c
            kv_s[n] = lax.dot_general(v[sl], kd[sl], (((0,), (0,)), ((), ())),
                                      preferred_element_type=F32)
            dl_s[n] = jnp.exp(b_last[c * CHUNK:c * CHUNK + 8, :])

    def intra(i, _):
        groups = [HG_GROUPS * i + u for u in range(HG_GROUPS)]
        decays = [stage_decay(gi) for gi in groups]
        mids = [stage_scores(*d) for d in decays]
        for gi, d, m in zip(groups, decays, mids):
            stage_outputs(gi, d[0], *m)
        return 0

    lax.fori_loop(0, n_grp // HG_GROUPS, intra, 0)

    def scan(n, state):
        upd = kv_s[n]
        kv_s[n] = state
        return dl_s[n][0:1, :] * state + upd

    lax.fori_loop(0, S // CHUNK, scan, jnp.zeros((B_VDIM, B_KDIM), F32))

    gain = g_ref[...]

    def inter(gi, _):
        r0 = pl.multiple_of(gi * HG_ROWS * HG_GROUPS, HG_ROWS * HG_GROUPS)
        for c in range(cpg * HG_GROUPS):
            rows = pl.ds(r0 + c * CHUNK, CHUNK)
            n = gi * cpg * HG_GROUPS + c
            o = oi_s[rows, :] + _dot_nt(qe_s[rows, :], kv_s[n].astype(BF16))
            o = o * lax.rsqrt(jnp.mean(o * o, axis=-1, keepdims=True) + NORM_EPS)
            o_ref[0, rows, :] = (o * gain * sz_ref[0, rows, :].astype(F32)).astype(o_ref.dtype)
        return 0

    lax.fori_loop(0, n_grp // HG_GROUPS, inter, 0)


def _hgrn(qb, fb, ib, szb, lb_logits, gain, tri, layer):
    B, S, _ = qb.shape
    n_ch = S // CHUNK
    tok = lambda: pl.BlockSpec((1, S, B_KDIM), lambda b, h: (b, 0, h))
    return pl.pallas_call(
        functools.partial(_hgrn_kernel, layer=layer),
        out_shape=jax.ShapeDtypeStruct((B, S, B_WIDTH), BF16),
        grid=(B, B_HEADS),
        in_specs=[tok(), tok(), tok(), tok(),
                  pl.BlockSpec((lb_logits.shape[0], B_KDIM), lambda b, h: (0, h)),
                  pl.BlockSpec((1, B_VDIM), lambda b, h: (0, h)),
                  pl.BlockSpec((HG_ROWS, HG_ROWS), lambda b, h: (0, 0))],
        out_specs=tok(),
        scratch_shapes=[pltpu.VMEM((S, B_KDIM), BF16),
                        pltpu.VMEM((n_ch, B_VDIM, B_KDIM), F32),
                        pltpu.VMEM((n_ch, 8, B_KDIM), F32),
                        pltpu.VMEM((S, B_VDIM), F32)],
        compiler_params=pltpu.CompilerParams(
            dimension_semantics=("parallel", "parallel"),
            vmem_limit_bytes=V7X_VMEM_LIMIT_BYTES),
        name="hgrn2",
    )(qb, fb, ib, szb, lb_logits, gain, tri)


def _out_kernel(x_ref, oa_ref, sza_ref, ob_ref, wgm_ref, bgm_ref, wa_ref, wb_ref,
                wo_ref, lng_ref, lnb_ref, o_ref, *, alpha):
    x = x_ref[...]
    sg = _sigmoid(_dot(x.astype(BF16), wgm_ref[...]) + bgm_ref[...])
    a = (oa_ref[...].astype(F32) * sza_ref[...].astype(F32)).astype(BF16)
    y = sg[:, :D_MODEL] * _dot(a, wa_ref[...]) + sg[:, D_MODEL:] * _dot(ob_ref[...], wb_ref[...])
    r = alpha * x + _dot(y.astype(BF16), wo_ref[...])
    mu = jnp.mean(r, axis=-1, keepdims=True)
    d = r - mu
    var = jnp.mean(d * d, axis=-1, keepdims=True)
    o_ref[...] = d * lax.rsqrt(var + NORM_EPS) * lng_ref[...] + lnb_ref[...]


def _merge_out(x2, oa, sza, ob, wgm, bgm, wa, wb, wo, lng, lnb, alpha):
    N, D = x2.shape
    T = OUT_ROWS
    full = lambda a: pl.BlockSpec(a.shape, lambda i: (0,) * a.ndim)
    rows = lambda w: pl.BlockSpec((T, w), lambda i: (i, 0))
    return pl.pallas_call(
        functools.partial(_out_kernel, alpha=alpha),
        out_shape=jax.ShapeDtypeStruct((N, D), x2.dtype),
        grid=(N // T,),
        in_specs=[rows(D), rows(A_WIDTH), rows(A_WIDTH), rows(B_WIDTH),
                  full(wgm), full(bgm), full(wa), full(wb), full(wo), full(lng), full(lnb)],
        out_specs=rows(D),
        compiler_params=pltpu.CompilerParams(
            dimension_semantics=("parallel",),
            vmem_limit_bytes=V7X_VMEM_LIMIT_BYTES),
        name="merge_out",
    )(x2, oa, sza, ob, wgm, bgm, wa, wb, wo, lng, lnb)


def _rope_tables(S):
    inv = ROPE_THETA ** (-jnp.arange(0, HEAD_DIM, 2, dtype=F32) / HEAD_DIM)
    ang = jnp.arange(S, dtype=F32)[:, None] * inv[None, :]
    cos = jnp.concatenate([jnp.cos(ang), jnp.cos(ang)], axis=-1)
    sin = jnp.concatenate([jnp.sin(ang), jnp.sin(ang)], axis=-1)
    first = (jnp.arange(HEAD_DIM) < HEAD_DIM // 2)[None, :]
    sina = jnp.where(first, -sin, 0.0)
    sinb = jnp.where(first, 0.0, sin)
    tile = lambda a: jnp.concatenate([a] * A_GROUPS, axis=-1)
    return cos.T, sin.T, tile(cos), tile(sina), tile(sinb)


def _overlap_t(S):
    n_cmp = S // CMP_STRIDE
    n_slc = S // SLC_BLOCK
    cs = np.arange(n_cmp)[None, :] * CMP_STRIDE
    ss = np.arange(n_slc)[:, None] * SLC_BLOCK
    ov = (cs < ss + SLC_BLOCK) & (cs + CMP_BLOCK > ss) & (np.arange(n_cmp)[None, :] < n_cmp - 1)
    return jnp.asarray(ov, dtype=BF16)


def _block_tri():
    r = np.arange(HG_ROWS)
    return jnp.asarray((r[:, None] // CHUNK == r[None, :] // CHUNK) & (r[None, :] <= r[:, None]), dtype=BF16)


def _layer(x, l, w_in, b_in, pe_k, w_k1, w_k2, pe_v, w_v1, w_v2, lb_logits, norm_g,
           w_a, w_b, w_o, ln_g, ln_b):
    B, S, D = x.shape
    alpha = (2 * DEPTH) ** 0.25
    o = _OFF
    wsl = lambda i: w_in[:, o[i]:o[i + 1]]
    bsl = lambda i: b_in[o[i]:o[i + 1]]
    kvw, kvb = wsl(1), bsl(1)
    kv_w = lambda j: kvw[:, j * KV_WIDTH:(j + 1) * KV_WIDTH]
    kv_b = lambda j: kvb[j * KV_WIDTH:(j + 1) * KV_WIDTH]
    gw, gb = wsl(2), bsl(2)
    gidx = np.zeros((A_GROUPS, GATE_ROWS), np.int32)
    gmask = np.zeros((A_GROUPS, GATE_ROWS), np.float32)
    for g in range(A_GROUPS):
        for br in range(3):
            for h in range(A_HPG):
                gidx[g, br * A_HPG + h] = (g * A_HPG + h) * 3 + br
                gmask[g, br * A_HPG + h] = 1.0
    gidx, gmask = gidx.reshape(-1), gmask.reshape(-1)
    gw_t = gw[:, gidx] * gmask[None, :]
    gb_t = gb[gidx] * gmask

    wt = jnp.concatenate([wsl(0), kv_w(3), kv_w(5), gw_t], axis=1).T.astype(BF16)
    bt = jnp.concatenate([bsl(0), kv_b(3), kv_b(5), gb_t])[:, None]
    wn = jnp.concatenate([kv_w(0), kv_w(1), kv_w(2), kv_w(4), wsl(3), wsl(4), wsl(5), wsl(6), wsl(7)],
                         axis=1).astype(BF16)
    bn = jnp.concatenate([kv_b(0), kv_b(1), kv_b(2), kv_b(4), bsl(3), bsl(4), bsl(5), bsl(6), bsl(7)])[None, :]
    cost, sint, cos, sina, sinb = _rope_tables(S)

    (qr, qn, vts, vtw, gt, kcmp, vcmp, ks, kw, sza, qb, fb, ib, szb) = _project(
        x, wn, bn, wt, bt, cost, sint, cos, sina, sinb)

    half = CMP_STRIDE * HEAD_DIM

    def w1_planes(w1):
        both = jnp.concatenate([w1[:half], w1[half:]], axis=1).reshape(CMP_STRIDE, HEAD_DIM, 2 * CMP_HIDDEN)
        z = jnp.zeros_like(both)
        return jnp.concatenate([jnp.concatenate([both, z], axis=2),
                                jnp.concatenate([z, both], axis=2)], axis=1).astype(BF16)

    kc, vct = _compress(kcmp, vcmp,
                        w1_planes(w_k1), w_k1, pe_k.reshape(-1, 1), w_k2.astype(BF16),
                        w1_planes(w_v1), w_v1, pe_v.reshape(-1, 1), w_v2.T.astype(BF16))

    oa = _nsa(qr, qn, gt, kc, vct, ks, kw, vts, vtw, _overlap_t(S))
    ob = _hgrn(qb, fb, ib, szb, lb_logits, norm_g[None, :], _block_tri(), l)

    wgm = jnp.concatenate([wsl(8), wsl(9)], axis=1).astype(BF16)
    bgm = jnp.concatenate([bsl(8), bsl(9)])[None, :]
    out = _merge_out(x.reshape(B * S, D), oa.reshape(B * S, A_WIDTH), sza.reshape(B * S, A_WIDTH),
                     ob.reshape(B * S, B_WIDTH), wgm, bgm, w_a.astype(BF16), w_b.astype(BF16),
                     w_o.astype(BF16), ln_g[None, :], ln_b[None, :], alpha)
    return out.reshape(B, S, D)


@jax.jit
def kernel(x, w_in, b_in, pe_cmp_k, w_cmp_k1, w_cmp_k2, pe_cmp_v, w_cmp_v1, w_cmp_v2,
           hgrn_lb_logits, hgrn_norm_g, w_branch_a, w_branch_b, w_out, ln_g, ln_b):
    B, S, D = x.shape
    assert D == D_MODEL and S % KV_CHUNK == 0 and S % PROJ_ROWS == 0 and S >= WIN_KEYS
    assert (B * S) % OUT_ROWS == 0 and S % (HG_ROWS * HG_GROUPS) == 0 and (S // KV_CHUNK) * KV_CHUNK == S
    for l in range(DEPTH):
        x = _layer(x, l, w_in[l], b_in[l], pe_cmp_k[l], w_cmp_k1[l], w_cmp_k2[l],
                   pe_cmp_v[l], w_cmp_v1[l], w_cmp_v2[l], hgrn_lb_logits, hgrn_norm_g[l],
                   w_branch_a[l], w_branch_b[l], w_out[l], ln_g[l], ln_b[l])
    return x
```

```python
import functools
import math

import numpy as np
import jax
import jax.numpy as jnp
from jax import lax
from jax.experimental import pallas as pl
from jax.experimental.pallas import tpu as pltpu

D_MODEL = 1024
DEPTH = 1
A_HEADS = 8
A_GROUPS = 2
A_HPG = A_HEADS // A_GROUPS
HEAD_DIM = 64
A_WIDTH = A_HEADS * HEAD_DIM
KV_WIDTH = A_GROUPS * HEAD_DIM
CMP_BLOCK = 32
CMP_STRIDE = 16
CMP_HIDDEN = 128
SLC_BLOCK = 64
SLC_TOP_N = 16
WINDOW = 512
ROPE_THETA = 10000.0
FORCE_SCORE = 1e30
B_HEADS = 4
B_KDIM = 128
B_VDIM = 128
B_FWIDTH = B_HEADS * B_KDIM
B_WIDTH = B_HEADS * B_VDIM
CHUNK = 64
NORM_EPS = 1e-5

IN_SPLITS = (A_WIDTH, 6 * KV_WIDTH, A_HEADS * 3, A_WIDTH, B_FWIDTH, B_FWIDTH,
             B_WIDTH, B_WIDTH, D_MODEL, D_MODEL)
_OFF = tuple(int(v) for v in np.cumsum((0,) + IN_SPLITS))

V7X_VMEM_LIMIT_BYTES = 56 * 1024 * 1024
PROJ_ROWS = 512
Q_TILE = 128
KV_CHUNK = 512
IN_FLIGHT = 4
AHEAD = 2
JOBS_PER_TRIP = 8
WIN_KEYS = WINDOW + Q_TILE
HG_ROWS = 256
HG_GROUPS = 4
OUT_ROWS = 512
MASKED = -1e30
LOG2E = math.log2(math.e)
GATE_ROWS = 16
V_ROWS = HEAD_DIM + 16

F32 = jnp.float32
BF16 = jnp.bfloat16


def _dot(a, b):
    return jnp.dot(a, b, preferred_element_type=F32)


def _dot_nt(a, b):
    return lax.dot_general(a, b, (((1,), (1,)), ((), ())), preferred_element_type=F32)


def _sigmoid(x):
    return 1.0 / (1.0 + jnp.exp(-x))


def _silu(x):
    return x * _sigmoid(x)


def _proj_kernel(x_ref, wn_ref, bn_ref, wt_ref, bt_ref, cost_ref, sint_ref,
                 cos_ref, sina_ref, sinb_ref,
                 qr_ref, qn_ref, vts_ref, vtw_ref, gt_ref, kc_ref, vc_ref,
                 ks_ref, kw_ref, sza_ref, qb_ref, fb_ref, ib_ref, szb_ref, cmp_s):
    xb = x_ref[0].astype(BF16)
    scale = HEAD_DIM ** -0.5 * LOG2E

    ht = _dot_nt(wt_ref[...], xb) + bt_ref[...]
    cost = cost_ref[...]
    sint = sint_ref[...]
    half = HEAD_DIM // 2
    for h in range(A_HEADS):
        blk = ht[h * HEAD_DIM:(h + 1) * HEAD_DIM]
        rot = jnp.concatenate([-blk[half:], blk[:half]], axis=0)
        qr_ref[0, h * HEAD_DIM:(h + 1) * HEAD_DIM, :] = ((blk * cost + rot * sint) * scale).astype(BF16)
        qn_ref[0, h * HEAD_DIM:(h + 1) * HEAD_DIM, :] = (blk * scale).astype(BF16)
    o = A_WIDTH
    ones_rows = jnp.where(lax.broadcasted_iota(jnp.int32, (V_ROWS - HEAD_DIM, ht.shape[1]), 0) == 0, 1.0, 0.0)
    for g in range(A_GROUPS):
        for ref, base in ((vts_ref, o), (vtw_ref, o + KV_WIDTH)):
            rows_g = ht[base + g * HEAD_DIM:base + (g + 1) * HEAD_DIM]
            ref[0, g] = jnp.concatenate([rows_g, ones_rows], axis=0).astype(BF16)
    gt_ref[0] = _sigmoid(ht[o + 2 * KV_WIDTH:o + 2 * KV_WIDTH + 2 * GATE_ROWS])

    def cols(lo, hi):
        return _dot(xb, wn_ref[:, lo:hi]) + bn_ref[:, lo:hi]

    kv = cols(0, 4 * KV_WIDTH)
    cmp_s[0] = kv[:, 0:KV_WIDTH]
    cmp_s[1] = kv[:, KV_WIDTH:2 * KV_WIDTH]
    pieces = cmp_s.shape[1] // CMP_STRIDE
    for t in range(CMP_STRIDE):
        kc_ref[0, t] = cmp_s[0, pl.ds(t, pieces, stride=CMP_STRIDE), :].astype(BF16)
        vc_ref[0, t] = cmp_s[1, pl.ds(t, pieces, stride=CMP_STRIDE), :].astype(BF16)
    cos = cos_ref[...]
    sina = sina_ref[...]
    sinb = sinb_ref[...]

    def rope_rows(k):
        return (k * cos + pltpu.roll(k, 128 - half, axis=1) * sina + pltpu.roll(k, half, axis=1) * sinb)

    ks = rope_rows(kv[:, 2 * KV_WIDTH:3 * KV_WIDTH])
    rows = ks.shape[0]
    lane = lax.broadcasted_iota(jnp.int32, (rows, KV_WIDTH), 1)
    pos = pl.program_id(1) * rows + lax.broadcasted_iota(jnp.int32, (rows, KV_WIDTH), 0)
    blocks_per_chunk = KV_CHUNK // SLC_BLOCK
    ind = jnp.where(lane == HEAD_DIM + (pos // SLC_BLOCK) % blocks_per_chunk, 1.0, 0.0)
    ks_ref[0, 0] = (jnp.where(lane < HEAD_DIM, ks, 0.0) + ind).astype(BF16)
    ks_ref[0, 1] = (jnp.where(lane < HEAD_DIM, pltpu.roll(ks, HEAD_DIM, axis=1), 0.0) + ind).astype(BF16)
    kw = rope_rows(kv[:, 3 * KV_WIDTH:4 * KV_WIDTH])
    kw_ref[0, 0] = jnp.where(lane < HEAD_DIM, kw, 0.0).astype(BF16)
    kw_ref[0, 1] = jnp.where(lane < HEAD_DIM, pltpu.roll(kw, HEAD_DIM, axis=1), 0.0).astype(BF16)
    o = 4 * KV_WIDTH
    sza_ref[0] = _silu(cols(o, o + A_WIDTH)).astype(BF16)
    o += A_WIDTH
    qb_ref[0] = cols(o, o + B_FWIDTH).astype(BF16)
    o += B_FWIDTH
    fb_ref[0] = cols(o, o + B_FWIDTH)
    o += B_FWIDTH
    ib_ref[0] = cols(o, o + B_WIDTH).astype(BF16)
    o += B_WIDTH
    szb_ref[0] = _silu(cols(o, o + B_WIDTH)).astype(BF16)


def _project(x, wn, bn, wt, bt, cost, sint, cos, sina, sinb):
    B, S, D = x.shape
    T = PROJ_ROWS
    n_t = wt.shape[0]
    n_n = wn.shape[1]
    full = lambda shape: pl.BlockSpec(shape, lambda b, s: (0,) * len(shape))
    row_out = lambda w: pl.BlockSpec((1, T, w), lambda b, s: (b, s, 0))
    col_out = lambda r: pl.BlockSpec((1, r, T), lambda b, s: (b, 0, s))
    sds = jax.ShapeDtypeStruct
    out_shape = (
        sds((B, A_WIDTH, S), BF16), sds((B, A_WIDTH, S), BF16),
        sds((B, A_GROUPS, V_ROWS, S), BF16), sds((B, A_GROUPS, V_ROWS, S), BF16),
        sds((B, 2 * GATE_ROWS, S), F32),
        sds((B, CMP_STRIDE, S // CMP_STRIDE, KV_WIDTH), BF16),
        sds((B, CMP_STRIDE, S // CMP_STRIDE, KV_WIDTH), BF16),
        sds((B, A_GROUPS, S, KV_WIDTH), BF16), sds((B, A_GROUPS, S, KV_WIDTH), BF16),
        sds((B, S, A_WIDTH), BF16),
        sds((B, S, B_FWIDTH), BF16), sds((B, S, B_FWIDTH), F32),
        sds((B, S, B_WIDTH), BF16), sds((B, S, B_WIDTH), BF16),
    )
    out_specs = (
        col_out(A_WIDTH), col_out(A_WIDTH),
        pl.BlockSpec((1, A_GROUPS, V_ROWS, T), lambda b, s: (b, 0, 0, s)),
        pl.BlockSpec((1, A_GROUPS, V_ROWS, T), lambda b, s: (b, 0, 0, s)),
        col_out(2 * GATE_ROWS),
        pl.BlockSpec((1, CMP_STRIDE, T // CMP_STRIDE, KV_WIDTH), lambda b, s: (b, 0, s, 0)),
        pl.BlockSpec((1, CMP_STRIDE, T // CMP_STRIDE, KV_WIDTH), lambda b, s: (b, 0, s, 0)),
        pl.BlockSpec((1, A_GROUPS, T, KV_WIDTH), lambda b, s: (b, 0, s, 0)),
        pl.BlockSpec((1, A_GROUPS, T, KV_WIDTH), lambda b, s: (b, 0, s, 0)),
        row_out(A_WIDTH), row_out(B_FWIDTH), row_out(B_FWIDTH), row_out(B_WIDTH), row_out(B_WIDTH),
    )
    in_specs = [
        pl.BlockSpec((1, T, D), lambda b, s: (b, s, 0)),
        full((D, n_n)), full((1, n_n)), full((n_t, D)), full((n_t, 1)),
        pl.BlockSpec((HEAD_DIM, T), lambda b, s: (0, s)),
        pl.BlockSpec((HEAD_DIM, T), lambda b, s: (0, s)),
        pl.BlockSpec((T, KV_WIDTH), lambda b, s: (s, 0)),
        pl.BlockSpec((T, KV_WIDTH), lambda b, s: (s, 0)),
        pl.BlockSpec((T, KV_WIDTH), lambda b, s: (s, 0)),
    ]
    return pl.pallas_call(
        _proj_kernel, out_shape=out_shape, grid=(B, S // T),
        in_specs=in_specs, out_specs=out_specs,
        scratch_shapes=[pltpu.VMEM((2, T, KV_WIDTH), F32)],
        compiler_params=pltpu.CompilerParams(
            dimension_semantics=("parallel", "parallel"),
            vmem_limit_bytes=V7X_VMEM_LIMIT_BYTES),
        name="in_proj",
    )(x, wn, bn, wt, bt, cost, sint, cos, sina, sinb)


def _compress_kernel(ck_ref, cv_ref, w1k_ref, w1kf_ref, pek_ref, w2k_ref,
                     w1v_ref, w1vf_ref, pev_ref, w2vt_ref, kc_ref, vct_ref):
    nc = ck_ref.shape[2]
    hid = CMP_HIDDEN

    def hidden(c_ref, w1_ref, w1f_ref, pe_ref):
        a = _dot(c_ref[0, 0], w1_ref[0])
        for t in range(1, CMP_STRIDE):
            a = a + _dot(c_ref[0, t], w1_ref[t])
        pe_term = jnp.sum(w1f_ref[...] * pe_ref[...], axis=0, keepdims=True)
        out = []
        for g in range(A_GROUPS):
            lo = a[:, 2 * g * hid:(2 * g + 1) * hid]
            hi = pltpu.roll(a[:, (2 * g + 1) * hid:(2 * g + 2) * hid], nc - 1, axis=0)
            out.append(_silu(lo + hi + pe_term).astype(BF16))
        return out

    hk = hidden(ck_ref, w1k_ref, w1kf_ref, pek_ref)
    hv = hidden(cv_ref, w1v_ref, w1vf_ref, pev_ref)
    for g in range(A_GROUPS):
        kc_ref[0, g] = _dot(hk[g], w2k_ref[...]).astype(BF16)
        vct_ref[0, g] = _dot_nt(w2vt_ref[...], hv[g]).astype(BF16)


def _compress(ck, cv, w1k, w1kf, pek, w2k, w1v, w1vf, pev, w2vt):
    B, P, NC, W = ck.shape
    full = lambda a: pl.BlockSpec(a.shape, lambda b: (0,) * a.ndim)
    blk = pl.BlockSpec((1, P, NC, W), lambda b: (b, 0, 0, 0))
    return pl.pallas_call(
        _compress_kernel,
        out_shape=(jax.ShapeDtypeStruct((B, A_GROUPS, NC, HEAD_DIM), BF16),
                   jax.ShapeDtypeStruct((B, A_GROUPS, HEAD_DIM, NC), BF16)),
        grid=(B,),
        in_specs=[blk, blk, full(w1k), full(w1kf), full(pek), full(w2k),
                  full(w1v), full(w1vf), full(pev), full(w2vt)],
        out_specs=(pl.BlockSpec((1, A_GROUPS, NC, HEAD_DIM), lambda b: (b, 0, 0, 0)),
                   pl.BlockSpec((1, A_GROUPS, HEAD_DIM, NC), lambda b: (b, 0, 0, 0))),
        compiler_params=pltpu.CompilerParams(
            dimension_semantics=("parallel",),
            vmem_limit_bytes=V7X_VMEM_LIMIT_BYTES),
        name="kv_compress",
    )(ck, cv, w1k, w1kf, pek, w2k, w1v, w1vf, pev, w2vt)


def _nsa_kernel(jc_ref, jq_ref, qr_ref, qn_ref, gt_ref, kc_ref, vct_ref, ks_ref, kw_ref,
                vts_ref, vtw_ref, ovl_ref, o_ref,
                bias_s, wq_s, m_s, acc_s, ocw_s, sbuf_s, dbuf_s, scb_s, swb_s, *, seq, n_sel, n_jobs):
    nc = kc_ref.shape[2]
    n_slc = seq // SLC_BLOCK
    n_q = seq // Q_TILE
    n_ck = seq // KV_CHUNK
    lanes = A_HPG * Q_TILE
    bpc = KV_CHUNK // SLC_BLOCK
    tpc = KV_CHUNK // Q_TILE
    win_tiles = WINDOW // Q_TILE
    last = n_q - 1

    def stack_heads(ref, q0):
        return jnp.concatenate(
            [ref[0, h * HEAD_DIM:(h + 1) * HEAD_DIM, pl.ds(q0, Q_TILE)] for h in range(A_HPG)], axis=1)

    def per_head(a):
        return jnp.concatenate([a] * A_HPG, axis=1)

    def gate_row(gt, branch):
        return jnp.concatenate(
            [gt[branch * A_HPG + h:branch * A_HPG + h + 1, :] for h in range(A_HPG)], axis=1)

    def tile_start(qi):
        return pl.multiple_of(qi * Q_TILE, Q_TILE)

    lane_q = lax.broadcasted_iota(jnp.int32, (1, lanes), 1) & (Q_TILE - 1)
    r_sq = lax.broadcasted_iota(jnp.int32, (Q_TILE, lanes), 0)
    causal_sq = r_sq <= lane_q
    lower_sq = r_sq > lane_q
    q_pad = jnp.zeros((KV_WIDTH - HEAD_DIM, lanes), BF16)

    def cmp_keys(rb):
        return min(nc, -(-(rb * KV_CHUNK // CMP_STRIDE) // 128) * 128)

    def cmp_scores(qi, rb):
        return _dot(kc_ref[0, 0, 0:cmp_keys(rb), :], stack_heads(qn_ref, tile_start(qi)))

    def select_tile(qi, rb, sc):
        nk = cmp_keys(rb)
        s0 = tile_start(qi)
        t_row = s0 + lax.broadcasted_iota(jnp.int32, (1, Q_TILE), 1)
        wq_s[qi] = stack_heads(qr_ref, s0)

        n_end = lax.broadcasted_iota(jnp.int32, (nk, Q_TILE), 0) * CMP_STRIDE + (CMP_BLOCK - 1)
        sc = sc + per_head(jnp.where(n_end <= t_row, 0.0, MASKED))
        mc = jnp.max(sc, axis=0, keepdims=True)
        pc = jnp.exp2(sc - mc)
        lc = jnp.sum(pc, axis=0, keepdims=True)
        pc = pc * jnp.where(s0 + lane_q >= CMP_BLOCK - 1, 1.0 / lc, 0.0)
        o_cmp = _dot(vct_ref[0, 0, :, 0:nk], pc.astype(BF16))

        ps = pc[:, 0:Q_TILE]
        for h in range(1, A_HPG):
            ps = ps + pc[:, h * Q_TILE:(h + 1) * Q_TILE]
        p_hi = ps.astype(BF16)
        r1 = ps - p_hi.astype(F32)
        p_mid = r1.astype(BF16)
        p_lo = (r1 - p_mid.astype(F32)).astype(BF16)
        ovl = ovl_ref[0:8 * rb, 0:nk]
        imp = _dot(ovl, p_hi) + _dot(ovl, p_mid) + _dot(ovl, p_lo)

        j_idx = lax.broadcasted_iota(jnp.int32, (8 * rb, Q_TILE), 0)
        cur = t_row // SLC_BLOCK
        sel = j_idx <= cur
        if 8 * rb > n_sel:
            forced = (j_idx == 0) | (j_idx == cur) | (j_idx == cur - 1)
            val = jnp.where(j_idx > cur, -1.0, jnp.where(forced, FORCE_SCORE, imp))
            rows = [val[8 * j:8 * j + 8] for j in range(rb)]
            cnt = [jnp.zeros((8, Q_TILE), F32) for _ in range(rb)]
            jl = lax.broadcasted_iota(jnp.int32, (8, Q_TILE), 0)
            for kb in range(rb):
                for kl in range(8):
                    row = rows[kb][kl:kl + 1, :]
                    for jb in range(rb):
                        if jb < kb:
                            beats = row > rows[jb]
                        elif jb > kb:
                            beats = row >= rows[jb]
                        else:
                            beats = (row > rows[jb]) | ((row == rows[jb]) & (jl > kl))
                        cnt[jb] = cnt[jb] + jnp.where(beats, 1.0, 0.0)
            sel = sel & (jnp.concatenate(cnt, axis=0) < n_sel)
        bias = per_head(jnp.where(sel, 0.0, MASKED))
        pad = jnp.zeros((bpc, lanes), F32)
        for c in range(rb):
            bias_s[qi, c] = jnp.concatenate([bias[c * bpc:(c + 1) * bpc], pad], axis=0).astype(BF16)

        gt = gt_ref[0, :, pl.ds(s0, Q_TILE)]
        ocw_s[qi] = gate_row(gt, 0) * o_cmp
        m_s[qi] = jnp.full((8, lanes), MASKED, F32)
        acc_s[qi] = jnp.zeros((V_ROWS, lanes), F32)

    for r in range(n_ck):
        rb = r + 1
        nk = cmp_keys(rb)
        scb_s[0, 0:nk, :] = cmp_scores(r * tpc, rb)

        def select_pair(i, _, rb=rb, nk=nk, base=r * tpc):
            q0 = base + 2 * i
            scb_s[1, 0:nk, :] = cmp_scores(q0 + 1, rb)
            select_tile(q0, rb, scb_s[0, 0:nk, :])
            scb_s[0, 0:nk, :] = cmp_scores(jnp.minimum(q0 + 2, base + tpc - 1), rb)
            select_tile(q0 + 1, rb, scb_s[1, 0:nk, :])
            return 0

        lax.fori_loop(0, tpc // 2, select_pair, 0)

    def win_finish(qi, sw, w0):
        mw = jnp.max(sw, axis=0, keepdims=True)
        pw = jnp.exp2(sw - mw)
        ow = _dot(vtw_ref[0, 0, :, pl.ds(w0, WIN_KEYS)], pw.astype(BF16))
        o_win = ow[0:HEAD_DIM] * (1.0 / ow[HEAD_DIM:HEAD_DIM + 1])
        gt = gt_ref[0, :, pl.ds(tile_start(qi), Q_TILE)]
        ocw_s[qi] = ocw_s[qi] + gate_row(gt, 2) * o_win

    def win_head(qi, _):
        wq = jnp.concatenate([wq_s[qi], q_pad], axis=0)
        sw = _dot(kw_ref[0, 0, 0:WIN_KEYS, :], wq)
        t_all = tile_start(qi) + lane_q
        kpos = lax.broadcasted_iota(jnp.int32, (WIN_KEYS, lanes), 0)
        sw = jnp.where((kpos <= t_all) & (kpos > t_all - WINDOW), sw, MASKED)
        win_finish(qi, sw, 0)
        return 0

    lax.fori_loop(0, min(win_tiles, n_q), win_head, 0)

    def win_scores(qi):
        w0 = pl.multiple_of(tile_start(qi) - WINDOW, Q_TILE)
        wq = jnp.concatenate([wq_s[qi], q_pad], axis=0)
        return _dot(kw_ref[0, 0, pl.ds(w0, WIN_KEYS), :], wq)

    def win_tile(qi, sw):
        sw = jnp.concatenate([jnp.where(lower_sq, sw[:Q_TILE], MASKED), sw[Q_TILE:WINDOW],
                              jnp.where(causal_sq, sw[WINDOW:], MASKED)], axis=0)
        win_finish(qi, sw, pl.multiple_of(tile_start(qi) - WINDOW, Q_TILE))

    def win_group(i, _):
        q0 = win_tiles + IN_FLIGHT * i
        for u in range(IN_FLIGHT):
            swb_s[(u + AHEAD) % IN_FLIGHT] = win_scores(jnp.minimum(q0 + u + AHEAD, last))
            win_tile(q0 + u, swb_s[u])
        return 0

    if n_q > win_tiles:
        for u in range(AHEAD):
            swb_s[u] = win_scores(jnp.minimum(win_tiles + u, last))
        lax.fori_loop(0, (n_q - win_tiles) // IN_FLIGHT, win_group, 0)

    wq_pad = jnp.zeros((KV_WIDTH - HEAD_DIM - 2 * bpc, lanes), BF16)

    def scores(c, qi, diag):
        rows = KV_CHUNK if diag is None else (diag + 1) * Q_TILE
        k0 = pl.multiple_of(c * KV_CHUNK, KV_CHUNK)
        wq = jnp.concatenate([wq_s[qi], bias_s[qi, c], wq_pad], axis=0)
        s = _dot(ks_ref[0, 0, pl.ds(k0, rows), :], wq)
        if diag is not None:
            head = [s[:rows - Q_TILE]] if diag > 0 else []
            s = jnp.concatenate(head + [jnp.where(causal_sq, s[rows - Q_TILE:], MASKED)], axis=0)
        return s

    def absorb(c, qi, s):
        rows = s.shape[0]
        k0 = pl.multiple_of(c * KV_CHUNK, KV_CHUNK)
        m_old = m_s[qi][0:1]
        m_new = jnp.maximum(m_old, jnp.max(s, axis=0, keepdims=True))
        alpha = jnp.exp2(m_old - m_new)
        p = jnp.exp2(s - m_new)
        m_s[qi] = jnp.broadcast_to(m_new, (8, lanes))
        acc_s[qi] = alpha * acc_s[qi] + _dot(vts_ref[0, 0, :, pl.ds(k0, rows)], p.astype(BF16))

    def full_scores(j):
        return scores(jc_ref[j], jq_ref[j], None)

    def full_group(i, _):
        j0 = JOBS_PER_TRIP * i
        for u in range(JOBS_PER_TRIP):
            sbuf_s[(u + AHEAD) % IN_FLIGHT] = full_scores(j0 + u + AHEAD)
            absorb(jc_ref[j0 + u], jq_ref[j0 + u], sbuf_s[u % IN_FLIGHT])
        return 0

    if n_jobs:
        for u in range(AHEAD):
            sbuf_s[u] = full_scores(u)
        lax.fori_loop(0, n_jobs // JOBS_PER_TRIP, full_group, 0)

    def diag_scores(c, r):
        dbuf_s[r, 0:(r + 1) * Q_TILE, :] = scores(c, c * tpc + r, r)

    def diag_chunk(c, _):
        nxt = jnp.minimum(c + 1, n_ck - 1)
        for r in range(tpc):
            ahead = r + AHEAD
            diag_scores(c if ahead < tpc else nxt, ahead % tpc)
            absorb(c, c * tpc + r, dbuf_s[r, 0:(r + 1) * Q_TILE, :])
        return 0

    for r in range(AHEAD):
        diag_scores(0, r)
    lax.fori_loop(0, n_ck, diag_chunk, 0)

    def finish(qi, _):
        s0 = tile_start(qi)
        gt = gt_ref[0, :, pl.ds(s0, Q_TILE)]
        acc = acc_s[qi]
        ot = ocw_s[qi] + gate_row(gt, 1) * acc[0:HEAD_DIM] * (1.0 / acc[HEAD_DIM:HEAD_DIM + 1])
        halves = []
        for pr in range(A_HPG // 2):
            two = jnp.concatenate([ot[:, (2 * pr) * Q_TILE:(2 * pr + 1) * Q_TILE],
                                   ot[:, (2 * pr + 1) * Q_TILE:(2 * pr + 2) * Q_TILE]], axis=0)
            halves.append(two.T)
        o_ref[0, pl.ds(s0, Q_TILE), :] = jnp.concatenate(halves, axis=1).astype(o_ref.dtype)
        return 0

    lax.fori_loop(0, n_q, finish, 0)


def _nsa(qr, qn, gt, kc, vct, ks, kw, vts, vtw, ovl):
    B, _, S = qr.shape
    NC = kc.shape[2]
    n_slc = S // SLC_BLOCK
    n_sel = min(SLC_TOP_N, n_slc)
    n_q = S // Q_TILE
    n_ck = S // KV_CHUNK
    tpc = KV_CHUNK // Q_TILE
    gw = A_HPG * HEAD_DIM
    lanes = A_HPG * Q_TILE
    jobs = [(c, q) for c in range(n_ck) for q in range((c + 1) * tpc, n_q)]
    n_jobs = len(jobs)
    assert n_jobs % JOBS_PER_TRIP == 0 and JOBS_PER_TRIP % IN_FLIGHT == 0 and tpc == IN_FLIGHT and AHEAD < IN_FLIGHT
    jobs = jobs + [jobs[-1] if jobs else (0, 0)] * AHEAD
    jc = jnp.asarray([j[0] for j in jobs], jnp.int32)
    jq = jnp.asarray([j[1] for j in jobs], jnp.int32)
    kernel = functools.partial(_nsa_kernel, seq=S, n_sel=n_sel, n_jobs=n_jobs)
    per_group = lambda rows, cols: pl.BlockSpec((1, 1, rows, cols), lambda b, g, *_: (b, g, 0, 0))
    in_specs = [
        pl.BlockSpec((1, gw, S), lambda b, g, *_: (b, g, 0)),
        pl.BlockSpec((1, gw, S), lambda b, g, *_: (b, g, 0)),
        pl.BlockSpec((1, GATE_ROWS, S), lambda b, g, *_: (b, g, 0)),
        per_group(NC, HEAD_DIM), per_group(HEAD_DIM, NC),
        per_group(S, KV_WIDTH), per_group(S, KV_WIDTH),
        per_group(V_ROWS, S), per_group(V_ROWS, S),
        pl.BlockSpec((n_slc, NC), lambda b, g, *_: (0, 0)),
    ]
    scratch = [
        pltpu.VMEM((n_q, n_ck, 2 * (KV_CHUNK // SLC_BLOCK), lanes), BF16),
        pltpu.VMEM((n_q, HEAD_DIM, lanes), BF16),
        pltpu.VMEM((n_q, 8, lanes), F32),
        pltpu.VMEM((n_q, V_ROWS, lanes), F32),
        pltpu.VMEM((n_q, HEAD_DIM, lanes), F32),
        pltpu.VMEM((IN_FLIGHT, KV_CHUNK, lanes), F32),
        pltpu.VMEM((tpc, KV_CHUNK, lanes), F32),
        pltpu.VMEM((2, NC, lanes), F32),
        pltpu.VMEM((IN_FLIGHT, WIN_KEYS, lanes), F32),
    ]
    return pl.pallas_call(
        kernel,
        out_shape=jax.ShapeDtypeStruct((B, S, A_WIDTH), BF16),
        grid_spec=pltpu.PrefetchScalarGridSpec(
            num_scalar_prefetch=2,
            grid=(B, A_GROUPS),
            in_specs=in_specs,
            out_specs=pl.BlockSpec((1, S, gw), lambda b, g, *_: (b, 0, g)),
            scratch_shapes=scratch),
        compiler_params=pltpu.CompilerParams(
            dimension_semantics=("parallel", "parallel"),
            vmem_limit_bytes=V7X_VMEM_LIMIT_BYTES),
        name="nsa_attention",
    )(jc, jq, qr, qn, gt, kc, vct, ks, kw, vts, vtw, ovl)


def _hgrn_kernel(q_ref, f_ref, i_ref, sz_ref, lbl_ref, g_ref, tri_ref, bd_ref, o_ref,
                 qe_s, kv_s, dl_s, oi_s, *, layer):
    S = q_ref.shape[1]
    n_grp = S // HG_ROWS
    cpg = HG_ROWS // CHUNK

    lg = lbl_ref[...]
    e = jnp.exp(lg - jnp.max(lg, axis=0, keepdims=True))
    lb = jnp.sum(e[0:layer + 1], axis=0, keepdims=True) / jnp.sum(e, axis=0, keepdims=True)

    tri = tri_ref[...]
    ri = lax.broadcasted_iota(jnp.int32, (HG_ROWS, HG_ROWS), 0)
    ci = lax.broadcasted_iota(jnp.int32, (HG_ROWS, HG_ROWS), 1)
    causal = (ri // CHUNK == ci // CHUNK) & (ci <= ri)

    def stage_decay(gi):
        rows = pl.ds(pl.multiple_of(gi * HG_ROWS, HG_ROWS), HG_ROWS)
        fg = lb + (1.0 - lb) * _sigmoid(f_ref[0, rows, :])
        logf = jnp.log(fg)
        l_hi = logf.astype(BF16)
        l_lo = (logf - l_hi.astype(F32)).astype(BF16)
        bb = _dot(tri, jnp.concatenate([l_hi, l_lo], axis=1))
        return rows, 1.0 - fg, bb[:, :B_KDIM] + bb[:, B_KDIM:]

    def stage_scores(rows, kk, b):
        b_last = jnp.concatenate(
            [jnp.broadcast_to(b[(c + 1) * CHUNK - 1:(c + 1) * CHUNK, :], (CHUNK, B_KDIM))
             for c in range(cpg)], axis=0)
        qe = (_silu(q_ref[0, rows, :].astype(F32)) * jnp.exp(b)).astype(BF16)
        ke = (kk * jnp.exp(-b)).astype(BF16)
        kd = (kk * jnp.exp(b_last - b)).astype(BF16)
        attn = jnp.where(causal, _dot_nt(qe, ke), 0.0).astype(BF16)
        return qe, kd, attn, b_last

    def stage_outputs(gi, rows, qe, kd, attn, b_last):
        v = i_ref[0, rows, :]
        oi_s[rows, :] = _dot(attn, v)
        qe_s[rows, :] = qe
        kd_bd = jnp.concatenate([kd] * cpg, axis=1) * bd_ref[...]
        kv_all = lax.dot_general(v, kd_bd, (((0,), (0,)), ((), ())), preferred_element_type=F32)
        for c in range(cpg):
            n = gi * cpg + c
            kv_s[n] = kv_all[:, c * B_KDIM:(c + 1) * B_KDIM]
            dl_s[n] = jnp.exp(b_last[c * CHUNK:c * CHUNK + 8, :])

    def intra(i, _):
        groups = [HG_GROUPS * i + u for u in range(HG_GROUPS)]
        decays = [stage_decay(gi) for gi in groups]
        mids = [stage_scores(*d) for d in decays]
        for gi, d, m in zip(groups, decays, mids):
            stage_outputs(gi, d[0], *m)
        return 0

    lax.fori_loop(0, n_grp // HG_GROUPS, intra, 0)

    def scan(n, state):
        upd = kv_s[n]
        kv_s[n] = state
        return dl_s[n][0:1, :] * state + upd

    lax.fori_loop(0, S // CHUNK, scan, jnp.zeros((B_VDIM, B_KDIM), F32))

    gain = g_ref[...]

    def inter(gi, _):
        r0 = pl.multiple_of(gi * HG_ROWS * HG_GROUPS, HG_ROWS * HG_GROUPS)
        for c in range(cpg * HG_GROUPS):
            rows = pl.ds(r0 + c * CHUNK, CHUNK)
            n = gi * cpg * HG_GROUPS + c
            o = oi_s[rows, :] + _dot_nt(qe_s[rows, :], kv_s[n].astype(BF16))
            o = o * lax.rsqrt(jnp.mean(o * o, axis=-1, keepdims=True) + NORM_EPS)
            o_ref[0, rows, :] = (o * gain * sz_ref[0, rows, :].astype(F32)).astype(o_ref.dtype)
        return 0

    lax.fori_loop(0, n_grp // HG_GROUPS, inter, 0)


def _hgrn(qb, fb, ib, szb, lb_logits, gain, tri, bd, layer):
    B, S, _ = qb.shape
    n_ch = S // CHUNK
    tok = lambda: pl.BlockSpec((1, S, B_KDIM), lambda b, h: (b, 0, h))
    return pl.pallas_call(
        functools.partial(_hgrn_kernel, layer=layer),
        out_shape=jax.ShapeDtypeStruct((B, S, B_WIDTH), BF16),
        grid=(B, B_HEADS),
        in_specs=[tok(), tok(), tok(), tok(),
                  pl.BlockSpec((lb_logits.shape[0], B_KDIM), lambda b, h: (0, h)),
                  pl.BlockSpec((1, B_VDIM), lambda b, h: (0, h)),
                  pl.BlockSpec((HG_ROWS, HG_ROWS), lambda b, h: (0, 0)),
                  pl.BlockSpec(bd.shape, lambda b, h: (0, 0))],
        out_specs=tok(),
        scratch_shapes=[pltpu.VMEM((S, B_KDIM), BF16),
                        pltpu.VMEM((n_ch, B_VDIM, B_KDIM), F32),
                        pltpu.VMEM((n_ch, 8, B_KDIM), F32),
                        pltpu.VMEM((S, B_VDIM), F32)],
        compiler_params=pltpu.CompilerParams(
            dimension_semantics=("parallel", "parallel"),
            vmem_limit_bytes=V7X_VMEM_LIMIT_BYTES),
        name="hgrn2",
    )(qb, fb, ib, szb, lb_logits, gain, tri, bd)


def _out_kernel(x_ref, oa_ref, sza_ref, ob_ref, wgm_ref, bgm_ref, wa_ref, wb_ref,
                wo_ref, lng_ref, lnb_ref, o_ref, *, alpha):
    x = x_ref[...]
    sg = _sigmoid(_dot(x.astype(BF16), wgm_ref[...]) + bgm_ref[...])
    a = (oa_ref[...].astype(F32) * sza_ref[...].astype(F32)).astype(BF16)
    y = sg[:, :D_MODEL] * _dot(a, wa_ref[...]) + sg[:, D_MODEL:] * _dot(ob_ref[...], wb_ref[...])
    r = alpha * x + _dot(y.astype(BF16), wo_ref[...])
    mu = jnp.mean(r, axis=-1, keepdims=True)
    d = r - mu
    var = jnp.mean(d * d, axis=-1, keepdims=True)
    o_ref[...] = d * lax.rsqrt(var + NORM_EPS) * lng_ref[...] + lnb_ref[...]


def _merge_out(x2, oa, sza, ob, wgm, bgm, wa, wb, wo, lng, lnb, alpha):
    N, D = x2.shape
    T = OUT_ROWS
    full = lambda a: pl.BlockSpec(a.shape, lambda i: (0,) * a.ndim)
    rows = lambda w: pl.BlockSpec((T, w), lambda i: (i, 0))
    return pl.pallas_call(
        functools.partial(_out_kernel, alpha=alpha),
        out_shape=jax.ShapeDtypeStruct((N, D), x2.dtype),
        grid=(N // T,),
        in_specs=[rows(D), rows(A_WIDTH), rows(A_WIDTH), rows(B_WIDTH),
                  full(wgm), full(bgm), full(wa), full(wb), full(wo), full(lng), full(lnb)],
        out_specs=rows(D),
        compiler_params=pltpu.CompilerParams(
            dimension_semantics=("parallel",),
            vmem_limit_bytes=V7X_VMEM_LIMIT_BYTES),
        name="merge_out",
    )(x2, oa, sza, ob, wgm, bgm, wa, wb, wo, lng, lnb)


def _rope_tables(S):
    inv = ROPE_THETA ** (-jnp.arange(0, HEAD_DIM, 2, dtype=F32) / HEAD_DIM)
    ang = jnp.arange(S, dtype=F32)[:, None] * inv[None, :]
    cos = jnp.concatenate([jnp.cos(ang), jnp.cos(ang)], axis=-1)
    sin = jnp.concatenate([jnp.sin(ang), jnp.sin(ang)], axis=-1)
    first = (jnp.arange(HEAD_DIM) < HEAD_DIM // 2)[None, :]
    sina = jnp.where(first, -sin, 0.0)
    sinb = jnp.where(first, 0.0, sin)
    tile = lambda a: jnp.concatenate([a] * A_GROUPS, axis=-1)
    return cos.T, sin.T, tile(cos), tile(sina), tile(sinb)


def _overlap_t(S):
    n_cmp = S // CMP_STRIDE
    n_slc = S // SLC_BLOCK
    cs = np.arange(n_cmp)[None, :] * CMP_STRIDE
    ss = np.arange(n_slc)[:, None] * SLC_BLOCK
    ov = (cs < ss + SLC_BLOCK) & (cs + CMP_BLOCK > ss) & (np.arange(n_cmp)[None, :] < n_cmp - 1)
    return jnp.asarray(ov, dtype=BF16)


def _block_tri():
    r = np.arange(HG_ROWS)
    return jnp.asarray((r[:, None] // CHUNK == r[None, :] // CHUNK) & (r[None, :] <= r[:, None]), dtype=BF16)


def _block_diag():
    r = np.arange(HG_ROWS)[:, None] // CHUNK
    c = np.arange(HG_ROWS // CHUNK * B_KDIM)[None, :] // B_KDIM
    return jnp.asarray(r == c, dtype=BF16)


def _layer(x, l, w_in, b_in, pe_k, w_k1, w_k2, pe_v, w_v1, w_v2, lb_logits, norm_g,
           w_a, w_b, w_o, ln_g, ln_b):
    B, S, D = x.shape
    alpha = (2 * DEPTH) ** 0.25
    o = _OFF
    wsl = lambda i: w_in[:, o[i]:o[i + 1]]
    bsl = lambda i: b_in[o[i]:o[i + 1]]
    kvw, kvb = wsl(1), bsl(1)
    kv_w = lambda j: kvw[:, j * KV_WIDTH:(j + 1) * KV_WIDTH]
    kv_b = lambda j: kvb[j * KV_WIDTH:(j + 1) * KV_WIDTH]
    gw, gb = wsl(2), bsl(2)
    gidx = np.zeros((A_GROUPS, GATE_ROWS), np.int32)
    gmask = np.zeros((A_GROUPS, GATE_ROWS), np.float32)
    for g in range(A_GROUPS):
        for br in range(3):
            for h in range(A_HPG):
                gidx[g, br * A_HPG + h] = (g * A_HPG + h) * 3 + br
                gmask[g, br * A_HPG + h] = 1.0
    gidx, gmask = gidx.reshape(-1), gmask.reshape(-1)
    gw_t = gw[:, gidx] * gmask[None, :]
    gb_t = gb[gidx] * gmask

    wt = jnp.concatenate([wsl(0), kv_w(3), kv_w(5), gw_t], axis=1).T.astype(BF16)
    bt = jnp.concatenate([bsl(0), kv_b(3), kv_b(5), gb_t])[:, None]
    wn = jnp.concatenate([kv_w(0), kv_w(1), kv_w(2), kv_w(4), wsl(3), wsl(4), wsl(5), wsl(6), wsl(7)],
                         axis=1).astype(BF16)
    bn = jnp.concatenate([kv_b(0), kv_b(1), kv_b(2), kv_b(4), bsl(3), bsl(4), bsl(5), bsl(6), bsl(7)])[None, :]
    cost, sint, cos, sina, sinb = _rope_tables(S)

    (qr, qn, vts, vtw, gt, kcmp, vcmp, ks, kw, sza, qb, fb, ib, szb) = _project(
        x, wn, bn, wt, bt, cost, sint, cos, sina, sinb)

    half = CMP_STRIDE * HEAD_DIM

    def w1_planes(w1):
        both = jnp.concatenate([w1[:half], w1[half:]], axis=1).reshape(CMP_STRIDE, HEAD_DIM, 2 * CMP_HIDDEN)
        z = jnp.zeros_like(both)
        return jnp.concatenate([jnp.concatenate([both, z], axis=2),
                                jnp.concatenate([z, both], axis=2)], axis=1).astype(BF16)

    kc, vct = _compress(kcmp, vcmp,
                        w1_planes(w_k1), w_k1, pe_k.reshape(-1, 1), w_k2.astype(BF16),
                        w1_planes(w_v1), w_v1, pe_v.reshape(-1, 1), w_v2.T.astype(BF16))

    oa = _nsa(qr, qn, gt, kc, vct, ks, kw, vts, vtw, _overlap_t(S))
    ob = _hgrn(qb, fb, ib, szb, lb_logits, norm_g[None, :], _block_tri(), _block_diag(), l)

    wgm = jnp.concatenate([wsl(8), wsl(9)], axis=1).astype(BF16)
    bgm = jnp.concatenate([bsl(8), bsl(9)])[None, :]
    out = _merge_out(x.reshape(B * S, D), oa.reshape(B * S, A_WIDTH), sza.reshape(B * S, A_WIDTH),
                     ob.reshape(B * S, B_WIDTH), wgm, bgm, w_a.astype(BF16), w_b.astype(BF16),
                     w_o.astype(BF16), ln_g[None, :], ln_b[None, :], alpha)
    return out.reshape(B, S, D)


@jax.jit
def kernel(x, w_in, b_in, pe_cmp_k, w_cmp_k1, w_cmp_k2, pe_cmp_v, w_cmp_v1, w_cmp_v2,
           hgrn_lb_logits, hgrn_norm_g, w_branch_a, w_branch_b, w_out, ln_g, ln_b):
    B, S, D = x.shape
    assert D == D_MODEL and S % KV_CHUNK == 0 and S % PROJ_ROWS == 0 and S >= WIN_KEYS
    assert (B * S) % OUT_ROWS == 0 and S % (HG_ROWS * HG_GROUPS) == 0 and (S // KV_CHUNK) * KV_CHUNK == S
    for l in range(DEPTH):
        x = _layer(x, l, w_in[l], b_in[l], pe_cmp_k[l], w_cmp_k1[l], w_cmp_k2[l],
                   pe_cmp_v[l], w_cmp_v1[l], w_cmp_v2[l], hgrn_lb_logits, hgrn_norm_g[l],
                   w_branch_a[l], w_branch_b[l], w_out[l], ln_g[l], ln_b[l])
    return x
```

```python
import functools
import math

import numpy as np
import jax
import jax.numpy as jnp
from jax import lax
from jax.experimental import pallas as pl
from jax.experimental.pallas import tpu as pltpu

D_MODEL = 1024
DEPTH = 1
A_HEADS = 8
A_GROUPS = 2
A_HPG = A_HEADS // A_GROUPS
HEAD_DIM = 64
A_WIDTH = A_HEADS * HEAD_DIM
KV_WIDTH = A_GROUPS * HEAD_DIM
CMP_BLOCK = 32
CMP_STRIDE = 16
CMP_HIDDEN = 128
SLC_BLOCK = 64
SLC_TOP_N = 16
WINDOW = 512
ROPE_THETA = 10000.0
FORCE_SCORE = 1e30
B_HEADS = 4
B_KDIM = 128
B_VDIM = 128
B_FWIDTH = B_HEADS * B_KDIM
B_WIDTH = B_HEADS * B_VDIM
CHUNK = 64
NORM_EPS = 1e-5

IN_SPLITS = (A_WIDTH, 6 * KV_WIDTH, A_HEADS * 3, A_WIDTH, B_FWIDTH, B_FWIDTH,
             B_WIDTH, B_WIDTH, D_MODEL, D_MODEL)
_OFF = tuple(int(v) for v in np.cumsum((0,) + IN_SPLITS))

V7X_VMEM_LIMIT_BYTES = 56 * 1024 * 1024
PROJ_ROWS = 512
Q_TILE = 128
KV_CHUNK = 512
IN_FLIGHT = 4
AHEAD = 2
JOBS_PER_TRIP = 8
WIN_KEYS = WINDOW + Q_TILE
HG_ROWS = 256
HG_GROUPS = 4
OUT_ROWS = 512
MASKED = -1e30
LOG2E = math.log2(math.e)
GATE_ROWS = 16
V_ROWS = HEAD_DIM + 16

F32 = jnp.float32
BF16 = jnp.bfloat16


def _dot(a, b):
    return jnp.dot(a, b, preferred_element_type=F32)


def _dot_nt(a, b):
    return lax.dot_general(a, b, (((1,), (1,)), ((), ())), preferred_element_type=F32)


def _sigmoid(x):
    return 1.0 / (1.0 + jnp.exp(-x))


def _silu(x):
    return x * _sigmoid(x)


def _proj_kernel(x_ref, wn_ref, bn_ref, wt_ref, bt_ref, cost_ref, sint_ref,
                 cos_ref, sina_ref, sinb_ref,
                 qr_ref, qn_ref, vts_ref, vtw_ref, gt_ref, szat_ref, kc_ref, vc_ref,
                 ks_ref, kw_ref, qb_ref, fb_ref, ib_ref, szb_ref, cmp_s):
    xb = x_ref[0].astype(BF16)
    scale = HEAD_DIM ** -0.5 * LOG2E

    ht = _dot_nt(wt_ref[...], xb) + bt_ref[...]
    cost = cost_ref[...]
    sint = sint_ref[...]
    half = HEAD_DIM // 2
    for h in range(A_HEADS):
        blk = ht[h * HEAD_DIM:(h + 1) * HEAD_DIM]
        rot = jnp.concatenate([-blk[half:], blk[:half]], axis=0)
        qr_ref[0, h * HEAD_DIM:(h + 1) * HEAD_DIM, :] = ((blk * cost + rot * sint) * scale).astype(BF16)
        qn_ref[0, h * HEAD_DIM:(h + 1) * HEAD_DIM, :] = (blk * scale).astype(BF16)
    o = A_WIDTH
    ones_rows = jnp.where(lax.broadcasted_iota(jnp.int32, (V_ROWS - HEAD_DIM, ht.shape[1]), 0) == 0, 1.0, 0.0)
    for g in range(A_GROUPS):
        for ref, base in ((vts_ref, o), (vtw_ref, o + KV_WIDTH)):
            rows_g = ht[base + g * HEAD_DIM:base + (g + 1) * HEAD_DIM]
            ref[0, g] = jnp.concatenate([rows_g, ones_rows], axis=0).astype(BF16)
    gt_ref[0] = _sigmoid(ht[o + 2 * KV_WIDTH:o + 2 * KV_WIDTH + 2 * GATE_ROWS])
    o += 2 * KV_WIDTH + 2 * GATE_ROWS
    szat_ref[0] = _silu(ht[o:o + A_WIDTH]).astype(BF16)

    def cols(lo, hi):
        return _dot(xb, wn_ref[:, lo:hi]) + bn_ref[:, lo:hi]

    kv = cols(0, 4 * KV_WIDTH)
    cmp_s[0] = kv[:, 0:KV_WIDTH]
    cmp_s[1] = kv[:, KV_WIDTH:2 * KV_WIDTH]
    pieces = cmp_s.shape[1] // CMP_STRIDE
    for t in range(CMP_STRIDE):
        kc_ref[0, t] = cmp_s[0, pl.ds(t, pieces, stride=CMP_STRIDE), :].astype(BF16)
        vc_ref[0, t] = cmp_s[1, pl.ds(t, pieces, stride=CMP_STRIDE), :].astype(BF16)
    cos = cos_ref[...]
    sina = sina_ref[...]
    sinb = sinb_ref[...]

    def rope_rows(k):
        return (k * cos + pltpu.roll(k, 128 - half, axis=1) * sina + pltpu.roll(k, half, axis=1) * sinb)

    ks = rope_rows(kv[:, 2 * KV_WIDTH:3 * KV_WIDTH])
    rows = ks.shape[0]
    lane = lax.broadcasted_iota(jnp.int32, (rows, KV_WIDTH), 1)
    pos = pl.program_id(1) * rows + lax.broadcasted_iota(jnp.int32, (rows, KV_WIDTH), 0)
    blocks_per_chunk = KV_CHUNK // SLC_BLOCK
    ind = jnp.where(lane == HEAD_DIM + (pos // SLC_BLOCK) % blocks_per_chunk, 1.0, 0.0)
    ks_ref[0, 0] = (jnp.where(lane < HEAD_DIM, ks, 0.0) + ind).astype(BF16)
    ks_ref[0, 1] = (jnp.where(lane < HEAD_DIM, pltpu.roll(ks, HEAD_DIM, axis=1), 0.0) + ind).astype(BF16)
    kw = rope_rows(kv[:, 3 * KV_WIDTH:4 * KV_WIDTH])
    kw_ref[0, 0] = jnp.where(lane < HEAD_DIM, kw, 0.0).astype(BF16)
    kw_ref[0, 1] = jnp.where(lane < HEAD_DIM, pltpu.roll(kw, HEAD_DIM, axis=1), 0.0).astype(BF16)
    o = 4 * KV_WIDTH
    qb_ref[0] = cols(o, o + B_FWIDTH).astype(BF16)
    o += B_FWIDTH
    fb_ref[0] = cols(o, o + B_FWIDTH)
    o += B_FWIDTH
    ib_ref[0] = cols(o, o + B_WIDTH).astype(BF16)
    o += B_WIDTH
    szb_ref[0] = _silu(cols(o, o + B_WIDTH)).astype(BF16)


def _project(x, wn, bn, wt, bt, cost, sint, cos, sina, sinb):
    B, S, D = x.shape
    T = PROJ_ROWS
    n_t = wt.shape[0]
    n_n = wn.shape[1]
    full = lambda shape: pl.BlockSpec(shape, lambda b, s: (0,) * len(shape))
    row_out = lambda w: pl.BlockSpec((1, T, w), lambda b, s: (b, s, 0))
    col_out = lambda r: pl.BlockSpec((1, r, T), lambda b, s: (b, 0, s))
    sds = jax.ShapeDtypeStruct
    out_shape = (
        sds((B, A_WIDTH, S), BF16), sds((B, A_WIDTH, S), BF16),
        sds((B, A_GROUPS, V_ROWS, S), BF16), sds((B, A_GROUPS, V_ROWS, S), BF16),
        sds((B, 2 * GATE_ROWS, S), F32),
        sds((B, A_WIDTH, S), BF16),
        sds((B, CMP_STRIDE, S // CMP_STRIDE, KV_WIDTH), BF16),
        sds((B, CMP_STRIDE, S // CMP_STRIDE, KV_WIDTH), BF16),
        sds((B, A_GROUPS, S, KV_WIDTH), BF16), sds((B, A_GROUPS, S, KV_WIDTH), BF16),
        sds((B, S, B_FWIDTH), BF16), sds((B, S, B_FWIDTH), F32),
        sds((B, S, B_WIDTH), BF16), sds((B, S, B_WIDTH), BF16),
    )
    out_specs = (
        col_out(A_WIDTH), col_out(A_WIDTH),
        pl.BlockSpec((1, A_GROUPS, V_ROWS, T), lambda b, s: (b, 0, 0, s)),
        pl.BlockSpec((1, A_GROUPS, V_ROWS, T), lambda b, s: (b, 0, 0, s)),
        col_out(2 * GATE_ROWS), col_out(A_WIDTH),
        pl.BlockSpec((1, CMP_STRIDE, T // CMP_STRIDE, KV_WIDTH), lambda b, s: (b, 0, s, 0)),
        pl.BlockSpec((1, CMP_STRIDE, T // CMP_STRIDE, KV_WIDTH), lambda b, s: (b, 0, s, 0)),
        pl.BlockSpec((1, A_GROUPS, T, KV_WIDTH), lambda b, s: (b, 0, s, 0)),
        pl.BlockSpec((1, A_GROUPS, T, KV_WIDTH), lambda b, s: (b, 0, s, 0)),
        row_out(B_FWIDTH), row_out(B_FWIDTH), row_out(B_WIDTH), row_out(B_WIDTH),
    )
    in_specs = [
        pl.BlockSpec((1, T, D), lambda b, s: (b, s, 0)),
        full((D, n_n)), full((1, n_n)), full((n_t, D)), full((n_t, 1)),
        pl.BlockSpec((HEAD_DIM, T), lambda b, s: (0, s)),
        pl.BlockSpec((HEAD_DIM, T), lambda b, s: (0, s)),
        pl.BlockSpec((T, KV_WIDTH), lambda b, s: (s, 0)),
        pl.BlockSpec((T, KV_WIDTH), lambda b, s: (s, 0)),
        pl.BlockSpec((T, KV_WIDTH), lambda b, s: (s, 0)),
    ]
    return pl.pallas_call(
        _proj_kernel, out_shape=out_shape, grid=(B, S // T),
        in_specs=in_specs, out_specs=out_specs,
        scratch_shapes=[pltpu.VMEM((2, T, KV_WIDTH), F32)],
        compiler_params=pltpu.CompilerParams(
            dimension_semantics=("parallel", "parallel"),
            vmem_limit_bytes=V7X_VMEM_LIMIT_BYTES),
        name="in_proj",
    )(x, wn, bn, wt, bt, cost, sint, cos, sina, sinb)


def _compress_kernel(ck_ref, cv_ref, w1k_ref, w1kf_ref, pek_ref, w2k_ref,
                     w1v_ref, w1vf_ref, pev_ref, w2vt_ref, kc_ref, vct_ref):
    nc = ck_ref.shape[2]
    hid = CMP_HIDDEN

    def hidden(c_ref, w1_ref, w1f_ref, pe_ref):
        a = _dot(c_ref[0, 0], w1_ref[0])
        for t in range(1, CMP_STRIDE):
            a = a + _dot(c_ref[0, t], w1_ref[t])
        pe_term = jnp.sum(w1f_ref[...] * pe_ref[...], axis=0, keepdims=True)
        out = []
        for g in range(A_GROUPS):
            lo = a[:, 2 * g * hid:(2 * g + 1) * hid]
            hi = pltpu.roll(a[:, (2 * g + 1) * hid:(2 * g + 2) * hid], nc - 1, axis=0)
            out.append(_silu(lo + hi + pe_term).astype(BF16))
        return out

    hk = hidden(ck_ref, w1k_ref, w1kf_ref, pek_ref)
    hv = hidden(cv_ref, w1v_ref, w1vf_ref, pev_ref)
    for g in range(A_GROUPS):
        kc_ref[0, g] = _dot(hk[g], w2k_ref[...]).astype(BF16)
        vct_ref[0, g] = _dot_nt(w2vt_ref[...], hv[g]).astype(BF16)


def _compress(ck, cv, w1k, w1kf, pek, w2k, w1v, w1vf, pev, w2vt):
    B, P, NC, W = ck.shape
    full = lambda a: pl.BlockSpec(a.shape, lambda b: (0,) * a.ndim)
    blk = pl.BlockSpec((1, P, NC, W), lambda b: (b, 0, 0, 0))
    return pl.pallas_call(
        _compress_kernel,
        out_shape=(jax.ShapeDtypeStruct((B, A_GROUPS, NC, HEAD_DIM), BF16),
                   jax.ShapeDtypeStruct((B, A_GROUPS, HEAD_DIM, NC), BF16)),
        grid=(B,),
        in_specs=[blk, blk, full(w1k), full(w1kf), full(pek), full(w2k),
                  full(w1v), full(w1vf), full(pev), full(w2vt)],
        out_specs=(pl.BlockSpec((1, A_GROUPS, NC, HEAD_DIM), lambda b: (b, 0, 0, 0)),
                   pl.BlockSpec((1, A_GROUPS, HEAD_DIM, NC), lambda b: (b, 0, 0, 0))),
        compiler_params=pltpu.CompilerParams(
            dimension_semantics=("parallel",),
            vmem_limit_bytes=V7X_VMEM_LIMIT_BYTES),
        name="kv_compress",
    )(ck, cv, w1k, w1kf, pek, w2k, w1v, w1vf, pev, w2vt)


def _nsa_kernel(jc_ref, jq_ref, qr_ref, qn_ref, gt_ref, szat_ref, kc_ref, vct_ref, ks_ref, kw_ref,
                vts_ref, vtw_ref, ovl_ref, o_ref,
                bias_s, wq_s, m_s, acc_s, ocw_s, sbuf_s, smax_s, dbuf_s, dmax_s, scb_s, swb_s, wmax_s,
                *, seq, n_sel, n_jobs):
    nc = kc_ref.shape[2]
    n_slc = seq // SLC_BLOCK
    n_q = seq // Q_TILE
    n_ck = seq // KV_CHUNK
    lanes = A_HPG * Q_TILE
    bpc = KV_CHUNK // SLC_BLOCK
    tpc = KV_CHUNK // Q_TILE
    win_tiles = WINDOW // Q_TILE
    last = n_q - 1

    def stack_heads(ref, q0):
        return jnp.concatenate(
            [ref[0, h * HEAD_DIM:(h + 1) * HEAD_DIM, pl.ds(q0, Q_TILE)] for h in range(A_HPG)], axis=1)

    def per_head(a):
        return jnp.concatenate([a] * A_HPG, axis=1)

    def gate_row(gt, branch):
        return jnp.concatenate(
            [gt[branch * A_HPG + h:branch * A_HPG + h + 1, :] for h in range(A_HPG)], axis=1)

    def tile_start(qi):
        return pl.multiple_of(qi * Q_TILE, Q_TILE)

    lane_q = lax.broadcasted_iota(jnp.int32, (1, lanes), 1) & (Q_TILE - 1)
    r_sq = lax.broadcasted_iota(jnp.int32, (Q_TILE, lanes), 0)
    causal_sq = r_sq <= lane_q
    lower_sq = r_sq > lane_q
    q_pad = jnp.zeros((KV_WIDTH - HEAD_DIM, lanes), BF16)

    def cmp_keys(rb):
        return min(nc, -(-(rb * KV_CHUNK // CMP_STRIDE) // 128) * 128)

    def cmp_scores(qi, rb):
        return _dot(kc_ref[0, 0, 0:cmp_keys(rb), :], stack_heads(qn_ref, tile_start(qi)))

    def select_tile(qi, rb, sc):
        nk = cmp_keys(rb)
        s0 = tile_start(qi)
        t_row = s0 + lax.broadcasted_iota(jnp.int32, (1, Q_TILE), 1)
        wq_s[qi] = stack_heads(qr_ref, s0)

        n_end = lax.broadcasted_iota(jnp.int32, (nk, Q_TILE), 0) * CMP_STRIDE + (CMP_BLOCK - 1)
        sc = sc + per_head(jnp.where(n_end <= t_row, 0.0, MASKED))
        mc = jnp.max(sc, axis=0, keepdims=True)
        pc = jnp.exp2(sc - mc)
        lc = jnp.sum(pc, axis=0, keepdims=True)
        pc = pc * jnp.where(s0 + lane_q >= CMP_BLOCK - 1, 1.0 / lc, 0.0)
        o_cmp = _dot(vct_ref[0, 0, :, 0:nk], pc.astype(BF16))

        ps = pc[:, 0:Q_TILE]
        for h in range(1, A_HPG):
            ps = ps + pc[:, h * Q_TILE:(h + 1) * Q_TILE]
        p_hi = ps.astype(BF16)
        r1 = ps - p_hi.astype(F32)
        p_mid = r1.astype(BF16)
        p_lo = (r1 - p_mid.astype(F32)).astype(BF16)
        ovl = ovl_ref[0:8 * rb, 0:nk]
        imp = _dot(ovl, p_hi) + _dot(ovl, p_mid) + _dot(ovl, p_lo)

        j_idx = lax.broadcasted_iota(jnp.int32, (8 * rb, Q_TILE), 0)
        cur = t_row // SLC_BLOCK
        sel = j_idx <= cur
        if 8 * rb > n_sel:
            forced = (j_idx == 0) | (j_idx == cur) | (j_idx == cur - 1)
            val = jnp.where(j_idx > cur, -1.0, jnp.where(forced, FORCE_SCORE, imp))
            rows = [val[8 * j:8 * j + 8] for j in range(rb)]
            cnt = [jnp.zeros((8, Q_TILE), F32) for _ in range(rb)]
            jl = lax.broadcasted_iota(jnp.int32, (8, Q_TILE), 0)
            for kb in range(rb):
                for kl in range(8):
                    row = rows[kb][kl:kl + 1, :]
                    for jb in range(rb):
                        if jb < kb:
                            beats = row > rows[jb]
                        elif jb > kb:
                            beats = row >= rows[jb]
                        else:
                            beats = (row > rows[jb]) | ((row == rows[jb]) & (jl > kl))
                        cnt[jb] = cnt[jb] + jnp.where(beats, 1.0, 0.0)
            sel = sel & (jnp.concatenate(cnt, axis=0) < n_sel)
        bias = per_head(jnp.where(sel, 0.0, MASKED))
        pad = jnp.zeros((bpc, lanes), F32)
        for c in range(rb):
            bias_s[qi, c] = jnp.concatenate([bias[c * bpc:(c + 1) * bpc], pad], axis=0).astype(BF16)

        gt = gt_ref[0, :, pl.ds(s0, Q_TILE)]
        ocw_s[qi] = gate_row(gt, 0) * o_cmp
        m_s[qi] = jnp.full((8, lanes), MASKED, F32)
        acc_s[qi] = jnp.zeros((V_ROWS, lanes), F32)

    scb_s[0, 0:cmp_keys(1), :] = cmp_scores(0, 1)
    for qi in range(n_q):
        rb = qi // tpc + 1
        nk = cmp_keys(rb)
        if qi + 1 < n_q:
            rb_next = (qi + 1) // tpc + 1
            scb_s[(qi + 1) & 1, 0:cmp_keys(rb_next), :] = cmp_scores(qi + 1, rb_next)
        select_tile(qi, rb, scb_s[qi & 1, 0:nk, :])

    def win_finish(qi, sw, mw, w0):
        pw = jnp.exp2(sw - mw)
        ow = _dot(vtw_ref[0, 0, :, pl.ds(w0, WIN_KEYS)], pw.astype(BF16))
        o_win = ow[0:HEAD_DIM] * (1.0 / ow[HEAD_DIM:HEAD_DIM + 1])
        gt = gt_ref[0, :, pl.ds(tile_start(qi), Q_TILE)]
        ocw_s[qi] = ocw_s[qi] + gate_row(gt, 2) * o_win

    def win_head(qi, _):
        wq = jnp.concatenate([wq_s[qi], q_pad], axis=0)
        sw = _dot(kw_ref[0, 0, 0:WIN_KEYS, :], wq)
        t_all = tile_start(qi) + lane_q
        kpos = lax.broadcasted_iota(jnp.int32, (WIN_KEYS, lanes), 0)
        sw = jnp.where((kpos <= t_all) & (kpos > t_all - WINDOW), sw, MASKED)
        win_finish(qi, sw, jnp.max(sw, axis=0, keepdims=True), 0)
        return 0

    lax.fori_loop(0, min(win_tiles, n_q), win_head, 0)

    def win_scores(qi, slot):
        w0 = pl.multiple_of(tile_start(qi) - WINDOW, Q_TILE)
        wq = jnp.concatenate([wq_s[qi], q_pad], axis=0)
        sw = _dot(kw_ref[0, 0, pl.ds(w0, WIN_KEYS), :], wq)
        sw = jnp.concatenate([jnp.where(lower_sq, sw[:Q_TILE], MASKED), sw[Q_TILE:WINDOW],
                              jnp.where(causal_sq, sw[WINDOW:], MASKED)], axis=0)
        swb_s[slot] = sw
        wmax_s[slot] = jnp.broadcast_to(jnp.max(sw, axis=0, keepdims=True), (8, lanes))

    def win_group(i, _):
        q0 = win_tiles + IN_FLIGHT * i
        for u in range(IN_FLIGHT):
            win_scores(jnp.minimum(q0 + u + AHEAD, last), (u + AHEAD) % IN_FLIGHT)
            win_finish(q0 + u, swb_s[u], wmax_s[u][0:1],
                       pl.multiple_of(tile_start(q0 + u) - WINDOW, Q_TILE))
        return 0

    if n_q > win_tiles:
        for u in range(AHEAD):
            win_scores(jnp.minimum(win_tiles + u, last), u)
        lax.fori_loop(0, (n_q - win_tiles) // IN_FLIGHT, win_group, 0)

    wq_pad = jnp.zeros((KV_WIDTH - HEAD_DIM - 2 * bpc, lanes), BF16)

    def scores(c, qi, diag):
        rows = KV_CHUNK if diag is None else (diag + 1) * Q_TILE
        k0 = pl.multiple_of(c * KV_CHUNK, KV_CHUNK)
        wq = jnp.concatenate([wq_s[qi], bias_s[qi, c], wq_pad], axis=0)
        s = _dot(ks_ref[0, 0, pl.ds(k0, rows), :], wq)
        if diag is not None:
            head = [s[:rows - Q_TILE]] if diag > 0 else []
            s = jnp.concatenate(head + [jnp.where(causal_sq, s[rows - Q_TILE:], MASKED)], axis=0)
        return s, jnp.broadcast_to(jnp.max(s, axis=0, keepdims=True), (8, lanes))

    def absorb(c, qi, s, s_max):
        rows = s.shape[0]
        k0 = pl.multiple_of(c * KV_CHUNK, KV_CHUNK)
        m_old = m_s[qi][0:1]
        m_new = jnp.maximum(m_old, s_max[0:1])
        alpha = jnp.exp2(m_old - m_new)
        p = jnp.exp2(s - m_new)
        m_s[qi] = jnp.broadcast_to(m_new, (8, lanes))
        acc_s[qi] = alpha * acc_s[qi] + _dot(vts_ref[0, 0, :, pl.ds(k0, rows)], p.astype(BF16))

    def full_scores(j, slot):
        sbuf_s[slot], smax_s[slot] = scores(jc_ref[j], jq_ref[j], None)

    def full_group(i, _):
        j0 = JOBS_PER_TRIP * i
        for u in range(JOBS_PER_TRIP):
            full_scores(j0 + u + AHEAD, (u + AHEAD) % IN_FLIGHT)
            absorb(jc_ref[j0 + u], jq_ref[j0 + u], sbuf_s[u % IN_FLIGHT], smax_s[u % IN_FLIGHT])
        return 0

    if n_jobs:
        for u in range(AHEAD):
            full_scores(u, u)
        lax.fori_loop(0, n_jobs // JOBS_PER_TRIP, full_group, 0)

    def diag_scores(c, r):
        dbuf_s[r, 0:(r + 1) * Q_TILE, :], dmax_s[r] = scores(c, c * tpc + r, r)

    def diag_chunk(c, _):
        nxt = jnp.minimum(c + 1, n_ck - 1)
        for r in range(tpc):
            ahead = r + AHEAD
            diag_scores(c if ahead < tpc else nxt, ahead % tpc)
            absorb(c, c * tpc + r, dbuf_s[r, 0:(r + 1) * Q_TILE, :], dmax_s[r])
        return 0

    for r in range(AHEAD):
        diag_scores(0, r)
    lax.fori_loop(0, n_ck, diag_chunk, 0)

    def finish(qi, _):
        s0 = tile_start(qi)
        gt = gt_ref[0, :, pl.ds(s0, Q_TILE)]
        acc = acc_s[qi]
        ot = ocw_s[qi] + gate_row(gt, 1) * acc[0:HEAD_DIM] * (1.0 / acc[HEAD_DIM:HEAD_DIM + 1])
        ot = (ot * stack_heads(szat_ref, s0).astype(F32)).astype(o_ref.dtype)
        for h in range(A_HPG):
            o_ref[0, h * HEAD_DIM:(h + 1) * HEAD_DIM, pl.ds(s0, Q_TILE)] = ot[:, h * Q_TILE:(h + 1) * Q_TILE]
        return 0

    lax.fori_loop(0, n_q, finish, 0)


def _nsa(qr, qn, gt, szat, kc, vct, ks, kw, vts, vtw, ovl):
    B, _, S = qr.shape
    NC = kc.shape[2]
    n_slc = S // SLC_BLOCK
    n_sel = min(SLC_TOP_N, n_slc)
    n_q = S // Q_TILE
    n_ck = S // KV_CHUNK
    tpc = KV_CHUNK // Q_TILE
    gw = A_HPG * HEAD_DIM
    lanes = A_HPG * Q_TILE
    jobs = [(c, q) for c in range(n_ck) for q in range((c + 1) * tpc, n_q)]
    n_jobs = len(jobs)
    assert n_jobs % JOBS_PER_TRIP == 0 and JOBS_PER_TRIP % IN_FLIGHT == 0 and tpc == IN_FLIGHT and AHEAD < IN_FLIGHT
    jobs = jobs + [jobs[-1] if jobs else (0, 0)] * AHEAD
    jc = jnp.asarray([j[0] for j in jobs], jnp.int32)
    jq = jnp.asarray([j[1] for j in jobs], jnp.int32)
    kernel = functools.partial(_nsa_kernel, seq=S, n_sel=n_sel, n_jobs=n_jobs)
    per_group = lambda rows, cols: pl.BlockSpec((1, 1, rows, cols), lambda b, g, *_: (b, g, 0, 0))
    in_specs = [
        pl.BlockSpec((1, gw, S), lambda b, g, *_: (b, g, 0)),
        pl.BlockSpec((1, gw, S), lambda b, g, *_: (b, g, 0)),
        pl.BlockSpec((1, GATE_ROWS, S), lambda b, g, *_: (b, g, 0)),
        pl.BlockSpec((1, gw, S), lambda b, g, *_: (b, g, 0)),
        per_group(NC, HEAD_DIM), per_group(HEAD_DIM, NC),
        per_group(S, KV_WIDTH), per_group(S, KV_WIDTH),
        per_group(V_ROWS, S), per_group(V_ROWS, S),
        pl.BlockSpec((n_slc, NC), lambda b, g, *_: (0, 0)),
    ]
    scratch = [
        pltpu.VMEM((n_q, n_ck, 2 * (KV_CHUNK // SLC_BLOCK), lanes), BF16),
        pltpu.VMEM((n_q, HEAD_DIM, lanes), BF16),
        pltpu.VMEM((n_q, 8, lanes), F32),
        pltpu.VMEM((n_q, V_ROWS, lanes), F32),
        pltpu.VMEM((n_q, HEAD_DIM, lanes), F32),
        pltpu.VMEM((IN_FLIGHT, KV_CHUNK, lanes), F32),
        pltpu.VMEM((IN_FLIGHT, 8, lanes), F32),
        pltpu.VMEM((tpc, KV_CHUNK, lanes), F32),
        pltpu.VMEM((tpc, 8, lanes), F32),
        pltpu.VMEM((2, NC, lanes), F32),
        pltpu.VMEM((IN_FLIGHT, WIN_KEYS, lanes), F32),
        pltpu.VMEM((IN_FLIGHT, 8, lanes), F32),
    ]
    return pl.pallas_call(
        kernel,
        out_shape=jax.ShapeDtypeStruct((B, A_WIDTH, S), BF16),
        grid_spec=pltpu.PrefetchScalarGridSpec(
            num_scalar_prefetch=2,
            grid=(B, A_GROUPS),
            in_specs=in_specs,
            out_specs=pl.BlockSpec((1, gw, S), lambda b, g, *_: (b, g, 0)),
            scratch_shapes=scratch),
        compiler_params=pltpu.CompilerParams(
            dimension_semantics=("parallel", "parallel"),
            vmem_limit_bytes=V7X_VMEM_LIMIT_BYTES),
        name="nsa_attention",
    )(jc, jq, qr, qn, gt, szat, kc, vct, ks, kw, vts, vtw, ovl)


def _hgrn_kernel(q_ref, f_ref, i_ref, sz_ref, lbl_ref, g_ref, tri_ref, bd_ref, o_ref,
                 qe_s, kv_s, dl_s, oi_s, *, layer):
    S = q_ref.shape[1]
    n_grp = S // HG_ROWS
    cpg = HG_ROWS // CHUNK

    lg = lbl_ref[...]
    e = jnp.exp(lg - jnp.max(lg, axis=0, keepdims=True))
    lb = jnp.sum(e[0:layer + 1], axis=0, keepdims=True) / jnp.sum(e, axis=0, keepdims=True)

    tri = tri_ref[...]
    ri = lax.broadcasted_iota(jnp.int32, (HG_ROWS, HG_ROWS), 0)
    ci = lax.broadcasted_iota(jnp.int32, (HG_ROWS, HG_ROWS), 1)
    causal = (ri // CHUNK == ci // CHUNK) & (ci <= ri)

    def stage_decay(gi):
        rows = pl.ds(pl.multiple_of(gi * HG_ROWS, HG_ROWS), HG_ROWS)
        fg = lb + (1.0 - lb) * _sigmoid(f_ref[0, rows, :])
        logf = jnp.log(fg)
        l_hi = logf.astype(BF16)
        l_lo = (logf - l_hi.astype(F32)).astype(BF16)
        bb = _dot(tri, jnp.concatenate([l_hi, l_lo], axis=1))
        return rows, 1.0 - fg, bb[:, :B_KDIM] + bb[:, B_KDIM:]

    def stage_scores(rows, kk, b):
        b_last = jnp.concatenate(
            [jnp.broadcast_to(b[(c + 1) * CHUNK - 1:(c + 1) * CHUNK, :], (CHUNK, B_KDIM))
             for c in range(cpg)], axis=0)
        qe = (_silu(q_ref[0, rows, :].astype(F32)) * jnp.exp(b)).astype(BF16)
        ke = (kk * jnp.exp(-b)).astype(BF16)
        kd = (kk * jnp.exp(b_last - b)).astype(BF16)
        attn = jnp.where(causal, _dot_nt(qe, ke), 0.0).astype(BF16)
        return qe, kd, attn, b_last

    def stage_outputs(gi, rows, qe, kd, attn, b_last):
        v = i_ref[0, rows, :]
        oi_s[rows, :] = _dot(attn, v)
        qe_s[rows, :] = qe
        kd_bd = jnp.concatenate([kd] * cpg, axis=1) * bd_ref[...]
        kv_all = lax.dot_general(v, kd_bd, (((0,), (0,)), ((), ())), preferred_element_type=F32)
        for c in range(cpg):
            n = gi * cpg + c
            kv_s[n] = kv_all[:, c * B_KDIM:(c + 1) * B_KDIM]
            dl_s[n] = jnp.exp(b_last[c * CHUNK:c * CHUNK + 8, :])

    def intra(i, _):
        groups = [HG_GROUPS * i + u for u in range(HG_GROUPS)]
        decays = [stage_decay(gi) for gi in groups]
        mids = [stage_scores(*d) for d in decays]
        for gi, d, m in zip(groups, decays, mids):
            stage_outputs(gi, d[0], *m)
        return 0

    lax.fori_loop(0, n_grp // HG_GROUPS, intra, 0)

    def scan(n, state):
        upd = kv_s[n]
        kv_s[n] = state
        return dl_s[n][0:1, :] * state + upd

    lax.fori_loop(0, S // CHUNK, scan, jnp.zeros((B_VDIM, B_KDIM), F32))

    gain = g_ref[...]

    def inter(gi, _):
        r0 = pl.multiple_of(gi * HG_ROWS * HG_GROUPS, HG_ROWS * HG_GROUPS)
        for c in range(cpg * HG_GROUPS):
            rows = pl.ds(r0 + c * CHUNK, CHUNK)
            n = gi * cpg * HG_GROUPS + c
            o = oi_s[rows, :] + _dot_nt(qe_s[rows, :], kv_s[n].astype(BF16))
            o = o * lax.rsqrt(jnp.mean(o * o, axis=-1, keepdims=True) + NORM_EPS)
            o_ref[0, rows, :] = (o * gain * sz_ref[0, rows, :].astype(F32)).astype(o_ref.dtype)
        return 0

    lax.fori_loop(0, n_grp // HG_GROUPS, inter, 0)


def _hgrn(qb, fb, ib, szb, lb_logits, gain, tri, bd, layer):
    B, S, _ = qb.shape
    n_ch = S // CHUNK
    tok = lambda: pl.BlockSpec((1, S, B_KDIM), lambda b, h: (b, 0, h))
    return pl.pallas_call(
        functools.partial(_hgrn_kernel, layer=layer),
        out_shape=jax.ShapeDtypeStruct((B, S, B_WIDTH), BF16),
        grid=(B, B_HEADS),
        in_specs=[tok(), tok(), tok(), tok(),
                  pl.BlockSpec((lb_logits.shape[0], B_KDIM), lambda b, h: (0, h)),
                  pl.BlockSpec((1, B_VDIM), lambda b, h: (0, h)),
                  pl.BlockSpec((HG_ROWS, HG_ROWS), lambda b, h: (0, 0)),
                  pl.BlockSpec(bd.shape, lambda b, h: (0, 0))],
        out_specs=tok(),
        scratch_shapes=[pltpu.VMEM((S, B_KDIM), BF16),
                        pltpu.VMEM((n_ch, B_VDIM, B_KDIM), F32),
                        pltpu.VMEM((n_ch, 8, B_KDIM), F32),
                        pltpu.VMEM((S, B_VDIM), F32)],
        compiler_params=pltpu.CompilerParams(
            dimension_semantics=("parallel", "parallel"),
            vmem_limit_bytes=V7X_VMEM_LIMIT_BYTES),
        name="hgrn2",
    )(qb, fb, ib, szb, lb_logits, gain, tri, bd)


def _out_kernel(x_ref, oat_ref, ob_ref, wgm_ref, bgm_ref, wa_ref, wb_ref,
                wo_ref, lng_ref, lnb_ref, o_ref, *, alpha):
    x = x_ref[0]
    sg = _sigmoid(_dot(x.astype(BF16), wgm_ref[...]) + bgm_ref[...])
    ya = lax.dot_general(oat_ref[0], wa_ref[...], (((0,), (0,)), ((), ())), preferred_element_type=F32)
    y = sg[:, :D_MODEL] * ya + sg[:, D_MODEL:] * _dot(ob_ref[0], wb_ref[...])
    r = alpha * x + _dot(y.astype(BF16), wo_ref[...])
    mu = jnp.mean(r, axis=-1, keepdims=True)
    d = r - mu
    var = jnp.mean(d * d, axis=-1, keepdims=True)
    o_ref[0] = d * lax.rsqrt(var + NORM_EPS) * lng_ref[...] + lnb_ref[...]


def _merge_out(x, oat, ob, wgm, bgm, wa, wb, wo, lng, lnb, alpha):
    B, S, D = x.shape
    T = OUT_ROWS
    full = lambda a: pl.BlockSpec(a.shape, lambda b, s: (0,) * a.ndim)
    rows = lambda w: pl.BlockSpec((1, T, w), lambda b, s: (b, s, 0))
    return pl.pallas_call(
        functools.partial(_out_kernel, alpha=alpha),
        out_shape=jax.ShapeDtypeStruct((B, S, D), x.dtype),
        grid=(B, S // T),
        in_specs=[rows(D), pl.BlockSpec((1, A_WIDTH, T), lambda b, s: (b, 0, s)), rows(B_WIDTH),
                  full(wgm), full(bgm), full(wa), full(wb), full(wo), full(lng), full(lnb)],
        out_specs=rows(D),
        compiler_params=pltpu.CompilerParams(
            dimension_semantics=("parallel", "parallel"),
            vmem_limit_bytes=V7X_VMEM_LIMIT_BYTES),
        name="merge_out",
    )(x, oat, ob, wgm, bgm, wa, wb, wo, lng, lnb)


def _rope_tables(S):
    inv = ROPE_THETA ** (-jnp.arange(0, HEAD_DIM, 2, dtype=F32) / HEAD_DIM)
    ang = jnp.arange(S, dtype=F32)[:, None] * inv[None, :]
    cos = jnp.concatenate([jnp.cos(ang), jnp.cos(ang)], axis=-1)
    sin = jnp.concatenate([jnp.sin(ang), jnp.sin(ang)], axis=-1)
    first = (jnp.arange(HEAD_DIM) < HEAD_DIM // 2)[None, :]
    sina = jnp.where(first, -sin, 0.0)
    sinb = jnp.where(first, 0.0, sin)
    tile = lambda a: jnp.concatenate([a] * A_GROUPS, axis=-1)
    return cos.T, sin.T, tile(cos), tile(sina), tile(sinb)


def _overlap_t(S):
    n_cmp = S // CMP_STRIDE
    n_slc = S // SLC_BLOCK
    cs = np.arange(n_cmp)[None, :] * CMP_STRIDE
    ss = np.arange(n_slc)[:, None] * SLC_BLOCK
    ov = (cs < ss + SLC_BLOCK) & (cs + CMP_BLOCK > ss) & (np.arange(n_cmp)[None, :] < n_cmp - 1)
    return jnp.asarray(ov, dtype=BF16)


def _block_tri():
    r = np.arange(HG_ROWS)
    return jnp.asarray((r[:, None] // CHUNK == r[None, :] // CHUNK) & (r[None, :] <= r[:, None]), dtype=BF16)


def _block_diag():
    r = np.arange(HG_ROWS)[:, None] // CHUNK
    c = np.arange(HG_ROWS // CHUNK * B_KDIM)[None, :] // B_KDIM
    return jnp.asarray(r == c, dtype=BF16)


def _layer(x, l, w_in, b_in, pe_k, w_k1, w_k2, pe_v, w_v1, w_v2, lb_logits, norm_g,
           w_a, w_b, w_o, ln_g, ln_b):
    B, S, D = x.shape
    alpha = (2 * DEPTH) ** 0.25
    o = _OFF
    wsl = lambda i: w_in[:, o[i]:o[i + 1]]
    bsl = lambda i: b_in[o[i]:o[i + 1]]
    kvw, kvb = wsl(1), bsl(1)
    kv_w = lambda j: kvw[:, j * KV_WIDTH:(j + 1) * KV_WIDTH]
    kv_b = lambda j: kvb[j * KV_WIDTH:(j + 1) * KV_WIDTH]
    gw, gb = wsl(2), bsl(2)
    gidx = np.zeros((A_GROUPS, GATE_ROWS), np.int32)
    gmask = np.zeros((A_GROUPS, GATE_ROWS), np.float32)
    for g in range(A_GROUPS):
        for br in range(3):
            for h in range(A_HPG):
                gidx[g, br * A_HPG + h] = (g * A_HPG + h) * 3 + br
                gmask[g, br * A_HPG + h] = 1.0
    gidx, gmask = gidx.reshape(-1), gmask.reshape(-1)
    gw_t = gw[:, gidx] * gmask[None, :]
    gb_t = gb[gidx] * gmask

    wt = jnp.concatenate([wsl(0), kv_w(3), kv_w(5), gw_t, wsl(3)], axis=1).T.astype(BF16)
    bt = jnp.concatenate([bsl(0), kv_b(3), kv_b(5), gb_t, bsl(3)])[:, None]
    wn = jnp.concatenate([kv_w(0), kv_w(1), kv_w(2), kv_w(4), wsl(4), wsl(5), wsl(6), wsl(7)],
                         axis=1).astype(BF16)
    bn = jnp.concatenate([kv_b(0), kv_b(1), kv_b(2), kv_b(4), bsl(4), bsl(5), bsl(6), bsl(7)])[None, :]
    cost, sint, cos, sina, sinb = _rope_tables(S)

    (qr, qn, vts, vtw, gt, szat, kcmp, vcmp, ks, kw, qb, fb, ib, szb) = _project(
        x, wn, bn, wt, bt, cost, sint, cos, sina, sinb)

    half = CMP_STRIDE * HEAD_DIM

    def w1_planes(w1):
        both = jnp.concatenate([w1[:half], w1[half:]], axis=1).reshape(CMP_STRIDE, HEAD_DIM, 2 * CMP_HIDDEN)
        z = jnp.zeros_like(both)
        return jnp.concatenate([jnp.concatenate([both, z], axis=2),
                                jnp.concatenate([z, both], axis=2)], axis=1).astype(BF16)

    kc, vct = _compress(kcmp, vcmp,
                        w1_planes(w_k1), w_k1, pe_k.reshape(-1, 1), w_k2.astype(BF16),
                        w1_planes(w_v1), w_v1, pe_v.reshape(-1, 1), w_v2.T.astype(BF16))

    oat = _nsa(qr, qn, gt, szat, kc, vct, ks, kw, vts, vtw, _overlap_t(S))
    ob = _hgrn(qb, fb, ib, szb, lb_logits, norm_g[None, :], _block_tri(), _block_diag(), l)

    wgm = jnp.concatenate([wsl(8), wsl(9)], axis=1).astype(BF16)
    bgm = jnp.concatenate([bsl(8), bsl(9)])[None, :]
    return _merge_out(x, oat, ob, wgm, bgm, w_a.astype(BF16), w_b.astype(BF16),
                      w_o.astype(BF16), ln_g[None, :], ln_b[None, :], alpha)


@jax.jit
def kernel(x, w_in, b_in, pe_cmp_k, w_cmp_k1, w_cmp_k2, pe_cmp_v, w_cmp_v1, w_cmp_v2,
           hgrn_lb_logits, hgrn_norm_g, w_branch_a, w_branch_b, w_out, ln_g, ln_b):
    B, S, D = x.shape
    assert D == D_MODEL and S % KV_CHUNK == 0 and S % PROJ_ROWS == 0 and S >= WIN_KEYS
    assert S % OUT_ROWS == 0 and S % (HG_ROWS * HG_GROUPS) == 0 and (S // KV_CHUNK) * KV_CHUNK == S
    for l in range(DEPTH):
        x = _layer(x, l, w_in[l], b_in[l], pe_cmp_k[l], w_cmp_k1[l], w_cmp_k2[l],
                   pe_cmp_v[l], w_cmp_v1[l], w_cmp_v2[l], hgrn_lb_logits, hgrn_norm_g[l],
                   w_branch_a[l], w_branch_b[l], w_out[l], ln_g[l], ln_b[l])
    return x
```

```python
import functools
import math

import numpy as np
import jax
import jax.numpy as jnp
from jax import lax
from jax.experimental import pallas as pl
from jax.experimental.pallas import tpu as pltpu

D_MODEL = 1024
DEPTH = 1
A_HEADS = 8
A_GROUPS = 2
A_HPG = A_HEADS // A_GROUPS
HEAD_DIM = 64
A_WIDTH = A_HEADS * HEAD_DIM
KV_WIDTH = A_GROUPS * HEAD_DIM
CMP_BLOCK = 32
CMP_STRIDE = 16
CMP_HIDDEN = 128
SLC_BLOCK = 64
SLC_TOP_N = 16
WINDOW = 512
ROPE_THETA = 10000.0
FORCE_SCORE = 1e30
B_HEADS = 4
B_KDIM = 128
B_VDIM = 128
B_FWIDTH = B_HEADS * B_KDIM
B_WIDTH = B_HEADS * B_VDIM
CHUNK = 64
NORM_EPS = 1e-5

IN_SPLITS = (A_WIDTH, 6 * KV_WIDTH, A_HEADS * 3, A_WIDTH, B_FWIDTH, B_FWIDTH,
             B_WIDTH, B_WIDTH, D_MODEL, D_MODEL)
_OFF = tuple(int(v) for v in np.cumsum((0,) + IN_SPLITS))

V7X_VMEM_LIMIT_BYTES = 56 * 1024 * 1024
PROJ_ROWS = 512
Q_TILE = 128
KV_CHUNK = 512
IN_FLIGHT = 4
AHEAD = 2
MAX_JOBS_PER_TRIP = 16
WIN_KEYS = WINDOW + Q_TILE
HG_ROWS = 256
HG_GROUPS = 4
OUT_ROWS = 512
MASKED = -1e30
LOG2E = math.log2(math.e)
GATE_ROWS = 16
V_ROWS = HEAD_DIM + 16

F32 = jnp.float32
BF16 = jnp.bfloat16


def _dot(a, b):
    return jnp.dot(a, b, preferred_element_type=F32)


def _dot_nt(a, b):
    return lax.dot_general(a, b, (((1,), (1,)), ((), ())), preferred_element_type=F32)


def _sigmoid(x):
    return 1.0 / (1.0 + jnp.exp(-x))


def _silu(x):
    return x * _sigmoid(x)


def _proj_kernel(x_ref, wn_ref, bn_ref, wt_ref, bt_ref, cost_ref, sint_ref,
                 cos_ref, sina_ref, sinb_ref,
                 qr_ref, qn_ref, vts_ref, vtw_ref, gt_ref, szat_ref, kc_ref, vc_ref,
                 ks_ref, kw_ref, qb_ref, fb_ref, ib_ref, szb_ref, cmp_s):
    xb = x_ref[0].astype(BF16)
    scale = HEAD_DIM ** -0.5 * LOG2E

    ht = _dot_nt(wt_ref[...], xb) + bt_ref[...]
    cost = cost_ref[...]
    sint = sint_ref[...]
    half = HEAD_DIM // 2
    for h in range(A_HEADS):
        blk = ht[h * HEAD_DIM:(h + 1) * HEAD_DIM]
        rot = jnp.concatenate([-blk[half:], blk[:half]], axis=0)
        qr_ref[0, h * HEAD_DIM:(h + 1) * HEAD_DIM, :] = ((blk * cost + rot * sint) * scale).astype(BF16)
        qn_ref[0, h * HEAD_DIM:(h + 1) * HEAD_DIM, :] = (blk * scale).astype(BF16)
    o = A_WIDTH
    ones_rows = jnp.where(lax.broadcasted_iota(jnp.int32, (V_ROWS - HEAD_DIM, ht.shape[1]), 0) == 0, 1.0, 0.0)
    for g in range(A_GROUPS):
        for ref, base in ((vts_ref, o), (vtw_ref, o + KV_WIDTH)):
            rows_g = ht[base + g * HEAD_DIM:base + (g + 1) * HEAD_DIM]
            ref[0, g] = jnp.concatenate([rows_g, ones_rows], axis=0).astype(BF16)
    gt_ref[0] = _sigmoid(ht[o + 2 * KV_WIDTH:o + 2 * KV_WIDTH + 2 * GATE_ROWS])
    o += 2 * KV_WIDTH + 2 * GATE_ROWS
    szat_ref[0] = _silu(ht[o:o + A_WIDTH]).astype(BF16)

    def cols(lo, hi):
        return _dot(xb, wn_ref[:, lo:hi]) + bn_ref[:, lo:hi]

    kv = cols(0, 4 * KV_WIDTH)
    cmp_s[0] = kv[:, 0:KV_WIDTH]
    cmp_s[1] = kv[:, KV_WIDTH:2 * KV_WIDTH]
    pieces = cmp_s.shape[1] // CMP_STRIDE
    for t in range(CMP_STRIDE):
        kc_ref[0, t] = cmp_s[0, pl.ds(t, pieces, stride=CMP_STRIDE), :].astype(BF16)
        vc_ref[0, t] = cmp_s[1, pl.ds(t, pieces, stride=CMP_STRIDE), :].astype(BF16)
    cos = cos_ref[...]
    sina = sina_ref[...]
    sinb = sinb_ref[...]

    def rope_rows(k):
        return (k * cos + pltpu.roll(k, 128 - half, axis=1) * sina + pltpu.roll(k, half, axis=1) * sinb)

    ks = rope_rows(kv[:, 2 * KV_WIDTH:3 * KV_WIDTH])
    rows = ks.shape[0]
    lane = lax.broadcasted_iota(jnp.int32, (rows, KV_WIDTH), 1)
    pos = pl.program_id(1) * rows + lax.broadcasted_iota(jnp.int32, (rows, KV_WIDTH), 0)
    blocks_per_chunk = KV_CHUNK // SLC_BLOCK
    ind = jnp.where(lane == HEAD_DIM + (pos // SLC_BLOCK) % blocks_per_chunk, 1.0, 0.0)
    ks_ref[0, 0] = (jnp.where(lane < HEAD_DIM, ks, 0.0) + ind).astype(BF16)
    ks_ref[0, 1] = (jnp.where(lane < HEAD_DIM, pltpu.roll(ks, HEAD_DIM, axis=1), 0.0) + ind).astype(BF16)
    kw = rope_rows(kv[:, 3 * KV_WIDTH:4 * KV_WIDTH])
    kw_ref[0, 0] = jnp.where(lane < HEAD_DIM, kw, 0.0).astype(BF16)
    kw_ref[0, 1] = jnp.where(lane < HEAD_DIM, pltpu.roll(kw, HEAD_DIM, axis=1), 0.0).astype(BF16)
    o = 4 * KV_WIDTH
    qb_ref[0] = cols(o, o + B_FWIDTH).astype(BF16)
    o += B_FWIDTH
    fb_ref[0] = cols(o, o + B_FWIDTH)
    o += B_FWIDTH
    ib_ref[0] = cols(o, o + B_WIDTH).astype(BF16)
    o += B_WIDTH
    szb_ref[0] = _silu(cols(o, o + B_WIDTH)).astype(BF16)


def _project(x, wn, bn, wt, bt, cost, sint, cos, sina, sinb):
    B, S, D = x.shape
    T = PROJ_ROWS
    n_t = wt.shape[0]
    n_n = wn.shape[1]
    full = lambda shape: pl.BlockSpec(shape, lambda b, s: (0,) * len(shape))
    row_out = lambda w: pl.BlockSpec((1, T, w), lambda b, s: (b, s, 0))
    col_out = lambda r: pl.BlockSpec((1, r, T), lambda b, s: (b, 0, s))
    sds = jax.ShapeDtypeStruct
    out_shape = (
        sds((B, A_WIDTH, S), BF16), sds((B, A_WIDTH, S), BF16),
        sds((B, A_GROUPS, V_ROWS, S), BF16), sds((B, A_GROUPS, V_ROWS, S), BF16),
        sds((B, 2 * GATE_ROWS, S), F32),
        sds((B, A_WIDTH, S), BF16),
        sds((B, CMP_STRIDE, S // CMP_STRIDE, KV_WIDTH), BF16),
        sds((B, CMP_STRIDE, S // CMP_STRIDE, KV_WIDTH), BF16),
        sds((B, A_GROUPS, S, KV_WIDTH), BF16), sds((B, A_GROUPS, S, KV_WIDTH), BF16),
        sds((B, S, B_FWIDTH), BF16), sds((B, S, B_FWIDTH), F32),
        sds((B, S, B_WIDTH), BF16), sds((B, S, B_WIDTH), BF16),
    )
    out_specs = (
        col_out(A_WIDTH), col_out(A_WIDTH),
        pl.BlockSpec((1, A_GROUPS, V_ROWS, T), lambda b, s: (b, 0, 0, s)),
        pl.BlockSpec((1, A_GROUPS, V_ROWS, T), lambda b, s: (b, 0, 0, s)),
        col_out(2 * GATE_ROWS), col_out(A_WIDTH),
        pl.BlockSpec((1, CMP_STRIDE, T // CMP_STRIDE, KV_WIDTH), lambda b, s: (b, 0, s, 0)),
        pl.BlockSpec((1, CMP_STRIDE, T // CMP_STRIDE, KV_WIDTH), lambda b, s: (b, 0, s, 0)),
        pl.BlockSpec((1, A_GROUPS, T, KV_WIDTH), lambda b, s: (b, 0, s, 0)),
        pl.BlockSpec((1, A_GROUPS, T, KV_WIDTH), lambda b, s: (b, 0, s, 0)),
        row_out(B_FWIDTH), row_out(B_FWIDTH), row_out(B_WIDTH), row_out(B_WIDTH),
    )
    in_specs = [
        pl.BlockSpec((1, T, D), lambda b, s: (b, s, 0)),
        full((D, n_n)), full((1, n_n)), full((n_t, D)), full((n_t, 1)),
        pl.BlockSpec((HEAD_DIM, T), lambda b, s: (0, s)),
        pl.BlockSpec((HEAD_DIM, T), lambda b, s: (0, s)),
        pl.BlockSpec((T, KV_WIDTH), lambda b, s: (s, 0)),
        pl.BlockSpec((T, KV_WIDTH), lambda b, s: (s, 0)),
        pl.BlockSpec((T, KV_WIDTH), lambda b, s: (s, 0)),
    ]
    return pl.pallas_call(
        _proj_kernel, out_shape=out_shape, grid=(B, S // T),
        in_specs=in_specs, out_specs=out_specs,
        scratch_shapes=[pltpu.VMEM((2, T, KV_WIDTH), F32)],
        compiler_params=pltpu.CompilerParams(
            dimension_semantics=("parallel", "parallel"),
            vmem_limit_bytes=V7X_VMEM_LIMIT_BYTES),
        name="in_proj",
    )(x, wn, bn, wt, bt, cost, sint, cos, sina, sinb)


def _compress_kernel(ck_ref, cv_ref, w1k_ref, w1kf_ref, pek_ref, w2k_ref,
                     w1v_ref, w1vf_ref, pev_ref, w2vt_ref, kc_ref, vct_ref):
    nc = ck_ref.shape[2]
    hid = CMP_HIDDEN

    def hidden(c_ref, w1_ref, w1f_ref, pe_ref):
        a = _dot(c_ref[0, 0], w1_ref[0])
        for t in range(1, CMP_STRIDE):
            a = a + _dot(c_ref[0, t], w1_ref[t])
        pe_term = jnp.sum(w1f_ref[...] * pe_ref[...], axis=0, keepdims=True)
        out = []
        for g in range(A_GROUPS):
            lo = a[:, 2 * g * hid:(2 * g + 1) * hid]
            hi = pltpu.roll(a[:, (2 * g + 1) * hid:(2 * g + 2) * hid], nc - 1, axis=0)
            out.append(_silu(lo + hi + pe_term).astype(BF16))
        return out

    hk = hidden(ck_ref, w1k_ref, w1kf_ref, pek_ref)
    hv = hidden(cv_ref, w1v_ref, w1vf_ref, pev_ref)
    for g in range(A_GROUPS):
        kc_ref[0, g] = _dot(hk[g], w2k_ref[...]).astype(BF16)
        vct_ref[0, g] = _dot_nt(w2vt_ref[...], hv[g]).astype(BF16)


def _compress(ck, cv, w1k, w1kf, pek, w2k, w1v, w1vf, pev, w2vt):
    B, P, NC, W = ck.shape
    full = lambda a: pl.BlockSpec(a.shape, lambda b: (0,) * a.ndim)
    blk = pl.BlockSpec((1, P, NC, W), lambda b: (b, 0, 0, 0))
    return pl.pallas_call(
        _compress_kernel,
        out_shape=(jax.ShapeDtypeStruct((B, A_GROUPS, NC, HEAD_DIM), BF16),
                   jax.ShapeDtypeStruct((B, A_GROUPS, HEAD_DIM, NC), BF16)),
        grid=(B,),
        in_specs=[blk, blk, full(w1k), full(w1kf), full(pek), full(w2k),
                  full(w1v), full(w1vf), full(pev), full(w2vt)],
        out_specs=(pl.BlockSpec((1, A_GROUPS, NC, HEAD_DIM), lambda b: (b, 0, 0, 0)),
                   pl.BlockSpec((1, A_GROUPS, HEAD_DIM, NC), lambda b: (b, 0, 0, 0))),
        compiler_params=pltpu.CompilerParams(
            dimension_semantics=("parallel",),
            vmem_limit_bytes=V7X_VMEM_LIMIT_BYTES),
        name="kv_compress",
    )(ck, cv, w1k, w1kf, pek, w2k, w1v, w1vf, pev, w2vt)


def _nsa_kernel(jc_ref, jq_ref, qr_ref, qn_ref, gt_ref, szat_ref, kc_ref, vct_ref, ks_ref, kw_ref,
                vts_ref, vtw_ref, ovl_ref, o_ref,
                bias_s, wq_s, m_s, acc_s, ocw_s, sbuf_s, smax_s, dbuf_s, dmax_s, scb_s, swb_s, wmax_s,
                *, seq, n_sel, n_jobs, jobs_per_trip):
    nc = kc_ref.shape[2]
    n_slc = seq // SLC_BLOCK
    n_q = seq // Q_TILE
    n_ck = seq // KV_CHUNK
    lanes = A_HPG * Q_TILE
    bpc = KV_CHUNK // SLC_BLOCK
    tpc = KV_CHUNK // Q_TILE
    win_tiles = WINDOW // Q_TILE
    last = n_q - 1

    def stack_heads(ref, q0):
        return jnp.concatenate(
            [ref[0, h * HEAD_DIM:(h + 1) * HEAD_DIM, pl.ds(q0, Q_TILE)] for h in range(A_HPG)], axis=1)

    def per_head(a):
        return jnp.concatenate([a] * A_HPG, axis=1)

    def gate_row(gt, branch):
        return jnp.concatenate(
            [gt[branch * A_HPG + h:branch * A_HPG + h + 1, :] for h in range(A_HPG)], axis=1)

    def tile_start(qi):
        return pl.multiple_of(qi * Q_TILE, Q_TILE)

    lane_q = lax.broadcasted_iota(jnp.int32, (1, lanes), 1) & (Q_TILE - 1)
    r_sq = lax.broadcasted_iota(jnp.int32, (Q_TILE, lanes), 0)
    causal_sq = r_sq <= lane_q
    lower_sq = r_sq > lane_q
    q_pad = jnp.zeros((KV_WIDTH - HEAD_DIM, lanes), BF16)

    def cmp_keys(rb):
        return min(nc, -(-(rb * KV_CHUNK // CMP_STRIDE) // 128) * 128)

    def cmp_scores(qi, rb):
        return _dot(kc_ref[0, 0, 0:cmp_keys(rb), :], stack_heads(qn_ref, tile_start(qi)))

    def select_tile(qi, rb, sc):
        nk = cmp_keys(rb)
        n_causal = (qi + 1) * Q_TILE // SLC_BLOCK
        s0 = tile_start(qi)
        t_row = s0 + lax.broadcasted_iota(jnp.int32, (1, Q_TILE), 1)
        wq_s[qi] = stack_heads(qr_ref, s0)

        n_end = lax.broadcasted_iota(jnp.int32, (nk, Q_TILE), 0) * CMP_STRIDE + (CMP_BLOCK - 1)
        sc = sc + per_head(jnp.where(n_end <= t_row, 0.0, MASKED))
        mc = jnp.max(sc, axis=0, keepdims=True)
        pc = jnp.exp2(sc - mc)
        lc = jnp.sum(pc, axis=0, keepdims=True)
        pc = pc * jnp.where(s0 + lane_q >= CMP_BLOCK - 1, 1.0 / lc, 0.0)
        o_cmp = _dot(vct_ref[0, 0, :, 0:nk], pc.astype(BF16))

        ps = pc[:, 0:Q_TILE]
        for h in range(1, A_HPG):
            ps = ps + pc[:, h * Q_TILE:(h + 1) * Q_TILE]
        p_hi = ps.astype(BF16)
        p_lo = (ps - p_hi.astype(F32)).astype(BF16)
        imp2 = _dot(ovl_ref[0:8 * rb, 0:nk], jnp.concatenate([p_hi, p_lo], axis=1))
        imp = imp2[:, :Q_TILE] + imp2[:, Q_TILE:]

        j_idx = lax.broadcasted_iota(jnp.int32, (8 * rb, Q_TILE), 0)
        cur = t_row // SLC_BLOCK
        sel = j_idx <= cur
        if 8 * rb > n_sel:
            forced = (j_idx == 0) | (j_idx == cur) | (j_idx == cur - 1)
            val = jnp.where(j_idx > cur, -1.0, jnp.where(forced, FORCE_SCORE, imp))
            rows = [val[8 * j:8 * j + 8] for j in range(rb)]
            cnt = [jnp.zeros((8, Q_TILE), F32) for _ in range(rb)]
            jl = lax.broadcasted_iota(jnp.int32, (8, Q_TILE), 0)
            for kb in range(rb):
                for kl in range(8):
                    if 8 * kb + kl >= n_causal:
                        continue
                    row = rows[kb][kl:kl + 1, :]
                    for jb in range(rb):
                        if jb < kb:
                            beats = row > rows[jb]
                        elif jb > kb:
                            beats = row >= rows[jb]
                        else:
                            beats = (row > rows[jb]) | ((row == rows[jb]) & (jl > kl))
                        cnt[jb] = cnt[jb] + jnp.where(beats, 1.0, 0.0)
            sel = sel & (jnp.concatenate(cnt, axis=0) < n_sel)
        bias = per_head(jnp.where(sel, 0.0, MASKED))
        pad = jnp.zeros((bpc, lanes), F32)
        for c in range(rb):
            bias_s[qi, c] = jnp.concatenate([bias[c * bpc:(c + 1) * bpc], pad], axis=0).astype(BF16)

        gt = gt_ref[0, :, pl.ds(s0, Q_TILE)]
        ocw_s[qi] = gate_row(gt, 0) * o_cmp
        m_s[qi] = jnp.full((8, lanes), MASKED, F32)
        acc_s[qi] = jnp.zeros((V_ROWS, lanes), F32)

    scb_s[0, 0:cmp_keys(1), :] = cmp_scores(0, 1)
    for qi in range(n_q):
        rb = qi // tpc + 1
        nk = cmp_keys(rb)
        if qi + 1 < n_q:
            rb_next = (qi + 1) // tpc + 1
            scb_s[(qi + 1) & 1, 0:cmp_keys(rb_next), :] = cmp_scores(qi + 1, rb_next)
        select_tile(qi, rb, scb_s[qi & 1, 0:nk, :])

    def win_finish(qi, sw, mw, w0, keys=WIN_KEYS):
        pw = jnp.exp2(sw - mw)
        ow = _dot(vtw_ref[0, 0, :, pl.ds(w0, keys)], pw.astype(BF16))
        o_win = ow[0:HEAD_DIM] * (1.0 / ow[HEAD_DIM:HEAD_DIM + 1])
        gt = gt_ref[0, :, pl.ds(tile_start(qi), Q_TILE)]
        ocw_s[qi] = ocw_s[qi] + gate_row(gt, 2) * o_win

    for qi in range(min(win_tiles, n_q)):
        keys = (qi + 1) * Q_TILE
        wq = jnp.concatenate([wq_s[qi], q_pad], axis=0)
        sw = _dot(kw_ref[0, 0, 0:keys, :], wq)
        sw = jnp.concatenate(([sw[:keys - Q_TILE]] if qi else [])
                             + [jnp.where(causal_sq, sw[keys - Q_TILE:], MASKED)], axis=0)
        win_finish(qi, sw, jnp.max(sw, axis=0, keepdims=True), 0, keys)

    def win_scores(qi, slot):
        w0 = pl.multiple_of(tile_start(qi) - WINDOW, Q_TILE)
        wq = jnp.concatenate([wq_s[qi], q_pad], axis=0)
        sw = _dot(kw_ref[0, 0, pl.ds(w0, WIN_KEYS), :], wq)
        sw = jnp.concatenate([jnp.where(lower_sq, sw[:Q_TILE], MASKED), sw[Q_TILE:WINDOW],
                              jnp.where(causal_sq, sw[WINDOW:], MASKED)], axis=0)
        swb_s[slot] = sw
        wmax_s[slot] = jnp.broadcast_to(jnp.max(sw, axis=0, keepdims=True), (8, lanes))

    def win_group(i, _):
        q0 = win_tiles + IN_FLIGHT * i
        for u in range(IN_FLIGHT):
            win_scores(jnp.minimum(q0 + u + AHEAD, last), (u + AHEAD) % IN_FLIGHT)
            win_finish(q0 + u, swb_s[u], wmax_s[u][0:1],
                       pl.multiple_of(tile_start(q0 + u) - WINDOW, Q_TILE))
        return 0

    if n_q > win_tiles:
        for u in range(AHEAD):
            win_scores(jnp.minimum(win_tiles + u, last), u)
        lax.fori_loop(0, (n_q - win_tiles) // IN_FLIGHT, win_group, 0)

    wq_pad = jnp.zeros((KV_WIDTH - HEAD_DIM - 2 * bpc, lanes), BF16)

    def scores(c, qi, diag):
        rows = KV_CHUNK if diag is None else (diag + 1) * Q_TILE
        k0 = pl.multiple_of(c * KV_CHUNK, KV_CHUNK)
        wq = jnp.concatenate([wq_s[qi], bias_s[qi, c], wq_pad], axis=0)
        s = _dot(ks_ref[0, 0, pl.ds(k0, rows), :], wq)
        if diag is not None:
            head = [s[:rows - Q_TILE]] if diag > 0 else []
            s = jnp.concatenate(head + [jnp.where(causal_sq, s[rows - Q_TILE:], MASKED)], axis=0)
        return s, jnp.broadcast_to(jnp.max(s, axis=0, keepdims=True), (8, lanes))

    def absorb(c, qi, s, s_max):
        rows = s.shape[0]
        k0 = pl.multiple_of(c * KV_CHUNK, KV_CHUNK)
        m_old = m_s[qi][0:1]
        m_new = jnp.maximum(m_old, s_max[0:1])
        alpha = jnp.exp2(m_old - m_new)
        p = jnp.exp2(s - m_new)
        m_s[qi] = jnp.broadcast_to(m_new, (8, lanes))
        acc_s[qi] = alpha * acc_s[qi] + _dot(vts_ref[0, 0, :, pl.ds(k0, rows)], p.astype(BF16))

    def full_scores(j, slot):
        sbuf_s[slot], smax_s[slot] = scores(jc_ref[j], jq_ref[j], None)

    def full_group(i, _):
        j0 = jobs_per_trip * i
        for u in range(jobs_per_trip):
            full_scores(j0 + u + AHEAD, (u + AHEAD) % IN_FLIGHT)
            absorb(jc_ref[j0 + u], jq_ref[j0 + u], sbuf_s[u % IN_FLIGHT], smax_s[u % IN_FLIGHT])
        return 0

    if n_jobs:
        for u in range(AHEAD):
            full_scores(u, u)
        lax.fori_loop(0, n_jobs // jobs_per_trip, full_group, 0)

    def diag_scores(c, r):
        dbuf_s[r, 0:(r + 1) * Q_TILE, :], dmax_s[r] = scores(c, c * tpc + r, r)

    def diag_chunk(c, _):
        nxt = jnp.minimum(c + 1, n_ck - 1)
        for r in range(tpc):
            ahead = r + AHEAD
            diag_scores(c if ahead < tpc else nxt, ahead % tpc)
            absorb(c, c * tpc + r, dbuf_s[r, 0:(r + 1) * Q_TILE, :], dmax_s[r])
        return 0

    for r in range(AHEAD):
        diag_scores(0, r)
    lax.fori_loop(0, n_ck, diag_chunk, 0)

    def finish(qi, _):
        s0 = tile_start(qi)
        gt = gt_ref[0, :, pl.ds(s0, Q_TILE)]
        acc = acc_s[qi]
        ot = ocw_s[qi] + gate_row(gt, 1) * acc[0:HEAD_DIM] * (1.0 / acc[HEAD_DIM:HEAD_DIM + 1])
        ot = (ot * stack_heads(szat_ref, s0).astype(F32)).astype(o_ref.dtype)
        for h in range(A_HPG):
            o_ref[0, h * HEAD_DIM:(h + 1) * HEAD_DIM, pl.ds(s0, Q_TILE)] = ot[:, h * Q_TILE:(h + 1) * Q_TILE]
        return 0

    lax.fori_loop(0, n_q, finish, 0)


def _nsa(qr, qn, gt, szat, kc, vct, ks, kw, vts, vtw, ovl):
    B, _, S = qr.shape
    NC = kc.shape[2]
    n_slc = S // SLC_BLOCK
    n_sel = min(SLC_TOP_N, n_slc)
    n_q = S // Q_TILE
    n_ck = S // KV_CHUNK
    tpc = KV_CHUNK // Q_TILE
    gw = A_HPG * HEAD_DIM
    lanes = A_HPG * Q_TILE
    jobs = [(c, q) for c in range(n_ck) for q in range((c + 1) * tpc, n_q)]
    n_jobs = len(jobs)
    jobs_per_trip = max(j for j in range(IN_FLIGHT, MAX_JOBS_PER_TRIP + 1, IN_FLIGHT) if n_jobs % j == 0)
    assert tpc == IN_FLIGHT and AHEAD < IN_FLIGHT
    jobs = jobs + [jobs[-1] if jobs else (0, 0)] * AHEAD
    jc = jnp.asarray([j[0] for j in jobs], jnp.int32)
    jq = jnp.asarray([j[1] for j in jobs], jnp.int32)
    kernel = functools.partial(_nsa_kernel, seq=S, n_sel=n_sel, n_jobs=n_jobs, jobs_per_trip=jobs_per_trip)
    per_group = lambda rows, cols: pl.BlockSpec((1, 1, rows, cols), lambda b, g, *_: (b, g, 0, 0))
    in_specs = [
        pl.BlockSpec((1, gw, S), lambda b, g, *_: (b, g, 0)),
        pl.BlockSpec((1, gw, S), lambda b, g, *_: (b, g, 0)),
        pl.BlockSpec((1, GATE_ROWS, S), lambda b, g, *_: (b, g, 0)),
        pl.BlockSpec((1, gw, S), lambda b, g, *_: (b, g, 0)),
        per_group(NC, HEAD_DIM), per_group(HEAD_DIM, NC),
        per_group(S, KV_WIDTH), per_group(S, KV_WIDTH),
        per_group(V_ROWS, S), per_group(V_ROWS, S),
        pl.BlockSpec((n_slc, NC), lambda b, g, *_: (0, 0)),
    ]
    scratch = [
        pltpu.VMEM((n_q, n_ck, 2 * (KV_CHUNK // SLC_BLOCK), lanes), BF16),
        pltpu.VMEM((n_q, HEAD_DIM, lanes), BF16),
        pltpu.VMEM((n_q, 8, lanes), F32),
        pltpu.VMEM((n_q, V_ROWS, lanes), F32),
        pltpu.VMEM((n_q, HEAD_DIM, lanes), F32),
        pltpu.VMEM((IN_FLIGHT, KV_CHUNK, lanes), F32),
        pltpu.VMEM((IN_FLIGHT, 8, lanes), F32),
        pltpu.VMEM((tpc, KV_CHUNK, lanes), F32),
        pltpu.VMEM((tpc, 8, lanes), F32),
        pltpu.VMEM((2, NC, lanes), F32),
        pltpu.VMEM((IN_FLIGHT, WIN_KEYS, lanes), F32),
        pltpu.VMEM((IN_FLIGHT, 8, lanes), F32),
    ]
    return pl.pallas_call(
        kernel,
        out_shape=jax.ShapeDtypeStruct((B, A_WIDTH, S), BF16),
        grid_spec=pltpu.PrefetchScalarGridSpec(
            num_scalar_prefetch=2,
            grid=(B, A_GROUPS),
            in_specs=in_specs,
            out_specs=pl.BlockSpec((1, gw, S), lambda b, g, *_: (b, g, 0)),
            scratch_shapes=scratch),
        compiler_params=pltpu.CompilerParams(
            dimension_semantics=("parallel", "parallel"),
            vmem_limit_bytes=V7X_VMEM_LIMIT_BYTES),
        name="nsa_attention",
    )(jc, jq, qr, qn, gt, szat, kc, vct, ks, kw, vts, vtw, ovl)


def _hgrn_kernel(q_ref, f_ref, i_ref, sz_ref, lbl_ref, g_ref, tri_ref, bd_ref, o_ref,
                 qe_s, kv_s, dl_s, oi_s, *, layer):
    S = q_ref.shape[1]
    n_grp = S // HG_ROWS
    cpg = HG_ROWS // CHUNK

    lg = lbl_ref[...]
    e = jnp.exp(lg - jnp.max(lg, axis=0, keepdims=True))
    lb = jnp.sum(e[0:layer + 1], axis=0, keepdims=True) / jnp.sum(e, axis=0, keepdims=True)

    tri = tri_ref[...]
    ri = lax.broadcasted_iota(jnp.int32, (HG_ROWS, HG_ROWS), 0)
    ci = lax.broadcasted_iota(jnp.int32, (HG_ROWS, HG_ROWS), 1)
    causal = (ri // CHUNK == ci // CHUNK) & (ci <= ri)

    def stage_decay(gi):
        rows = pl.ds(pl.multiple_of(gi * HG_ROWS, HG_ROWS), HG_ROWS)
        fg = lb + (1.0 - lb) * _sigmoid(f_ref[0, rows, :])
        logf = jnp.log(fg)
        l_hi = logf.astype(BF16)
        l_lo = (logf - l_hi.astype(F32)).astype(BF16)
        bb = _dot(tri, jnp.concatenate([l_hi, l_lo], axis=1))
        return rows, 1.0 - fg, bb[:, :B_KDIM] + bb[:, B_KDIM:]

    def stage_scores(rows, kk, b):
        b_last = jnp.concatenate(
            [jnp.broadcast_to(b[(c + 1) * CHUNK - 1:(c + 1) * CHUNK, :], (CHUNK, B_KDIM))
             for c in range(cpg)], axis=0)
        qe = (_silu(q_ref[0, rows, :].astype(F32)) * jnp.exp(b)).astype(BF16)
        ke = (kk * jnp.exp(-b)).astype(BF16)
        kd = (kk * jnp.exp(b_last - b)).astype(BF16)
        attn = jnp.where(causal, _dot_nt(qe, ke), 0.0).astype(BF16)
        return qe, kd, attn, b_last

    def stage_outputs(gi, rows, qe, kd, attn, b_last):
        v = i_ref[0, rows, :]
        oi_s[rows, :] = _dot(attn, v)
        qe_s[rows, :] = qe
        kd_bd = jnp.concatenate([kd] * cpg, axis=1) * bd_ref[...]
        kv_all = lax.dot_general(v, kd_bd, (((0,), (0,)), ((), ())), preferred_element_type=F32)
        for c in range(cpg):
            n = gi * cpg + c
            kv_s[n] = kv_all[:, c * B_KDIM:(c + 1) * B_KDIM]
            dl_s[n] = jnp.exp(b_last[c * CHUNK:c * CHUNK + 8, :])

    def intra(i, _):
        groups = [HG_GROUPS * i + u for u in range(HG_GROUPS)]
        decays = [stage_decay(gi) for gi in groups]
        mids = [stage_scores(*d) for d in decays]
        for gi, d, m in zip(groups, decays, mids):
            stage_outputs(gi, d[0], *m)
        return 0

    lax.fori_loop(0, n_grp // HG_GROUPS, intra, 0)

    def scan(n, state):
        upd = kv_s[n]
        kv_s[n] = state
        return dl_s[n][0:1, :] * state + upd

    lax.fori_loop(0, S // CHUNK, scan, jnp.zeros((B_VDIM, B_KDIM), F32))

    gain = g_ref[...]

    def inter(gi, _):
        r0 = pl.multiple_of(gi * HG_ROWS * HG_GROUPS, HG_ROWS * HG_GROUPS)
        for c in range(cpg * HG_GROUPS):
            rows = pl.ds(r0 + c * CHUNK, CHUNK)
            n = gi * cpg * HG_GROUPS + c
            o = oi_s[rows, :] + _dot_nt(qe_s[rows, :], kv_s[n].astype(BF16))
            o = o * lax.rsqrt(jnp.mean(o * o, axis=-1, keepdims=True) + NORM_EPS)
            o_ref[0, rows, :] = (o * gain * sz_ref[0, rows, :].astype(F32)).astype(o_ref.dtype)
        return 0

    lax.fori_loop(0, n_grp // HG_GROUPS, inter, 0)


def _hgrn(qb, fb, ib, szb, lb_logits, gain, tri, bd, layer):
    B, S, _ = qb.shape
    n_ch = S // CHUNK
    tok = lambda: pl.BlockSpec((1, S, B_KDIM), lambda b, h: (b, 0, h))
    return pl.pallas_call(
        functools.partial(_hgrn_kernel, layer=layer),
        out_shape=jax.ShapeDtypeStruct((B, S, B_WIDTH), BF16),
        grid=(B, B_HEADS),
        in_specs=[tok(), tok(), tok(), tok(),
                  pl.BlockSpec((lb_logits.shape[0], B_KDIM), lambda b, h: (0, h)),
                  pl.BlockSpec((1, B_VDIM), lambda b, h: (0, h)),
                  pl.BlockSpec((HG_ROWS, HG_ROWS), lambda b, h: (0, 0)),
                  pl.BlockSpec(bd.shape, lambda b, h: (0, 0))],
        out_specs=tok(),
        scratch_shapes=[pltpu.VMEM((S, B_KDIM), BF16),
                        pltpu.VMEM((n_ch, B_VDIM, B_KDIM), F32),
                        pltpu.VMEM((n_ch, 8, B_KDIM), F32),
                        pltpu.VMEM((S, B_VDIM), F32)],
        compiler_params=pltpu.CompilerParams(
            dimension_semantics=("parallel", "parallel"),
            vmem_limit_bytes=V7X_VMEM_LIMIT_BYTES),
        name="hgrn2",
    )(qb, fb, ib, szb, lb_logits, gain, tri, bd)


def _out_kernel(x_ref, oat_ref, ob_ref, wgm_ref, bgm_ref, wa_ref, wb_ref,
                wo_ref, lng_ref, lnb_ref, o_ref, *, alpha):
    x = x_ref[0]
    sg = _sigmoid(_dot(x.astype(BF16), wgm_ref[...]) + bgm_ref[...])
    ya = lax.dot_general(oat_ref[0], wa_ref[...], (((0,), (0,)), ((), ())), preferred_element_type=F32)
    y = sg[:, :D_MODEL] * ya + sg[:, D_MODEL:] * _dot(ob_ref[0], wb_ref[...])
    r = alpha * x + _dot(y.astype(BF16), wo_ref[...])
    mu = jnp.mean(r, axis=-1, keepdims=True)
    d = r - mu
    var = jnp.mean(d * d, axis=-1, keepdims=True)
    o_ref[0] = d * lax.rsqrt(var + NORM_EPS) * lng_ref[...] + lnb_ref[...]


def _merge_out(x, oat, ob, wgm, bgm, wa, wb, wo, lng, lnb, alpha):
    B, S, D = x.shape
    T = OUT_ROWS
    full = lambda a: pl.BlockSpec(a.shape, lambda b, s: (0,) * a.ndim)
    rows = lambda w: pl.BlockSpec((1, T, w), lambda b, s: (b, s, 0))
    return pl.pallas_call(
        functools.partial(_out_kernel, alpha=alpha),
        out_shape=jax.ShapeDtypeStruct((B, S, D), x.dtype),
        grid=(B, S // T),
        in_specs=[rows(D), pl.BlockSpec((1, A_WIDTH, T), lambda b, s: (b, 0, s)), rows(B_WIDTH),
                  full(wgm), full(bgm), full(wa), full(wb), full(wo), full(lng), full(lnb)],
        out_specs=rows(D),
        compiler_params=pltpu.CompilerParams(
            dimension_semantics=("parallel", "parallel"),
            vmem_limit_bytes=V7X_VMEM_LIMIT_BYTES),
        name="merge_out",
    )(x, oat, ob, wgm, bgm, wa, wb, wo, lng, lnb)


def _rope_tables(S):
    inv = ROPE_THETA ** (-jnp.arange(0, HEAD_DIM, 2, dtype=F32) / HEAD_DIM)
    ang = jnp.arange(S, dtype=F32)[:, None] * inv[None, :]
    cos = jnp.concatenate([jnp.cos(ang), jnp.cos(ang)], axis=-1)
    sin = jnp.concatenate([jnp.sin(ang), jnp.sin(ang)], axis=-1)
    first = (jnp.arange(HEAD_DIM) < HEAD_DIM // 2)[None, :]
    sina = jnp.where(first, -sin, 0.0)
    sinb = jnp.where(first, 0.0, sin)
    tile = lambda a: jnp.concatenate([a] * A_GROUPS, axis=-1)
    return cos.T, sin.T, tile(cos), tile(sina), tile(sinb)


def _overlap_t(S):
    n_cmp = S // CMP_STRIDE
    n_slc = S // SLC_BLOCK
    cs = np.arange(n_cmp)[None, :] * CMP_STRIDE
    ss = np.arange(n_slc)[:, None] * SLC_BLOCK
    ov = (cs < ss + SLC_BLOCK) & (cs + CMP_BLOCK > ss) & (np.arange(n_cmp)[None, :] < n_cmp - 1)
    return jnp.asarray(ov, dtype=BF16)


def _block_tri():
    r = np.arange(HG_ROWS)
    return jnp.asarray((r[:, None] // CHUNK == r[None, :] // CHUNK) & (r[None, :] <= r[:, None]), dtype=BF16)


def _block_diag():
    r = np.arange(HG_ROWS)[:, None] // CHUNK
    c = np.arange(HG_ROWS // CHUNK * B_KDIM)[None, :] // B_KDIM
    return jnp.asarray(r == c, dtype=BF16)


def _layer(x, l, w_in, b_in, pe_k, w_k1, w_k2, pe_v, w_v1, w_v2, lb_logits, norm_g,
           w_a, w_b, w_o, ln_g, ln_b):
    B, S, D = x.shape
    alpha = (2 * DEPTH) ** 0.25
    o = _OFF
    wsl = lambda i: w_in[:, o[i]:o[i + 1]]
    bsl = lambda i: b_in[o[i]:o[i + 1]]
    kvw, kvb = wsl(1), bsl(1)
    kv_w = lambda j: kvw[:, j * KV_WIDTH:(j + 1) * KV_WIDTH]
    kv_b = lambda j: kvb[j * KV_WIDTH:(j + 1) * KV_WIDTH]
    gw, gb = wsl(2), bsl(2)
    gidx = np.zeros((A_GROUPS, GATE_ROWS), np.int32)
    gmask = np.zeros((A_GROUPS, GATE_ROWS), np.float32)
    for g in range(A_GROUPS):
        for br in range(3):
            for h in range(A_HPG):
                gidx[g, br * A_HPG + h] = (g * A_HPG + h) * 3 + br
                gmask[g, br * A_HPG + h] = 1.0
    gidx, gmask = gidx.reshape(-1), gmask.reshape(-1)
    gw_t = gw[:, gidx] * gmask[None, :]
    gb_t = gb[gidx] * gmask

    wt = jnp.concatenate([wsl(0), kv_w(3), kv_w(5), gw_t, wsl(3)], axis=1).T.astype(BF16)
    bt = jnp.concatenate([bsl(0), kv_b(3), kv_b(5), gb_t, bsl(3)])[:, None]
    wn = jnp.concatenate([kv_w(0), kv_w(1), kv_w(2), kv_w(4), wsl(4), wsl(5), wsl(6), wsl(7)],
                         axis=1).astype(BF16)
    bn = jnp.concatenate([kv_b(0), kv_b(1), kv_b(2), kv_b(4), bsl(4), bsl(5), bsl(6), bsl(7)])[None, :]
    cost, sint, cos, sina, sinb = _rope_tables(S)

    (qr, qn, vts, vtw, gt, szat, kcmp, vcmp, ks, kw, qb, fb, ib, szb) = _project(
        x, wn, bn, wt, bt, cost, sint, cos, sina, sinb)

    half = CMP_STRIDE * HEAD_DIM

    def w1_planes(w1):
        both = jnp.concatenate([w1[:half], w1[half:]], axis=1).reshape(CMP_STRIDE, HEAD_DIM, 2 * CMP_HIDDEN)
        z = jnp.zeros_like(both)
        return jnp.concatenate([jnp.concatenate([both, z], axis=2),
                                jnp.concatenate([z, both], axis=2)], axis=1).astype(BF16)

    kc, vct = _compress(kcmp, vcmp,
                        w1_planes(w_k1), w_k1, pe_k.reshape(-1, 1), w_k2.astype(BF16),
                        w1_planes(w_v1), w_v1, pe_v.reshape(-1, 1), w_v2.T.astype(BF16))

    oat = _nsa(qr, qn, gt, szat, kc, vct, ks, kw, vts, vtw, _overlap_t(S))
    ob = _hgrn(qb, fb, ib, szb, lb_logits, norm_g[None, :], _block_tri(), _block_diag(), l)

    wgm = jnp.concatenate([wsl(8), wsl(9)], axis=1).astype(BF16)
    bgm = jnp.concatenate([bsl(8), bsl(9)])[None, :]
    return _merge_out(x, oat, ob, wgm, bgm, w_a.astype(BF16), w_b.astype(BF16),
                      w_o.astype(BF16), ln_g[None, :], ln_b[None, :], alpha)


@jax.jit
def kernel(x, w_in, b_in, pe_cmp_k, w_cmp_k1, w_cmp_k2, pe_cmp_v, w_cmp_v1, w_cmp_v2,
           hgrn_lb_logits, hgrn_norm_g, w_branch_a, w_branch_b, w_out, ln_g, ln_b):
    B, S, D = x.shape
    assert D == D_MODEL and S % KV_CHUNK == 0 and S % PROJ_ROWS == 0 and S >= WIN_KEYS
    assert S % OUT_ROWS == 0 and S % (HG_ROWS * HG_GROUPS) == 0 and (S // KV_CHUNK) * KV_CHUNK == S
    for l in range(DEPTH):
        x = _layer(x, l, w_in[l], b_in[l], pe_cmp_k[l], w_cmp_k1[l], w_cmp_k2[l],
                   pe_cmp_v[l], w_cmp_v1[l], w_cmp_v2[l], hgrn_lb_logits, hgrn_norm_g[l],
                   w_branch_a[l], w_branch_b[l], w_out[l], ln_g[l], ln_b[l])
    return x
```

```python
import functools
import math

import numpy as np
import jax
import jax.numpy as jnp
from jax import lax
from jax.experimental import pallas as pl
from jax.experimental.pallas import tpu as pltpu

D_MODEL = 1024
DEPTH = 1
A_HEADS = 8
A_GROUPS = 2
A_HPG = A_HEADS // A_GROUPS
HEAD_DIM = 64
A_WIDTH = A_HEADS * HEAD_DIM
KV_WIDTH = A_GROUPS * HEAD_DIM
CMP_BLOCK = 32
CMP_STRIDE = 16
CMP_HIDDEN = 128
SLC_BLOCK = 64
SLC_TOP_N = 16
WINDOW = 512
ROPE_THETA = 10000.0
FORCE_SCORE = 1e30
B_HEADS = 4
B_KDIM = 128
B_VDIM = 128
B_FWIDTH = B_HEADS * B_KDIM
B_WIDTH = B_HEADS * B_VDIM
CHUNK = 64
NORM_EPS = 1e-5

IN_SPLITS = (A_WIDTH, 6 * KV_WIDTH, A_HEADS * 3, A_WIDTH, B_FWIDTH, B_FWIDTH,
             B_WIDTH, B_WIDTH, D_MODEL, D_MODEL)
_OFF = tuple(int(v) for v in np.cumsum((0,) + IN_SPLITS))

V7X_VMEM_LIMIT_BYTES = 56 * 1024 * 1024
PROJ_ROWS = 512
Q_TILE = 128
KV_CHUNK = 512
IN_FLIGHT = 4
AHEAD = 2
MAX_JOBS_PER_TRIP = 16
WIN_KEYS = WINDOW + Q_TILE
HG_ROWS = 256
OUT_ROWS = 512
GATE_COLS = 256
YA_COLS = 512
TAIL_SPLIT = 2
MASKED = -1e30
LOG2E = math.log2(math.e)
GATE_ROWS = 16
V_ROWS = HEAD_DIM + 16

F32 = jnp.float32
BF16 = jnp.bfloat16


def _dot(a, b):
    return jnp.dot(a, b, preferred_element_type=F32)


def _dot_nt(a, b):
    return lax.dot_general(a, b, (((1,), (1,)), ((), ())), preferred_element_type=F32)


def _sigmoid(x):
    return 1.0 / (1.0 + jnp.exp(-x))


def _silu(x):
    return x * _sigmoid(x)


def _proj_kernel(x_ref, wn_ref, bn_ref, wt_ref, bt_ref, cost_ref, sint_ref,
                 cos_ref, sina_ref, sinb_ref,
                 qr_ref, qn_ref, vts_ref, vtw_ref, gt_ref, szat_ref, kc_ref, vc_ref,
                 ks_ref, kw_ref, qb_ref, fb_ref, ib_ref, szb_ref, cmp_s):
    xb = x_ref[0].astype(BF16)
    scale = HEAD_DIM ** -0.5 * LOG2E

    ht = _dot_nt(wt_ref[...], xb) + bt_ref[...]
    cost = cost_ref[...]
    sint = sint_ref[...]
    half = HEAD_DIM // 2
    for h in range(A_HEADS):
        blk = ht[h * HEAD_DIM:(h + 1) * HEAD_DIM]
        rot = jnp.concatenate([-blk[half:], blk[:half]], axis=0)
        qr_ref[0, h * HEAD_DIM:(h + 1) * HEAD_DIM, :] = ((blk * cost + rot * sint) * scale).astype(BF16)
        qn_ref[0, h * HEAD_DIM:(h + 1) * HEAD_DIM, :] = (blk * scale).astype(BF16)
    o = A_WIDTH
    ones_rows = jnp.where(lax.broadcasted_iota(jnp.int32, (V_ROWS - HEAD_DIM, ht.shape[1]), 0) == 0, 1.0, 0.0)
    for g in range(A_GROUPS):
        for ref, base in ((vts_ref, o), (vtw_ref, o + KV_WIDTH)):
            rows_g = ht[base + g * HEAD_DIM:base + (g + 1) * HEAD_DIM]
            ref[0, g] = jnp.concatenate([rows_g, ones_rows], axis=0).astype(BF16)
    gt_ref[0] = _sigmoid(ht[o + 2 * KV_WIDTH:o + 2 * KV_WIDTH + 2 * GATE_ROWS])
    o += 2 * KV_WIDTH + 2 * GATE_ROWS
    szat_ref[0] = _silu(ht[o:o + A_WIDTH]).astype(BF16)

    def cols(lo, hi):
        return _dot(xb, wn_ref[:, lo:hi]) + bn_ref[:, lo:hi]

    kv = cols(0, 4 * KV_WIDTH)
    cmp_s[0] = kv[:, 0:KV_WIDTH]
    cmp_s[1] = kv[:, KV_WIDTH:2 * KV_WIDTH]
    pieces = cmp_s.shape[1] // CMP_STRIDE
    for t in range(CMP_STRIDE):
        kc_ref[0, t] = cmp_s[0, pl.ds(t, pieces, stride=CMP_STRIDE), :].astype(BF16)
        vc_ref[0, t] = cmp_s[1, pl.ds(t, pieces, stride=CMP_STRIDE), :].astype(BF16)
    cos = cos_ref[...]
    sina = sina_ref[...]
    sinb = sinb_ref[...]

    def rope_rows(k):
        return (k * cos + pltpu.roll(k, 128 - half, axis=1) * sina + pltpu.roll(k, half, axis=1) * sinb)

    ks = rope_rows(kv[:, 2 * KV_WIDTH:3 * KV_WIDTH])
    rows = ks.shape[0]
    lane = lax.broadcasted_iota(jnp.int32, (rows, KV_WIDTH), 1)
    pos = pl.program_id(1) * rows + lax.broadcasted_iota(jnp.int32, (rows, KV_WIDTH), 0)
    blocks_per_chunk = KV_CHUNK // SLC_BLOCK
    ind = jnp.where(lane == HEAD_DIM + (pos // SLC_BLOCK) % blocks_per_chunk, 1.0, 0.0)
    ks_ref[0, 0] = (jnp.where(lane < HEAD_DIM, ks, 0.0) + ind).astype(BF16)
    ks_ref[0, 1] = (jnp.where(lane < HEAD_DIM, pltpu.roll(ks, HEAD_DIM, axis=1), 0.0) + ind).astype(BF16)
    kw = rope_rows(kv[:, 3 * KV_WIDTH:4 * KV_WIDTH])
    kw_ref[0, 0] = jnp.where(lane < HEAD_DIM, kw, 0.0).astype(BF16)
    kw_ref[0, 1] = jnp.where(lane < HEAD_DIM, pltpu.roll(kw, HEAD_DIM, axis=1), 0.0).astype(BF16)
    o = 4 * KV_WIDTH
    qb_ref[0] = cols(o, o + B_FWIDTH).astype(BF16)
    o += B_FWIDTH
    fb_ref[0] = cols(o, o + B_FWIDTH)
    o += B_FWIDTH
    ib_ref[0] = cols(o, o + B_WIDTH).astype(BF16)
    o += B_WIDTH
    szb_ref[0] = _silu(cols(o, o + B_WIDTH)).astype(BF16)


def _project(x, wn, bn, wt, bt, cost, sint, cos, sina, sinb):
    B, S, D = x.shape
    T = PROJ_ROWS
    n_t = wt.shape[0]
    n_n = wn.shape[1]
    full = lambda shape: pl.BlockSpec(shape, lambda b, s: (0,) * len(shape))
    row_out = lambda w: pl.BlockSpec((1, T, w), lambda b, s: (b, s, 0))
    col_out = lambda r: pl.BlockSpec((1, r, T), lambda b, s: (b, 0, s))
    sds = jax.ShapeDtypeStruct
    out_shape = (
        sds((B, A_WIDTH, S), BF16), sds((B, A_WIDTH, S), BF16),
        sds((B, A_GROUPS, V_ROWS, S), BF16), sds((B, A_GROUPS, V_ROWS, S), BF16),
        sds((B, 2 * GATE_ROWS, S), F32),
        sds((B, A_WIDTH, S), BF16),
        sds((B, CMP_STRIDE, S // CMP_STRIDE, KV_WIDTH), BF16),
        sds((B, CMP_STRIDE, S // CMP_STRIDE, KV_WIDTH), BF16),
        sds((B, A_GROUPS, S, KV_WIDTH), BF16), sds((B, A_GROUPS, S, KV_WIDTH), BF16),
        sds((B, S, B_FWIDTH), BF16), sds((B, S, B_FWIDTH), F32),
        sds((B, S, B_WIDTH), BF16), sds((B, S, B_WIDTH), BF16),
    )
    out_specs = (
        col_out(A_WIDTH), col_out(A_WIDTH),
        pl.BlockSpec((1, A_GROUPS, V_ROWS, T), lambda b, s: (b, 0, 0, s)),
        pl.BlockSpec((1, A_GROUPS, V_ROWS, T), lambda b, s: (b, 0, 0, s)),
        col_out(2 * GATE_ROWS), col_out(A_WIDTH),
        pl.BlockSpec((1, CMP_STRIDE, T // CMP_STRIDE, KV_WIDTH), lambda b, s: (b, 0, s, 0)),
        pl.BlockSpec((1, CMP_STRIDE, T // CMP_STRIDE, KV_WIDTH), lambda b, s: (b, 0, s, 0)),
        pl.BlockSpec((1, A_GROUPS, T, KV_WIDTH), lambda b, s: (b, 0, s, 0)),
        pl.BlockSpec((1, A_GROUPS, T, KV_WIDTH), lambda b, s: (b, 0, s, 0)),
        row_out(B_FWIDTH), row_out(B_FWIDTH), row_out(B_WIDTH), row_out(B_WIDTH),
    )
    in_specs = [
        pl.BlockSpec((1, T, D), lambda b, s: (b, s, 0)),
        full((D, n_n)), full((1, n_n)), full((n_t, D)), full((n_t, 1)),
        pl.BlockSpec((HEAD_DIM, T), lambda b, s: (0, s)),
        pl.BlockSpec((HEAD_DIM, T), lambda b, s: (0, s)),
        pl.BlockSpec((T, KV_WIDTH), lambda b, s: (s, 0)),
        pl.BlockSpec((T, KV_WIDTH), lambda b, s: (s, 0)),
        pl.BlockSpec((T, KV_WIDTH), lambda b, s: (s, 0)),
    ]
    return pl.pallas_call(
        _proj_kernel, out_shape=out_shape, grid=(B, S // T),
        in_specs=in_specs, out_specs=out_specs,
        scratch_shapes=[pltpu.VMEM((2, T, KV_WIDTH), F32)],
        compiler_params=pltpu.CompilerParams(
            dimension_semantics=("parallel", "parallel"),
            vmem_limit_bytes=V7X_VMEM_LIMIT_BYTES),
        name="in_proj",
    )(x, wn, bn, wt, bt, cost, sint, cos, sina, sinb)


def _compress_kernel(ck_ref, cv_ref, w1k_ref, w1kf_ref, pek_ref, w2k_ref,
                     w1v_ref, w1vf_ref, pev_ref, w2vt_ref, kc_ref, vct_ref):
    nc = ck_ref.shape[2]
    hid = CMP_HIDDEN

    def hidden(c_ref, w1_ref, w1f_ref, pe_ref):
        a = _dot(c_ref[0, 0], w1_ref[0])
        for t in range(1, CMP_STRIDE):
            a = a + _dot(c_ref[0, t], w1_ref[t])
        pe_term = jnp.sum(w1f_ref[...] * pe_ref[...], axis=0, keepdims=True)
        out = []
        for g in range(A_GROUPS):
            lo = a[:, 2 * g * hid:(2 * g + 1) * hid]
            hi = pltpu.roll(a[:, (2 * g + 1) * hid:(2 * g + 2) * hid], nc - 1, axis=0)
            out.append(_silu(lo + hi + pe_term).astype(BF16))
        return out

    hk = hidden(ck_ref, w1k_ref, w1kf_ref, pek_ref)
    hv = hidden(cv_ref, w1v_ref, w1vf_ref, pev_ref)
    for g in range(A_GROUPS):
        kc_ref[0, g] = _dot(hk[g], w2k_ref[...]).astype(BF16)
        vct_ref[0, g] = _dot_nt(w2vt_ref[...], hv[g]).astype(BF16)


def _compress(ck, cv, w1k, w1kf, pek, w2k, w1v, w1vf, pev, w2vt):
    B, P, NC, W = ck.shape
    full = lambda a: pl.BlockSpec(a.shape, lambda b: (0,) * a.ndim)
    blk = pl.BlockSpec((1, P, NC, W), lambda b: (b, 0, 0, 0))
    return pl.pallas_call(
        _compress_kernel,
        out_shape=(jax.ShapeDtypeStruct((B, A_GROUPS, NC, HEAD_DIM), BF16),
                   jax.ShapeDtypeStruct((B, A_GROUPS, HEAD_DIM, NC), BF16)),
        grid=(B,),
        in_specs=[blk, blk, full(w1k), full(w1kf), full(pek), full(w2k),
                  full(w1v), full(w1vf), full(pev), full(w2vt)],
        out_specs=(pl.BlockSpec((1, A_GROUPS, NC, HEAD_DIM), lambda b: (b, 0, 0, 0)),
                   pl.BlockSpec((1, A_GROUPS, HEAD_DIM, NC), lambda b: (b, 0, 0, 0))),
        compiler_params=pltpu.CompilerParams(
            dimension_semantics=("parallel",),
            vmem_limit_bytes=V7X_VMEM_LIMIT_BYTES),
        name="kv_compress",
    )(ck, cv, w1k, w1kf, pek, w2k, w1v, w1vf, pev, w2vt)


def _nsa_kernel(jc_ref, jq_ref, qr_ref, qn_ref, gt_ref, szat_ref, kc_ref, vct_ref, ks_ref, kw_ref,
                vts_ref, vtw_ref, ovl_ref, o_ref,
                bias_s, wq_s, m_s, acc_s, ocw_s, sbuf_s, smax_s, dbuf_s, dmax_s, scb_s, swb_s, wmax_s,
                *, seq, n_sel, n_jobs, jobs_per_trip):
    nc = kc_ref.shape[2]
    n_slc = seq // SLC_BLOCK
    n_q = seq // Q_TILE
    n_ck = seq // KV_CHUNK
    lanes = A_HPG * Q_TILE
    bpc = KV_CHUNK // SLC_BLOCK
    tpc = KV_CHUNK // Q_TILE
    win_tiles = WINDOW // Q_TILE
    last = n_q - 1

    def stack_heads(ref, q0):
        return jnp.concatenate(
            [ref[0, h * HEAD_DIM:(h + 1) * HEAD_DIM, pl.ds(q0, Q_TILE)] for h in range(A_HPG)], axis=1)

    def per_head(a):
        return jnp.concatenate([a] * A_HPG, axis=1)

    def gate_row(gt, branch):
        return jnp.concatenate(
            [gt[branch * A_HPG + h:branch * A_HPG + h + 1, :] for h in range(A_HPG)], axis=1)

    def tile_start(qi):
        return pl.multiple_of(qi * Q_TILE, Q_TILE)

    lane_q = lax.broadcasted_iota(jnp.int32, (1, lanes), 1) & (Q_TILE - 1)
    r_sq = lax.broadcasted_iota(jnp.int32, (Q_TILE, lanes), 0)
    causal_sq = r_sq <= lane_q
    lower_sq = r_sq > lane_q
    q_pad = jnp.zeros((KV_WIDTH - HEAD_DIM, lanes), BF16)

    def cmp_keys(rb):
        return min(nc, -(-(rb * KV_CHUNK // CMP_STRIDE) // 128) * 128)

    def cmp_scores(qi, rb):
        return _dot(kc_ref[0, 0, 0:cmp_keys(rb), :], stack_heads(qn_ref, tile_start(qi)))

    def select_tile(qi, rb, sc):
        nk = cmp_keys(rb)
        n_causal = (qi + 1) * Q_TILE // SLC_BLOCK
        s0 = tile_start(qi)
        t_row = s0 + lax.broadcasted_iota(jnp.int32, (1, Q_TILE), 1)
        wq_s[qi] = stack_heads(qr_ref, s0)

        n_end = lax.broadcasted_iota(jnp.int32, (nk, Q_TILE), 0) * CMP_STRIDE + (CMP_BLOCK - 1)
        sc = sc + per_head(jnp.where(n_end <= t_row, 0.0, MASKED))
        mc = jnp.max(sc, axis=0, keepdims=True)
        pc = jnp.exp2(sc - mc)
        lc = jnp.sum(pc, axis=0, keepdims=True)
        pc = pc * jnp.where(s0 + lane_q >= CMP_BLOCK - 1, 1.0 / lc, 0.0)
        o_cmp = _dot(vct_ref[0, 0, :, 0:nk], pc.astype(BF16))

        ps = pc[:, 0:Q_TILE]
        for h in range(1, A_HPG):
            ps = ps + pc[:, h * Q_TILE:(h + 1) * Q_TILE]
        p_hi = ps.astype(BF16)
        p_lo = (ps - p_hi.astype(F32)).astype(BF16)
        imp2 = _dot(ovl_ref[0:8 * rb, 0:nk], jnp.concatenate([p_hi, p_lo], axis=1))
        imp = imp2[:, :Q_TILE] + imp2[:, Q_TILE:]

        j_idx = lax.broadcasted_iota(jnp.int32, (8 * rb, Q_TILE), 0)
        cur = t_row // SLC_BLOCK
        sel = j_idx <= cur
        if 8 * rb > n_sel:
            forced = (j_idx == 0) | (j_idx == cur) | (j_idx == cur - 1)
            val = jnp.where(j_idx > cur, -1.0, jnp.where(forced, FORCE_SCORE, imp))
            rows = [val[8 * j:8 * j + 8] for j in range(rb)]
            cnt = [jnp.zeros((8, Q_TILE), F32) for _ in range(rb)]
            jl = lax.broadcasted_iota(jnp.int32, (8, Q_TILE), 0)
            for kb in range(rb):
                for kl in range(8):
                    if 8 * kb + kl >= n_causal:
                        continue
                    row = rows[kb][kl:kl + 1, :]
                    for jb in range(rb):
                        if jb < kb:
                            beats = row > rows[jb]
                        elif jb > kb:
                            beats = row >= rows[jb]
                        else:
                            beats = (row > rows[jb]) | ((row == rows[jb]) & (jl > kl))
                        cnt[jb] = cnt[jb] + jnp.where(beats, 1.0, 0.0)
            sel = sel & (jnp.concatenate(cnt, axis=0) < n_sel)
        bias = per_head(jnp.where(sel, 0.0, MASKED))
        pad = jnp.zeros((bpc, lanes), F32)
        for c in range(rb):
            bias_s[qi, c] = jnp.concatenate([bias[c * bpc:(c + 1) * bpc], pad], axis=0).astype(BF16)

        gt = gt_ref[0, :, pl.ds(s0, Q_TILE)]
        ocw_s[qi] = gate_row(gt, 0) * o_cmp
        m_s[qi] = jnp.full((8, lanes), MASKED, F32)
        acc_s[qi] = jnp.zeros((V_ROWS, lanes), F32)

    scb_s[0, 0:cmp_keys(1), :] = cmp_scores(0, 1)
    for qi in range(n_q):
        rb = qi // tpc + 1
        nk = cmp_keys(rb)
        if qi + 1 < n_q:
            rb_next = (qi + 1) // tpc + 1
            scb_s[(qi + 1) & 1, 0:cmp_keys(rb_next), :] = cmp_scores(qi + 1, rb_next)
        select_tile(qi, rb, scb_s[qi & 1, 0:nk, :])

    def win_finish(qi, sw, mw, w0, keys=WIN_KEYS):
        pw = jnp.exp2(sw - mw)
        ow = _dot(vtw_ref[0, 0, :, pl.ds(w0, keys)], pw.astype(BF16))
        o_win = ow[0:HEAD_DIM] * (1.0 / ow[HEAD_DIM:HEAD_DIM + 1])
        gt = gt_ref[0, :, pl.ds(tile_start(qi), Q_TILE)]
        ocw_s[qi] = ocw_s[qi] + gate_row(gt, 2) * o_win

    for qi in range(min(win_tiles, n_q)):
        keys = (qi + 1) * Q_TILE
        wq = jnp.concatenate([wq_s[qi], q_pad], axis=0)
        sw = _dot(kw_ref[0, 0, 0:keys, :], wq)
        sw = jnp.concatenate(([sw[:keys - Q_TILE]] if qi else [])
                             + [jnp.where(causal_sq, sw[keys - Q_TILE:], MASKED)], axis=0)
        win_finish(qi, sw, jnp.max(sw, axis=0, keepdims=True), 0, keys)

    def win_scores(qi, slot):
        w0 = pl.multiple_of(tile_start(qi) - WINDOW, Q_TILE)
        wq = jnp.concatenate([wq_s[qi], q_pad], axis=0)
        sw = _dot(kw_ref[0, 0, pl.ds(w0, WIN_KEYS), :], wq)
        sw = jnp.concatenate([jnp.where(lower_sq, sw[:Q_TILE], MASKED), sw[Q_TILE:WINDOW],
                              jnp.where(causal_sq, sw[WINDOW:], MASKED)], axis=0)
        swb_s[slot] = sw
        wmax_s[slot] = jnp.broadcast_to(jnp.max(sw, axis=0, keepdims=True), (8, lanes))

    def win_group(i, _):
        q0 = win_tiles + IN_FLIGHT * i
        for u in range(IN_FLIGHT):
            win_scores(jnp.minimum(q0 + u + AHEAD, last), (u + AHEAD) % IN_FLIGHT)
            win_finish(q0 + u, swb_s[u], wmax_s[u][0:1],
                       pl.multiple_of(tile_start(q0 + u) - WINDOW, Q_TILE))
        return 0

    if n_q > win_tiles:
        for u in range(AHEAD):
            win_scores(jnp.minimum(win_tiles + u, last), u)
        lax.fori_loop(0, (n_q - win_tiles) // IN_FLIGHT, win_group, 0)

    wq_pad = jnp.zeros((KV_WIDTH - HEAD_DIM - 2 * bpc, lanes), BF16)

    def scores(c, qi, diag):
        rows = KV_CHUNK if diag is None else (diag + 1) * Q_TILE
        k0 = pl.multiple_of(c * KV_CHUNK, KV_CHUNK)
        wq = jnp.concatenate([wq_s[qi], bias_s[qi, c], wq_pad], axis=0)
        s = _dot(ks_ref[0, 0, pl.ds(k0, rows), :], wq)
        if diag is not None:
            head = [s[:rows - Q_TILE]] if diag > 0 else []
            s = jnp.concatenate(head + [jnp.where(causal_sq, s[rows - Q_TILE:], MASKED)], axis=0)
        return s, jnp.broadcast_to(jnp.max(s, axis=0, keepdims=True), (8, lanes))

    def absorb(c, qi, s, s_max):
        rows = s.shape[0]
        k0 = pl.multiple_of(c * KV_CHUNK, KV_CHUNK)
        m_old = m_s[qi][0:1]
        m_new = jnp.maximum(m_old, s_max[0:1])
        alpha = jnp.exp2(m_old - m_new)
        p = jnp.exp2(s - m_new)
        m_s[qi] = jnp.broadcast_to(m_new, (8, lanes))
        acc_s[qi] = alpha * acc_s[qi] + _dot(vts_ref[0, 0, :, pl.ds(k0, rows)], p.astype(BF16))

    def full_scores(j, slot):
        sbuf_s[slot], smax_s[slot] = scores(jc_ref[j], jq_ref[j], None)

    def full_group(i, _):
        j0 = jobs_per_trip * i
        for u in range(jobs_per_trip):
            full_scores(j0 + u + AHEAD, (u + AHEAD) % IN_FLIGHT)
            absorb(jc_ref[j0 + u], jq_ref[j0 + u], sbuf_s[u % IN_FLIGHT], smax_s[u % IN_FLIGHT])
        return 0

    if n_jobs:
        for u in range(AHEAD):
            full_scores(u, u)
        lax.fori_loop(0, n_jobs // jobs_per_trip, full_group, 0)

    def diag_scores(c, r):
        dbuf_s[r, 0:(r + 1) * Q_TILE, :], dmax_s[r] = scores(c, c * tpc + r, r)

    def diag_chunk(c, _):
        nxt = jnp.minimum(c + 1, n_ck - 1)
        for r in range(tpc):
            ahead = r + AHEAD
            diag_scores(c if ahead < tpc else nxt, ahead % tpc)
            absorb(c, c * tpc + r, dbuf_s[r, 0:(r + 1) * Q_TILE, :], dmax_s[r])
        return 0

    for r in range(AHEAD):
        diag_scores(0, r)
    lax.fori_loop(0, n_ck, diag_chunk, 0)

    def finish(qi, _):
        s0 = tile_start(qi)
        gt = gt_ref[0, :, pl.ds(s0, Q_TILE)]
        acc = acc_s[qi]
        ot = ocw_s[qi] + gate_row(gt, 1) * acc[0:HEAD_DIM] * (1.0 / acc[HEAD_DIM:HEAD_DIM + 1])
        ot = (ot * stack_heads(szat_ref, s0).astype(F32)).astype(o_ref.dtype)
        for h in range(A_HPG):
            o_ref[0, h * HEAD_DIM:(h + 1) * HEAD_DIM, pl.ds(s0, Q_TILE)] = ot[:, h * Q_TILE:(h + 1) * Q_TILE]
        return 0

    lax.fori_loop(0, n_q, finish, 0)


def _nsa(qr, qn, gt, szat, kc, vct, ks, kw, vts, vtw, ovl):
    B, _, S = qr.shape
    NC = kc.shape[2]
    n_slc = S // SLC_BLOCK
    n_sel = min(SLC_TOP_N, n_slc)
    n_q = S // Q_TILE
    n_ck = S // KV_CHUNK
    tpc = KV_CHUNK // Q_TILE
    gw = A_HPG * HEAD_DIM
    lanes = A_HPG * Q_TILE
    jobs = [(c, q) for c in range(n_ck) for q in range((c + 1) * tpc, n_q)]
    n_jobs = len(jobs)
    jobs_per_trip = max(j for j in range(IN_FLIGHT, MAX_JOBS_PER_TRIP + 1, IN_FLIGHT) if n_jobs % j == 0)
    assert tpc == IN_FLIGHT and AHEAD < IN_FLIGHT
    jobs = jobs + [jobs[-1] if jobs else (0, 0)] * AHEAD
    jc = jnp.asarray([j[0] for j in jobs], jnp.int32)
    jq = jnp.asarray([j[1] for j in jobs], jnp.int32)
    kernel = functools.partial(_nsa_kernel, seq=S, n_sel=n_sel, n_jobs=n_jobs, jobs_per_trip=jobs_per_trip)
    per_group = lambda rows, cols: pl.BlockSpec((1, 1, rows, cols), lambda b, g, *_: (b, g, 0, 0))
    in_specs = [
        pl.BlockSpec((1, gw, S), lambda b, g, *_: (b, g, 0)),
        pl.BlockSpec((1, gw, S), lambda b, g, *_: (b, g, 0)),
        pl.BlockSpec((1, GATE_ROWS, S), lambda b, g, *_: (b, g, 0)),
        pl.BlockSpec((1, gw, S), lambda b, g, *_: (b, g, 0)),
        per_group(NC, HEAD_DIM), per_group(HEAD_DIM, NC),
        per_group(S, KV_WIDTH), per_group(S, KV_WIDTH),
        per_group(V_ROWS, S), per_group(V_ROWS, S),
        pl.BlockSpec((n_slc, NC), lambda b, g, *_: (0, 0)),
    ]
    scratch = [
        pltpu.VMEM((n_q, n_ck, 2 * (KV_CHUNK // SLC_BLOCK), lanes), BF16),
        pltpu.VMEM((n_q, HEAD_DIM, lanes), BF16),
        pltpu.VMEM((n_q, 8, lanes), F32),
        pltpu.VMEM((n_q, V_ROWS, lanes), F32),
        pltpu.VMEM((n_q, HEAD_DIM, lanes), F32),
        pltpu.VMEM((IN_FLIGHT, KV_CHUNK, lanes), F32),
        pltpu.VMEM((IN_FLIGHT, 8, lanes), F32),
        pltpu.VMEM((tpc, KV_CHUNK, lanes), F32),
        pltpu.VMEM((tpc, 8, lanes), F32),
        pltpu.VMEM((2, NC, lanes), F32),
        pltpu.VMEM((IN_FLIGHT, WIN_KEYS, lanes), F32),
        pltpu.VMEM((IN_FLIGHT, 8, lanes), F32),
    ]
    return pl.pallas_call(
        kernel,
        out_shape=jax.ShapeDtypeStruct((B, A_WIDTH, S), BF16),
        grid_spec=pltpu.PrefetchScalarGridSpec(
            num_scalar_prefetch=2,
            grid=(B, A_GROUPS),
            in_specs=in_specs,
            out_specs=pl.BlockSpec((1, gw, S), lambda b, g, *_: (b, g, 0)),
            scratch_shapes=scratch),
        compiler_params=pltpu.CompilerParams(
            dimension_semantics=("parallel", "parallel"),
            vmem_limit_bytes=V7X_VMEM_LIMIT_BYTES),
        name="nsa_attention",
    )(jc, jq, qr, qn, gt, szat, kc, vct, ks, kw, vts, vtw, ovl)


def _hgrn_stages(q_ref, f_ref, i_ref, sz_ref, lb, gain, tri, bd, causal, state_s, h, out):
    T = q_ref.shape[1]
    cpg = HG_ROWS // CHUNK
    lanes = slice(h * B_KDIM, (h + 1) * B_KDIM)
    groups = [slice(gi * HG_ROWS, (gi + 1) * HG_ROWS) for gi in range(T // HG_ROWS)]
    mid = {}

    def decay():
        mid["kk"], mid["b"] = [], []
        for rows in groups:
            fg = lb + (1.0 - lb) * _sigmoid(f_ref[0, rows, lanes])
            logf = jnp.log(fg)
            l_hi = logf.astype(BF16)
            l_lo = (logf - l_hi.astype(F32)).astype(BF16)
            bb = _dot(tri, jnp.concatenate([l_hi, l_lo], axis=1))
            mid["kk"].append(1.0 - fg)
            mid["b"].append(bb[:, :B_KDIM] + bb[:, B_KDIM:])

    def intra():
        mid["qe"], mid["oi"], mid["kv"], mid["dl"] = [], [], [], []
        for rows, kk, b in zip(groups, mid["kk"], mid["b"]):
            b_last = jnp.concatenate(
                [jnp.broadcast_to(b[(c + 1) * CHUNK - 1:(c + 1) * CHUNK, :], (CHUNK, B_KDIM))
                 for c in range(cpg)], axis=0)
            qe = (_silu(q_ref[0, rows, lanes].astype(F32)) * jnp.exp(b)).astype(BF16)
            ke = (kk * jnp.exp(-b)).astype(BF16)
            kd = (kk * jnp.exp(b_last - b)).astype(BF16)
            attn = jnp.where(causal, _dot_nt(qe, ke), 0.0).astype(BF16)
            v = i_ref[0, rows, lanes]
            mid["oi"].append(_dot(attn, v))
            mid["qe"].append(qe)
            kd_bd = jnp.concatenate([kd] * cpg, axis=1) * bd
            kv_all = lax.dot_general(v, kd_bd, (((0,), (0,)), ((), ())), preferred_element_type=F32)
            for c in range(cpg):
                mid["kv"].append(kv_all[:, c * B_KDIM:(c + 1) * B_KDIM])
                mid["dl"].append(jnp.exp(b_last[c * CHUNK:c * CHUNK + 1, :]))

    def inter():
        state = state_s[h]
        outs = []
        for n in range(T // CHUNK):
            gi, c = divmod(n, cpg)
            sl = slice(c * CHUNK, (c + 1) * CHUNK)
            rows = slice(n * CHUNK, (n + 1) * CHUNK)
            o = mid["oi"][gi][sl] + _dot_nt(mid["qe"][gi][sl], state.astype(BF16))
            o = o * lax.rsqrt(jnp.mean(o * o, axis=-1, keepdims=True) + NORM_EPS)
            outs.append((o * gain * sz_ref[0, rows, lanes].astype(F32)).astype(BF16))
            state = mid["dl"][n] * state + mid["kv"][n]
        state_s[h] = state
        out.append(jnp.concatenate(outs, axis=0))

    return [decay, intra, inter]


def _out_kernel(x_ref, oat_ref, q_ref, f_ref, i_ref, sz_ref, lbl_ref, g_ref, tri_ref, bd_ref,
                wgm_ref, bgm_ref, wa_ref, wb_ref, wo_ref, lng_ref, lnb_ref, o_ref, state_s, sg_s, ya_s,
                *, alpha, layer):
    @pl.when(pl.program_id(1) == 0)
    def _():
        state_s[...] = jnp.zeros_like(state_s)

    x = x_ref[0]
    xb = x.astype(BF16)

    def gate_chunk(j):
        def run():
            sg_s[:, j:j + GATE_COLS] = _sigmoid(_dot(xb, wgm_ref[:, j:j + GATE_COLS]) + bgm_ref[:, j:j + GATE_COLS])
        return run

    def ya_chunk(j):
        def run():
            ya_s[:, j:j + YA_COLS] = lax.dot_general(oat_ref[0], wa_ref[:, j:j + YA_COLS], (((0,), (0,)), ((), ())),
                                                     preferred_element_type=F32)
        return run

    fillers = [gate_chunk(j) for j in range(0, 2 * D_MODEL, GATE_COLS)] + \
              [ya_chunk(j) for j in range(0, D_MODEL, YA_COLS)]

    lg = lbl_ref[...]
    e = jnp.exp(lg - jnp.max(lg, axis=0, keepdims=True))
    lb_all = jnp.sum(e[0:layer + 1], axis=0, keepdims=True) / jnp.sum(e, axis=0, keepdims=True)
    tri = tri_ref[...]
    bd = bd_ref[...]
    ri = lax.broadcasted_iota(jnp.int32, (HG_ROWS, HG_ROWS), 0)
    ci = lax.broadcasted_iota(jnp.int32, (HG_ROWS, HG_ROWS), 1)
    causal = (ri // CHUNK == ci // CHUNK) & (ci <= ri)

    obs = []
    stages = []
    for h in range(B_HEADS):
        lanes = slice(h * B_KDIM, (h + 1) * B_KDIM)
        stages += _hgrn_stages(q_ref, f_ref, i_ref, sz_ref, lb_all[:, lanes], g_ref[:, lanes],
                               tri, bd, causal, state_s, h, obs)
    for k, stage in enumerate(stages):
        if k < len(fillers):
            fillers[k]()
        stage()
    for filler in fillers[len(stages):]:
        filler()
    ob = jnp.concatenate(obs, axis=1)

    sub = x.shape[0] // TAIL_SPLIT
    halves = [slice(i * sub, (i + 1) * sub) for i in range(TAIL_SPLIT)]
    ybs = [_dot(ob[r], wb_ref[...]) for r in halves]
    ys = [(sg_s[r, :D_MODEL] * ya_s[r, :] + sg_s[r, D_MODEL:] * yb).astype(BF16) for r, yb in zip(halves, ybs)]
    outs = [_dot(y, wo_ref[...]) for y in ys]
    for r, out in zip(halves, outs):
        res = alpha * x[r] + out
        mu = jnp.mean(res, axis=-1, keepdims=True)
        d = res - mu
        var = jnp.mean(d * d, axis=-1, keepdims=True)
        o_ref[0, r, :] = d * lax.rsqrt(var + NORM_EPS) * lng_ref[...] + lnb_ref[...]


def _merge_out(x, oat, qb, fb, ib, szb, lb_logits, gain, tri, bd, wgm, bgm, wa, wb, wo, lng, lnb, alpha, layer):
    B, S, D = x.shape
    T = OUT_ROWS
    full = lambda a: pl.BlockSpec(a.shape, lambda b, s: (0,) * a.ndim)
    rows = lambda w: pl.BlockSpec((1, T, w), lambda b, s: (b, s, 0))
    return pl.pallas_call(
        functools.partial(_out_kernel, alpha=alpha, layer=layer),
        out_shape=jax.ShapeDtypeStruct((B, S, D), x.dtype),
        grid=(B, S // T),
        in_specs=[rows(D), pl.BlockSpec((1, A_WIDTH, T), lambda b, s: (b, 0, s)),
                  rows(B_FWIDTH), rows(B_FWIDTH), rows(B_WIDTH), rows(B_WIDTH),
                  full(lb_logits), full(gain), full(tri), full(bd),
                  full(wgm), full(bgm), full(wa), full(wb), full(wo), full(lng), full(lnb)],
        out_specs=rows(D),
        scratch_shapes=[pltpu.VMEM((B_HEADS, B_VDIM, B_KDIM), F32),
                        pltpu.VMEM((T, 2 * D), F32),
                        pltpu.VMEM((T, D), F32)],
        compiler_params=pltpu.CompilerParams(
            dimension_semantics=("parallel", "arbitrary"),
            vmem_limit_bytes=V7X_VMEM_LIMIT_BYTES),
        name="hgrn_merge_out",
    )(x, oat, qb, fb, ib, szb, lb_logits, gain, tri, bd, wgm, bgm, wa, wb, wo, lng, lnb)


def _rope_tables(S):
    inv = ROPE_THETA ** (-jnp.arange(0, HEAD_DIM, 2, dtype=F32) / HEAD_DIM)
    ang = jnp.arange(S, dtype=F32)[:, None] * inv[None, :]
    cos = jnp.concatenate([jnp.cos(ang), jnp.cos(ang)], axis=-1)
    sin = jnp.concatenate([jnp.sin(ang), jnp.sin(ang)], axis=-1)
    first = (jnp.arange(HEAD_DIM) < HEAD_DIM // 2)[None, :]
    sina = jnp.where(first, -sin, 0.0)
    sinb = jnp.where(first, 0.0, sin)
    tile = lambda a: jnp.concatenate([a] * A_GROUPS, axis=-1)
    return cos.T, sin.T, tile(cos), tile(sina), tile(sinb)


def _overlap_t(S):
    n_cmp = S // CMP_STRIDE
    n_slc = S // SLC_BLOCK
    cs = np.arange(n_cmp)[None, :] * CMP_STRIDE
    ss = np.arange(n_slc)[:, None] * SLC_BLOCK
    ov = (cs < ss + SLC_BLOCK) & (cs + CMP_BLOCK > ss) & (np.arange(n_cmp)[None, :] < n_cmp - 1)
    return jnp.asarray(ov, dtype=BF16)


def _block_tri():
    r = np.arange(HG_ROWS)
    return jnp.asarray((r[:, None] // CHUNK == r[None, :] // CHUNK) & (r[None, :] <= r[:, None]), dtype=BF16)


def _block_diag():
    r = np.arange(HG_ROWS)[:, None] // CHUNK
    c = np.arange(HG_ROWS // CHUNK * B_KDIM)[None, :] // B_KDIM
    return jnp.asarray(r == c, dtype=BF16)


def _layer(x, l, w_in, b_in, pe_k, w_k1, w_k2, pe_v, w_v1, w_v2, lb_logits, norm_g,
           w_a, w_b, w_o, ln_g, ln_b):
    B, S, D = x.shape
    alpha = (2 * DEPTH) ** 0.25
    o = _OFF
    wsl = lambda i: w_in[:, o[i]:o[i + 1]]
    bsl = lambda i: b_in[o[i]:o[i + 1]]
    kvw, kvb = wsl(1), bsl(1)
    kv_w = lambda j: kvw[:, j * KV_WIDTH:(j + 1) * KV_WIDTH]
    kv_b = lambda j: kvb[j * KV_WIDTH:(j + 1) * KV_WIDTH]
    gw, gb = wsl(2), bsl(2)
    gidx = np.zeros((A_GROUPS, GATE_ROWS), np.int32)
    gmask = np.zeros((A_GROUPS, GATE_ROWS), np.float32)
    for g in range(A_GROUPS):
        for br in range(3):
            for h in range(A_HPG):
                gidx[g, br * A_HPG + h] = (g * A_HPG + h) * 3 + br
                gmask[g, br * A_HPG + h] = 1.0
    gidx, gmask = gidx.reshape(-1), gmask.reshape(-1)
    gw_t = gw[:, gidx] * gmask[None, :]
    gb_t = gb[gidx] * gmask

    wt = jnp.concatenate([wsl(0), kv_w(3), kv_w(5), gw_t, wsl(3)], axis=1).T.astype(BF16)
    bt = jnp.concatenate([bsl(0), kv_b(3), kv_b(5), gb_t, bsl(3)])[:, None]
    wn = jnp.concatenate([kv_w(0), kv_w(1), kv_w(2), kv_w(4), wsl(4), wsl(5), wsl(6), wsl(7)],
                         axis=1).astype(BF16)
    bn = jnp.concatenate([kv_b(0), kv_b(1), kv_b(2), kv_b(4), bsl(4), bsl(5), bsl(6), bsl(7)])[None, :]
    cost, sint, cos, sina, sinb = _rope_tables(S)

    (qr, qn, vts, vtw, gt, szat, kcmp, vcmp, ks, kw, qb, fb, ib, szb) = _project(
        x, wn, bn, wt, bt, cost, sint, cos, sina, sinb)

    half = CMP_STRIDE * HEAD_DIM

    def w1_planes(w1):
        both = jnp.concatenate([w1[:half], w1[half:]], axis=1).reshape(CMP_STRIDE, HEAD_DIM, 2 * CMP_HIDDEN)
        z = jnp.zeros_like(both)
        return jnp.concatenate([jnp.concatenate([both, z], axis=2),
                                jnp.concatenate([z, both], axis=2)], axis=1).astype(BF16)

    kc, vct = _compress(kcmp, vcmp,
                        w1_planes(w_k1), w_k1, pe_k.reshape(-1, 1), w_k2.astype(BF16),
                        w1_planes(w_v1), w_v1, pe_v.reshape(-1, 1), w_v2.T.astype(BF16))

    oat = _nsa(qr, qn, gt, szat, kc, vct, ks, kw, vts, vtw, _overlap_t(S))
    wgm = jnp.concatenate([wsl(8), wsl(9)], axis=1).astype(BF16)
    bgm = jnp.concatenate([bsl(8), bsl(9)])[None, :]
    return _merge_out(x, oat, qb, fb, ib, szb, lb_logits, norm_g[None, :], _block_tri(), _block_diag(),
                      wgm, bgm, w_a.astype(BF16), w_b.astype(BF16), w_o.astype(BF16),
                      ln_g[None, :], ln_b[None, :], alpha, l)


@jax.jit
def kernel(x, w_in, b_in, pe_cmp_k, w_cmp_k1, w_cmp_k2, pe_cmp_v, w_cmp_v1, w_cmp_v2,
           hgrn_lb_logits, hgrn_norm_g, w_branch_a, w_branch_b, w_out, ln_g, ln_b):
    B, S, D = x.shape
    assert D == D_MODEL and S % KV_CHUNK == 0 and S % PROJ_ROWS == 0 and S >= WIN_KEYS
    assert S % OUT_ROWS == 0 and OUT_ROWS % HG_ROWS == 0 and (S // KV_CHUNK) * KV_CHUNK == S
    for l in range(DEPTH):
        x = _layer(x, l, w_in[l], b_in[l], pe_cmp_k[l], w_cmp_k1[l], w_cmp_k2[l],
                   pe_cmp_v[l], w_cmp_v1[l], w_cmp_v2[l], hgrn_lb_logits, hgrn_norm_g[l],
                   w_branch_a[l], w_branch_b[l], w_out[l], ln_g[l], ln_b[l])
    return x
```

```python
import functools
import math

import numpy as np
import jax
import jax.numpy as jnp
from jax import lax
from jax.experimental import pallas as pl
from jax.experimental.pallas import tpu as pltpu

D_MODEL = 1024
DEPTH = 1
A_HEADS = 8
A_GROUPS = 2
A_HPG = A_HEADS // A_GROUPS
HEAD_DIM = 64
A_WIDTH = A_HEADS * HEAD_DIM
KV_WIDTH = A_GROUPS * HEAD_DIM
CMP_BLOCK = 32
CMP_STRIDE = 16
CMP_HIDDEN = 128
SLC_BLOCK = 64
SLC_TOP_N = 16
WINDOW = 512
ROPE_THETA = 10000.0
FORCE_SCORE = 1e30
B_HEADS = 4
B_KDIM = 128
B_VDIM = 128
B_FWIDTH = B_HEADS * B_KDIM
B_WIDTH = B_HEADS * B_VDIM
CHUNK = 64
NORM_EPS = 1e-5

IN_SPLITS = (A_WIDTH, 6 * KV_WIDTH, A_HEADS * 3, A_WIDTH, B_FWIDTH, B_FWIDTH,
             B_WIDTH, B_WIDTH, D_MODEL, D_MODEL)
_OFF = tuple(int(v) for v in np.cumsum((0,) + IN_SPLITS))

V7X_VMEM_LIMIT_BYTES = 56 * 1024 * 1024
PROJ_ROWS = 512
Q_TILE = 128
KV_CHUNK = 512
IN_FLIGHT = 4
AHEAD = 2
MAX_JOBS_PER_TRIP = 16
WIN_KEYS = WINDOW + Q_TILE
HG_ROWS = 256
OUT_ROWS = 512
GATE_COLS = 256
YA_COLS = 512
TAIL_SPLIT = 2
MASKED = -1e30
LOG2E = math.log2(math.e)
GATE_ROWS = 16
V_ROWS = HEAD_DIM + 16

F32 = jnp.float32
BF16 = jnp.bfloat16


def _dot(a, b):
    return jnp.dot(a, b, preferred_element_type=F32)


def _dot_nt(a, b):
    return lax.dot_general(a, b, (((1,), (1,)), ((), ())), preferred_element_type=F32)


def _sigmoid(x):
    return 1.0 / (1.0 + jnp.exp(-x))


def _silu(x):
    return x * _sigmoid(x)


def _proj_kernel(x_ref, wn_ref, bn_ref, wt_ref, bt_ref, cost_ref, sint_ref,
                 cos_ref, sina_ref, sinb_ref,
                 qr_ref, qn_ref, vts_ref, vtw_ref, gt_ref, szat_ref, kc_ref, vc_ref,
                 ks_ref, kw_ref, qb_ref, fb_ref, ib_ref, szb_ref, cmp_s):
    xb = x_ref[0].astype(BF16)
    scale = HEAD_DIM ** -0.5 * LOG2E

    ht = _dot_nt(wt_ref[...], xb) + bt_ref[...]
    cost = cost_ref[...]
    sint = sint_ref[...]
    half = HEAD_DIM // 2
    for h in range(A_HEADS):
        blk = ht[h * HEAD_DIM:(h + 1) * HEAD_DIM]
        rot = jnp.concatenate([-blk[half:], blk[:half]], axis=0)
        qr_ref[0, h * HEAD_DIM:(h + 1) * HEAD_DIM, :] = ((blk * cost + rot * sint) * scale).astype(BF16)
        qn_ref[0, h * HEAD_DIM:(h + 1) * HEAD_DIM, :] = (blk * scale).astype(BF16)
    o = A_WIDTH
    ones_rows = jnp.where(lax.broadcasted_iota(jnp.int32, (V_ROWS - HEAD_DIM, ht.shape[1]), 0) == 0, 1.0, 0.0)
    for g in range(A_GROUPS):
        for ref, base in ((vts_ref, o), (vtw_ref, o + KV_WIDTH)):
            rows_g = ht[base + g * HEAD_DIM:base + (g + 1) * HEAD_DIM]
            ref[0, g] = jnp.concatenate([rows_g, ones_rows], axis=0).astype(BF16)
    gt_ref[0] = _sigmoid(ht[o + 2 * KV_WIDTH:o + 2 * KV_WIDTH + 2 * GATE_ROWS])
    o += 2 * KV_WIDTH + 2 * GATE_ROWS
    szat_ref[0] = _silu(ht[o:o + A_WIDTH]).astype(BF16)

    def cols(lo, hi):
        return _dot(xb, wn_ref[:, lo:hi]) + bn_ref[:, lo:hi]

    kv = cols(0, 4 * KV_WIDTH)
    cmp_s[0] = kv[:, 0:KV_WIDTH]
    cmp_s[1] = kv[:, KV_WIDTH:2 * KV_WIDTH]
    pieces = cmp_s.shape[1] // CMP_STRIDE
    for t in range(CMP_STRIDE):
        kc_ref[0, t] = cmp_s[0, pl.ds(t, pieces, stride=CMP_STRIDE), :].astype(BF16)
        vc_ref[0, t] = cmp_s[1, pl.ds(t, pieces, stride=CMP_STRIDE), :].astype(BF16)
    cos = cos_ref[...]
    sina = sina_ref[...]
    sinb = sinb_ref[...]

    def rope_rows(k):
        return (k * cos + pltpu.roll(k, 128 - half, axis=1) * sina + pltpu.roll(k, half, axis=1) * sinb)

    ks = rope_rows(kv[:, 2 * KV_WIDTH:3 * KV_WIDTH])
    rows = ks.shape[0]
    lane = lax.broadcasted_iota(jnp.int32, (rows, KV_WIDTH), 1)
    pos = pl.program_id(1) * rows + lax.broadcasted_iota(jnp.int32, (rows, KV_WIDTH), 0)
    blocks_per_chunk = KV_CHUNK // SLC_BLOCK
    ind = jnp.where(lane == HEAD_DIM + (pos // SLC_BLOCK) % blocks_per_chunk, 1.0, 0.0)
    ks_ref[0, 0] = (jnp.where(lane < HEAD_DIM, ks, 0.0) + ind).astype(BF16)
    ks_ref[0, 1] = (jnp.where(lane < HEAD_DIM, pltpu.roll(ks, HEAD_DIM, axis=1), 0.0) + ind).astype(BF16)
    kw = rope_rows(kv[:, 3 * KV_WIDTH:4 * KV_WIDTH])
    kw_ref[0, 0] = jnp.where(lane < HEAD_DIM, kw, 0.0).astype(BF16)
    kw_ref[0, 1] = jnp.where(lane < HEAD_DIM, pltpu.roll(kw, HEAD_DIM, axis=1), 0.0).astype(BF16)
    o = 4 * KV_WIDTH
    qb_ref[0] = cols(o, o + B_FWIDTH).astype(BF16)
    o += B_FWIDTH
    fb_ref[0] = cols(o, o + B_FWIDTH)
    o += B_FWIDTH
    ib_ref[0] = cols(o, o + B_WIDTH).astype(BF16)
    o += B_WIDTH
    szb_ref[0] = _silu(cols(o, o + B_WIDTH)).astype(BF16)


def _project(x, wn, bn, wt, bt, cost, sint, cos, sina, sinb):
    B, S, D = x.shape
    T = PROJ_ROWS
    n_t = wt.shape[0]
    n_n = wn.shape[1]
    full = lambda shape: pl.BlockSpec(shape, lambda b, s: (0,) * len(shape))
    row_out = lambda w: pl.BlockSpec((1, T, w), lambda b, s: (b, s, 0))
    col_out = lambda r: pl.BlockSpec((1, r, T), lambda b, s: (b, 0, s))
    sds = jax.ShapeDtypeStruct
    out_shape = (
        sds((B, A_WIDTH, S), BF16), sds((B, A_WIDTH, S), BF16),
        sds((B, A_GROUPS, V_ROWS, S), BF16), sds((B, A_GROUPS, V_ROWS, S), BF16),
        sds((B, 2 * GATE_ROWS, S), F32),
        sds((B, A_WIDTH, S), BF16),
        sds((B, CMP_STRIDE, S // CMP_STRIDE, KV_WIDTH), BF16),
        sds((B, CMP_STRIDE, S // CMP_STRIDE, KV_WIDTH), BF16),
        sds((B, A_GROUPS, S, KV_WIDTH), BF16), sds((B, A_GROUPS, S, KV_WIDTH), BF16),
        sds((B, S, B_FWIDTH), BF16), sds((B, S, B_FWIDTH), F32),
        sds((B, S, B_WIDTH), BF16), sds((B, S, B_WIDTH), BF16),
    )
    out_specs = (
        col_out(A_WIDTH), col_out(A_WIDTH),
        pl.BlockSpec((1, A_GROUPS, V_ROWS, T), lambda b, s: (b, 0, 0, s)),
        pl.BlockSpec((1, A_GROUPS, V_ROWS, T), lambda b, s: (b, 0, 0, s)),
        col_out(2 * GATE_ROWS), col_out(A_WIDTH),
        pl.BlockSpec((1, CMP_STRIDE, T // CMP_STRIDE, KV_WIDTH), lambda b, s: (b, 0, s, 0)),
        pl.BlockSpec((1, CMP_STRIDE, T // CMP_STRIDE, KV_WIDTH), lambda b, s: (b, 0, s, 0)),
        pl.BlockSpec((1, A_GROUPS, T, KV_WIDTH), lambda b, s: (b, 0, s, 0)),
        pl.BlockSpec((1, A_GROUPS, T, KV_WIDTH), lambda b, s: (b, 0, s, 0)),
        row_out(B_FWIDTH), row_out(B_FWIDTH), row_out(B_WIDTH), row_out(B_WIDTH),
    )
    in_specs = [
        pl.BlockSpec((1, T, D), lambda b, s: (b, s, 0)),
        full((D, n_n)), full((1, n_n)), full((n_t, D)), full((n_t, 1)),
        pl.BlockSpec((HEAD_DIM, T), lambda b, s: (0, s)),
        pl.BlockSpec((HEAD_DIM, T), lambda b, s: (0, s)),
        pl.BlockSpec((T, KV_WIDTH), lambda b, s: (s, 0)),
        pl.BlockSpec((T, KV_WIDTH), lambda b, s: (s, 0)),
        pl.BlockSpec((T, KV_WIDTH), lambda b, s: (s, 0)),
    ]
    return pl.pallas_call(
        _proj_kernel, out_shape=out_shape, grid=(B, S // T),
        in_specs=in_specs, out_specs=out_specs,
        scratch_shapes=[pltpu.VMEM((2, T, KV_WIDTH), F32)],
        compiler_params=pltpu.CompilerParams(
            dimension_semantics=("parallel", "parallel"),
            vmem_limit_bytes=V7X_VMEM_LIMIT_BYTES),
        name="in_proj",
    )(x, wn, bn, wt, bt, cost, sint, cos, sina, sinb)


def _compress_kernel(ck_ref, cv_ref, w1k_ref, w1kf_ref, pek_ref, w2k_ref,
                     w1v_ref, w1vf_ref, pev_ref, w2vt_ref, kc_ref, vct_ref):
    nc = ck_ref.shape[2]
    hid = CMP_HIDDEN

    def hidden(c_ref, w1_ref, w1f_ref, pe_ref):
        a = _dot(c_ref[0, 0], w1_ref[0])
        for t in range(1, CMP_STRIDE):
            a = a + _dot(c_ref[0, t], w1_ref[t])
        pe_term = jnp.sum(w1f_ref[...] * pe_ref[...], axis=0, keepdims=True)
        out = []
        for g in range(A_GROUPS):
            lo = a[:, 2 * g * hid:(2 * g + 1) * hid]
            hi = pltpu.roll(a[:, (2 * g + 1) * hid:(2 * g + 2) * hid], nc - 1, axis=0)
            out.append(_silu(lo + hi + pe_term).astype(BF16))
        return out

    hk = hidden(ck_ref, w1k_ref, w1kf_ref, pek_ref)
    hv = hidden(cv_ref, w1v_ref, w1vf_ref, pev_ref)
    for g in range(A_GROUPS):
        kc_ref[0, g] = _dot(hk[g], w2k_ref[...]).astype(BF16)
        vct_ref[0, g] = _dot_nt(w2vt_ref[...], hv[g]).astype(BF16)


def _compress(ck, cv, w1k, w1kf, pek, w2k, w1v, w1vf, pev, w2vt):
    B, P, NC, W = ck.shape
    full = lambda a: pl.BlockSpec(a.shape, lambda b: (0,) * a.ndim)
    blk = pl.BlockSpec((1, P, NC, W), lambda b: (b, 0, 0, 0))
    return pl.pallas_call(
        _compress_kernel,
        out_shape=(jax.ShapeDtypeStruct((B, A_GROUPS, NC, HEAD_DIM), BF16),
                   jax.ShapeDtypeStruct((B, A_GROUPS, HEAD_DIM, NC), BF16)),
        grid=(B,),
        in_specs=[blk, blk, full(w1k), full(w1kf), full(pek), full(w2k),
                  full(w1v), full(w1vf), full(pev), full(w2vt)],
        out_specs=(pl.BlockSpec((1, A_GROUPS, NC, HEAD_DIM), lambda b: (b, 0, 0, 0)),
                   pl.BlockSpec((1, A_GROUPS, HEAD_DIM, NC), lambda b: (b, 0, 0, 0))),
        compiler_params=pltpu.CompilerParams(
            dimension_semantics=("parallel",),
            vmem_limit_bytes=V7X_VMEM_LIMIT_BYTES),
        name="kv_compress",
    )(ck, cv, w1k, w1kf, pek, w2k, w1v, w1vf, pev, w2vt)


def _nsa_kernel(jc_ref, jq_ref, qr_ref, qn_ref, gt_ref, szat_ref, kc_ref, vct_ref, ks_ref, kw_ref,
                vts_ref, vtw_ref, ovl_ref, o_ref,
                bias_s, wq_s, m_s, acc_s, ocw_s, sbuf_s, smax_s, dbuf_s, dmax_s, scb_s, swb_s, wmax_s,
                *, seq, n_sel, n_jobs, jobs_per_trip):
    nc = kc_ref.shape[2]
    n_slc = seq // SLC_BLOCK
    n_q = seq // Q_TILE
    n_ck = seq // KV_CHUNK
    lanes = A_HPG * Q_TILE
    bpc = KV_CHUNK // SLC_BLOCK
    tpc = KV_CHUNK // Q_TILE
    win_tiles = WINDOW // Q_TILE
    last = n_q - 1

    def stack_heads(ref, q0):
        return jnp.concatenate(
            [ref[0, h * HEAD_DIM:(h + 1) * HEAD_DIM, pl.ds(q0, Q_TILE)] for h in range(A_HPG)], axis=1)

    def per_head(a):
        return jnp.concatenate([a] * A_HPG, axis=1)

    def gate_row(gt, branch):
        return jnp.concatenate(
            [gt[branch * A_HPG + h:branch * A_HPG + h + 1, :] for h in range(A_HPG)], axis=1)

    def tile_start(qi):
        return pl.multiple_of(qi * Q_TILE, Q_TILE)

    lane_q = lax.broadcasted_iota(jnp.int32, (1, lanes), 1) & (Q_TILE - 1)
    r_sq = lax.broadcasted_iota(jnp.int32, (Q_TILE, lanes), 0)
    causal_sq = r_sq <= lane_q
    lower_sq = r_sq > lane_q
    q_pad = jnp.zeros((KV_WIDTH - HEAD_DIM, lanes), BF16)

    def cmp_keys(rb):
        return min(nc, -(-(rb * KV_CHUNK // CMP_STRIDE) // 128) * 128)

    def cmp_scores(qi, rb):
        return _dot(kc_ref[0, 0, 0:cmp_keys(rb), :], stack_heads(qn_ref, tile_start(qi)))

    def select_tile(qi, rb, sc):
        nk = cmp_keys(rb)
        n_causal = (qi + 1) * Q_TILE // SLC_BLOCK
        s0 = tile_start(qi)
        t_row = s0 + lax.broadcasted_iota(jnp.int32, (1, Q_TILE), 1)
        wq_s[qi] = stack_heads(qr_ref, s0)

        n_end = lax.broadcasted_iota(jnp.int32, (nk, Q_TILE), 0) * CMP_STRIDE + (CMP_BLOCK - 1)
        sc = sc + per_head(jnp.where(n_end <= t_row, 0.0, MASKED))
        mc = jnp.max(sc, axis=0, keepdims=True)
        pc = jnp.exp2(sc - mc)
        lc = jnp.sum(pc, axis=0, keepdims=True)
        pc = pc * jnp.where(s0 + lane_q >= CMP_BLOCK - 1, 1.0 / lc, 0.0)
        o_cmp = _dot(vct_ref[0, 0, :, 0:nk], pc.astype(BF16))

        ps = pc[:, 0:Q_TILE]
        for h in range(1, A_HPG):
            ps = ps + pc[:, h * Q_TILE:(h + 1) * Q_TILE]
        p_hi = ps.astype(BF16)
        p_lo = (ps - p_hi.astype(F32)).astype(BF16)
        imp2 = _dot(ovl_ref[0:8 * rb, 0:nk], jnp.concatenate([p_hi, p_lo], axis=1))
        imp = imp2[:, :Q_TILE] + imp2[:, Q_TILE:]

        j_idx = lax.broadcasted_iota(jnp.int32, (8 * rb, Q_TILE), 0)
        cur = t_row // SLC_BLOCK
        sel = j_idx <= cur
        if 8 * rb > n_sel:
            forced = (j_idx == 0) | (j_idx == cur) | (j_idx == cur - 1)
            val = jnp.where(j_idx > cur, -1.0, jnp.where(forced, FORCE_SCORE, imp))
            rows = [val[8 * j:8 * j + 8] for j in range(rb)]
            cnt = [jnp.zeros((8, Q_TILE), F32) for _ in range(rb)]
            jl = lax.broadcasted_iota(jnp.int32, (8, Q_TILE), 0)
            for kb in range(rb):
                for kl in range(8):
                    if 8 * kb + kl >= n_causal:
                        continue
                    row = rows[kb][kl:kl + 1, :]
                    for jb in range(rb):
                        if jb < kb:
                            beats = row > rows[jb]
                        elif jb > kb:
                            beats = row >= rows[jb]
                        else:
                            beats = (row > rows[jb]) | ((row == rows[jb]) & (jl > kl))
                        cnt[jb] = cnt[jb] + jnp.where(beats, 1.0, 0.0)
            sel = sel & (jnp.concatenate(cnt, axis=0) < n_sel)
        bias = per_head(jnp.where(sel, 0.0, MASKED))
        pad = jnp.zeros((bpc, lanes), F32)
        for c in range(rb):
            bias_s[qi, c] = jnp.concatenate([bias[c * bpc:(c + 1) * bpc], pad], axis=0).astype(BF16)

        gt = gt_ref[0, :, pl.ds(s0, Q_TILE)]
        ocw_s[qi] = gate_row(gt, 0) * o_cmp
        m_s[qi] = jnp.full((8, lanes), MASKED, F32)
        acc_s[qi] = jnp.zeros((V_ROWS, lanes), F32)

    def win_finish(qi, sw, mw, w0, keys=WIN_KEYS):
        pw = jnp.exp2(sw - mw)
        ow = _dot(vtw_ref[0, 0, :, pl.ds(w0, keys)], pw.astype(BF16))
        o_win = ow[0:HEAD_DIM] * (1.0 / ow[HEAD_DIM:HEAD_DIM + 1])
        gt = gt_ref[0, :, pl.ds(tile_start(qi), Q_TILE)]
        ocw_s[qi] = ocw_s[qi] + gate_row(gt, 2) * o_win

    def win_scores(qi):
        slot = qi % IN_FLIGHT
        keys = min(qi + 1, win_tiles + 1) * Q_TILE
        w0 = (qi + 1) * Q_TILE - keys
        wq = jnp.concatenate([stack_heads(qr_ref, qi * Q_TILE), q_pad], axis=0)
        sw = _dot(kw_ref[0, 0, w0:w0 + keys, :], wq)
        parts = [jnp.where(causal_sq, sw[keys - Q_TILE:], MASKED)]
        if keys > Q_TILE:
            oldest = sw[:Q_TILE]
            if qi >= win_tiles:
                oldest = jnp.where(lower_sq, oldest, MASKED)
            parts = [oldest, sw[Q_TILE:keys - Q_TILE]] + parts if keys > 2 * Q_TILE else [oldest] + parts
        sw = jnp.concatenate(parts, axis=0)
        swb_s[slot, 0:keys, :] = sw
        wmax_s[slot] = jnp.broadcast_to(jnp.max(sw, axis=0, keepdims=True), (8, lanes))

    def win_tile(qi):
        slot = qi % IN_FLIGHT
        keys = min(qi + 1, win_tiles + 1) * Q_TILE
        win_finish(qi, swb_s[slot, 0:keys, :], wmax_s[slot][0:1], (qi + 1) * Q_TILE - keys, keys)

    scb_s[0, 0:cmp_keys(1), :] = cmp_scores(0, 1)
    for u in range(min(AHEAD, n_q)):
        win_scores(u)
    for qi in range(n_q):
        rb = qi // tpc + 1
        nk = cmp_keys(rb)
        if qi + 1 < n_q:
            rb_next = (qi + 1) // tpc + 1
            scb_s[(qi + 1) & 1, 0:cmp_keys(rb_next), :] = cmp_scores(qi + 1, rb_next)
        if qi + AHEAD < n_q:
            win_scores(qi + AHEAD)
        select_tile(qi, rb, scb_s[qi & 1, 0:nk, :])
        win_tile(qi)

    wq_pad = jnp.zeros((KV_WIDTH - HEAD_DIM - 2 * bpc, lanes), BF16)

    def scores(c, qi, diag):
        rows = KV_CHUNK if diag is None else (diag + 1) * Q_TILE
        k0 = pl.multiple_of(c * KV_CHUNK, KV_CHUNK)
        wq = jnp.concatenate([wq_s[qi], bias_s[qi, c], wq_pad], axis=0)
        s = _dot(ks_ref[0, 0, pl.ds(k0, rows), :], wq)
        if diag is not None:
            head = [s[:rows - Q_TILE]] if diag > 0 else []
            s = jnp.concatenate(head + [jnp.where(causal_sq, s[rows - Q_TILE:], MASKED)], axis=0)
        return s, jnp.broadcast_to(jnp.max(s, axis=0, keepdims=True), (8, lanes))

    def absorb(c, qi, s, s_max):
        rows = s.shape[0]
        k0 = pl.multiple_of(c * KV_CHUNK, KV_CHUNK)
        m_old = m_s[qi][0:1]
        m_new = jnp.maximum(m_old, s_max[0:1])
        alpha = jnp.exp2(m_old - m_new)
        p = jnp.exp2(s - m_new)
        m_s[qi] = jnp.broadcast_to(m_new, (8, lanes))
        acc_s[qi] = alpha * acc_s[qi] + _dot(vts_ref[0, 0, :, pl.ds(k0, rows)], p.astype(BF16))

    def full_scores(j, slot):
        sbuf_s[slot], smax_s[slot] = scores(jc_ref[j], jq_ref[j], None)

    def full_group(i, _):
        j0 = jobs_per_trip * i
        for u in range(jobs_per_trip):
            full_scores(j0 + u + AHEAD, (u + AHEAD) % IN_FLIGHT)
            absorb(jc_ref[j0 + u], jq_ref[j0 + u], sbuf_s[u % IN_FLIGHT], smax_s[u % IN_FLIGHT])
        return 0

    if n_jobs:
        for u in range(AHEAD):
            full_scores(u, u)
        lax.fori_loop(0, n_jobs // jobs_per_trip, full_group, 0)

    def diag_scores(c, r):
        dbuf_s[r, 0:(r + 1) * Q_TILE, :], dmax_s[r] = scores(c, c * tpc + r, r)

    def diag_chunk(c, _):
        nxt = jnp.minimum(c + 1, n_ck - 1)
        for r in range(tpc):
            ahead = r + AHEAD
            diag_scores(c if ahead < tpc else nxt, ahead % tpc)
            absorb(c, c * tpc + r, dbuf_s[r, 0:(r + 1) * Q_TILE, :], dmax_s[r])
        return 0

    for r in range(AHEAD):
        diag_scores(0, r)
    lax.fori_loop(0, n_ck, diag_chunk, 0)

    def finish(qi, _):
        s0 = tile_start(qi)
        gt = gt_ref[0, :, pl.ds(s0, Q_TILE)]
        acc = acc_s[qi]
        ot = ocw_s[qi] + gate_row(gt, 1) * acc[0:HEAD_DIM] * (1.0 / acc[HEAD_DIM:HEAD_DIM + 1])
        ot = (ot * stack_heads(szat_ref, s0).astype(F32)).astype(o_ref.dtype)
        for h in range(A_HPG):
            o_ref[0, h * HEAD_DIM:(h + 1) * HEAD_DIM, pl.ds(s0, Q_TILE)] = ot[:, h * Q_TILE:(h + 1) * Q_TILE]
        return 0

    lax.fori_loop(0, n_q, finish, 0)


def _nsa(qr, qn, gt, szat, kc, vct, ks, kw, vts, vtw, ovl):
    B, _, S = qr.shape
    NC = kc.shape[2]
    n_slc = S // SLC_BLOCK
    n_sel = min(SLC_TOP_N, n_slc)
    n_q = S // Q_TILE
    n_ck = S // KV_CHUNK
    tpc = KV_CHUNK // Q_TILE
    gw = A_HPG * HEAD_DIM
    lanes = A_HPG * Q_TILE
    jobs = [(c, q) for c in range(n_ck) for q in range((c + 1) * tpc, n_q)]
    n_jobs = len(jobs)
    jobs_per_trip = max(j for j in range(IN_FLIGHT, MAX_JOBS_PER_TRIP + 1, IN_FLIGHT) if n_jobs % j == 0)
    assert tpc == IN_FLIGHT and AHEAD < IN_FLIGHT
    jobs = jobs + [jobs[-1] if jobs else (0, 0)] * AHEAD
    jc = jnp.asarray([j[0] for j in jobs], jnp.int32)
    jq = jnp.asarray([j[1] for j in jobs], jnp.int32)
    kernel = functools.partial(_nsa_kernel, seq=S, n_sel=n_sel, n_jobs=n_jobs, jobs_per_trip=jobs_per_trip)
    per_group = lambda rows, cols: pl.BlockSpec((1, 1, rows, cols), lambda b, g, *_: (b, g, 0, 0))
    in_specs = [
        pl.BlockSpec((1, gw, S), lambda b, g, *_: (b, g, 0)),
        pl.BlockSpec((1, gw, S), lambda b, g, *_: (b, g, 0)),
        pl.BlockSpec((1, GATE_ROWS, S), lambda b, g, *_: (b, g, 0)),
        pl.BlockSpec((1, gw, S), lambda b, g, *_: (b, g, 0)),
        per_group(NC, HEAD_DIM), per_group(HEAD_DIM, NC),
        per_group(S, KV_WIDTH), per_group(S, KV_WIDTH),
        per_group(V_ROWS, S), per_group(V_ROWS, S),
        pl.BlockSpec((n_slc, NC), lambda b, g, *_: (0, 0)),
    ]
    scratch = [
        pltpu.VMEM((n_q, n_ck, 2 * (KV_CHUNK // SLC_BLOCK), lanes), BF16),
        pltpu.VMEM((n_q, HEAD_DIM, lanes), BF16),
        pltpu.VMEM((n_q, 8, lanes), F32),
        pltpu.VMEM((n_q, V_ROWS, lanes), F32),
        pltpu.VMEM((n_q, HEAD_DIM, lanes), F32),
        pltpu.VMEM((IN_FLIGHT, KV_CHUNK, lanes), F32),
        pltpu.VMEM((IN_FLIGHT, 8, lanes), F32),
        pltpu.VMEM((tpc, KV_CHUNK, lanes), F32),
        pltpu.VMEM((tpc, 8, lanes), F32),
        pltpu.VMEM((2, NC, lanes), F32),
        pltpu.VMEM((IN_FLIGHT, WIN_KEYS, lanes), F32),
        pltpu.VMEM((IN_FLIGHT, 8, lanes), F32),
    ]
    return pl.pallas_call(
        kernel,
        out_shape=jax.ShapeDtypeStruct((B, A_WIDTH, S), BF16),
        grid_spec=pltpu.PrefetchScalarGridSpec(
            num_scalar_prefetch=2,
            grid=(B, A_GROUPS),
            in_specs=in_specs,
            out_specs=pl.BlockSpec((1, gw, S), lambda b, g, *_: (b, g, 0)),
            scratch_shapes=scratch),
        compiler_params=pltpu.CompilerParams(
            dimension_semantics=("parallel", "parallel"),
            vmem_limit_bytes=V7X_VMEM_LIMIT_BYTES),
        name="nsa_attention",
    )(jc, jq, qr, qn, gt, szat, kc, vct, ks, kw, vts, vtw, ovl)


def _hgrn_stages(q_ref, f_ref, i_ref, sz_ref, lb, gain, tri, bd, causal, state_s, h, out):
    T = q_ref.shape[1]
    cpg = HG_ROWS // CHUNK
    lanes = slice(h * B_KDIM, (h + 1) * B_KDIM)
    groups = [slice(gi * HG_ROWS, (gi + 1) * HG_ROWS) for gi in range(T // HG_ROWS)]
    mid = {}

    def decay():
        mid["kk"], mid["b"] = [], []
        for rows in groups:
            fg = lb + (1.0 - lb) * _sigmoid(f_ref[0, rows, lanes])
            logf = jnp.log(fg)
            l_hi = logf.astype(BF16)
            l_lo = (logf - l_hi.astype(F32)).astype(BF16)
            bb = _dot(tri, jnp.concatenate([l_hi, l_lo], axis=1))
            mid["kk"].append(1.0 - fg)
            mid["b"].append(bb[:, :B_KDIM] + bb[:, B_KDIM:])

    def intra():
        mid["qe"], mid["oi"], mid["kv"], mid["dl"] = [], [], [], []
        for rows, kk, b in zip(groups, mid["kk"], mid["b"]):
            b_last = jnp.concatenate(
                [jnp.broadcast_to(b[(c + 1) * CHUNK - 1:(c + 1) * CHUNK, :], (CHUNK, B_KDIM))
                 for c in range(cpg)], axis=0)
            qe = (_silu(q_ref[0, rows, lanes].astype(F32)) * jnp.exp(b)).astype(BF16)
            ke = (kk * jnp.exp(-b)).astype(BF16)
            kd = (kk * jnp.exp(b_last - b)).astype(BF16)
            attn = jnp.where(causal, _dot_nt(qe, ke), 0.0).astype(BF16)
            v = i_ref[0, rows, lanes]
            mid["oi"].append(_dot(attn, v))
            mid["qe"].append(qe)
            kd_bd = jnp.concatenate([kd] * cpg, axis=1) * bd
            kv_all = lax.dot_general(v, kd_bd, (((0,), (0,)), ((), ())), preferred_element_type=F32)
            for c in range(cpg):
                mid["kv"].append(kv_all[:, c * B_KDIM:(c + 1) * B_KDIM])
                mid["dl"].append(jnp.exp(b_last[c * CHUNK:c * CHUNK + 1, :]))

    def inter():
        state = state_s[h]
        outs = []
        for n in range(T // CHUNK):
            gi, c = divmod(n, cpg)
            sl = slice(c * CHUNK, (c + 1) * CHUNK)
            rows = slice(n * CHUNK, (n + 1) * CHUNK)
            o = mid["oi"][gi][sl] + _dot_nt(mid["qe"][gi][sl], state.astype(BF16))
            o = o * lax.rsqrt(jnp.mean(o * o, axis=-1, keepdims=True) + NORM_EPS)
            outs.append((o * gain * sz_ref[0, rows, lanes].astype(F32)).astype(BF16))
            state = mid["dl"][n] * state + mid["kv"][n]
        state_s[h] = state
        out.append(jnp.concatenate(outs, axis=0))

    return [decay, intra, inter]


def _out_kernel(x_ref, oat_ref, q_ref, f_ref, i_ref, sz_ref, lbl_ref, g_ref, tri_ref, bd_ref,
                wgm_ref, bgm_ref, wa_ref, wb_ref, wo_ref, lng_ref, lnb_ref, o_ref, state_s, sg_s, ya_s,
                *, alpha, layer):
    @pl.when(pl.program_id(1) == 0)
    def _():
        state_s[...] = jnp.zeros_like(state_s)

    x = x_ref[0]
    xb = x.astype(BF16)

    def gate_chunk(j):
        def run():
            sg_s[:, j:j + GATE_COLS] = _sigmoid(_dot(xb, wgm_ref[:, j:j + GATE_COLS]) + bgm_ref[:, j:j + GATE_COLS])
        return run

    def ya_chunk(j):
        def run():
            ya_s[:, j:j + YA_COLS] = lax.dot_general(oat_ref[0], wa_ref[:, j:j + YA_COLS], (((0,), (0,)), ((), ())),
                                                     preferred_element_type=F32)
        return run

    fillers = [gate_chunk(j) for j in range(0, 2 * D_MODEL, GATE_COLS)] + \
              [ya_chunk(j) for j in range(0, D_MODEL, YA_COLS)]

    lg = lbl_ref[...]
    e = jnp.exp(lg - jnp.max(lg, axis=0, keepdims=True))
    lb_all = jnp.sum(e[0:layer + 1], axis=0, keepdims=True) / jnp.sum(e, axis=0, keepdims=True)
    tri = tri_ref[...]
    bd = bd_ref[...]
    ri = lax.broadcasted_iota(jnp.int32, (HG_ROWS, HG_ROWS), 0)
    ci = lax.broadcasted_iota(jnp.int32, (HG_ROWS, HG_ROWS), 1)
    causal = (ri // CHUNK == ci // CHUNK) & (ci <= ri)

    obs = []
    stages = []
    for h in range(B_HEADS):
        lanes = slice(h * B_KDIM, (h + 1) * B_KDIM)
        stages += _hgrn_stages(q_ref, f_ref, i_ref, sz_ref, lb_all[:, lanes], g_ref[:, lanes],
                               tri, bd, causal, state_s, h, obs)
    for k, stage in enumerate(stages):
        if k < len(fillers):
            fillers[k]()
        stage()
    for filler in fillers[len(stages):]:
        filler()
    ob = jnp.concatenate(obs, axis=1)

    sub = x.shape[0] // TAIL_SPLIT
    halves = [slice(i * sub, (i + 1) * sub) for i in range(TAIL_SPLIT)]
    ybs = [_dot(ob[r], wb_ref[...]) for r in halves]
    ys = [(sg_s[r, :D_MODEL] * ya_s[r, :] + sg_s[r, D_MODEL:] * yb).astype(BF16) for r, yb in zip(halves, ybs)]
    outs = [_dot(y, wo_ref[...]) for y in ys]
    for r, out in zip(halves, outs):
        res = alpha * x[r] + out
        mu = jnp.mean(res, axis=-1, keepdims=True)
        d = res - mu
        var = jnp.mean(d * d, axis=-1, keepdims=True)
        o_ref[0, r, :] = d * lax.rsqrt(var + NORM_EPS) * lng_ref[...] + lnb_ref[...]


def _merge_out(x, oat, qb, fb, ib, szb, lb_logits, gain, tri, bd, wgm, bgm, wa, wb, wo, lng, lnb, alpha, layer):
    B, S, D = x.shape
    T = OUT_ROWS
    full = lambda a: pl.BlockSpec(a.shape, lambda b, s: (0,) * a.ndim)
    rows = lambda w: pl.BlockSpec((1, T, w), lambda b, s: (b, s, 0))
    return pl.pallas_call(
        functools.partial(_out_kernel, alpha=alpha, layer=layer),
        out_shape=jax.ShapeDtypeStruct((B, S, D), x.dtype),
        grid=(B, S // T),
        in_specs=[rows(D), pl.BlockSpec((1, A_WIDTH, T), lambda b, s: (b, 0, s)),
                  rows(B_FWIDTH), rows(B_FWIDTH), rows(B_WIDTH), rows(B_WIDTH),
                  full(lb_logits), full(gain), full(tri), full(bd),
                  full(wgm), full(bgm), full(wa), full(wb), full(wo), full(lng), full(lnb)],
        out_specs=rows(D),
        scratch_shapes=[pltpu.VMEM((B_HEADS, B_VDIM, B_KDIM), F32),
                        pltpu.VMEM((T, 2 * D), F32),
                        pltpu.VMEM((T, D), F32)],
        compiler_params=pltpu.CompilerParams(
            dimension_semantics=("parallel", "arbitrary"),
            vmem_limit_bytes=V7X_VMEM_LIMIT_BYTES),
        name="hgrn_merge_out",
    )(x, oat, qb, fb, ib, szb, lb_logits, gain, tri, bd, wgm, bgm, wa, wb, wo, lng, lnb)


def _rope_tables(S):
    inv = ROPE_THETA ** (-jnp.arange(0, HEAD_DIM, 2, dtype=F32) / HEAD_DIM)
    ang = jnp.arange(S, dtype=F32)[:, None] * inv[None, :]
    cos = jnp.concatenate([jnp.cos(ang), jnp.cos(ang)], axis=-1)
    sin = jnp.concatenate([jnp.sin(ang), jnp.sin(ang)], axis=-1)
    first = (jnp.arange(HEAD_DIM) < HEAD_DIM // 2)[None, :]
    sina = jnp.where(first, -sin, 0.0)
    sinb = jnp.where(first, 0.0, sin)
    tile = lambda a: jnp.concatenate([a] * A_GROUPS, axis=-1)
    return cos.T, sin.T, tile(cos), tile(sina), tile(sinb)


def _overlap_t(S):
    n_cmp = S // CMP_STRIDE
    n_slc = S // SLC_BLOCK
    cs = np.arange(n_cmp)[None, :] * CMP_STRIDE
    ss = np.arange(n_slc)[:, None] * SLC_BLOCK
    ov = (cs < ss + SLC_BLOCK) & (cs + CMP_BLOCK > ss) & (np.arange(n_cmp)[None, :] < n_cmp - 1)
    return jnp.asarray(ov, dtype=BF16)


def _block_tri():
    r = np.arange(HG_ROWS)
    return jnp.asarray((r[:, None] // CHUNK == r[None, :] // CHUNK) & (r[None, :] <= r[:, None]), dtype=BF16)


def _block_diag():
    r = np.arange(HG_ROWS)[:, None] // CHUNK
    c = np.arange(HG_ROWS // CHUNK * B_KDIM)[None, :] // B_KDIM
    return jnp.asarray(r == c, dtype=BF16)


def _layer(x, l, w_in, b_in, pe_k, w_k1, w_k2, pe_v, w_v1, w_v2, lb_logits, norm_g,
           w_a, w_b, w_o, ln_g, ln_b):
    B, S, D = x.shape
    alpha = (2 * DEPTH) ** 0.25
    o = _OFF
    wsl = lambda i: w_in[:, o[i]:o[i + 1]]
    bsl = lambda i: b_in[o[i]:o[i + 1]]
    kvw, kvb = wsl(1), bsl(1)
    kv_w = lambda j: kvw[:, j * KV_WIDTH:(j + 1) * KV_WIDTH]
    kv_b = lambda j: kvb[j * KV_WIDTH:(j + 1) * KV_WIDTH]
    gw, gb = wsl(2), bsl(2)
    gidx = np.zeros((A_GROUPS, GATE_ROWS), np.int32)
    gmask = np.zeros((A_GROUPS, GATE_ROWS), np.float32)
    for g in range(A_GROUPS):
        for br in range(3):
            for h in range(A_HPG):
                gidx[g, br * A_HPG + h] = (g * A_HPG + h) * 3 + br
                gmask[g, br * A_HPG + h] = 1.0
    gidx, gmask = gidx.reshape(-1), gmask.reshape(-1)
    gw_t = gw[:, gidx] * gmask[None, :]
    gb_t = gb[gidx] * gmask

    wt = jnp.concatenate([wsl(0), kv_w(3), kv_w(5), gw_t, wsl(3)], axis=1).T.astype(BF16)
    bt = jnp.concatenate([bsl(0), kv_b(3), kv_b(5), gb_t, bsl(3)])[:, None]
    wn = jnp.concatenate([kv_w(0), kv_w(1), kv_w(2), kv_w(4), wsl(4), wsl(5), wsl(6), wsl(7)],
                         axis=1).astype(BF16)
    bn = jnp.concatenate([kv_b(0), kv_b(1), kv_b(2), kv_b(4), bsl(4), bsl(5), bsl(6), bsl(7)])[None, :]
    cost, sint, cos, sina, sinb = _rope_tables(S)

    (qr, qn, vts, vtw, gt, szat, kcmp, vcmp, ks, kw, qb, fb, ib, szb) = _project(
        x, wn, bn, wt, bt, cost, sint, cos, sina, sinb)

    half = CMP_STRIDE * HEAD_DIM

    def w1_planes(w1):
        both = jnp.concatenate([w1[:half], w1[half:]], axis=1).reshape(CMP_STRIDE, HEAD_DIM, 2 * CMP_HIDDEN)
        z = jnp.zeros_like(both)
        return jnp.concatenate([jnp.concatenate([both, z], axis=2),
                                jnp.concatenate([z, both], axis=2)], axis=1).astype(BF16)

    kc, vct = _compress(kcmp, vcmp,
                        w1_planes(w_k1), w_k1, pe_k.reshape(-1, 1), w_k2.astype(BF16),
                        w1_planes(w_v1), w_v1, pe_v.reshape(-1, 1), w_v2.T.astype(BF16))

    oat = _nsa(qr, qn, gt, szat, kc, vct, ks, kw, vts, vtw, _overlap_t(S))
    wgm = jnp.concatenate([wsl(8), wsl(9)], axis=1).astype(BF16)
    bgm = jnp.concatenate([bsl(8), bsl(9)])[None, :]
    return _merge_out(x, oat, qb, fb, ib, szb, lb_logits, norm_g[None, :], _block_tri(), _block_diag(),
                      wgm, bgm, w_a.astype(BF16), w_b.astype(BF16), w_o.astype(BF16),
                      ln_g[None, :], ln_b[None, :], alpha, l)


@jax.jit
def kernel(x, w_in, b_in, pe_cmp_k, w_cmp_k1, w_cmp_k2, pe_cmp_v, w_cmp_v1, w_cmp_v2,
           hgrn_lb_logits, hgrn_norm_g, w_branch_a, w_branch_b, w_out, ln_g, ln_b):
    B, S, D = x.shape
    assert D == D_MODEL and S % KV_CHUNK == 0 and S % PROJ_ROWS == 0 and S >= WIN_KEYS
    assert S % OUT_ROWS == 0 and OUT_ROWS % HG_ROWS == 0 and (S // KV_CHUNK) * KV_CHUNK == S
    for l in range(DEPTH):
        x = _layer(x, l, w_in[l], b_in[l], pe_cmp_k[l], w_cmp_k1[l], w_cmp_k2[l],
                   pe_cmp_v[l], w_cmp_v1[l], w_cmp_v2[l], hgrn_lb_logits, hgrn_norm_g[l],
                   w_branch_a[l], w_branch_b[l], w_out[l], ln_g[l], ln_b[l])
    return x
```

```python
import functools
import math

import numpy as np
import jax
import jax.numpy as jnp
from jax import lax
from jax.experimental import pallas as pl
from jax.experimental.pallas import tpu as pltpu

D_MODEL = 1024
DEPTH = 1
A_HEADS = 8
A_GROUPS = 2
A_HPG = A_HEADS // A_GROUPS
HEAD_DIM = 64
A_WIDTH = A_HEADS * HEAD_DIM
KV_WIDTH = A_GROUPS * HEAD_DIM
CMP_BLOCK = 32
CMP_STRIDE = 16
CMP_HIDDEN = 128
SLC_BLOCK = 64
SLC_TOP_N = 16
WINDOW = 512
ROPE_THETA = 10000.0
FORCE_SCORE = 1e30
B_HEADS = 4
B_KDIM = 128
B_VDIM = 128
B_FWIDTH = B_HEADS * B_KDIM
B_WIDTH = B_HEADS * B_VDIM
CHUNK = 64
NORM_EPS = 1e-5

IN_SPLITS = (A_WIDTH, 6 * KV_WIDTH, A_HEADS * 3, A_WIDTH, B_FWIDTH, B_FWIDTH,
             B_WIDTH, B_WIDTH, D_MODEL, D_MODEL)
_OFF = tuple(int(v) for v in np.cumsum((0,) + IN_SPLITS))

V7X_VMEM_LIMIT_BYTES = 56 * 1024 * 1024
PROJ_ROWS = 512
Q_TILE = 128
KV_CHUNK = 512
IN_FLIGHT = 4
AHEAD = 2
DIAG_CHUNKS_PER_TRIP = 4
MAX_JOBS_PER_TRIP = 16
WIN_KEYS = WINDOW + Q_TILE
HG_ROWS = 256
OUT_ROWS = 512
GATE_COLS = 256
YA_COLS = 512
TAIL_SPLIT = 2
MASKED = -1e30
LOG2E = math.log2(math.e)
GATE_ROWS = 16
V_ROWS = HEAD_DIM + 16

F32 = jnp.float32
BF16 = jnp.bfloat16


def _dot(a, b):
    return jnp.dot(a, b, preferred_element_type=F32)


def _dot_nt(a, b):
    return lax.dot_general(a, b, (((1,), (1,)), ((), ())), preferred_element_type=F32)


def _sigmoid(x):
    return 1.0 / (1.0 + jnp.exp(-x))


def _silu(x):
    return x * _sigmoid(x)


def _proj_kernel(x_ref, wn_ref, bn_ref, wt_ref, bt_ref, cost_ref, sint_ref,
                 cos_ref, sina_ref, sinb_ref,
                 qr_ref, qn_ref, vts_ref, vtw_ref, gt_ref, szat_ref, kc_ref, vc_ref,
                 ks_ref, kw_ref, qb_ref, fb_ref, ib_ref, szb_ref, cmp_s):
    xb = x_ref[0].astype(BF16)
    scale = HEAD_DIM ** -0.5 * LOG2E

    ht = _dot_nt(wt_ref[...], xb) + bt_ref[...]
    cost = cost_ref[...]
    sint = sint_ref[...]
    half = HEAD_DIM // 2
    for h in range(A_HEADS):
        blk = ht[h * HEAD_DIM:(h + 1) * HEAD_DIM]
        rot = jnp.concatenate([-blk[half:], blk[:half]], axis=0)
        qr_ref[0, h * HEAD_DIM:(h + 1) * HEAD_DIM, :] = ((blk * cost + rot * sint) * scale).astype(BF16)
        qn_ref[0, h * HEAD_DIM:(h + 1) * HEAD_DIM, :] = (blk * scale).astype(BF16)
    o = A_WIDTH
    ones_rows = jnp.where(lax.broadcasted_iota(jnp.int32, (V_ROWS - HEAD_DIM, ht.shape[1]), 0) == 0, 1.0, 0.0)
    for g in range(A_GROUPS):
        for ref, base in ((vts_ref, o), (vtw_ref, o + KV_WIDTH)):
            rows_g = ht[base + g * HEAD_DIM:base + (g + 1) * HEAD_DIM]
            ref[0, g] = jnp.concatenate([rows_g, ones_rows], axis=0).astype(BF16)
    gt_ref[0] = _sigmoid(ht[o + 2 * KV_WIDTH:o + 2 * KV_WIDTH + 2 * GATE_ROWS])
    o += 2 * KV_WIDTH + 2 * GATE_ROWS
    szat_ref[0] = _silu(ht[o:o + A_WIDTH]).astype(BF16)

    def cols(lo, hi):
        return _dot(xb, wn_ref[:, lo:hi]) + bn_ref[:, lo:hi]

    kv = cols(0, 4 * KV_WIDTH)
    cmp_s[0] = kv[:, 0:KV_WIDTH]
    cmp_s[1] = kv[:, KV_WIDTH:2 * KV_WIDTH]
    pieces = cmp_s.shape[1] // CMP_STRIDE
    for t in range(CMP_STRIDE):
        lanes_t = slice(t * KV_WIDTH, (t + 1) * KV_WIDTH)
        kc_ref[0, :, lanes_t] = cmp_s[0, pl.ds(t, pieces, stride=CMP_STRIDE), :].astype(BF16)
        vc_ref[0, :, lanes_t] = cmp_s[1, pl.ds(t, pieces, stride=CMP_STRIDE), :].astype(BF16)
    cos = cos_ref[...]
    sina = sina_ref[...]
    sinb = sinb_ref[...]

    def rope_rows(k):
        return (k * cos + pltpu.roll(k, 128 - half, axis=1) * sina + pltpu.roll(k, half, axis=1) * sinb)

    ks = rope_rows(kv[:, 2 * KV_WIDTH:3 * KV_WIDTH])
    rows = ks.shape[0]
    lane = lax.broadcasted_iota(jnp.int32, (rows, KV_WIDTH), 1)
    pos = pl.program_id(1) * rows + lax.broadcasted_iota(jnp.int32, (rows, KV_WIDTH), 0)
    blocks_per_chunk = KV_CHUNK // SLC_BLOCK
    ind = jnp.where(lane == HEAD_DIM + (pos // SLC_BLOCK) % blocks_per_chunk, 1.0, 0.0)
    ks_ref[0, 0] = (jnp.where(lane < HEAD_DIM, ks, 0.0) + ind).astype(BF16)
    ks_ref[0, 1] = (jnp.where(lane < HEAD_DIM, pltpu.roll(ks, HEAD_DIM, axis=1), 0.0) + ind).astype(BF16)
    kw = rope_rows(kv[:, 3 * KV_WIDTH:4 * KV_WIDTH])
    kw_ref[0, 0] = jnp.where(lane < HEAD_DIM, kw, 0.0).astype(BF16)
    kw_ref[0, 1] = jnp.where(lane < HEAD_DIM, pltpu.roll(kw, HEAD_DIM, axis=1), 0.0).astype(BF16)
    o = 4 * KV_WIDTH
    szb_ref[0] = _silu(cols(o, o + B_WIDTH)).astype(BF16)
    o += B_WIDTH
    qb_ref[0] = cols(o, o + B_FWIDTH).astype(BF16)
    o += B_FWIDTH
    ib_ref[0] = cols(o, o + B_WIDTH).astype(BF16)
    o += B_WIDTH
    fb_ref[0] = cols(o, o + B_FWIDTH)


def _project(x, wn, bn, wt, bt, cost, sint, cos, sina, sinb):
    B, S, D = x.shape
    T = PROJ_ROWS
    n_t = wt.shape[0]
    n_n = wn.shape[1]
    full = lambda shape: pl.BlockSpec(shape, lambda b, s: (0,) * len(shape))
    row_out = lambda w: pl.BlockSpec((1, T, w), lambda b, s: (b, s, 0))
    col_out = lambda r: pl.BlockSpec((1, r, T), lambda b, s: (b, 0, s))
    sds = jax.ShapeDtypeStruct
    out_shape = (
        sds((B, A_WIDTH, S), BF16), sds((B, A_WIDTH, S), BF16),
        sds((B, A_GROUPS, V_ROWS, S), BF16), sds((B, A_GROUPS, V_ROWS, S), BF16),
        sds((B, 2 * GATE_ROWS, S), F32),
        sds((B, A_WIDTH, S), BF16),
        sds((B, S // CMP_STRIDE, CMP_STRIDE * KV_WIDTH), BF16),
        sds((B, S // CMP_STRIDE, CMP_STRIDE * KV_WIDTH), BF16),
        sds((B, A_GROUPS, S, KV_WIDTH), BF16), sds((B, A_GROUPS, S, KV_WIDTH), BF16),
        sds((B, S, B_FWIDTH), BF16), sds((B, S, B_FWIDTH), F32),
        sds((B, S, B_WIDTH), BF16), sds((B, S, B_WIDTH), BF16),
    )
    out_specs = (
        col_out(A_WIDTH), col_out(A_WIDTH),
        pl.BlockSpec((1, A_GROUPS, V_ROWS, T), lambda b, s: (b, 0, 0, s)),
        pl.BlockSpec((1, A_GROUPS, V_ROWS, T), lambda b, s: (b, 0, 0, s)),
        col_out(2 * GATE_ROWS), col_out(A_WIDTH),
        pl.BlockSpec((1, T // CMP_STRIDE, CMP_STRIDE * KV_WIDTH), lambda b, s: (b, s, 0)),
        pl.BlockSpec((1, T // CMP_STRIDE, CMP_STRIDE * KV_WIDTH), lambda b, s: (b, s, 0)),
        pl.BlockSpec((1, A_GROUPS, T, KV_WIDTH), lambda b, s: (b, 0, s, 0)),
        pl.BlockSpec((1, A_GROUPS, T, KV_WIDTH), lambda b, s: (b, 0, s, 0)),
        row_out(B_FWIDTH), row_out(B_FWIDTH), row_out(B_WIDTH), row_out(B_WIDTH),
    )
    in_specs = [
        pl.BlockSpec((1, T, D), lambda b, s: (b, s, 0)),
        full((D, n_n)), full((1, n_n)), full((n_t, D)), full((n_t, 1)),
        pl.BlockSpec((HEAD_DIM, T), lambda b, s: (0, s)),
        pl.BlockSpec((HEAD_DIM, T), lambda b, s: (0, s)),
        pl.BlockSpec((T, KV_WIDTH), lambda b, s: (s, 0)),
        pl.BlockSpec((T, KV_WIDTH), lambda b, s: (s, 0)),
        pl.BlockSpec((T, KV_WIDTH), lambda b, s: (s, 0)),
    ]
    return pl.pallas_call(
        _proj_kernel, out_shape=out_shape, grid=(B, S // T),
        in_specs=in_specs, out_specs=out_specs,
        scratch_shapes=[pltpu.VMEM((2, T, KV_WIDTH), F32)],
        compiler_params=pltpu.CompilerParams(
            dimension_semantics=("parallel", "parallel"),
            vmem_limit_bytes=V7X_VMEM_LIMIT_BYTES),
        name="in_proj",
    )(x, wn, bn, wt, bt, cost, sint, cos, sina, sinb)


def _compress_kernel(ck_ref, cv_ref, w1k_ref, w1kf_ref, pek_ref, w2k_ref,
                     w1v_ref, w1vf_ref, pev_ref, w2vt_ref, kc_ref, vct_ref):
    nc = ck_ref.shape[1]
    hid = CMP_HIDDEN

    def hidden(c_ref, w1_ref, w1f_ref, pe_ref):
        a = _dot(c_ref[0], w1_ref[...])
        pe_term = jnp.sum(w1f_ref[...] * pe_ref[...], axis=0, keepdims=True)
        out = []
        for g in range(A_GROUPS):
            lo = a[:, 2 * g * hid:(2 * g + 1) * hid]
            hi = pltpu.roll(a[:, (2 * g + 1) * hid:(2 * g + 2) * hid], nc - 1, axis=0)
            out.append(_silu(lo + hi + pe_term).astype(BF16))
        return out

    hk = hidden(ck_ref, w1k_ref, w1kf_ref, pek_ref)
    hv = hidden(cv_ref, w1v_ref, w1vf_ref, pev_ref)
    for g in range(A_GROUPS):
        kc_ref[0, g] = _dot(hk[g], w2k_ref[...]).astype(BF16)
        vct_ref[0, g] = _dot_nt(w2vt_ref[...], hv[g]).astype(BF16)


def _compress(ck, cv, w1k, w1kf, pek, w2k, w1v, w1vf, pev, w2vt):
    B, NC, W = ck.shape
    full = lambda a: pl.BlockSpec(a.shape, lambda b: (0,) * a.ndim)
    blk = pl.BlockSpec((1, NC, W), lambda b: (b, 0, 0))
    return pl.pallas_call(
        _compress_kernel,
        out_shape=(jax.ShapeDtypeStruct((B, A_GROUPS, NC, HEAD_DIM), BF16),
                   jax.ShapeDtypeStruct((B, A_GROUPS, HEAD_DIM, NC), BF16)),
        grid=(B,),
        in_specs=[blk, blk, full(w1k), full(w1kf), full(pek), full(w2k),
                  full(w1v), full(w1vf), full(pev), full(w2vt)],
        out_specs=(pl.BlockSpec((1, A_GROUPS, NC, HEAD_DIM), lambda b: (b, 0, 0, 0)),
                   pl.BlockSpec((1, A_GROUPS, HEAD_DIM, NC), lambda b: (b, 0, 0, 0))),
        compiler_params=pltpu.CompilerParams(
            dimension_semantics=("parallel",),
            vmem_limit_bytes=V7X_VMEM_LIMIT_BYTES),
        name="kv_compress",
    )(ck, cv, w1k, w1kf, pek, w2k, w1v, w1vf, pev, w2vt)


def _nsa_kernel(jc_ref, jq_ref, qr_ref, qn_ref, gt_ref, szat_ref, kc_ref, vct_ref, ks_ref, kw_ref,
                vts_ref, vtw_ref, ovl_ref, o_ref,
                bias_s, wq_s, m_s, acc_s, ocw_s, sbuf_s, smax_s, dbuf_s, dmax_s, scb_s, swb_s, wmax_s,
                *, seq, n_sel, n_jobs, jobs_per_trip):
    nc = kc_ref.shape[2]
    n_slc = seq // SLC_BLOCK
    n_q = seq // Q_TILE
    n_ck = seq // KV_CHUNK
    lanes = A_HPG * Q_TILE
    bpc = KV_CHUNK // SLC_BLOCK
    tpc = KV_CHUNK // Q_TILE
    win_tiles = WINDOW // Q_TILE
    last = n_q - 1

    def stack_heads(ref, q0):
        return jnp.concatenate(
            [ref[0, h * HEAD_DIM:(h + 1) * HEAD_DIM, pl.ds(q0, Q_TILE)] for h in range(A_HPG)], axis=1)

    def per_head(a):
        return jnp.concatenate([a] * A_HPG, axis=1)

    def gate_row(gt, branch):
        return jnp.concatenate(
            [gt[branch * A_HPG + h:branch * A_HPG + h + 1, :] for h in range(A_HPG)], axis=1)

    def tile_start(qi):
        return pl.multiple_of(qi * Q_TILE, Q_TILE)

    lane_q = lax.broadcasted_iota(jnp.int32, (1, lanes), 1) & (Q_TILE - 1)
    r_sq = lax.broadcasted_iota(jnp.int32, (Q_TILE, lanes), 0)
    causal_sq = r_sq <= lane_q
    lower_sq = r_sq > lane_q
    q_pad = jnp.zeros((KV_WIDTH - HEAD_DIM, lanes), BF16)

    def cmp_keys(rb):
        return min(nc, -(-(rb * KV_CHUNK // CMP_STRIDE) // 128) * 128)

    def cmp_scores(qi, rb):
        return _dot(kc_ref[0, 0, 0:cmp_keys(rb), :], stack_heads(qn_ref, tile_start(qi)))

    def select_tile(qi, rb, sc):
        nk = cmp_keys(rb)
        n_causal = (qi + 1) * Q_TILE // SLC_BLOCK
        s0 = tile_start(qi)
        t_row = s0 + lax.broadcasted_iota(jnp.int32, (1, Q_TILE), 1)
        wq_s[qi] = stack_heads(qr_ref, s0)

        n_end = lax.broadcasted_iota(jnp.int32, (nk, Q_TILE), 0) * CMP_STRIDE + (CMP_BLOCK - 1)
        sc = sc + per_head(jnp.where(n_end <= t_row, 0.0, MASKED))
        mc = jnp.max(sc, axis=0, keepdims=True)
        pc = jnp.exp2(sc - mc)
        lc = jnp.sum(pc, axis=0, keepdims=True)
        pc = pc * jnp.where(s0 + lane_q >= CMP_BLOCK - 1, 1.0 / lc, 0.0)
        o_cmp = _dot(vct_ref[0, 0, :, 0:nk], pc.astype(BF16))

        ps = pc[:, 0:Q_TILE]
        for h in range(1, A_HPG):
            ps = ps + pc[:, h * Q_TILE:(h + 1) * Q_TILE]
        p_hi = ps.astype(BF16)
        p_lo = (ps - p_hi.astype(F32)).astype(BF16)
        imp2 = _dot(ovl_ref[0:8 * rb, 0:nk], jnp.concatenate([p_hi, p_lo], axis=1))
        imp = imp2[:, :Q_TILE] + imp2[:, Q_TILE:]

        j_idx = lax.broadcasted_iota(jnp.int32, (8 * rb, Q_TILE), 0)
        cur = t_row // SLC_BLOCK
        sel = j_idx <= cur
        if 8 * rb > n_sel:
            forced = (j_idx == 0) | (j_idx == cur) | (j_idx == cur - 1)
            val = jnp.where(j_idx > cur, -1.0, jnp.where(forced, FORCE_SCORE, imp))
            rows = [val[8 * j:8 * j + 8] for j in range(rb)]
            cnt = [jnp.zeros((8, Q_TILE), F32) for _ in range(rb)]
            jl = lax.broadcasted_iota(jnp.int32, (8, Q_TILE), 0)
            for kb in range(rb):
                for kl in range(8):
                    if 8 * kb + kl >= n_causal:
                        continue
                    row = rows[kb][kl:kl + 1, :]
                    for jb in range(rb):
                        if jb < kb:
                            beats = row > rows[jb]
                        elif jb > kb:
                            beats = row >= rows[jb]
                        else:
                            beats = (row > rows[jb]) | ((row == rows[jb]) & (jl > kl))
                        cnt[jb] = cnt[jb] + jnp.where(beats, 1.0, 0.0)
            sel = sel & (jnp.concatenate(cnt, axis=0) < n_sel)
        bias = per_head(jnp.where(sel, 0.0, MASKED))
        pad = jnp.zeros((bpc, lanes), F32)
        for c in range(rb):
            bias_s[qi, c] = jnp.concatenate([bias[c * bpc:(c + 1) * bpc], pad], axis=0).astype(BF16)

        gt = gt_ref[0, :, pl.ds(s0, Q_TILE)]
        ocw_s[qi] = gate_row(gt, 0) * o_cmp
        m_s[qi] = jnp.full((8, lanes), MASKED, F32)
        acc_s[qi] = jnp.zeros((V_ROWS, lanes), F32)

    def win_finish(qi, sw, mw, w0, keys=WIN_KEYS):
        pw = jnp.exp2(sw - mw)
        ow = _dot(vtw_ref[0, 0, :, pl.ds(w0, keys)], pw.astype(BF16))
        o_win = ow[0:HEAD_DIM] * (1.0 / ow[HEAD_DIM:HEAD_DIM + 1])
        gt = gt_ref[0, :, pl.ds(tile_start(qi), Q_TILE)]
        ocw_s[qi] = ocw_s[qi] + gate_row(gt, 2) * o_win

    def win_scores(qi):
        slot = qi % IN_FLIGHT
        keys = min(qi + 1, win_tiles + 1) * Q_TILE
        w0 = (qi + 1) * Q_TILE - keys
        wq = jnp.concatenate([stack_heads(qr_ref, qi * Q_TILE), q_pad], axis=0)
        sw = _dot(kw_ref[0, 0, w0:w0 + keys, :], wq)
        parts = [jnp.where(causal_sq, sw[keys - Q_TILE:], MASKED)]
        if keys > Q_TILE:
            oldest = sw[:Q_TILE]
            if qi >= win_tiles:
                oldest = jnp.where(lower_sq, oldest, MASKED)
            parts = [oldest, sw[Q_TILE:keys - Q_TILE]] + parts if keys > 2 * Q_TILE else [oldest] + parts
        sw = jnp.concatenate(parts, axis=0)
        swb_s[slot, 0:keys, :] = sw
        wmax_s[slot] = jnp.broadcast_to(jnp.max(sw, axis=0, keepdims=True), (8, lanes))

    def win_tile(qi):
        slot = qi % IN_FLIGHT
        keys = min(qi + 1, win_tiles + 1) * Q_TILE
        win_finish(qi, swb_s[slot, 0:keys, :], wmax_s[slot][0:1], (qi + 1) * Q_TILE - keys, keys)

    scb_s[0, 0:cmp_keys(1), :] = cmp_scores(0, 1)
    for u in range(min(AHEAD, n_q)):
        win_scores(u)
    for qi in range(n_q):
        rb = qi // tpc + 1
        nk = cmp_keys(rb)
        if qi + 1 < n_q:
            rb_next = (qi + 1) // tpc + 1
            scb_s[(qi + 1) & 1, 0:cmp_keys(rb_next), :] = cmp_scores(qi + 1, rb_next)
        if qi + AHEAD < n_q:
            win_scores(qi + AHEAD)
        select_tile(qi, rb, scb_s[qi & 1, 0:nk, :])
        win_tile(qi)

    wq_pad = jnp.zeros((KV_WIDTH - HEAD_DIM - 2 * bpc, lanes), BF16)

    def scores(c, qi, diag):
        rows = KV_CHUNK if diag is None else (diag + 1) * Q_TILE
        k0 = pl.multiple_of(c * KV_CHUNK, KV_CHUNK)
        wq = jnp.concatenate([wq_s[qi], bias_s[qi, c], wq_pad], axis=0)
        s = _dot(ks_ref[0, 0, pl.ds(k0, rows), :], wq)
        if diag is not None:
            head = [s[:rows - Q_TILE]] if diag > 0 else []
            s = jnp.concatenate(head + [jnp.where(causal_sq, s[rows - Q_TILE:], MASKED)], axis=0)
        return s, jnp.broadcast_to(jnp.max(s, axis=0, keepdims=True), (8, lanes))

    def absorb(c, qi, s, s_max):
        rows = s.shape[0]
        k0 = pl.multiple_of(c * KV_CHUNK, KV_CHUNK)
        m_old = m_s[qi][0:1]
        m_new = jnp.maximum(m_old, s_max[0:1])
        alpha = jnp.exp2(m_old - m_new)
        p = jnp.exp2(s - m_new)
        m_s[qi] = jnp.broadcast_to(m_new, (8, lanes))
        acc_s[qi] = alpha * acc_s[qi] + _dot(vts_ref[0, 0, :, pl.ds(k0, rows)], p.astype(BF16))

    def full_scores(j, slot):
        sbuf_s[slot], smax_s[slot] = scores(jc_ref[j], jq_ref[j], None)

    def full_group(i, _):
        j0 = jobs_per_trip * i
        for u in range(jobs_per_trip):
            full_scores(j0 + u + AHEAD, (u + AHEAD) % IN_FLIGHT)
            absorb(jc_ref[j0 + u], jq_ref[j0 + u], sbuf_s[u % IN_FLIGHT], smax_s[u % IN_FLIGHT])
        return 0

    if n_jobs:
        for u in range(AHEAD):
            full_scores(u, u)
        lax.fori_loop(0, n_jobs // jobs_per_trip, full_group, 0)

    def diag_scores(c, r):
        dbuf_s[r, 0:(r + 1) * Q_TILE, :], dmax_s[r] = scores(c, c * tpc + r, r)

    def diag_chunks(i, _):
        for k in range(DIAG_CHUNKS_PER_TRIP):
            c = DIAG_CHUNKS_PER_TRIP * i + k
            nxt = jnp.minimum(c + 1, n_ck - 1)
            for r in range(tpc):
                ahead = r + AHEAD
                diag_scores(c if ahead < tpc else nxt, ahead % tpc)
                absorb(c, c * tpc + r, dbuf_s[r, 0:(r + 1) * Q_TILE, :], dmax_s[r])
        return 0

    for r in range(AHEAD):
        diag_scores(0, r)
    lax.fori_loop(0, n_ck // DIAG_CHUNKS_PER_TRIP, diag_chunks, 0)

    def finish(qi, _):
        s0 = tile_start(qi)
        gt = gt_ref[0, :, pl.ds(s0, Q_TILE)]
        acc = acc_s[qi]
        ot = ocw_s[qi] + gate_row(gt, 1) * acc[0:HEAD_DIM] * (1.0 / acc[HEAD_DIM:HEAD_DIM + 1])
        ot = (ot * stack_heads(szat_ref, s0).astype(F32)).astype(o_ref.dtype)
        for h in range(A_HPG):
            o_ref[0, h * HEAD_DIM:(h + 1) * HEAD_DIM, pl.ds(s0, Q_TILE)] = ot[:, h * Q_TILE:(h + 1) * Q_TILE]
        return 0

    lax.fori_loop(0, n_q, finish, 0)


def _nsa(qr, qn, gt, szat, kc, vct, ks, kw, vts, vtw, ovl):
    B, _, S = qr.shape
    NC = kc.shape[2]
    n_slc = S // SLC_BLOCK
    n_sel = min(SLC_TOP_N, n_slc)
    n_q = S // Q_TILE
    n_ck = S // KV_CHUNK
    tpc = KV_CHUNK // Q_TILE
    gw = A_HPG * HEAD_DIM
    lanes = A_HPG * Q_TILE
    jobs = [(c, q) for c in range(n_ck) for q in range((c + 1) * tpc, n_q)]
    n_jobs = len(jobs)
    jobs_per_trip = max(j for j in range(IN_FLIGHT, MAX_JOBS_PER_TRIP + 1, IN_FLIGHT) if n_jobs % j == 0)
    assert tpc == IN_FLIGHT and AHEAD < IN_FLIGHT and n_ck % DIAG_CHUNKS_PER_TRIP == 0
    jobs = jobs + [jobs[-1] if jobs else (0, 0)] * AHEAD
    jc = jnp.asarray([j[0] for j in jobs], jnp.int32)
    jq = jnp.asarray([j[1] for j in jobs], jnp.int32)
    kernel = functools.partial(_nsa_kernel, seq=S, n_sel=n_sel, n_jobs=n_jobs, jobs_per_trip=jobs_per_trip)
    per_group = lambda rows, cols: pl.BlockSpec((1, 1, rows, cols), lambda b, g, *_: (b, g, 0, 0))
    in_specs = [
        pl.BlockSpec((1, gw, S), lambda b, g, *_: (b, g, 0)),
        pl.BlockSpec((1, gw, S), lambda b, g, *_: (b, g, 0)),
        pl.BlockSpec((1, GATE_ROWS, S), lambda b, g, *_: (b, g, 0)),
        pl.BlockSpec((1, gw, S), lambda b, g, *_: (b, g, 0)),
        per_group(NC, HEAD_DIM), per_group(HEAD_DIM, NC),
        per_group(S, KV_WIDTH), per_group(S, KV_WIDTH),
        per_group(V_ROWS, S), per_group(V_ROWS, S),
        pl.BlockSpec((n_slc, NC), lambda b, g, *_: (0, 0)),
    ]
    scratch = [
        pltpu.VMEM((n_q, n_ck, 2 * (KV_CHUNK // SLC_BLOCK), lanes), BF16),
        pltpu.VMEM((n_q, HEAD_DIM, lanes), BF16),
        pltpu.VMEM((n_q, 8, lanes), F32),
        pltpu.VMEM((n_q, V_ROWS, lanes), F32),
        pltpu.VMEM((n_q, HEAD_DIM, lanes), F32),
        pltpu.VMEM((IN_FLIGHT, KV_CHUNK, lanes), F32),
        pltpu.VMEM((IN_FLIGHT, 8, lanes), F32),
        pltpu.VMEM((tpc, KV_CHUNK, lanes), F32),
        pltpu.VMEM((tpc, 8, lanes), F32),
        pltpu.VMEM((2, NC, lanes), F32),
        pltpu.VMEM((IN_FLIGHT, WIN_KEYS, lanes), F32),
        pltpu.VMEM((IN_FLIGHT, 8, lanes), F32),
    ]
    return pl.pallas_call(
        kernel,
        out_shape=jax.ShapeDtypeStruct((B, A_WIDTH, S), BF16),
        grid_spec=pltpu.PrefetchScalarGridSpec(
            num_scalar_prefetch=2,
            grid=(B, A_GROUPS),
            in_specs=in_specs,
            out_specs=pl.BlockSpec((1, gw, S), lambda b, g, *_: (b, g, 0)),
            scratch_shapes=scratch),
        compiler_params=pltpu.CompilerParams(
            dimension_semantics=("parallel", "parallel"),
            vmem_limit_bytes=V7X_VMEM_LIMIT_BYTES),
        name="nsa_attention",
    )(jc, jq, qr, qn, gt, szat, kc, vct, ks, kw, vts, vtw, ovl)


def _hgrn_stages(q_ref, f_ref, i_ref, sz_ref, lb, gain, tri, bd, causal, state_s, h, out):
    T = q_ref.shape[1]
    cpg = HG_ROWS // CHUNK
    lanes = slice(h * B_KDIM, (h + 1) * B_KDIM)
    groups = [slice(gi * HG_ROWS, (gi + 1) * HG_ROWS) for gi in range(T // HG_ROWS)]
    mid = {}

    def decay():
        mid["kk"], mid["b"] = [], []
        for rows in groups:
            fg = lb + (1.0 - lb) * _sigmoid(f_ref[0, rows, lanes])
            logf = jnp.log(fg)
            l_hi = logf.astype(BF16)
            l_lo = (logf - l_hi.astype(F32)).astype(BF16)
            bb = _dot(tri, jnp.concatenate([l_hi, l_lo], axis=1))
            mid["kk"].append(1.0 - fg)
            mid["b"].append(bb[:, :B_KDIM] + bb[:, B_KDIM:])

    def intra():
        mid["qe"], mid["oi"], mid["kv"], mid["dl"] = [], [], [], []
        for rows, kk, b in zip(groups, mid["kk"], mid["b"]):
            b_last = jnp.concatenate(
                [jnp.broadcast_to(b[(c + 1) * CHUNK - 1:(c + 1) * CHUNK, :], (CHUNK, B_KDIM))
                 for c in range(cpg)], axis=0)
            qe = (_silu(q_ref[0, rows, lanes].astype(F32)) * jnp.exp(b)).astype(BF16)
            ke = (kk * jnp.exp(-b)).astype(BF16)
            kd = (kk * jnp.exp(b_last - b)).astype(BF16)
            attn = jnp.where(causal, _dot_nt(qe, ke), 0.0).astype(BF16)
            v = i_ref[0, rows, lanes]
            mid["oi"].append(_dot(attn, v))
            mid["qe"].append(qe)
            kd_bd = jnp.concatenate([kd] * cpg, axis=1) * bd
            kv_all = lax.dot_general(v, kd_bd, (((0,), (0,)), ((), ())), preferred_element_type=F32)
            for c in range(cpg):
                mid["kv"].append(kv_all[:, c * B_KDIM:(c + 1) * B_KDIM])
                mid["dl"].append(jnp.exp(b_last[c * CHUNK:c * CHUNK + 1, :]))

    def inter():
        state = state_s[h]
        outs = []
        for n in range(T // CHUNK):
            gi, c = divmod(n, cpg)
            sl = slice(c * CHUNK, (c + 1) * CHUNK)
            rows = slice(n * CHUNK, (n + 1) * CHUNK)
            o = mid["oi"][gi][sl] + _dot_nt(mid["qe"][gi][sl], state.astype(BF16))
            o = o * lax.rsqrt(jnp.mean(o * o, axis=-1, keepdims=True) + NORM_EPS)
            outs.append((o * gain * sz_ref[0, rows, lanes].astype(F32)).astype(BF16))
            state = mid["dl"][n] * state + mid["kv"][n]
        state_s[h] = state
        out.append(jnp.concatenate(outs, axis=0))

    return [decay, intra, inter]


def _out_kernel(x_ref, oat_ref, q_ref, f_ref, i_ref, sz_ref, lbl_ref, g_ref, tri_ref, bd_ref,
                wgm_ref, bgm_ref, wa_ref, wb_ref, wo_ref, lng_ref, lnb_ref, o_ref, state_s, sg_s, ya_s,
                *, alpha, layer):
    @pl.when(pl.program_id(1) == 0)
    def _():
        state_s[...] = jnp.zeros_like(state_s)

    x = x_ref[0]
    xb = x.astype(BF16)

    def gate_chunk(j):
        def run():
            sg_s[:, j:j + GATE_COLS] = _sigmoid(_dot(xb, wgm_ref[:, j:j + GATE_COLS]) + bgm_ref[:, j:j + GATE_COLS])
        return run

    def ya_chunk(j):
        def run():
            ya_s[:, j:j + YA_COLS] = lax.dot_general(oat_ref[0], wa_ref[:, j:j + YA_COLS], (((0,), (0,)), ((), ())),
                                                     preferred_element_type=F32)
        return run

    fillers = [gate_chunk(j) for j in range(0, 2 * D_MODEL, GATE_COLS)] + \
              [ya_chunk(j) for j in range(0, D_MODEL, YA_COLS)]

    lg = lbl_ref[...]
    e = jnp.exp(lg - jnp.max(lg, axis=0, keepdims=True))
    lb_all = jnp.sum(e[0:layer + 1], axis=0, keepdims=True) / jnp.sum(e, axis=0, keepdims=True)
    tri = tri_ref[...]
    bd = bd_ref[...]
    ri = lax.broadcasted_iota(jnp.int32, (HG_ROWS, HG_ROWS), 0)
    ci = lax.broadcasted_iota(jnp.int32, (HG_ROWS, HG_ROWS), 1)
    causal = (ri // CHUNK == ci // CHUNK) & (ci <= ri)

    obs = []
    stages = []
    for h in range(B_HEADS):
        lanes = slice(h * B_KDIM, (h + 1) * B_KDIM)
        stages += _hgrn_stages(q_ref, f_ref, i_ref, sz_ref, lb_all[:, lanes], g_ref[:, lanes],
                               tri, bd, causal, state_s, h, obs)
    for k, stage in enumerate(stages):
        if k < len(fillers):
            fillers[k]()
        stage()
    for filler in fillers[len(stages):]:
        filler()
    ob = jnp.concatenate(obs, axis=1)

    sub = x.shape[0] // TAIL_SPLIT
    halves = [slice(i * sub, (i + 1) * sub) for i in range(TAIL_SPLIT)]
    ybs = [_dot(ob[r], wb_ref[...]) for r in halves]
    ys = [(sg_s[r, :D_MODEL] * ya_s[r, :] + sg_s[r, D_MODEL:] * yb).astype(BF16) for r, yb in zip(halves, ybs)]
    outs = [_dot(y, wo_ref[...]) for y in ys]
    for r, out in zip(halves, outs):
        res = alpha * x[r] + out
        mu = jnp.mean(res, axis=-1, keepdims=True)
        d = res - mu
        var = jnp.mean(d * d, axis=-1, keepdims=True)
        o_ref[0, r, :] = d * lax.rsqrt(var + NORM_EPS) * lng_ref[...] + lnb_ref[...]


def _merge_out(x, oat, qb, fb, ib, szb, lb_logits, gain, tri, bd, wgm, bgm, wa, wb, wo, lng, lnb, alpha, layer):
    B, S, D = x.shape
    T = OUT_ROWS
    full = lambda a: pl.BlockSpec(a.shape, lambda b, s: (0,) * a.ndim)
    rows = lambda w: pl.BlockSpec((1, T, w), lambda b, s: (b, s, 0))
    return pl.pallas_call(
        functools.partial(_out_kernel, alpha=alpha, layer=layer),
        out_shape=jax.ShapeDtypeStruct((B, S, D), x.dtype),
        grid=(B, S // T),
        in_specs=[rows(D), pl.BlockSpec((1, A_WIDTH, T), lambda b, s: (b, 0, s)),
                  rows(B_FWIDTH), rows(B_FWIDTH), rows(B_WIDTH), rows(B_WIDTH),
                  full(lb_logits), full(gain), full(tri), full(bd),
                  full(wgm), full(bgm), full(wa), full(wb), full(wo), full(lng), full(lnb)],
        out_specs=rows(D),
        scratch_shapes=[pltpu.VMEM((B_HEADS, B_VDIM, B_KDIM), F32),
                        pltpu.VMEM((T, 2 * D), F32),
                        pltpu.VMEM((T, D), F32)],
        compiler_params=pltpu.CompilerParams(
            dimension_semantics=("parallel", "arbitrary"),
            vmem_limit_bytes=V7X_VMEM_LIMIT_BYTES),
        name="hgrn_merge_out",
    )(x, oat, qb, fb, ib, szb, lb_logits, gain, tri, bd, wgm, bgm, wa, wb, wo, lng, lnb)


def _rope_tables(S):
    inv = ROPE_THETA ** (-jnp.arange(0, HEAD_DIM, 2, dtype=F32) / HEAD_DIM)
    ang = jnp.arange(S, dtype=F32)[:, None] * inv[None, :]
    cos = jnp.concatenate([jnp.cos(ang), jnp.cos(ang)], axis=-1)
    sin = jnp.concatenate([jnp.sin(ang), jnp.sin(ang)], axis=-1)
    first = (jnp.arange(HEAD_DIM) < HEAD_DIM // 2)[None, :]
    sina = jnp.where(first, -sin, 0.0)
    sinb = jnp.where(first, 0.0, sin)
    tile = lambda a: jnp.concatenate([a] * A_GROUPS, axis=-1)
    return cos.T, sin.T, tile(cos), tile(sina), tile(sinb)


def _overlap_t(S):
    n_cmp = S // CMP_STRIDE
    n_slc = S // SLC_BLOCK
    cs = np.arange(n_cmp)[None, :] * CMP_STRIDE
    ss = np.arange(n_slc)[:, None] * SLC_BLOCK
    ov = (cs < ss + SLC_BLOCK) & (cs + CMP_BLOCK > ss) & (np.arange(n_cmp)[None, :] < n_cmp - 1)
    return jnp.asarray(ov, dtype=BF16)


def _block_tri():
    r = np.arange(HG_ROWS)
    return jnp.asarray((r[:, None] // CHUNK == r[None, :] // CHUNK) & (r[None, :] <= r[:, None]), dtype=BF16)


def _block_diag():
    r = np.arange(HG_ROWS)[:, None] // CHUNK
    c = np.arange(HG_ROWS // CHUNK * B_KDIM)[None, :] // B_KDIM
    return jnp.asarray(r == c, dtype=BF16)


def _layer(x, l, w_in, b_in, pe_k, w_k1, w_k2, pe_v, w_v1, w_v2, lb_logits, norm_g,
           w_a, w_b, w_o, ln_g, ln_b):
    B, S, D = x.shape
    alpha = (2 * DEPTH) ** 0.25
    o = _OFF
    wsl = lambda i: w_in[:, o[i]:o[i + 1]]
    bsl = lambda i: b_in[o[i]:o[i + 1]]
    kvw, kvb = wsl(1), bsl(1)
    kv_w = lambda j: kvw[:, j * KV_WIDTH:(j + 1) * KV_WIDTH]
    kv_b = lambda j: kvb[j * KV_WIDTH:(j + 1) * KV_WIDTH]
    gw, gb = wsl(2), bsl(2)
    gidx = np.zeros((A_GROUPS, GATE_ROWS), np.int32)
    gmask = np.zeros((A_GROUPS, GATE_ROWS), np.float32)
    for g in range(A_GROUPS):
        for br in range(3):
            for h in range(A_HPG):
                gidx[g, br * A_HPG + h] = (g * A_HPG + h) * 3 + br
                gmask[g, br * A_HPG + h] = 1.0
    gidx, gmask = gidx.reshape(-1), gmask.reshape(-1)
    gw_t = gw[:, gidx] * gmask[None, :]
    gb_t = gb[gidx] * gmask

    wt = jnp.concatenate([wsl(0), kv_w(3), kv_w(5), gw_t, wsl(3)], axis=1).T.astype(BF16)
    bt = jnp.concatenate([bsl(0), kv_b(3), kv_b(5), gb_t, bsl(3)])[:, None]
    wn = jnp.concatenate([kv_w(0), kv_w(1), kv_w(2), kv_w(4), wsl(7), wsl(4), wsl(6), wsl(5)],
                         axis=1).astype(BF16)
    bn = jnp.concatenate([kv_b(0), kv_b(1), kv_b(2), kv_b(4), bsl(7), bsl(4), bsl(6), bsl(5)])[None, :]
    cost, sint, cos, sina, sinb = _rope_tables(S)

    (qr, qn, vts, vtw, gt, szat, kcmp, vcmp, ks, kw, qb, fb, ib, szb) = _project(
        x, wn, bn, wt, bt, cost, sint, cos, sina, sinb)

    half = CMP_STRIDE * HEAD_DIM

    def w1_planes(w1):
        both = jnp.concatenate([w1[:half], w1[half:]], axis=1).reshape(CMP_STRIDE, HEAD_DIM, 2 * CMP_HIDDEN)
        z = jnp.zeros_like(both)
        planes = jnp.concatenate([jnp.concatenate([both, z], axis=2),
                                  jnp.concatenate([z, both], axis=2)], axis=1)
        return planes.reshape(CMP_STRIDE * KV_WIDTH, 2 * A_GROUPS * CMP_HIDDEN).astype(BF16)

    kc, vct = _compress(kcmp, vcmp,
                        w1_planes(w_k1), w_k1, pe_k.reshape(-1, 1), w_k2.astype(BF16),
                        w1_planes(w_v1), w_v1, pe_v.reshape(-1, 1), w_v2.T.astype(BF16))

    oat = _nsa(qr, qn, gt, szat, kc, vct, ks, kw, vts, vtw, _overlap_t(S))
    wgm = jnp.concatenate([wsl(8), wsl(9)], axis=1).astype(BF16)
    bgm = jnp.concatenate([bsl(8), bsl(9)])[None, :]
    return _merge_out(x, oat, qb, fb, ib, szb, lb_logits, norm_g[None, :], _block_tri(), _block_diag(),
                      wgm, bgm, w_a.astype(BF16), w_b.astype(BF16), w_o.astype(BF16),
                      ln_g[None, :], ln_b[None, :], alpha, l)


@jax.jit
def kernel(x, w_in, b_in, pe_cmp_k, w_cmp_k1, w_cmp_k2, pe_cmp_v, w_cmp_v1, w_cmp_v2,
           hgrn_lb_logits, hgrn_norm_g, w_branch_a, w_branch_b, w_out, ln_g, ln_b):
    B, S, D = x.shape
    assert D == D_MODEL and S % KV_CHUNK == 0 and S % PROJ_ROWS == 0 and S >= WIN_KEYS
    assert S % OUT_ROWS == 0 and OUT_ROWS % HG_ROWS == 0 and (S // KV_CHUNK) * KV_CHUNK == S
    for l in range(DEPTH):
        x = _layer(x, l, w_in[l], b_in[l], pe_cmp_k[l], w_cmp_k1[l], w_cmp_k2[l],
                   pe_cmp_v[l], w_cmp_v1[l], w_cmp_v2[l], hgrn_lb_logits, hgrn_norm_g[l],
                   w_branch_a[l], w_branch_b[l], w_out[l], ln_g[l], ln_b[l])
    return x
```

```python
import functools
import math

import numpy as np
import jax
import jax.numpy as jnp
from jax import lax
from jax.experimental import pallas as pl
from jax.experimental.pallas import tpu as pltpu

D_MODEL = 1024
DEPTH = 1
A_HEADS = 8
A_GROUPS = 2
A_HPG = A_HEADS // A_GROUPS
HEAD_DIM = 64
A_WIDTH = A_HEADS * HEAD_DIM
KV_WIDTH = A_GROUPS * HEAD_DIM
CMP_BLOCK = 32
CMP_STRIDE = 16
CMP_HIDDEN = 128
SLC_BLOCK = 64
SLC_TOP_N = 16
WINDOW = 512
ROPE_THETA = 10000.0
FORCE_SCORE = 1e30
B_HEADS = 4
B_KDIM = 128
B_VDIM = 128
B_FWIDTH = B_HEADS * B_KDIM
B_WIDTH = B_HEADS * B_VDIM
CHUNK = 64
NORM_EPS = 1e-5

IN_SPLITS = (A_WIDTH, 6 * KV_WIDTH, A_HEADS * 3, A_WIDTH, B_FWIDTH, B_FWIDTH,
             B_WIDTH, B_WIDTH, D_MODEL, D_MODEL)
_OFF = tuple(int(v) for v in np.cumsum((0,) + IN_SPLITS))

V7X_VMEM_LIMIT_BYTES = 56 * 1024 * 1024
PROJ_ROWS = 512
Q_TILE = 128
KV_CHUNK = 512
IN_FLIGHT = 4
AHEAD = 2
DIAG_CHUNKS_PER_TRIP = 4
MAX_JOBS_PER_TRIP = 16
WIN_KEYS = WINDOW + Q_TILE
HG_ROWS = 256
OUT_ROWS = 512
GATE_COLS = 256
YA_COLS = 512
TAIL_SPLIT = 2
MASKED = -1e30
LOG2E = math.log2(math.e)
GATE_ROWS = 16
V_ROWS = HEAD_DIM + 16

F32 = jnp.float32
BF16 = jnp.bfloat16


def _dot(a, b):
    return jnp.dot(a, b, preferred_element_type=F32)


def _dot_nt(a, b):
    return lax.dot_general(a, b, (((1,), (1,)), ((), ())), preferred_element_type=F32)


def _sigmoid(x):
    return 1.0 / (1.0 + jnp.exp(-x))


def _silu(x):
    return x * _sigmoid(x)


def _proj_kernel(x_ref, wn_ref, bn_ref, wt_ref, bt_ref, cost_ref, sint_ref,
                 cos_ref, sina_ref, sinb_ref,
                 qr_ref, qn_ref, vts_ref, vtw_ref, gt_ref, szat_ref, kc_ref, vc_ref,
                 ks_ref, kw_ref, qb_ref, fb_ref, ib_ref, szb_ref, cmp_s):
    xb = x_ref[0].astype(BF16)
    scale = HEAD_DIM ** -0.5 * LOG2E

    ht = _dot_nt(wt_ref[...], xb) + bt_ref[...]
    cost = cost_ref[...]
    sint = sint_ref[...]
    half = HEAD_DIM // 2
    for h in range(A_HEADS):
        blk = ht[h * HEAD_DIM:(h + 1) * HEAD_DIM]
        rot = jnp.concatenate([-blk[half:], blk[:half]], axis=0)
        qr_ref[0, h * HEAD_DIM:(h + 1) * HEAD_DIM, :] = ((blk * cost + rot * sint) * scale).astype(BF16)
        qn_ref[0, h * HEAD_DIM:(h + 1) * HEAD_DIM, :] = (blk * scale).astype(BF16)
    o = A_WIDTH
    ones_rows = jnp.where(lax.broadcasted_iota(jnp.int32, (V_ROWS - HEAD_DIM, ht.shape[1]), 0) == 0, 1.0, 0.0)
    for g in range(A_GROUPS):
        for ref, base in ((vts_ref, o), (vtw_ref, o + KV_WIDTH)):
            rows_g = ht[base + g * HEAD_DIM:base + (g + 1) * HEAD_DIM]
            ref[0, g] = jnp.concatenate([rows_g, ones_rows], axis=0).astype(BF16)
    gt_ref[0] = _sigmoid(ht[o + 2 * KV_WIDTH:o + 2 * KV_WIDTH + 2 * GATE_ROWS])
    o += 2 * KV_WIDTH + 2 * GATE_ROWS
    szat_ref[0] = _silu(ht[o:o + A_WIDTH]).astype(BF16)

    def cols(lo, hi):
        return _dot(xb, wn_ref[:, lo:hi]) + bn_ref[:, lo:hi]

    kv = cols(0, 4 * KV_WIDTH)
    cmp_s[0] = kv[:, 0:KV_WIDTH]
    cmp_s[1] = kv[:, KV_WIDTH:2 * KV_WIDTH]
    pieces = cmp_s.shape[1] // CMP_STRIDE
    for t in range(CMP_STRIDE):
        lanes_t = slice(t * KV_WIDTH, (t + 1) * KV_WIDTH)
        kc_ref[0, :, lanes_t] = cmp_s[0, pl.ds(t, pieces, stride=CMP_STRIDE), :].astype(BF16)
        vc_ref[0, :, lanes_t] = cmp_s[1, pl.ds(t, pieces, stride=CMP_STRIDE), :].astype(BF16)
    cos = cos_ref[...]
    sina = sina_ref[...]
    sinb = sinb_ref[...]

    def rope_rows(k):
        return (k * cos + pltpu.roll(k, 128 - half, axis=1) * sina + pltpu.roll(k, half, axis=1) * sinb)

    ks = rope_rows(kv[:, 2 * KV_WIDTH:3 * KV_WIDTH])
    rows = ks.shape[0]
    lane = lax.broadcasted_iota(jnp.int32, (rows, KV_WIDTH), 1)
    pos = pl.program_id(1) * rows + lax.broadcasted_iota(jnp.int32, (rows, KV_WIDTH), 0)
    blocks_per_chunk = KV_CHUNK // SLC_BLOCK
    ind = jnp.where(lane == HEAD_DIM + (pos // SLC_BLOCK) % blocks_per_chunk, 1.0, 0.0)
    ks_ref[0, 0] = (jnp.where(lane < HEAD_DIM, ks, 0.0) + ind).astype(BF16)
    ks_ref[0, 1] = (jnp.where(lane < HEAD_DIM, pltpu.roll(ks, HEAD_DIM, axis=1), 0.0) + ind).astype(BF16)
    kw = rope_rows(kv[:, 3 * KV_WIDTH:4 * KV_WIDTH])
    kw_ref[0, 0] = jnp.where(lane < HEAD_DIM, kw, 0.0).astype(BF16)
    kw_ref[0, 1] = jnp.where(lane < HEAD_DIM, pltpu.roll(kw, HEAD_DIM, axis=1), 0.0).astype(BF16)
    o = 4 * KV_WIDTH
    szb_ref[0] = _silu(cols(o, o + B_WIDTH)).astype(BF16)
    o += B_WIDTH
    qb_ref[0] = cols(o, o + B_FWIDTH).astype(BF16)
    o += B_FWIDTH
    ib_ref[0] = cols(o, o + B_WIDTH).astype(BF16)
    o += B_WIDTH
    fb_ref[0] = cols(o, o + B_FWIDTH)


def _project(x, wn, bn, wt, bt, cost, sint, cos, sina, sinb):
    B, S, D = x.shape
    T = PROJ_ROWS
    n_t = wt.shape[0]
    n_n = wn.shape[1]
    full = lambda shape: pl.BlockSpec(shape, lambda b, s: (0,) * len(shape))
    row_out = lambda w: pl.BlockSpec((1, T, w), lambda b, s: (b, s, 0))
    col_out = lambda r: pl.BlockSpec((1, r, T), lambda b, s: (b, 0, s))
    sds = jax.ShapeDtypeStruct
    out_shape = (
        sds((B, A_WIDTH, S), BF16), sds((B, A_WIDTH, S), BF16),
        sds((B, A_GROUPS, V_ROWS, S), BF16), sds((B, A_GROUPS, V_ROWS, S), BF16),
        sds((B, 2 * GATE_ROWS, S), F32),
        sds((B, A_WIDTH, S), BF16),
        sds((B, S // CMP_STRIDE, CMP_STRIDE * KV_WIDTH), BF16),
        sds((B, S // CMP_STRIDE, CMP_STRIDE * KV_WIDTH), BF16),
        sds((B, A_GROUPS, S, KV_WIDTH), BF16), sds((B, A_GROUPS, S, KV_WIDTH), BF16),
        sds((B, S, B_FWIDTH), BF16), sds((B, S, B_FWIDTH), F32),
        sds((B, S, B_WIDTH), BF16), sds((B, S, B_WIDTH), BF16),
    )
    out_specs = (
        col_out(A_WIDTH), col_out(A_WIDTH),
        pl.BlockSpec((1, A_GROUPS, V_ROWS, T), lambda b, s: (b, 0, 0, s)),
        pl.BlockSpec((1, A_GROUPS, V_ROWS, T), lambda b, s: (b, 0, 0, s)),
        col_out(2 * GATE_ROWS), col_out(A_WIDTH),
        pl.BlockSpec((1, T // CMP_STRIDE, CMP_STRIDE * KV_WIDTH), lambda b, s: (b, s, 0)),
        pl.BlockSpec((1, T // CMP_STRIDE, CMP_STRIDE * KV_WIDTH), lambda b, s: (b, s, 0)),
        pl.BlockSpec((1, A_GROUPS, T, KV_WIDTH), lambda b, s: (b, 0, s, 0)),
        pl.BlockSpec((1, A_GROUPS, T, KV_WIDTH), lambda b, s: (b, 0, s, 0)),
        row_out(B_FWIDTH), row_out(B_FWIDTH), row_out(B_WIDTH), row_out(B_WIDTH),
    )
    in_specs = [
        pl.BlockSpec((1, T, D), lambda b, s: (b, s, 0)),
        full((D, n_n)), full((1, n_n)), full((n_t, D)), full((n_t, 1)),
        pl.BlockSpec((HEAD_DIM, T), lambda b, s: (0, s)),
        pl.BlockSpec((HEAD_DIM, T), lambda b, s: (0, s)),
        pl.BlockSpec((T, KV_WIDTH), lambda b, s: (s, 0)),
        pl.BlockSpec((T, KV_WIDTH), lambda b, s: (s, 0)),
        pl.BlockSpec((T, KV_WIDTH), lambda b, s: (s, 0)),
    ]
    return pl.pallas_call(
        _proj_kernel, out_shape=out_shape, grid=(B, S // T),
        in_specs=in_specs, out_specs=out_specs,
        scratch_shapes=[pltpu.VMEM((2, T, KV_WIDTH), F32)],
        compiler_params=pltpu.CompilerParams(
            dimension_semantics=("parallel", "parallel"),
            vmem_limit_bytes=V7X_VMEM_LIMIT_BYTES),
        name="in_proj",
    )(x, wn, bn, wt, bt, cost, sint, cos, sina, sinb)


def _compress_kernel(ck_ref, cv_ref, w1k_ref, w1kf_ref, pek_ref, w2k_ref,
                     w1v_ref, w1vf_ref, pev_ref, w2vt_ref, kc_ref, vct_ref):
    nc = ck_ref.shape[1]
    hid = CMP_HIDDEN

    def hidden(c_ref, w1_ref, w1f_ref, pe_ref):
        a = _dot(c_ref[0], w1_ref[...])
        pe_term = jnp.sum(w1f_ref[...] * pe_ref[...], axis=0, keepdims=True)
        out = []
        for g in range(A_GROUPS):
            lo = a[:, 2 * g * hid:(2 * g + 1) * hid]
            hi = pltpu.roll(a[:, (2 * g + 1) * hid:(2 * g + 2) * hid], nc - 1, axis=0)
            out.append(_silu(lo + hi + pe_term).astype(BF16))
        return out

    hk = hidden(ck_ref, w1k_ref, w1kf_ref, pek_ref)
    hv = hidden(cv_ref, w1v_ref, w1vf_ref, pev_ref)
    for g in range(A_GROUPS):
        kc_ref[0, g] = _dot(hk[g], w2k_ref[...]).astype(BF16)
        vct_ref[0, g] = _dot_nt(w2vt_ref[...], hv[g]).astype(BF16)


def _compress(ck, cv, w1k, w1kf, pek, w2k, w1v, w1vf, pev, w2vt):
    B, NC, W = ck.shape
    full = lambda a: pl.BlockSpec(a.shape, lambda b: (0,) * a.ndim)
    blk = pl.BlockSpec((1, NC, W), lambda b: (b, 0, 0))
    return pl.pallas_call(
        _compress_kernel,
        out_shape=(jax.ShapeDtypeStruct((B, A_GROUPS, NC, HEAD_DIM), BF16),
                   jax.ShapeDtypeStruct((B, A_GROUPS, HEAD_DIM, NC), BF16)),
        grid=(B,),
        in_specs=[blk, blk, full(w1k), full(w1kf), full(pek), full(w2k),
                  full(w1v), full(w1vf), full(pev), full(w2vt)],
        out_specs=(pl.BlockSpec((1, A_GROUPS, NC, HEAD_DIM), lambda b: (b, 0, 0, 0)),
                   pl.BlockSpec((1, A_GROUPS, HEAD_DIM, NC), lambda b: (b, 0, 0, 0))),
        compiler_params=pltpu.CompilerParams(
            dimension_semantics=("parallel",),
            vmem_limit_bytes=V7X_VMEM_LIMIT_BYTES),
        name="kv_compress",
    )(ck, cv, w1k, w1kf, pek, w2k, w1v, w1vf, pev, w2vt)


def _nsa_kernel(jc_ref, jq_ref, qr_ref, qn_ref, gt_ref, szat_ref, kc_ref, vct_ref, ks_ref, kw_ref,
                vts_ref, vtw_ref, ovl_ref, o_ref,
                bias_s, wq_s, m_s, acc_s, ocw_s, sbuf_s, smax_s, dbuf_s, dmax_s, scb_s, swb_s, wmax_s,
                *, seq, n_sel, n_jobs, jobs_per_trip):
    nc = kc_ref.shape[2]
    n_q = seq // Q_TILE
    n_ck = seq // KV_CHUNK
    lanes = A_HPG * Q_TILE
    bpc = KV_CHUNK // SLC_BLOCK
    tpc = KV_CHUNK // Q_TILE
    win_tiles = WINDOW // Q_TILE

    def stack_heads(ref, q0):
        return jnp.concatenate(
            [ref[0, h * HEAD_DIM:(h + 1) * HEAD_DIM, pl.ds(q0, Q_TILE)] for h in range(A_HPG)], axis=1)

    def per_head(a):
        return jnp.concatenate([a] * A_HPG, axis=1)

    def gate_row(gt, branch):
        return jnp.concatenate(
            [gt[branch * A_HPG + h:branch * A_HPG + h + 1, :] for h in range(A_HPG)], axis=1)

    def tile_start(qi):
        return pl.multiple_of(qi * Q_TILE, Q_TILE)

    lane_q = lax.broadcasted_iota(jnp.int32, (1, lanes), 1) & (Q_TILE - 1)
    r_sq = lax.broadcasted_iota(jnp.int32, (Q_TILE, lanes), 0)
    causal_sq = r_sq <= lane_q
    lower_sq = r_sq > lane_q
    q_pad = jnp.zeros((KV_WIDTH - HEAD_DIM, lanes), BF16)

    def cmp_keys(rb):
        return min(nc, -(-(rb * KV_CHUNK // CMP_STRIDE) // 128) * 128)

    def cmp_rows(qi):
        return min(nc, -(-((qi + 1) * Q_TILE // CMP_STRIDE) // 16) * 16)

    def cmp_scores(qi):
        return _dot(kc_ref[0, 0, 0:cmp_rows(qi), :], stack_heads(qn_ref, tile_start(qi)))

    def zero_rows(a, rows):
        if a.shape[0] == rows:
            return a
        return jnp.concatenate([a, jnp.zeros((rows - a.shape[0], a.shape[1]), a.dtype)], axis=0)

    def select_tile(qi, rb, sc):
        nk = cmp_keys(rb)
        n_causal = (qi + 1) * Q_TILE // SLC_BLOCK
        s0 = tile_start(qi)
        t_row = s0 + lax.broadcasted_iota(jnp.int32, (1, Q_TILE), 1)
        wq_s[qi] = stack_heads(qr_ref, s0)

        nv = cmp_rows(qi)
        lo = max(0, (qi * Q_TILE - (CMP_BLOCK - 1)) // CMP_STRIDE // 8 * 8)
        n_end = (lo + lax.broadcasted_iota(jnp.int32, (nv - lo, Q_TILE), 0)) * CMP_STRIDE + (CMP_BLOCK - 1)
        edge = sc[lo:] + per_head(jnp.where(n_end <= t_row, 0.0, MASKED))
        sc = jnp.concatenate([sc[:lo], edge], axis=0) if lo else edge
        mc = jnp.max(sc, axis=0, keepdims=True)
        pc = jnp.exp2(sc - mc)
        lc = jnp.sum(pc, axis=0, keepdims=True)
        pc = pc * jnp.where(s0 + lane_q >= CMP_BLOCK - 1, 1.0 / lc, 0.0)
        o_cmp = _dot(vct_ref[0, 0, :, 0:nk], zero_rows(pc.astype(BF16), nk))

        ps = pc[:, 0:Q_TILE]
        for h in range(1, A_HPG):
            ps = ps + pc[:, h * Q_TILE:(h + 1) * Q_TILE]
        p_hi = ps.astype(BF16)
        p_lo = (ps - p_hi.astype(F32)).astype(BF16)
        imp2 = _dot(ovl_ref[0:8 * rb, 0:nk],
                    zero_rows(jnp.concatenate([p_hi, p_lo], axis=1), nk))
        imp = imp2[:, :Q_TILE] + imp2[:, Q_TILE:]

        j_idx = lax.broadcasted_iota(jnp.int32, (8 * rb, Q_TILE), 0)
        cur = t_row // SLC_BLOCK
        sel = j_idx <= cur
        if 8 * rb > n_sel:
            forced = (j_idx == 0) | (j_idx == cur) | (j_idx == cur - 1)
            val = jnp.where(j_idx > cur, -1.0, jnp.where(forced, FORCE_SCORE, imp))
            rows = [val[8 * j:8 * j + 8] for j in range(rb)]
            cnt = [jnp.zeros((8, Q_TILE), F32) for _ in range(rb)]
            jl = lax.broadcasted_iota(jnp.int32, (8, Q_TILE), 0)
            for kb in range(rb):
                for kl in range(8):
                    if 8 * kb + kl >= n_causal:
                        continue
                    row = rows[kb][kl:kl + 1, :]
                    for jb in range(rb):
                        if jb < kb:
                            beats = row > rows[jb]
                        elif jb > kb:
                            beats = row >= rows[jb]
                        else:
                            beats = (row > rows[jb]) | ((row == rows[jb]) & (jl > kl))
                        cnt[jb] = cnt[jb] + jnp.where(beats, 1.0, 0.0)
            sel = sel & (jnp.concatenate(cnt, axis=0) < n_sel)
        bias = per_head(jnp.where(sel, 0.0, MASKED))
        pad = jnp.zeros((bpc, lanes), F32)
        for c in range(rb):
            bias_s[qi, c] = jnp.concatenate([bias[c * bpc:(c + 1) * bpc], pad], axis=0).astype(BF16)

        gt = gt_ref[0, :, pl.ds(s0, Q_TILE)]
        ocw_s[qi] = gate_row(gt, 0) * o_cmp
        m_s[qi] = jnp.full((8, lanes), MASKED, F32)
        acc_s[qi] = jnp.zeros((V_ROWS, lanes), F32)

    def win_finish(qi, sw, mw, w0, keys=WIN_KEYS):
        pw = jnp.exp2(sw - mw)
        ow = _dot(vtw_ref[0, 0, :, pl.ds(w0, keys)], pw.astype(BF16))
        o_win = ow[0:HEAD_DIM] * (1.0 / ow[HEAD_DIM:HEAD_DIM + 1])
        gt = gt_ref[0, :, pl.ds(tile_start(qi), Q_TILE)]
        ocw_s[qi] = ocw_s[qi] + gate_row(gt, 2) * o_win

    def win_scores(qi):
        slot = qi % IN_FLIGHT
        keys = min(qi + 1, win_tiles + 1) * Q_TILE
        w0 = (qi + 1) * Q_TILE - keys
        wq = jnp.concatenate([stack_heads(qr_ref, qi * Q_TILE), q_pad], axis=0)
        sw = _dot(kw_ref[0, 0, w0:w0 + keys, :], wq)
        parts = [jnp.where(causal_sq, sw[keys - Q_TILE:], MASKED)]
        if keys > Q_TILE:
            oldest = sw[:Q_TILE]
            if qi >= win_tiles:
                oldest = jnp.where(lower_sq, oldest, MASKED)
            parts = [oldest, sw[Q_TILE:keys - Q_TILE]] + parts if keys > 2 * Q_TILE else [oldest] + parts
        sw = jnp.concatenate(parts, axis=0)
        swb_s[slot, 0:keys, :] = sw
        wmax_s[slot] = jnp.broadcast_to(jnp.max(sw, axis=0, keepdims=True), (8, lanes))

    def win_tile(qi):
        slot = qi % IN_FLIGHT
        keys = min(qi + 1, win_tiles + 1) * Q_TILE
        win_finish(qi, swb_s[slot, 0:keys, :], wmax_s[slot][0:1], (qi + 1) * Q_TILE - keys, keys)

    scb_s[0, 0:cmp_rows(0), :] = cmp_scores(0)
    for u in range(min(AHEAD, n_q)):
        win_scores(u)
    for qi in range(n_q):
        rb = qi // tpc + 1
        if qi + 1 < n_q:
            scb_s[(qi + 1) & 1, 0:cmp_rows(qi + 1), :] = cmp_scores(qi + 1)
        if qi + AHEAD < n_q:
            win_scores(qi + AHEAD)
        select_tile(qi, rb, scb_s[qi & 1, 0:cmp_rows(qi), :])
        win_tile(qi)

    wq_pad = jnp.zeros((KV_WIDTH - HEAD_DIM - 2 * bpc, lanes), BF16)

    def scores(c, qi, diag):
        rows = KV_CHUNK if diag is None else (diag + 1) * Q_TILE
        k0 = pl.multiple_of(c * KV_CHUNK, KV_CHUNK)
        wq = jnp.concatenate([wq_s[qi], bias_s[qi, c], wq_pad], axis=0)
        s = _dot(ks_ref[0, 0, pl.ds(k0, rows), :], wq)
        if diag is not None:
            head = [s[:rows - Q_TILE]] if diag > 0 else []
            s = jnp.concatenate(head + [jnp.where(causal_sq, s[rows - Q_TILE:], MASKED)], axis=0)
        return s, jnp.broadcast_to(jnp.max(s, axis=0, keepdims=True), (8, lanes))

    def absorb(c, qi, s, s_max):
        rows = s.shape[0]
        k0 = pl.multiple_of(c * KV_CHUNK, KV_CHUNK)
        m_old = m_s[qi][0:1]
        m_new = jnp.maximum(m_old, s_max[0:1])
        alpha = jnp.exp2(m_old - m_new)
        p = jnp.exp2(s - m_new)
        m_s[qi] = jnp.broadcast_to(m_new, (8, lanes))
        acc_s[qi] = alpha * acc_s[qi] + _dot(vts_ref[0, 0, :, pl.ds(k0, rows)], p.astype(BF16))

    def full_scores(j, slot):
        sbuf_s[slot], smax_s[slot] = scores(jc_ref[j], jq_ref[j], None)

    def full_group(i, _):
        j0 = jobs_per_trip * i
        for u in range(jobs_per_trip):
            full_scores(j0 + u + AHEAD, (u + AHEAD) % IN_FLIGHT)
            absorb(jc_ref[j0 + u], jq_ref[j0 + u], sbuf_s[u % IN_FLIGHT], smax_s[u % IN_FLIGHT])
        return 0

    if n_jobs:
        for u in range(AHEAD):
            full_scores(u, u)
        lax.fori_loop(0, n_jobs // jobs_per_trip, full_group, 0)

    def diag_scores(c, r):
        dbuf_s[r, 0:(r + 1) * Q_TILE, :], dmax_s[r] = scores(c, c * tpc + r, r)

    def diag_chunks(i, _):
        for k in range(DIAG_CHUNKS_PER_TRIP):
            c = DIAG_CHUNKS_PER_TRIP * i + k
            nxt = jnp.minimum(c + 1, n_ck - 1)
            for r in range(tpc):
                ahead = r + AHEAD
                diag_scores(c if ahead < tpc else nxt, ahead % tpc)
                absorb(c, c * tpc + r, dbuf_s[r, 0:(r + 1) * Q_TILE, :], dmax_s[r])
        return 0

    for r in range(AHEAD):
        diag_scores(0, r)
    lax.fori_loop(0, n_ck // DIAG_CHUNKS_PER_TRIP, diag_chunks, 0)

    def finish(qi, _):
        s0 = tile_start(qi)
        gt = gt_ref[0, :, pl.ds(s0, Q_TILE)]
        acc = acc_s[qi]
        ot = ocw_s[qi] + gate_row(gt, 1) * acc[0:HEAD_DIM] * (1.0 / acc[HEAD_DIM:HEAD_DIM + 1])
        ot = (ot * stack_heads(szat_ref, s0).astype(F32)).astype(o_ref.dtype)
        for h in range(A_HPG):
            o_ref[0, h * HEAD_DIM:(h + 1) * HEAD_DIM, pl.ds(s0, Q_TILE)] = ot[:, h * Q_TILE:(h + 1) * Q_TILE]
        return 0

    lax.fori_loop(0, n_q, finish, 0)


def _nsa(qr, qn, gt, szat, kc, vct, ks, kw, vts, vtw, ovl):
    B, _, S = qr.shape
    NC = kc.shape[2]
    n_slc = S // SLC_BLOCK
    n_sel = min(SLC_TOP_N, n_slc)
    n_q = S // Q_TILE
    n_ck = S // KV_CHUNK
    tpc = KV_CHUNK // Q_TILE
    gw = A_HPG * HEAD_DIM
    lanes = A_HPG * Q_TILE
    jobs = [(c, q) for c in range(n_ck) for q in range((c + 1) * tpc, n_q)]
    n_jobs = len(jobs)
    jobs_per_trip = max(j for j in range(IN_FLIGHT, MAX_JOBS_PER_TRIP + 1, IN_FLIGHT) if n_jobs % j == 0)
    assert tpc == IN_FLIGHT and AHEAD < IN_FLIGHT and n_ck % DIAG_CHUNKS_PER_TRIP == 0
    jobs = jobs + [jobs[-1] if jobs else (0, 0)] * AHEAD
    jc = jnp.asarray([j[0] for j in jobs], jnp.int32)
    jq = jnp.asarray([j[1] for j in jobs], jnp.int32)
    kernel = functools.partial(_nsa_kernel, seq=S, n_sel=n_sel, n_jobs=n_jobs, jobs_per_trip=jobs_per_trip)
    per_group = lambda rows, cols: pl.BlockSpec((1, 1, rows, cols), lambda b, g, *_: (b, g, 0, 0))
    in_specs = [
        pl.BlockSpec((1, gw, S), lambda b, g, *_: (b, g, 0)),
        pl.BlockSpec((1, gw, S), lambda b, g, *_: (b, g, 0)),
        pl.BlockSpec((1, GATE_ROWS, S), lambda b, g, *_: (b, g, 0)),
        pl.BlockSpec((1, gw, S), lambda b, g, *_: (b, g, 0)),
        per_group(NC, HEAD_DIM), per_group(HEAD_DIM, NC),
        per_group(S, KV_WIDTH), per_group(S, KV_WIDTH),
        per_group(V_ROWS, S), per_group(V_ROWS, S),
        pl.BlockSpec((n_slc, NC), lambda b, g, *_: (0, 0)),
    ]
    scratch = [
        pltpu.VMEM((n_q, n_ck, 2 * (KV_CHUNK // SLC_BLOCK), lanes), BF16),
        pltpu.VMEM((n_q, HEAD_DIM, lanes), BF16),
        pltpu.VMEM((n_q, 8, lanes), F32),
        pltpu.VMEM((n_q, V_ROWS, lanes), F32),
        pltpu.VMEM((n_q, HEAD_DIM, lanes), F32),
        pltpu.VMEM((IN_FLIGHT, KV_CHUNK, lanes), F32),
        pltpu.VMEM((IN_FLIGHT, 8, lanes), F32),
        pltpu.VMEM((tpc, KV_CHUNK, lanes), F32),
        pltpu.VMEM((tpc, 8, lanes), F32),
        pltpu.VMEM((2, NC, lanes), F32),
        pltpu.VMEM((IN_FLIGHT, WIN_KEYS, lanes), F32),
        pltpu.VMEM((IN_FLIGHT, 8, lanes), F32),
    ]
    return pl.pallas_call(
        kernel,
        out_shape=jax.ShapeDtypeStruct((B, A_WIDTH, S), BF16),
        grid_spec=pltpu.PrefetchScalarGridSpec(
            num_scalar_prefetch=2,
            grid=(B, A_GROUPS),
            in_specs=in_specs,
            out_specs=pl.BlockSpec((1, gw, S), lambda b, g, *_: (b, g, 0)),
            scratch_shapes=scratch),
        compiler_params=pltpu.CompilerParams(
            dimension_semantics=("parallel", "parallel"),
            vmem_limit_bytes=V7X_VMEM_LIMIT_BYTES),
        name="nsa_attention",
    )(jc, jq, qr, qn, gt, szat, kc, vct, ks, kw, vts, vtw, ovl)


def _hgrn_stages(q_ref, f_ref, i_ref, sz_ref, lb, gain, tri, bd, causal, state_s, h, out):
    T = q_ref.shape[1]
    cpg = HG_ROWS // CHUNK
    lanes = slice(h * B_KDIM, (h + 1) * B_KDIM)
    groups = [slice(gi * HG_ROWS, (gi + 1) * HG_ROWS) for gi in range(T // HG_ROWS)]
    mid = {}

    def decay():
        mid["kk"], mid["b"] = [], []
        for rows in groups:
            fg = lb + (1.0 - lb) * _sigmoid(f_ref[0, rows, lanes])
            logf = jnp.log(fg)
            l_hi = logf.astype(BF16)
            l_lo = (logf - l_hi.astype(F32)).astype(BF16)
            bb = _dot(tri, jnp.concatenate([l_hi, l_lo], axis=1))
            mid["kk"].append(1.0 - fg)
            mid["b"].append(bb[:, :B_KDIM] + bb[:, B_KDIM:])

    def intra():
        mid["qe"], mid["oi"], mid["kv"], mid["dl"] = [], [], [], []
        for rows, kk, b in zip(groups, mid["kk"], mid["b"]):
            b_last = jnp.concatenate(
                [jnp.broadcast_to(b[(c + 1) * CHUNK - 1:(c + 1) * CHUNK, :], (CHUNK, B_KDIM))
                 for c in range(cpg)], axis=0)
            qe = (_silu(q_ref[0, rows, lanes].astype(F32)) * jnp.exp(b)).astype(BF16)
            ke = (kk * jnp.exp(-b)).astype(BF16)
            kd = (kk * jnp.exp(b_last - b)).astype(BF16)
            attn = jnp.where(causal, _dot_nt(qe, ke), 0.0).astype(BF16)
            v = i_ref[0, rows, lanes]
            mid["oi"].append(_dot(attn, v))
            mid["qe"].append(qe)
            kd_bd = jnp.concatenate([kd] * cpg, axis=1) * bd
            kv_all = lax.dot_general(v, kd_bd, (((0,), (0,)), ((), ())), preferred_element_type=F32)
            for c in range(cpg):
                mid["kv"].append(kv_all[:, c * B_KDIM:(c + 1) * B_KDIM])
                mid["dl"].append(jnp.exp(b_last[c * CHUNK:c * CHUNK + 1, :]))

    def inter():
        state = state_s[h]
        outs = []
        for n in range(T // CHUNK):
            gi, c = divmod(n, cpg)
            sl = slice(c * CHUNK, (c + 1) * CHUNK)
            rows = slice(n * CHUNK, (n + 1) * CHUNK)
            o = mid["oi"][gi][sl] + _dot_nt(mid["qe"][gi][sl], state.astype(BF16))
            o = o * lax.rsqrt(jnp.mean(o * o, axis=-1, keepdims=True) + NORM_EPS)
            outs.append((o * gain * sz_ref[0, rows, lanes].astype(F32)).astype(BF16))
            state = mid["dl"][n] * state + mid["kv"][n]
        state_s[h] = state
        out.append(jnp.concatenate(outs, axis=0))

    return [decay, intra, inter]


def _out_kernel(x_ref, oat_ref, q_ref, f_ref, i_ref, sz_ref, lbl_ref, g_ref, tri_ref, bd_ref,
                wgm_ref, bgm_ref, wa_ref, wb_ref, wo_ref, lng_ref, lnb_ref, o_ref, state_s, sg_s, ya_s,
                *, alpha, layer):
    @pl.when(pl.program_id(1) == 0)
    def _():
        state_s[...] = jnp.zeros_like(state_s)

    x = x_ref[0]
    xb = x.astype(BF16)

    def gate_chunk(j):
        def run():
            sg_s[:, j:j + GATE_COLS] = _sigmoid(_dot(xb, wgm_ref[:, j:j + GATE_COLS]) + bgm_ref[:, j:j + GATE_COLS])
        return run

    def ya_chunk(j):
        def run():
            ya_s[:, j:j + YA_COLS] = lax.dot_general(oat_ref[0], wa_ref[:, j:j + YA_COLS], (((0,), (0,)), ((), ())),
                                                     preferred_element_type=F32)
        return run

    fillers = [gate_chunk(j) for j in range(0, 2 * D_MODEL, GATE_COLS)] + \
              [ya_chunk(j) for j in range(0, D_MODEL, YA_COLS)]

    lg = lbl_ref[...]
    e = jnp.exp(lg - jnp.max(lg, axis=0, keepdims=True))
    lb_all = jnp.sum(e[0:layer + 1], axis=0, keepdims=True) / jnp.sum(e, axis=0, keepdims=True)
    tri = tri_ref[...]
    bd = bd_ref[...]
    ri = lax.broadcasted_iota(jnp.int32, (HG_ROWS, HG_ROWS), 0)
    ci = lax.broadcasted_iota(jnp.int32, (HG_ROWS, HG_ROWS), 1)
    causal = (ri // CHUNK == ci // CHUNK) & (ci <= ri)

    obs = []
    stages = []
    for h in range(B_HEADS):
        lanes = slice(h * B_KDIM, (h + 1) * B_KDIM)
        stages += _hgrn_stages(q_ref, f_ref, i_ref, sz_ref, lb_all[:, lanes], g_ref[:, lanes],
                               tri, bd, causal, state_s, h, obs)
    for k, stage in enumerate(stages):
        if k < len(fillers):
            fillers[k]()
        stage()
    for filler in fillers[len(stages):]:
        filler()
    ob = jnp.concatenate(obs, axis=1)

    sub = x.shape[0] // TAIL_SPLIT
    halves = [slice(i * sub, (i + 1) * sub) for i in range(TAIL_SPLIT)]
    ybs = [_dot(ob[r], wb_ref[...]) for r in halves]
    ys = [(sg_s[r, :D_MODEL] * ya_s[r, :] + sg_s[r, D_MODEL:] * yb).astype(BF16) for r, yb in zip(halves, ybs)]
    outs = [_dot(y, wo_ref[...]) for y in ys]
    for r, out in zip(halves, outs):
        res = alpha * x[r] + out
        mu = jnp.mean(res, axis=-1, keepdims=True)
        d = res - mu
        var = jnp.mean(d * d, axis=-1, keepdims=True)
        o_ref[0, r, :] = d * lax.rsqrt(var + NORM_EPS) * lng_ref[...] + lnb_ref[...]


def _merge_out(x, oat, qb, fb, ib, szb, lb_logits, gain, tri, bd, wgm, bgm, wa, wb, wo, lng, lnb, alpha, layer):
    B, S, D = x.shape
    T = OUT_ROWS
    full = lambda a: pl.BlockSpec(a.shape, lambda b, s: (0,) * a.ndim)
    rows = lambda w: pl.BlockSpec((1, T, w), lambda b, s: (b, s, 0))
    return pl.pallas_call(
        functools.partial(_out_kernel, alpha=alpha, layer=layer),
        out_shape=jax.ShapeDtypeStruct((B, S, D), x.dtype),
        grid=(B, S // T),
        in_specs=[rows(D), pl.BlockSpec((1, A_WIDTH, T), lambda b, s: (b, 0, s)),
                  rows(B_FWIDTH), rows(B_FWIDTH), rows(B_WIDTH), rows(B_WIDTH),
                  full(lb_logits), full(gain), full(tri), full(bd),
                  full(wgm), full(bgm), full(wa), full(wb), full(wo), full(lng), full(lnb)],
        out_specs=rows(D),
        scratch_shapes=[pltpu.VMEM((B_HEADS, B_VDIM, B_KDIM), F32),
                        pltpu.VMEM((T, 2 * D), F32),
                        pltpu.VMEM((T, D), F32)],
        compiler_params=pltpu.CompilerParams(
            dimension_semantics=("parallel", "arbitrary"),
            vmem_limit_bytes=V7X_VMEM_LIMIT_BYTES),
        name="hgrn_merge_out",
    )(x, oat, qb, fb, ib, szb, lb_logits, gain, tri, bd, wgm, bgm, wa, wb, wo, lng, lnb)


def _rope_tables(S):
    inv = ROPE_THETA ** (-jnp.arange(0, HEAD_DIM, 2, dtype=F32) / HEAD_DIM)
    ang = jnp.arange(S, dtype=F32)[:, None] * inv[None, :]
    cos = jnp.concatenate([jnp.cos(ang), jnp.cos(ang)], axis=-1)
    sin = jnp.concatenate([jnp.sin(ang), jnp.sin(ang)], axis=-1)
    first = (jnp.arange(HEAD_DIM) < HEAD_DIM // 2)[None, :]
    sina = jnp.where(first, -sin, 0.0)
    sinb = jnp.where(first, 0.0, sin)
    tile = lambda a: jnp.concatenate([a] * A_GROUPS, axis=-1)
    return cos.T, sin.T, tile(cos), tile(sina), tile(sinb)


def _overlap_t(S):
    n_cmp = S // CMP_STRIDE
    n_slc = S // SLC_BLOCK
    cs = np.arange(n_cmp)[None, :] * CMP_STRIDE
    ss = np.arange(n_slc)[:, None] * SLC_BLOCK
    ov = (cs < ss + SLC_BLOCK) & (cs + CMP_BLOCK > ss) & (np.arange(n_cmp)[None, :] < n_cmp - 1)
    return jnp.asarray(ov, dtype=BF16)


def _block_tri():
    r = np.arange(HG_ROWS)
    return jnp.asarray((r[:, None] // CHUNK == r[None, :] // CHUNK) & (r[None, :] <= r[:, None]), dtype=BF16)


def _block_diag():
    r = np.arange(HG_ROWS)[:, None] // CHUNK
    c = np.arange(HG_ROWS // CHUNK * B_KDIM)[None, :] // B_KDIM
    return jnp.asarray(r == c, dtype=BF16)


def _layer(x, l, w_in, b_in, pe_k, w_k1, w_k2, pe_v, w_v1, w_v2, lb_logits, norm_g,
           w_a, w_b, w_o, ln_g, ln_b):
    B, S, D = x.shape
    alpha = (2 * DEPTH) ** 0.25
    o = _OFF
    wsl = lambda i: w_in[:, o[i]:o[i + 1]]
    bsl = lambda i: b_in[o[i]:o[i + 1]]
    kvw, kvb = wsl(1), bsl(1)
    kv_w = lambda j: kvw[:, j * KV_WIDTH:(j + 1) * KV_WIDTH]
    kv_b = lambda j: kvb[j * KV_WIDTH:(j + 1) * KV_WIDTH]
    gw, gb = wsl(2), bsl(2)
    gidx = np.zeros((A_GROUPS, GATE_ROWS), np.int32)
    gmask = np.zeros((A_GROUPS, GATE_ROWS), np.float32)
    for g in range(A_GROUPS):
        for br in range(3):
            for h in range(A_HPG):
                gidx[g, br * A_HPG + h] = (g * A_HPG + h) * 3 + br
                gmask[g, br * A_HPG + h] = 1.0
    gidx, gmask = gidx.reshape(-1), gmask.reshape(-1)
    gw_t = gw[:, gidx] * gmask[None, :]
    gb_t = gb[gidx] * gmask

    wt = jnp.concatenate([wsl(0), kv_w(3), kv_w(5), gw_t, wsl(3)], axis=1).T.astype(BF16)
    bt = jnp.concatenate([bsl(0), kv_b(3), kv_b(5), gb_t, bsl(3)])[:, None]
    wn = jnp.concatenate([kv_w(0), kv_w(1), kv_w(2), kv_w(4), wsl(7), wsl(4), wsl(6), wsl(5)],
                         axis=1).astype(BF16)
    bn = jnp.concatenate([kv_b(0), kv_b(1), kv_b(2), kv_b(4), bsl(7), bsl(4), bsl(6), bsl(5)])[None, :]
    cost, sint, cos, sina, sinb = _rope_tables(S)

    (qr, qn, vts, vtw, gt, szat, kcmp, vcmp, ks, kw, qb, fb, ib, szb) = _project(
        x, wn, bn, wt, bt, cost, sint, cos, sina, sinb)

    half = CMP_STRIDE * HEAD_DIM

    def w1_planes(w1):
        both = jnp.concatenate([w1[:half], w1[half:]], axis=1).reshape(CMP_STRIDE, HEAD_DIM, 2 * CMP_HIDDEN)
        z = jnp.zeros_like(both)
        planes = jnp.concatenate([jnp.concatenate([both, z], axis=2),
                                  jnp.concatenate([z, both], axis=2)], axis=1)
        return planes.reshape(CMP_STRIDE * KV_WIDTH, 2 * A_GROUPS * CMP_HIDDEN).astype(BF16)

    kc, vct = _compress(kcmp, vcmp,
                        w1_planes(w_k1), w_k1, pe_k.reshape(-1, 1), w_k2.astype(BF16),
                        w1_planes(w_v1), w_v1, pe_v.reshape(-1, 1), w_v2.T.astype(BF16))

    oat = _nsa(qr, qn, gt, szat, kc, vct, ks, kw, vts, vtw, _overlap_t(S))
    wgm = jnp.concatenate([wsl(8), wsl(9)], axis=1).astype(BF16)
    bgm = jnp.concatenate([bsl(8), bsl(9)])[None, :]
    return _merge_out(x, oat, qb, fb, ib, szb, lb_logits, norm_g[None, :], _block_tri(), _block_diag(),
                      wgm, bgm, w_a.astype(BF16), w_b.astype(BF16), w_o.astype(BF16),
                      ln_g[None, :], ln_b[None, :], alpha, l)


@jax.jit
def kernel(x, w_in, b_in, pe_cmp_k, w_cmp_k1, w_cmp_k2, pe_cmp_v, w_cmp_v1, w_cmp_v2,
           hgrn_lb_logits, hgrn_norm_g, w_branch_a, w_branch_b, w_out, ln_g, ln_b):
    B, S, D = x.shape
    assert D == D_MODEL and S % KV_CHUNK == 0 and S % PROJ_ROWS == 0 and S >= WIN_KEYS
    assert S % OUT_ROWS == 0 and OUT_ROWS % HG_ROWS == 0 and (S // KV_CHUNK) * KV_CHUNK == S
    for l in range(DEPTH):
        x = _layer(x, l, w_in[l], b_in[l], pe_cmp_k[l], w_cmp_k1[l], w_cmp_k2[l],
                   pe_cmp_v[l], w_cmp_v1[l], w_cmp_v2[l], hgrn_lb_logits, hgrn_norm_g[l],
                   w_branch_a[l], w_branch_b[l], w_out[l], ln_g[l], ln_b[l])
    return x
```

```python
import functools
import math

import numpy as np
import jax
import jax.numpy as jnp
from jax import lax
from jax.experimental import pallas as pl
from jax.experimental.pallas import tpu as pltpu

D_MODEL = 1024
DEPTH = 1
A_HEADS = 8
A_GROUPS = 2
A_HPG = A_HEADS // A_GROUPS
HEAD_DIM = 64
A_WIDTH = A_HEADS * HEAD_DIM
KV_WIDTH = A_GROUPS * HEAD_DIM
CMP_BLOCK = 32
CMP_STRIDE = 16
CMP_HIDDEN = 128
SLC_BLOCK = 64
SLC_TOP_N = 16
WINDOW = 512
ROPE_THETA = 10000.0
FORCE_SCORE = 1e30
B_HEADS = 4
B_KDIM = 128
B_VDIM = 128
B_FWIDTH = B_HEADS * B_KDIM
B_WIDTH = B_HEADS * B_VDIM
CHUNK = 64
NORM_EPS = 1e-5

IN_SPLITS = (A_WIDTH, 6 * KV_WIDTH, A_HEADS * 3, A_WIDTH, B_FWIDTH, B_FWIDTH,
             B_WIDTH, B_WIDTH, D_MODEL, D_MODEL)
_OFF = tuple(int(v) for v in np.cumsum((0,) + IN_SPLITS))

V7X_VMEM_LIMIT_BYTES = 56 * 1024 * 1024
PROJ_ROWS = 512
Q_TILE = 128
KV_CHUNK = 512
IN_FLIGHT = 4
AHEAD = 2
DIAG_CHUNKS_PER_TRIP = 4
MAX_JOBS_PER_TRIP = 16
WIN_KEYS = WINDOW + Q_TILE
HG_ROWS = 256
OUT_ROWS = 512
GATE_COLS = 256
YA_COLS = 512
TAIL_SPLIT = 2
MASKED = -1e30
LOG2E = math.log2(math.e)
GATE_ROWS = 16
V_ROWS = HEAD_DIM + 16

F32 = jnp.float32
BF16 = jnp.bfloat16


def _dot(a, b):
    return jnp.dot(a, b, preferred_element_type=F32)


def _dot_nt(a, b):
    return lax.dot_general(a, b, (((1,), (1,)), ((), ())), preferred_element_type=F32)


def _sigmoid(x):
    return 1.0 / (1.0 + jnp.exp(-x))


def _silu(x):
    return x * _sigmoid(x)


def _proj_kernel(x_ref, wn_ref, bn_ref, wt_ref, bt_ref, cost_ref, sint_ref,
                 cos_ref, sina_ref, sinb_ref,
                 qr_ref, qn_ref, vts_ref, vtw_ref, gt_ref, szat_ref, kc_ref, vc_ref,
                 ks_ref, kw_ref, qb_ref, fb_ref, ib_ref, szb_ref, cmp_s):
    xb = x_ref[0].astype(BF16)
    scale = HEAD_DIM ** -0.5 * LOG2E

    ht = _dot_nt(wt_ref[...], xb) + bt_ref[...]
    cost = cost_ref[...]
    sint = sint_ref[...]
    half = HEAD_DIM // 2
    for h in range(A_HEADS):
        blk = ht[h * HEAD_DIM:(h + 1) * HEAD_DIM]
        rot = jnp.concatenate([-blk[half:], blk[:half]], axis=0)
        qr_ref[0, h * HEAD_DIM:(h + 1) * HEAD_DIM, :] = ((blk * cost + rot * sint) * scale).astype(BF16)
        qn_ref[0, h * HEAD_DIM:(h + 1) * HEAD_DIM, :] = (blk * scale).astype(BF16)
    o = A_WIDTH
    ones_rows = jnp.where(lax.broadcasted_iota(jnp.int32, (V_ROWS - HEAD_DIM, ht.shape[1]), 0) == 0, 1.0, 0.0)
    for g in range(A_GROUPS):
        for ref, base in ((vts_ref, o), (vtw_ref, o + KV_WIDTH)):
            rows_g = ht[base + g * HEAD_DIM:base + (g + 1) * HEAD_DIM]
            ref[0, g] = jnp.concatenate([rows_g, ones_rows], axis=0).astype(BF16)
    gt_ref[0] = _sigmoid(ht[o + 2 * KV_WIDTH:o + 2 * KV_WIDTH + 2 * GATE_ROWS])
    o += 2 * KV_WIDTH + 2 * GATE_ROWS
    szat_ref[0] = _silu(ht[o:o + A_WIDTH]).astype(BF16)

    def cols(lo, hi):
        return _dot(xb, wn_ref[:, lo:hi]) + bn_ref[:, lo:hi]

    kv = cols(0, 4 * KV_WIDTH)
    cmp_s[0] = kv[:, 0:KV_WIDTH]
    cmp_s[1] = kv[:, KV_WIDTH:2 * KV_WIDTH]
    pieces = cmp_s.shape[1] // CMP_STRIDE
    for t in range(CMP_STRIDE):
        lanes_t = slice(t * KV_WIDTH, (t + 1) * KV_WIDTH)
        kc_ref[0, :, lanes_t] = cmp_s[0, pl.ds(t, pieces, stride=CMP_STRIDE), :].astype(BF16)
        vc_ref[0, :, lanes_t] = cmp_s[1, pl.ds(t, pieces, stride=CMP_STRIDE), :].astype(BF16)
    cos = cos_ref[...]
    sina = sina_ref[...]
    sinb = sinb_ref[...]

    def rope_rows(k):
        return (k * cos + pltpu.roll(k, 128 - half, axis=1) * sina + pltpu.roll(k, half, axis=1) * sinb)

    ks = rope_rows(kv[:, 2 * KV_WIDTH:3 * KV_WIDTH])
    rows = ks.shape[0]
    lane = lax.broadcasted_iota(jnp.int32, (rows, KV_WIDTH), 1)
    pos = pl.program_id(1) * rows + lax.broadcasted_iota(jnp.int32, (rows, KV_WIDTH), 0)
    blocks_per_chunk = KV_CHUNK // SLC_BLOCK
    ind = jnp.where(lane == HEAD_DIM + (pos // SLC_BLOCK) % blocks_per_chunk, 1.0, 0.0)
    ks_ref[0, 0] = (jnp.where(lane < HEAD_DIM, ks, 0.0) + ind).astype(BF16)
    ks_ref[0, 1] = (jnp.where(lane < HEAD_DIM, pltpu.roll(ks, HEAD_DIM, axis=1), 0.0) + ind).astype(BF16)
    kw = rope_rows(kv[:, 3 * KV_WIDTH:4 * KV_WIDTH])
    kw_ref[0, 0] = jnp.where(lane < HEAD_DIM, kw, 0.0).astype(BF16)
    kw_ref[0, 1] = jnp.where(lane < HEAD_DIM, pltpu.roll(kw, HEAD_DIM, axis=1), 0.0).astype(BF16)
    o = 4 * KV_WIDTH
    szb_ref[0] = _silu(cols(o, o + B_WIDTH)).astype(BF16)
    o += B_WIDTH
    qb_ref[0] = cols(o, o + B_FWIDTH).astype(BF16)
    o += B_FWIDTH
    ib_ref[0] = cols(o, o + B_WIDTH).astype(BF16)
    o += B_WIDTH
    fb_ref[0] = cols(o, o + B_FWIDTH)


def _project(x, wn, bn, wt, bt, cost, sint, cos, sina, sinb):
    B, S, D = x.shape
    T = PROJ_ROWS
    n_t = wt.shape[0]
    n_n = wn.shape[1]
    full = lambda shape: pl.BlockSpec(shape, lambda b, s: (0,) * len(shape))
    row_out = lambda w: pl.BlockSpec((1, T, w), lambda b, s: (b, s, 0))
    col_out = lambda r: pl.BlockSpec((1, r, T), lambda b, s: (b, 0, s))
    sds = jax.ShapeDtypeStruct
    out_shape = (
        sds((B, A_WIDTH, S), BF16), sds((B, A_WIDTH, S), BF16),
        sds((B, A_GROUPS, V_ROWS, S), BF16), sds((B, A_GROUPS, V_ROWS, S), BF16),
        sds((B, 2 * GATE_ROWS, S), F32),
        sds((B, A_WIDTH, S), BF16),
        sds((B, S // CMP_STRIDE, CMP_STRIDE * KV_WIDTH), BF16),
        sds((B, S // CMP_STRIDE, CMP_STRIDE * KV_WIDTH), BF16),
        sds((B, A_GROUPS, S, KV_WIDTH), BF16), sds((B, A_GROUPS, S, KV_WIDTH), BF16),
        sds((B, S, B_FWIDTH), BF16), sds((B, S, B_FWIDTH), F32),
        sds((B, S, B_WIDTH), BF16), sds((B, S, B_WIDTH), BF16),
    )
    out_specs = (
        col_out(A_WIDTH), col_out(A_WIDTH),
        pl.BlockSpec((1, A_GROUPS, V_ROWS, T), lambda b, s: (b, 0, 0, s)),
        pl.BlockSpec((1, A_GROUPS, V_ROWS, T), lambda b, s: (b, 0, 0, s)),
        col_out(2 * GATE_ROWS), col_out(A_WIDTH),
        pl.BlockSpec((1, T // CMP_STRIDE, CMP_STRIDE * KV_WIDTH), lambda b, s: (b, s, 0)),
        pl.BlockSpec((1, T // CMP_STRIDE, CMP_STRIDE * KV_WIDTH), lambda b, s: (b, s, 0)),
        pl.BlockSpec((1, A_GROUPS, T, KV_WIDTH), lambda b, s: (b, 0, s, 0)),
        pl.BlockSpec((1, A_GROUPS, T, KV_WIDTH), lambda b, s: (b, 0, s, 0)),
        row_out(B_FWIDTH), row_out(B_FWIDTH), row_out(B_WIDTH), row_out(B_WIDTH),
    )
    in_specs = [
        pl.BlockSpec((1, T, D), lambda b, s: (b, s, 0)),
        full((D, n_n)), full((1, n_n)), full((n_t, D)), full((n_t, 1)),
        pl.BlockSpec((HEAD_DIM, T), lambda b, s: (0, s)),
        pl.BlockSpec((HEAD_DIM, T), lambda b, s: (0, s)),
        pl.BlockSpec((T, KV_WIDTH), lambda b, s: (s, 0)),
        pl.BlockSpec((T, KV_WIDTH), lambda b, s: (s, 0)),
        pl.BlockSpec((T, KV_WIDTH), lambda b, s: (s, 0)),
    ]
    return pl.pallas_call(
        _proj_kernel, out_shape=out_shape, grid=(B, S // T),
        in_specs=in_specs, out_specs=out_specs,
        scratch_shapes=[pltpu.VMEM((2, T, KV_WIDTH), F32)],
        compiler_params=pltpu.CompilerParams(
            dimension_semantics=("parallel", "parallel"),
            vmem_limit_bytes=V7X_VMEM_LIMIT_BYTES),
        name="in_proj",
    )(x, wn, bn, wt, bt, cost, sint, cos, sina, sinb)


def _wt_kernel(w_ref, g_ref, o_ref, *, gate_row, gate_rows):
    def block_t(ref, c):
        return ref[:, c * 128:(c + 1) * 128].astype(F32).T.astype(o_ref.dtype)

    head = gate_row // 128
    for c in range(w_ref.shape[1] // 128):
        r0 = c * 128 if c < head else c * 128 + gate_rows
        o_ref[r0:r0 + 128, :] = block_t(w_ref, c)
    o_ref[gate_row:gate_row + gate_rows, :] = block_t(g_ref, 0)[0:gate_rows]


def _transpose_weights(w, g, gate_row, gate_rows):
    D, n = w.shape
    assert n % 128 == 0 and gate_row % 128 == 0 and gate_rows % 16 == 0 and g.shape == (D, 128)
    return pl.pallas_call(
        functools.partial(_wt_kernel, gate_row=gate_row, gate_rows=gate_rows),
        out_shape=jax.ShapeDtypeStruct((n + gate_rows, D), w.dtype),
        compiler_params=pltpu.CompilerParams(vmem_limit_bytes=V7X_VMEM_LIMIT_BYTES),
        name="wt_transpose",
    )(w, g)


def _compress_kernel(ck_ref, cv_ref, w1k_ref, w1kf_ref, pek_ref, w2k_ref,
                     w1v_ref, w1vf_ref, pev_ref, w2vt_ref, kc_ref, vct_ref):
    nc = ck_ref.shape[1]
    hid = CMP_HIDDEN

    def hidden(c_ref, w1_ref, w1f_ref, pe_ref):
        a = _dot(c_ref[0], w1_ref[...])
        pe_term = jnp.sum(w1f_ref[...] * pe_ref[...], axis=0, keepdims=True)
        out = []
        for g in range(A_GROUPS):
            lo = a[:, 2 * g * hid:(2 * g + 1) * hid]
            hi = pltpu.roll(a[:, (2 * g + 1) * hid:(2 * g + 2) * hid], nc - 1, axis=0)
            out.append(_silu(lo + hi + pe_term).astype(BF16))
        return out

    hk = hidden(ck_ref, w1k_ref, w1kf_ref, pek_ref)
    hv = hidden(cv_ref, w1v_ref, w1vf_ref, pev_ref)
    for g in range(A_GROUPS):
        kc_ref[0, g] = _dot(hk[g], w2k_ref[...]).astype(BF16)
        vct_ref[0, g] = _dot_nt(w2vt_ref[...], hv[g]).astype(BF16)


def _compress(ck, cv, w1k, w1kf, pek, w2k, w1v, w1vf, pev, w2vt):
    B, NC, W = ck.shape
    full = lambda a: pl.BlockSpec(a.shape, lambda b: (0,) * a.ndim)
    blk = pl.BlockSpec((1, NC, W), lambda b: (b, 0, 0))
    return pl.pallas_call(
        _compress_kernel,
        out_shape=(jax.ShapeDtypeStruct((B, A_GROUPS, NC, HEAD_DIM), BF16),
                   jax.ShapeDtypeStruct((B, A_GROUPS, HEAD_DIM, NC), BF16)),
        grid=(B,),
        in_specs=[blk, blk, full(w1k), full(w1kf), full(pek), full(w2k),
                  full(w1v), full(w1vf), full(pev), full(w2vt)],
        out_specs=(pl.BlockSpec((1, A_GROUPS, NC, HEAD_DIM), lambda b: (b, 0, 0, 0)),
                   pl.BlockSpec((1, A_GROUPS, HEAD_DIM, NC), lambda b: (b, 0, 0, 0))),
        compiler_params=pltpu.CompilerParams(
            dimension_semantics=("parallel",),
            vmem_limit_bytes=V7X_VMEM_LIMIT_BYTES),
        name="kv_compress",
    )(ck, cv, w1k, w1kf, pek, w2k, w1v, w1vf, pev, w2vt)


def _nsa_kernel(jc_ref, jq_ref, qr_ref, qn_ref, gt_ref, szat_ref, kc_ref, vct_ref, ks_ref, kw_ref,
                vts_ref, vtw_ref, ovl_ref, o_ref,
                bias_s, wq_s, m_s, acc_s, ocw_s, sbuf_s, smax_s, dbuf_s, dmax_s, scb_s, swb_s, wmax_s,
                *, seq, n_sel, n_jobs, jobs_per_trip):
    nc = kc_ref.shape[2]
    n_q = seq // Q_TILE
    n_ck = seq // KV_CHUNK
    lanes = A_HPG * Q_TILE
    bpc = KV_CHUNK // SLC_BLOCK
    tpc = KV_CHUNK // Q_TILE
    win_tiles = WINDOW // Q_TILE

    def stack_heads(ref, q0):
        return jnp.concatenate(
            [ref[0, h * HEAD_DIM:(h + 1) * HEAD_DIM, pl.ds(q0, Q_TILE)] for h in range(A_HPG)], axis=1)

    def per_head(a):
        return jnp.concatenate([a] * A_HPG, axis=1)

    def gate_row(gt, branch):
        return jnp.concatenate(
            [gt[branch * A_HPG + h:branch * A_HPG + h + 1, :] for h in range(A_HPG)], axis=1)

    def tile_start(qi):
        return pl.multiple_of(qi * Q_TILE, Q_TILE)

    lane_q = lax.broadcasted_iota(jnp.int32, (1, lanes), 1) & (Q_TILE - 1)
    r_sq = lax.broadcasted_iota(jnp.int32, (Q_TILE, lanes), 0)
    causal_sq = r_sq <= lane_q
    lower_sq = r_sq > lane_q
    q_pad = jnp.zeros((KV_WIDTH - HEAD_DIM, lanes), BF16)

    def cmp_keys(rb):
        return min(nc, -(-(rb * KV_CHUNK // CMP_STRIDE) // 128) * 128)

    def cmp_rows(qi):
        return min(nc, -(-((qi + 1) * Q_TILE // CMP_STRIDE) // 16) * 16)

    def cmp_scores(qi):
        return _dot(kc_ref[0, 0, 0:cmp_rows(qi), :], stack_heads(qn_ref, tile_start(qi)))

    def zero_rows(a, rows):
        if a.shape[0] == rows:
            return a
        return jnp.concatenate([a, jnp.zeros((rows - a.shape[0], a.shape[1]), a.dtype)], axis=0)

    def select_tile(qi, rb, sc):
        nk = cmp_keys(rb)
        n_causal = (qi + 1) * Q_TILE // SLC_BLOCK
        s0 = tile_start(qi)
        t_row = s0 + lax.broadcasted_iota(jnp.int32, (1, Q_TILE), 1)
        wq_s[qi] = stack_heads(qr_ref, s0)

        nv = cmp_rows(qi)
        lo = max(0, (qi * Q_TILE - (CMP_BLOCK - 1)) // CMP_STRIDE // 8 * 8)
        n_end = (lo + lax.broadcasted_iota(jnp.int32, (nv - lo, Q_TILE), 0)) * CMP_STRIDE + (CMP_BLOCK - 1)
        edge = sc[lo:] + per_head(jnp.where(n_end <= t_row, 0.0, MASKED))
        sc = jnp.concatenate([sc[:lo], edge], axis=0) if lo else edge
        mc = jnp.max(sc, axis=0, keepdims=True)
        pc = jnp.exp2(sc - mc)
        lc = jnp.sum(pc, axis=0, keepdims=True)
        pc = pc * jnp.where(s0 + lane_q >= CMP_BLOCK - 1, 1.0 / lc, 0.0)
        o_cmp = _dot(vct_ref[0, 0, :, 0:nk], zero_rows(pc.astype(BF16), nk))

        ps = pc[:, 0:Q_TILE]
        for h in range(1, A_HPG):
            ps = ps + pc[:, h * Q_TILE:(h + 1) * Q_TILE]
        p_hi = ps.astype(BF16)
        p_lo = (ps - p_hi.astype(F32)).astype(BF16)
        imp2 = _dot(ovl_ref[0:8 * rb, 0:nk],
                    zero_rows(jnp.concatenate([p_hi, p_lo], axis=1), nk))
        imp = imp2[:, :Q_TILE] + imp2[:, Q_TILE:]

        j_idx = lax.broadcasted_iota(jnp.int32, (8 * rb, Q_TILE), 0)
        cur = t_row // SLC_BLOCK
        sel = j_idx <= cur
        if 8 * rb > n_sel:
            forced = (j_idx == 0) | (j_idx == cur) | (j_idx == cur - 1)
            val = jnp.where(j_idx > cur, -1.0, jnp.where(forced, FORCE_SCORE, imp))
            rows = [val[8 * j:8 * j + 8] for j in range(rb)]
            cnt = [jnp.zeros((8, Q_TILE), F32) for _ in range(rb)]
            jl = lax.broadcasted_iota(jnp.int32, (8, Q_TILE), 0)
            for kb in range(rb):
                for kl in range(8):
                    if 8 * kb + kl >= n_causal:
                        continue
                    row = rows[kb][kl:kl + 1, :]
                    for jb in range(rb):
                        if jb < kb:
                            beats = row > rows[jb]
                        elif jb > kb:
                            beats = row >= rows[jb]
                        else:
                            beats = (row > rows[jb]) | ((row == rows[jb]) & (jl > kl))
                        cnt[jb] = cnt[jb] + jnp.where(beats, 1.0, 0.0)
            sel = sel & (jnp.concatenate(cnt, axis=0) < n_sel)
        bias = per_head(jnp.where(sel, 0.0, MASKED))
        pad = jnp.zeros((bpc, lanes), F32)
        for c in range(rb):
            bias_s[qi, c] = jnp.concatenate([bias[c * bpc:(c + 1) * bpc], pad], axis=0).astype(BF16)

        gt = gt_ref[0, :, pl.ds(s0, Q_TILE)]
        ocw_s[qi] = gate_row(gt, 0) * o_cmp
        m_s[qi] = jnp.full((8, lanes), MASKED, F32)
        acc_s[qi] = jnp.zeros((V_ROWS, lanes), F32)

    def win_finish(qi, sw, mw, w0, keys=WIN_KEYS):
        pw = jnp.exp2(sw - mw)
        ow = _dot(vtw_ref[0, 0, :, pl.ds(w0, keys)], pw.astype(BF16))
        o_win = ow[0:HEAD_DIM] * (1.0 / ow[HEAD_DIM:HEAD_DIM + 1])
        gt = gt_ref[0, :, pl.ds(tile_start(qi), Q_TILE)]
        ocw_s[qi] = ocw_s[qi] + gate_row(gt, 2) * o_win

    def win_scores(qi):
        slot = qi % IN_FLIGHT
        keys = min(qi + 1, win_tiles + 1) * Q_TILE
        w0 = (qi + 1) * Q_TILE - keys
        wq = jnp.concatenate([stack_heads(qr_ref, qi * Q_TILE), q_pad], axis=0)
        sw = _dot(kw_ref[0, 0, w0:w0 + keys, :], wq)
        parts = [jnp.where(causal_sq, sw[keys - Q_TILE:], MASKED)]
        if keys > Q_TILE:
            oldest = sw[:Q_TILE]
            if qi >= win_tiles:
                oldest = jnp.where(lower_sq, oldest, MASKED)
            parts = [oldest, sw[Q_TILE:keys - Q_TILE]] + parts if keys > 2 * Q_TILE else [oldest] + parts
        sw = jnp.concatenate(parts, axis=0)
        swb_s[slot, 0:keys, :] = sw
        wmax_s[slot] = jnp.broadcast_to(jnp.max(sw, axis=0, keepdims=True), (8, lanes))

    def win_tile(qi):
        slot = qi % IN_FLIGHT
        keys = min(qi + 1, win_tiles + 1) * Q_TILE
        win_finish(qi, swb_s[slot, 0:keys, :], wmax_s[slot][0:1], (qi + 1) * Q_TILE - keys, keys)

    scb_s[0, 0:cmp_rows(0), :] = cmp_scores(0)
    for u in range(min(AHEAD, n_q)):
        win_scores(u)
    for qi in range(n_q):
        rb = qi // tpc + 1
        if qi + 1 < n_q:
            scb_s[(qi + 1) & 1, 0:cmp_rows(qi + 1), :] = cmp_scores(qi + 1)
        if qi + AHEAD < n_q:
            win_scores(qi + AHEAD)
        select_tile(qi, rb, scb_s[qi & 1, 0:cmp_rows(qi), :])
        win_tile(qi)

    wq_pad = jnp.zeros((KV_WIDTH - HEAD_DIM - 2 * bpc, lanes), BF16)

    def scores(c, qi, diag):
        rows = KV_CHUNK if diag is None else (diag + 1) * Q_TILE
        k0 = pl.multiple_of(c * KV_CHUNK, KV_CHUNK)
        wq = jnp.concatenate([wq_s[qi], bias_s[qi, c], wq_pad], axis=0)
        s = _dot(ks_ref[0, 0, pl.ds(k0, rows), :], wq)
        if diag is not None:
            head = [s[:rows - Q_TILE]] if diag > 0 else []
            s = jnp.concatenate(head + [jnp.where(causal_sq, s[rows - Q_TILE:], MASKED)], axis=0)
        return s, jnp.broadcast_to(jnp.max(s, axis=0, keepdims=True), (8, lanes))

    def absorb(c, qi, s, s_max):
        rows = s.shape[0]
        k0 = pl.multiple_of(c * KV_CHUNK, KV_CHUNK)
        m_old = m_s[qi][0:1]
        m_new = jnp.maximum(m_old, s_max[0:1])
        alpha = jnp.exp2(m_old - m_new)
        p = jnp.exp2(s - m_new)
        m_s[qi] = jnp.broadcast_to(m_new, (8, lanes))
        acc_s[qi] = alpha * acc_s[qi] + _dot(vts_ref[0, 0, :, pl.ds(k0, rows)], p.astype(BF16))

    def full_scores(j, slot):
        sbuf_s[slot], smax_s[slot] = scores(jc_ref[j], jq_ref[j], None)

    def full_group(i, _):
        j0 = jobs_per_trip * i
        for u in range(jobs_per_trip):
            full_scores(j0 + u + AHEAD, (u + AHEAD) % IN_FLIGHT)
            absorb(jc_ref[j0 + u], jq_ref[j0 + u], sbuf_s[u % IN_FLIGHT], smax_s[u % IN_FLIGHT])
        return 0

    if n_jobs:
        for u in range(AHEAD):
            full_scores(u, u)
        lax.fori_loop(0, n_jobs // jobs_per_trip, full_group, 0)

    def diag_scores(c, r):
        dbuf_s[r, 0:(r + 1) * Q_TILE, :], dmax_s[r] = scores(c, c * tpc + r, r)

    def diag_chunks(i, _):
        for k in range(DIAG_CHUNKS_PER_TRIP):
            c = DIAG_CHUNKS_PER_TRIP * i + k
            nxt = jnp.minimum(c + 1, n_ck - 1)
            for r in range(tpc):
                ahead = r + AHEAD
                diag_scores(c if ahead < tpc else nxt, ahead % tpc)
                absorb(c, c * tpc + r, dbuf_s[r, 0:(r + 1) * Q_TILE, :], dmax_s[r])
        return 0

    for r in range(AHEAD):
        diag_scores(0, r)
    lax.fori_loop(0, n_ck // DIAG_CHUNKS_PER_TRIP, diag_chunks, 0)

    def finish(qi, _):
        s0 = tile_start(qi)
        gt = gt_ref[0, :, pl.ds(s0, Q_TILE)]
        acc = acc_s[qi]
        ot = ocw_s[qi] + gate_row(gt, 1) * acc[0:HEAD_DIM] * (1.0 / acc[HEAD_DIM:HEAD_DIM + 1])
        ot = (ot * stack_heads(szat_ref, s0).astype(F32)).astype(o_ref.dtype)
        for h in range(A_HPG):
            o_ref[0, h * HEAD_DIM:(h + 1) * HEAD_DIM, pl.ds(s0, Q_TILE)] = ot[:, h * Q_TILE:(h + 1) * Q_TILE]
        return 0

    lax.fori_loop(0, n_q, finish, 0)


def _nsa(qr, qn, gt, szat, kc, vct, ks, kw, vts, vtw, ovl):
    B, _, S = qr.shape
    NC = kc.shape[2]
    n_slc = S // SLC_BLOCK
    n_sel = min(SLC_TOP_N, n_slc)
    n_q = S // Q_TILE
    n_ck = S // KV_CHUNK
    tpc = KV_CHUNK // Q_TILE
    gw = A_HPG * HEAD_DIM
    lanes = A_HPG * Q_TILE
    jobs = [(c, q) for c in range(n_ck) for q in range((c + 1) * tpc, n_q)]
    n_jobs = len(jobs)
    jobs_per_trip = max(j for j in range(IN_FLIGHT, MAX_JOBS_PER_TRIP + 1, IN_FLIGHT) if n_jobs % j == 0)
    assert tpc == IN_FLIGHT and AHEAD < IN_FLIGHT and n_ck % DIAG_CHUNKS_PER_TRIP == 0
    jobs = jobs + [jobs[-1] if jobs else (0, 0)] * AHEAD
    jc = jnp.asarray([j[0] for j in jobs], jnp.int32)
    jq = jnp.asarray([j[1] for j in jobs], jnp.int32)
    kernel = functools.partial(_nsa_kernel, seq=S, n_sel=n_sel, n_jobs=n_jobs, jobs_per_trip=jobs_per_trip)
    per_group = lambda rows, cols: pl.BlockSpec((1, 1, rows, cols), lambda b, g, *_: (b, g, 0, 0))
    in_specs = [
        pl.BlockSpec((1, gw, S), lambda b, g, *_: (b, g, 0)),
        pl.BlockSpec((1, gw, S), lambda b, g, *_: (b, g, 0)),
        pl.BlockSpec((1, GATE_ROWS, S), lambda b, g, *_: (b, g, 0)),
        pl.BlockSpec((1, gw, S), lambda b, g, *_: (b, g, 0)),
        per_group(NC, HEAD_DIM), per_group(HEAD_DIM, NC),
        per_group(S, KV_WIDTH), per_group(S, KV_WIDTH),
        per_group(V_ROWS, S), per_group(V_ROWS, S),
        pl.BlockSpec((n_slc, NC), lambda b, g, *_: (0, 0)),
    ]
    scratch = [
        pltpu.VMEM((n_q, n_ck, 2 * (KV_CHUNK // SLC_BLOCK), lanes), BF16),
        pltpu.VMEM((n_q, HEAD_DIM, lanes), BF16),
        pltpu.VMEM((n_q, 8, lanes), F32),
        pltpu.VMEM((n_q, V_ROWS, lanes), F32),
        pltpu.VMEM((n_q, HEAD_DIM, lanes), F32),
        pltpu.VMEM((IN_FLIGHT, KV_CHUNK, lanes), F32),
        pltpu.VMEM((IN_FLIGHT, 8, lanes), F32),
        pltpu.VMEM((tpc, KV_CHUNK, lanes), F32),
        pltpu.VMEM((tpc, 8, lanes), F32),
        pltpu.VMEM((2, NC, lanes), F32),
        pltpu.VMEM((IN_FLIGHT, WIN_KEYS, lanes), F32),
        pltpu.VMEM((IN_FLIGHT, 8, lanes), F32),
    ]
    return pl.pallas_call(
        kernel,
        out_shape=jax.ShapeDtypeStruct((B, A_WIDTH, S), BF16),
        grid_spec=pltpu.PrefetchScalarGridSpec(
            num_scalar_prefetch=2,
            grid=(B, A_GROUPS),
            in_specs=in_specs,
            out_specs=pl.BlockSpec((1, gw, S), lambda b, g, *_: (b, g, 0)),
            scratch_shapes=scratch),
        compiler_params=pltpu.CompilerParams(
            dimension_semantics=("parallel", "parallel"),
            vmem_limit_bytes=V7X_VMEM_LIMIT_BYTES),
        name="nsa_attention",
    )(jc, jq, qr, qn, gt, szat, kc, vct, ks, kw, vts, vtw, ovl)


def _hgrn_stages(q_ref, f_ref, i_ref, sz_ref, lb, gain, tri, bd, causal, state_s, h, out):
    T = q_ref.shape[1]
    cpg = HG_ROWS // CHUNK
    lanes = slice(h * B_KDIM, (h + 1) * B_KDIM)
    groups = [slice(gi * HG_ROWS, (gi + 1) * HG_ROWS) for gi in range(T // HG_ROWS)]
    mid = {}

    def decay():
        mid["kk"], mid["b"] = [], []
        for rows in groups:
            fg = lb + (1.0 - lb) * _sigmoid(f_ref[0, rows, lanes])
            logf = jnp.log(fg)
            l_hi = logf.astype(BF16)
            l_lo = (logf - l_hi.astype(F32)).astype(BF16)
            bb = _dot(tri, jnp.concatenate([l_hi, l_lo], axis=1))
            mid["kk"].append(1.0 - fg)
            mid["b"].append(bb[:, :B_KDIM] + bb[:, B_KDIM:])

    def intra():
        mid["qe"], mid["oi"], mid["kv"], mid["dl"] = [], [], [], []
        for rows, kk, b in zip(groups, mid["kk"], mid["b"]):
            b_last = jnp.concatenate(
                [jnp.broadcast_to(b[(c + 1) * CHUNK - 1:(c + 1) * CHUNK, :], (CHUNK, B_KDIM))
                 for c in range(cpg)], axis=0)
            qe = (_silu(q_ref[0, rows, lanes].astype(F32)) * jnp.exp(b)).astype(BF16)
            ke = (kk * jnp.exp(-b)).astype(BF16)
            kd = (kk * jnp.exp(b_last - b)).astype(BF16)
            attn = jnp.where(causal, _dot_nt(qe, ke), 0.0).astype(BF16)
            v = i_ref[0, rows, lanes]
            mid["oi"].append(_dot(attn, v))
            mid["qe"].append(qe)
            kd_bd = jnp.concatenate([kd] * cpg, axis=1) * bd
            kv_all = lax.dot_general(v, kd_bd, (((0,), (0,)), ((), ())), preferred_element_type=F32)
            for c in range(cpg):
                mid["kv"].append(kv_all[:, c * B_KDIM:(c + 1) * B_KDIM])
                mid["dl"].append(jnp.exp(b_last[c * CHUNK:c * CHUNK + 1, :]))

    def inter():
        state = state_s[h]
        outs = []
        for n in range(T // CHUNK):
            gi, c = divmod(n, cpg)
            sl = slice(c * CHUNK, (c + 1) * CHUNK)
            rows = slice(n * CHUNK, (n + 1) * CHUNK)
            o = mid["oi"][gi][sl] + _dot_nt(mid["qe"][gi][sl], state.astype(BF16))
            o = o * lax.rsqrt(jnp.mean(o * o, axis=-1, keepdims=True) + NORM_EPS)
            outs.append((o * gain * sz_ref[0, rows, lanes].astype(F32)).astype(BF16))
            state = mid["dl"][n] * state + mid["kv"][n]
        state_s[h] = state
        out.append(jnp.concatenate(outs, axis=0))

    return [decay, intra, inter]


def _out_kernel(x_ref, oat_ref, q_ref, f_ref, i_ref, sz_ref, lbl_ref, g_ref, tri_ref, bd_ref,
                wgm_ref, bgm_ref, wa_ref, wb_ref, wo_ref, lng_ref, lnb_ref, o_ref, state_s, sg_s, ya_s,
                *, alpha, layer):
    @pl.when(pl.program_id(1) == 0)
    def _():
        state_s[...] = jnp.zeros_like(state_s)

    x = x_ref[0]
    xb = x.astype(BF16)

    def gate_chunk(j):
        def run():
            sg_s[:, j:j + GATE_COLS] = _sigmoid(_dot(xb, wgm_ref[:, j:j + GATE_COLS]) + bgm_ref[:, j:j + GATE_COLS])
        return run

    def ya_chunk(j):
        def run():
            ya_s[:, j:j + YA_COLS] = lax.dot_general(oat_ref[0], wa_ref[:, j:j + YA_COLS], (((0,), (0,)), ((), ())),
                                                     preferred_element_type=F32)
        return run

    fillers = [gate_chunk(j) for j in range(0, 2 * D_MODEL, GATE_COLS)] + \
              [ya_chunk(j) for j in range(0, D_MODEL, YA_COLS)]

    lg = lbl_ref[...]
    e = jnp.exp(lg - jnp.max(lg, axis=0, keepdims=True))
    lb_all = jnp.sum(e[0:layer + 1], axis=0, keepdims=True) / jnp.sum(e, axis=0, keepdims=True)
    tri = tri_ref[...]
    bd = bd_ref[...]
    ri = lax.broadcasted_iota(jnp.int32, (HG_ROWS, HG_ROWS), 0)
    ci = lax.broadcasted_iota(jnp.int32, (HG_ROWS, HG_ROWS), 1)
    causal = (ri // CHUNK == ci // CHUNK) & (ci <= ri)

    obs = []
    stages = []
    for h in range(B_HEADS):
        lanes = slice(h * B_KDIM, (h + 1) * B_KDIM)
        stages += _hgrn_stages(q_ref, f_ref, i_ref, sz_ref, lb_all[:, lanes], g_ref[:, lanes],
                               tri, bd, causal, state_s, h, obs)
    for k, stage in enumerate(stages):
        if k < len(fillers):
            fillers[k]()
        stage()
    for filler in fillers[len(stages):]:
        filler()
    ob = jnp.concatenate(obs, axis=1)

    sub = x.shape[0] // TAIL_SPLIT
    halves = [slice(i * sub, (i + 1) * sub) for i in range(TAIL_SPLIT)]
    ybs = [_dot(ob[r], wb_ref[...]) for r in halves]
    ys = [(sg_s[r, :D_MODEL] * ya_s[r, :] + sg_s[r, D_MODEL:] * yb).astype(BF16) for r, yb in zip(halves, ybs)]
    outs = [_dot(y, wo_ref[...]) for y in ys]
    for r, out in zip(halves, outs):
        res = alpha * x[r] + out
        mu = jnp.mean(res, axis=-1, keepdims=True)
        d = res - mu
        var = jnp.mean(d * d, axis=-1, keepdims=True)
        o_ref[0, r, :] = d * lax.rsqrt(var + NORM_EPS) * lng_ref[...] + lnb_ref[...]


def _merge_out(x, oat, qb, fb, ib, szb, lb_logits, gain, tri, bd, wgm, bgm, wa, wb, wo, lng, lnb, alpha, layer):
    B, S, D = x.shape
    T = OUT_ROWS
    full = lambda a: pl.BlockSpec(a.shape, lambda b, s: (0,) * a.ndim)
    rows = lambda w: pl.BlockSpec((1, T, w), lambda b, s: (b, s, 0))
    return pl.pallas_call(
        functools.partial(_out_kernel, alpha=alpha, layer=layer),
        out_shape=jax.ShapeDtypeStruct((B, S, D), x.dtype),
        grid=(B, S // T),
        in_specs=[rows(D), pl.BlockSpec((1, A_WIDTH, T), lambda b, s: (b, 0, s)),
                  rows(B_FWIDTH), rows(B_FWIDTH), rows(B_WIDTH), rows(B_WIDTH),
                  full(lb_logits), full(gain), full(tri), full(bd),
                  full(wgm), full(bgm), full(wa), full(wb), full(wo), full(lng), full(lnb)],
        out_specs=rows(D),
        scratch_shapes=[pltpu.VMEM((B_HEADS, B_VDIM, B_KDIM), F32),
                        pltpu.VMEM((T, 2 * D), F32),
                        pltpu.VMEM((T, D), F32)],
        compiler_params=pltpu.CompilerParams(
            dimension_semantics=("parallel", "arbitrary"),
            vmem_limit_bytes=V7X_VMEM_LIMIT_BYTES),
        name="hgrn_merge_out",
    )(x, oat, qb, fb, ib, szb, lb_logits, gain, tri, bd, wgm, bgm, wa, wb, wo, lng, lnb)


def _rope_tables(S):
    inv = ROPE_THETA ** (-np.arange(0, HEAD_DIM, 2, dtype=np.float64) / HEAD_DIM)
    ang = np.arange(S, dtype=np.float64)[:, None] * inv[None, :]
    cos = np.concatenate([np.cos(ang), np.cos(ang)], axis=-1)
    sin = np.concatenate([np.sin(ang), np.sin(ang)], axis=-1)
    first = (np.arange(HEAD_DIM) < HEAD_DIM // 2)[None, :]
    sina = np.where(first, -sin, 0.0)
    sinb = np.where(first, 0.0, sin)
    tile = lambda a: np.concatenate([a] * A_GROUPS, axis=-1)
    return tuple(jnp.asarray(np.ascontiguousarray(a), dtype=F32)
                 for a in (cos.T, sin.T, tile(cos), tile(sina), tile(sinb)))


def _overlap_t(S):
    n_cmp = S // CMP_STRIDE
    n_slc = S // SLC_BLOCK
    cs = np.arange(n_cmp)[None, :] * CMP_STRIDE
    ss = np.arange(n_slc)[:, None] * SLC_BLOCK
    ov = (cs < ss + SLC_BLOCK) & (cs + CMP_BLOCK > ss) & (np.arange(n_cmp)[None, :] < n_cmp - 1)
    return jnp.asarray(ov, dtype=BF16)


def _block_tri():
    r = np.arange(HG_ROWS)
    return jnp.asarray((r[:, None] // CHUNK == r[None, :] // CHUNK) & (r[None, :] <= r[:, None]), dtype=BF16)


def _block_diag():
    r = np.arange(HG_ROWS)[:, None] // CHUNK
    c = np.arange(HG_ROWS // CHUNK * B_KDIM)[None, :] // B_KDIM
    return jnp.asarray(r == c, dtype=BF16)


def _layer(x, l, w_in, b_in, pe_k, w_k1, w_k2, pe_v, w_v1, w_v2, lb_logits, norm_g,
           w_a, w_b, w_o, ln_g, ln_b):
    B, S, D = x.shape
    alpha = (2 * DEPTH) ** 0.25
    o = _OFF
    wsl = lambda i: w_in[:, o[i]:o[i + 1]]
    bsl = lambda i: b_in[o[i]:o[i + 1]]
    kvw, kvb = wsl(1), bsl(1)
    kv_w = lambda j: kvw[:, j * KV_WIDTH:(j + 1) * KV_WIDTH]
    kv_b = lambda j: kvb[j * KV_WIDTH:(j + 1) * KV_WIDTH]
    gw, gb = wsl(2), bsl(2)
    gidx = np.zeros((A_GROUPS, GATE_ROWS), np.int32)
    gmask = np.zeros((A_GROUPS, GATE_ROWS), np.float32)
    for g in range(A_GROUPS):
        for br in range(3):
            for h in range(A_HPG):
                gidx[g, br * A_HPG + h] = (g * A_HPG + h) * 3 + br
                gmask[g, br * A_HPG + h] = 1.0
    gidx, gmask = gidx.reshape(-1), gmask.reshape(-1)
    gw_t = gw[:, gidx] * gmask[None, :]
    gw_pad = jnp.concatenate([gw_t, jnp.zeros((D, 128 - gw_t.shape[1]), gw_t.dtype)], axis=1)
    gb_t = gb[gidx] * gmask

    wt = _transpose_weights(jnp.concatenate([wsl(0), kv_w(3), kv_w(5), wsl(3)], axis=1).astype(BF16),
                            gw_pad.astype(BF16), A_WIDTH + 2 * KV_WIDTH, gw_t.shape[1])
    bt = jnp.concatenate([bsl(0), kv_b(3), kv_b(5), gb_t, bsl(3)])[:, None]
    wn = jnp.concatenate([kv_w(0), kv_w(1), kv_w(2), kv_w(4), wsl(7), wsl(4), wsl(6), wsl(5)],
                         axis=1).astype(BF16)
    bn = jnp.concatenate([kv_b(0), kv_b(1), kv_b(2), kv_b(4), bsl(7), bsl(4), bsl(6), bsl(5)])[None, :]
    cost, sint, cos, sina, sinb = _rope_tables(S)

    (qr, qn, vts, vtw, gt, szat, kcmp, vcmp, ks, kw, qb, fb, ib, szb) = _project(
        x, wn, bn, wt, bt, cost, sint, cos, sina, sinb)

    half = CMP_STRIDE * HEAD_DIM

    def w1_planes(w1):
        both = jnp.concatenate([w1[:half], w1[half:]], axis=1).reshape(CMP_STRIDE, HEAD_DIM, 2 * CMP_HIDDEN)
        z = jnp.zeros_like(both)
        planes = jnp.concatenate([jnp.concatenate([both, z], axis=2),
                                  jnp.concatenate([z, both], axis=2)], axis=1)
        return planes.reshape(CMP_STRIDE * KV_WIDTH, 2 * A_GROUPS * CMP_HIDDEN).astype(BF16)

    kc, vct = _compress(kcmp, vcmp,
                        w1_planes(w_k1), w_k1, pe_k.reshape(-1, 1), w_k2.astype(BF16),
                        w1_planes(w_v1), w_v1, pe_v.reshape(-1, 1), w_v2.T.astype(BF16))

    oat = _nsa(qr, qn, gt, szat, kc, vct, ks, kw, vts, vtw, _overlap_t(S))
    wgm = jnp.concatenate([wsl(8), wsl(9)], axis=1).astype(BF16)
    bgm = jnp.concatenate([bsl(8), bsl(9)])[None, :]
    return _merge_out(x, oat, qb, fb, ib, szb, lb_logits, norm_g[None, :], _block_tri(), _block_diag(),
                      wgm, bgm, w_a.astype(BF16), w_b.astype(BF16), w_o.astype(BF16),
                      ln_g[None, :], ln_b[None, :], alpha, l)


@jax.jit
def kernel(x, w_in, b_in, pe_cmp_k, w_cmp_k1, w_cmp_k2, pe_cmp_v, w_cmp_v1, w_cmp_v2,
           hgrn_lb_logits, hgrn_norm_g, w_branch_a, w_branch_b, w_out, ln_g, ln_b):
    B, S, D = x.shape
    assert D == D_MODEL and S % KV_CHUNK == 0 and S % PROJ_ROWS == 0 and S >= WIN_KEYS
    assert S % OUT_ROWS == 0 and OUT_ROWS % HG_ROWS == 0 and (S // KV_CHUNK) * KV_CHUNK == S
    for l in range(DEPTH):
        x = _layer(x, l, w_in[l], b_in[l], pe_cmp_k[l], w_cmp_k1[l], w_cmp_k2[l],
                   pe_cmp_v[l], w_cmp_v1[l], w_cmp_v2[l], hgrn_lb_logits, hgrn_norm_g[l],
                   w_branch_a[l], w_branch_b[l], w_out[l], ln_g[l], ln_b[l])
    return x
```

```python
import functools
import math

import numpy as np
import jax
import jax.numpy as jnp
from jax import lax
from jax.experimental import pallas as pl
from jax.experimental.pallas import tpu as pltpu

D_MODEL = 1024
DEPTH = 1
A_HEADS = 8
A_GROUPS = 2
A_HPG = A_HEADS // A_GROUPS
HEAD_DIM = 64
A_WIDTH = A_HEADS * HEAD_DIM
KV_WIDTH = A_GROUPS * HEAD_DIM
CMP_BLOCK = 32
CMP_STRIDE = 16
CMP_HIDDEN = 128
SLC_BLOCK = 64
SLC_TOP_N = 16
WINDOW = 512
ROPE_THETA = 10000.0
FORCE_SCORE = 1e30
B_HEADS = 4
B_KDIM = 128
B_VDIM = 128
B_FWIDTH = B_HEADS * B_KDIM
B_WIDTH = B_HEADS * B_VDIM
CHUNK = 64
NORM_EPS = 1e-5

IN_SPLITS = (A_WIDTH, 6 * KV_WIDTH, A_HEADS * 3, A_WIDTH, B_FWIDTH, B_FWIDTH,
             B_WIDTH, B_WIDTH, D_MODEL, D_MODEL)
_OFF = tuple(int(v) for v in np.cumsum((0,) + IN_SPLITS))

V7X_VMEM_LIMIT_BYTES = 56 * 1024 * 1024
PROJ_ROWS = 512
Q_TILE = 128
KV_CHUNK = 512
IN_FLIGHT = 4
AHEAD = 2
DIAG_CHUNKS_PER_TRIP = 4
MAX_JOBS_PER_TRIP = 16
WIN_KEYS = WINDOW + Q_TILE
HG_ROWS = 256
OUT_ROWS = 512
GATE_COLS = 256
YA_COLS = 512
TAIL_SPLIT = 2
MASKED = -1e30
LOG2E = math.log2(math.e)
GATE_ROWS = 16
V_ROWS = HEAD_DIM + 16

F32 = jnp.float32
BF16 = jnp.bfloat16


def _dot(a, b):
    return jnp.dot(a, b, preferred_element_type=F32)


def _dot_nt(a, b):
    return lax.dot_general(a, b, (((1,), (1,)), ((), ())), preferred_element_type=F32)


def _sigmoid(x):
    return 1.0 / (1.0 + jnp.exp(-x))


def _silu(x):
    return x * _sigmoid(x)


def _proj_kernel(x_ref, wn_ref, bn_ref, wt_ref, bt_ref, cost_ref, sint_ref,
                 cos_ref, sina_ref, sinb_ref,
                 qr_ref, qn_ref, vts_ref, vtw_ref, gt_ref, szat_ref, kc_ref, vc_ref,
                 ks_ref, kw_ref, qb_ref, fb_ref, ib_ref, szb_ref, cmp_s):
    xb = x_ref[0].astype(BF16)
    scale = HEAD_DIM ** -0.5 * LOG2E

    ht = _dot_nt(wt_ref[...], xb) + bt_ref[...]
    cost = cost_ref[...]
    sint = sint_ref[...]
    half = HEAD_DIM // 2
    for h in range(A_HEADS):
        blk = ht[h * HEAD_DIM:(h + 1) * HEAD_DIM]
        rot = jnp.concatenate([-blk[half:], blk[:half]], axis=0)
        qr_ref[0, h * HEAD_DIM:(h + 1) * HEAD_DIM, :] = ((blk * cost + rot * sint) * scale).astype(BF16)
        qn_ref[0, h * HEAD_DIM:(h + 1) * HEAD_DIM, :] = (blk * scale).astype(BF16)
    o = A_WIDTH
    ones_rows = jnp.where(lax.broadcasted_iota(jnp.int32, (V_ROWS - HEAD_DIM, ht.shape[1]), 0) == 0, 1.0, 0.0)
    for g in range(A_GROUPS):
        for ref, base in ((vts_ref, o), (vtw_ref, o + KV_WIDTH)):
            rows_g = ht[base + g * HEAD_DIM:base + (g + 1) * HEAD_DIM]
            ref[0, g] = jnp.concatenate([rows_g, ones_rows], axis=0).astype(BF16)
    gt_ref[0] = _sigmoid(ht[o + 2 * KV_WIDTH:o + 2 * KV_WIDTH + 2 * GATE_ROWS])
    o += 2 * KV_WIDTH + 2 * GATE_ROWS
    szat_ref[0] = _silu(ht[o:o + A_WIDTH]).astype(BF16)

    def cols(lo, hi):
        return _dot_nt(xb, wn_ref[lo:hi, :]) + bn_ref[:, lo:hi]

    kv = cols(0, 4 * KV_WIDTH)
    cmp_s[0] = kv[:, 0:KV_WIDTH]
    cmp_s[1] = kv[:, KV_WIDTH:2 * KV_WIDTH]
    pieces = cmp_s.shape[1] // CMP_STRIDE
    for t in range(CMP_STRIDE):
        lanes_t = slice(t * KV_WIDTH, (t + 1) * KV_WIDTH)
        kc_ref[0, :, lanes_t] = cmp_s[0, pl.ds(t, pieces, stride=CMP_STRIDE), :].astype(BF16)
        vc_ref[0, :, lanes_t] = cmp_s[1, pl.ds(t, pieces, stride=CMP_STRIDE), :].astype(BF16)
    cos = cos_ref[...]
    sina = sina_ref[...]
    sinb = sinb_ref[...]

    def rope_rows(k):
        return (k * cos + pltpu.roll(k, 128 - half, axis=1) * sina + pltpu.roll(k, half, axis=1) * sinb)

    ks = rope_rows(kv[:, 2 * KV_WIDTH:3 * KV_WIDTH])
    rows = ks.shape[0]
    lane = lax.broadcasted_iota(jnp.int32, (rows, KV_WIDTH), 1)
    pos = pl.program_id(1) * rows + lax.broadcasted_iota(jnp.int32, (rows, KV_WIDTH), 0)
    blocks_per_chunk = KV_CHUNK // SLC_BLOCK
    ind = jnp.where(lane == HEAD_DIM + (pos // SLC_BLOCK) % blocks_per_chunk, 1.0, 0.0)
    ks_ref[0, 0] = (jnp.where(lane < HEAD_DIM, ks, 0.0) + ind).astype(BF16)
    ks_ref[0, 1] = (jnp.where(lane < HEAD_DIM, pltpu.roll(ks, HEAD_DIM, axis=1), 0.0) + ind).astype(BF16)
    kw = rope_rows(kv[:, 3 * KV_WIDTH:4 * KV_WIDTH])
    kw_ref[0, 0] = jnp.where(lane < HEAD_DIM, kw, 0.0).astype(BF16)
    kw_ref[0, 1] = jnp.where(lane < HEAD_DIM, pltpu.roll(kw, HEAD_DIM, axis=1), 0.0).astype(BF16)
    o = 4 * KV_WIDTH
    szb_ref[0] = _silu(cols(o, o + B_WIDTH)).astype(BF16)
    o += B_WIDTH
    qb_ref[0] = cols(o, o + B_FWIDTH).astype(BF16)
    o += B_FWIDTH
    ib_ref[0] = cols(o, o + B_WIDTH).astype(BF16)
    o += B_WIDTH
    fb_ref[0] = cols(o, o + B_FWIDTH)


def _project(x, wn, bn, wt, bt, cost, sint, cos, sina, sinb):
    B, S, D = x.shape
    T = PROJ_ROWS
    n_t = wt.shape[0]
    n_n = wn.shape[0]
    full = lambda shape: pl.BlockSpec(shape, lambda b, s: (0,) * len(shape))
    row_out = lambda w: pl.BlockSpec((1, T, w), lambda b, s: (b, s, 0))
    col_out = lambda r: pl.BlockSpec((1, r, T), lambda b, s: (b, 0, s))
    sds = jax.ShapeDtypeStruct
    out_shape = (
        sds((B, A_WIDTH, S), BF16), sds((B, A_WIDTH, S), BF16),
        sds((B, A_GROUPS, V_ROWS, S), BF16), sds((B, A_GROUPS, V_ROWS, S), BF16),
        sds((B, 2 * GATE_ROWS, S), F32),
        sds((B, A_WIDTH, S), BF16),
        sds((B, S // CMP_STRIDE, CMP_STRIDE * KV_WIDTH), BF16),
        sds((B, S // CMP_STRIDE, CMP_STRIDE * KV_WIDTH), BF16),
        sds((B, A_GROUPS, S, KV_WIDTH), BF16), sds((B, A_GROUPS, S, KV_WIDTH), BF16),
        sds((B, S, B_FWIDTH), BF16), sds((B, S, B_FWIDTH), F32),
        sds((B, S, B_WIDTH), BF16), sds((B, S, B_WIDTH), BF16),
    )
    out_specs = (
        col_out(A_WIDTH), col_out(A_WIDTH),
        pl.BlockSpec((1, A_GROUPS, V_ROWS, T), lambda b, s: (b, 0, 0, s)),
        pl.BlockSpec((1, A_GROUPS, V_ROWS, T), lambda b, s: (b, 0, 0, s)),
        col_out(2 * GATE_ROWS), col_out(A_WIDTH),
        pl.BlockSpec((1, T // CMP_STRIDE, CMP_STRIDE * KV_WIDTH), lambda b, s: (b, s, 0)),
        pl.BlockSpec((1, T // CMP_STRIDE, CMP_STRIDE * KV_WIDTH), lambda b, s: (b, s, 0)),
        pl.BlockSpec((1, A_GROUPS, T, KV_WIDTH), lambda b, s: (b, 0, s, 0)),
        pl.BlockSpec((1, A_GROUPS, T, KV_WIDTH), lambda b, s: (b, 0, s, 0)),
        row_out(B_FWIDTH), row_out(B_FWIDTH), row_out(B_WIDTH), row_out(B_WIDTH),
    )
    in_specs = [
        pl.BlockSpec((1, T, D), lambda b, s: (b, s, 0)),
        full((n_n, D)), full((1, n_n)), full((n_t, D)), full((n_t, 1)),
        pl.BlockSpec((HEAD_DIM, T), lambda b, s: (0, s)),
        pl.BlockSpec((HEAD_DIM, T), lambda b, s: (0, s)),
        pl.BlockSpec((T, KV_WIDTH), lambda b, s: (s, 0)),
        pl.BlockSpec((T, KV_WIDTH), lambda b, s: (s, 0)),
        pl.BlockSpec((T, KV_WIDTH), lambda b, s: (s, 0)),
    ]
    return pl.pallas_call(
        _proj_kernel, out_shape=out_shape, grid=(B, S // T),
        in_specs=in_specs, out_specs=out_specs,
        scratch_shapes=[pltpu.VMEM((2, T, KV_WIDTH), F32)],
        compiler_params=pltpu.CompilerParams(
            dimension_semantics=("parallel", "parallel"),
            vmem_limit_bytes=V7X_VMEM_LIMIT_BYTES),
        name="in_proj",
    )(x, wn, bn, wt, bt, cost, sint, cos, sina, sinb)


def _compress_kernel(ck_ref, cv_ref, w1k_ref, w1kf_ref, pek_ref, w2k_ref,
                     w1v_ref, w1vf_ref, pev_ref, w2vt_ref, kc_ref, vct_ref):
    nc = ck_ref.shape[1]
    hid = CMP_HIDDEN

    def hidden(c_ref, w1_ref, w1f_ref, pe_ref):
        a = _dot(c_ref[0], w1_ref[...])
        pe_term = jnp.sum(w1f_ref[...] * pe_ref[...], axis=0, keepdims=True)
        out = []
        for g in range(A_GROUPS):
            lo = a[:, 2 * g * hid:(2 * g + 1) * hid]
            hi = pltpu.roll(a[:, (2 * g + 1) * hid:(2 * g + 2) * hid], nc - 1, axis=0)
            out.append(_silu(lo + hi + pe_term).astype(BF16))
        return out

    hk = hidden(ck_ref, w1k_ref, w1kf_ref, pek_ref)
    hv = hidden(cv_ref, w1v_ref, w1vf_ref, pev_ref)
    for g in range(A_GROUPS):
        kc_ref[0, g] = _dot(hk[g], w2k_ref[...]).astype(BF16)
        vct_ref[0, g] = _dot_nt(w2vt_ref[...], hv[g]).astype(BF16)


def _compress(ck, cv, w1k, w1kf, pek, w2k, w1v, w1vf, pev, w2vt):
    B, NC, W = ck.shape
    full = lambda a: pl.BlockSpec(a.shape, lambda b: (0,) * a.ndim)
    blk = pl.BlockSpec((1, NC, W), lambda b: (b, 0, 0))
    return pl.pallas_call(
        _compress_kernel,
        out_shape=(jax.ShapeDtypeStruct((B, A_GROUPS, NC, HEAD_DIM), BF16),
                   jax.ShapeDtypeStruct((B, A_GROUPS, HEAD_DIM, NC), BF16)),
        grid=(B,),
        in_specs=[blk, blk, full(w1k), full(w1kf), full(pek), full(w2k),
                  full(w1v), full(w1vf), full(pev), full(w2vt)],
        out_specs=(pl.BlockSpec((1, A_GROUPS, NC, HEAD_DIM), lambda b: (b, 0, 0, 0)),
                   pl.BlockSpec((1, A_GROUPS, HEAD_DIM, NC), lambda b: (b, 0, 0, 0))),
        compiler_params=pltpu.CompilerParams(
            dimension_semantics=("parallel",),
            vmem_limit_bytes=V7X_VMEM_LIMIT_BYTES),
        name="kv_compress",
    )(ck, cv, w1k, w1kf, pek, w2k, w1v, w1vf, pev, w2vt)


def _nsa_kernel(jc_ref, jq_ref, qr_ref, qn_ref, gt_ref, szat_ref, kc_ref, vct_ref, ks_ref, kw_ref,
                vts_ref, vtw_ref, ovl_ref, o_ref,
                bias_s, wq_s, m_s, acc_s, ocw_s, sbuf_s, smax_s, dbuf_s, dmax_s, scb_s, swb_s, wmax_s,
                *, seq, n_sel, n_jobs, jobs_per_trip):
    nc = kc_ref.shape[2]
    n_q = seq // Q_TILE
    n_ck = seq // KV_CHUNK
    lanes = A_HPG * Q_TILE
    bpc = KV_CHUNK // SLC_BLOCK
    tpc = KV_CHUNK // Q_TILE
    win_tiles = WINDOW // Q_TILE

    def stack_heads(ref, q0):
        return jnp.concatenate(
            [ref[0, h * HEAD_DIM:(h + 1) * HEAD_DIM, pl.ds(q0, Q_TILE)] for h in range(A_HPG)], axis=1)

    def per_head(a):
        return jnp.concatenate([a] * A_HPG, axis=1)

    def gate_row(gt, branch):
        return jnp.concatenate(
            [gt[branch * A_HPG + h:branch * A_HPG + h + 1, :] for h in range(A_HPG)], axis=1)

    def tile_start(qi):
        return pl.multiple_of(qi * Q_TILE, Q_TILE)

    lane_q = lax.broadcasted_iota(jnp.int32, (1, lanes), 1) & (Q_TILE - 1)
    r_sq = lax.broadcasted_iota(jnp.int32, (Q_TILE, lanes), 0)
    causal_sq = r_sq <= lane_q
    lower_sq = r_sq > lane_q
    q_pad = jnp.zeros((KV_WIDTH - HEAD_DIM, lanes), BF16)

    def cmp_keys(rb):
        return min(nc, -(-(rb * KV_CHUNK // CMP_STRIDE) // 128) * 128)

    def cmp_rows(qi):
        return min(nc, -(-((qi + 1) * Q_TILE // CMP_STRIDE) // 16) * 16)

    def cmp_scores(qi):
        return _dot(kc_ref[0, 0, 0:cmp_rows(qi), :], stack_heads(qn_ref, tile_start(qi)))

    def zero_rows(a, rows):
        if a.shape[0] == rows:
            return a
        return jnp.concatenate([a, jnp.zeros((rows - a.shape[0], a.shape[1]), a.dtype)], axis=0)

    def select_tile(qi, rb, sc):
        nk = cmp_keys(rb)
        n_causal = (qi + 1) * Q_TILE // SLC_BLOCK
        s0 = tile_start(qi)
        t_row = s0 + lax.broadcasted_iota(jnp.int32, (1, Q_TILE), 1)
        wq_s[qi] = stack_heads(qr_ref, s0)

        nv = cmp_rows(qi)
        lo = max(0, (qi * Q_TILE - (CMP_BLOCK - 1)) // CMP_STRIDE // 8 * 8)
        n_end = (lo + lax.broadcasted_iota(jnp.int32, (nv - lo, Q_TILE), 0)) * CMP_STRIDE + (CMP_BLOCK - 1)
        edge = sc[lo:] + per_head(jnp.where(n_end <= t_row, 0.0, MASKED))
        sc = jnp.concatenate([sc[:lo], edge], axis=0) if lo else edge
        mc = jnp.max(sc, axis=0, keepdims=True)
        pc = jnp.exp2(sc - mc)
        lc = jnp.sum(pc, axis=0, keepdims=True)
        pc = pc * jnp.where(s0 + lane_q >= CMP_BLOCK - 1, 1.0 / lc, 0.0)
        o_cmp = _dot(vct_ref[0, 0, :, 0:nk], zero_rows(pc.astype(BF16), nk))

        ps = pc[:, 0:Q_TILE]
        for h in range(1, A_HPG):
            ps = ps + pc[:, h * Q_TILE:(h + 1) * Q_TILE]
        p_hi = ps.astype(BF16)
        p_lo = (ps - p_hi.astype(F32)).astype(BF16)
        imp2 = _dot(ovl_ref[0:8 * rb, 0:nk],
                    zero_rows(jnp.concatenate([p_hi, p_lo], axis=1), nk))
        imp = imp2[:, :Q_TILE] + imp2[:, Q_TILE:]

        j_idx = lax.broadcasted_iota(jnp.int32, (8 * rb, Q_TILE), 0)
        cur = t_row // SLC_BLOCK
        sel = j_idx <= cur
        if 8 * rb > n_sel:
            forced = (j_idx == 0) | (j_idx == cur) | (j_idx == cur - 1)
            val = jnp.where(j_idx > cur, -1.0, jnp.where(forced, FORCE_SCORE, imp))
            rows = [val[8 * j:8 * j + 8] for j in range(rb)]
            cnt = [jnp.zeros((8, Q_TILE), F32) for _ in range(rb)]
            jl = lax.broadcasted_iota(jnp.int32, (8, Q_TILE), 0)
            for kb in range(rb):
                for kl in range(8):
                    if 8 * kb + kl >= n_causal:
                        continue
                    row = rows[kb][kl:kl + 1, :]
                    for jb in range(rb):
                        if jb < kb:
                            beats = row > rows[jb]
                        elif jb > kb:
                            beats = row >= rows[jb]
                        else:
                            beats = (row > rows[jb]) | ((row == rows[jb]) & (jl > kl))
                        cnt[jb] = cnt[jb] + jnp.where(beats, 1.0, 0.0)
            sel = sel & (jnp.concatenate(cnt, axis=0) < n_sel)
        bias = per_head(jnp.where(sel, 0.0, MASKED))
        pad = jnp.zeros((bpc, lanes), F32)
        for c in range(rb):
            bias_s[qi, c] = jnp.concatenate([bias[c * bpc:(c + 1) * bpc], pad], axis=0).astype(BF16)

        gt = gt_ref[0, :, pl.ds(s0, Q_TILE)]
        ocw_s[qi] = gate_row(gt, 0) * o_cmp
        m_s[qi] = jnp.full((8, lanes), MASKED, F32)
        acc_s[qi] = jnp.zeros((V_ROWS, lanes), F32)

    def win_finish(qi, sw, mw, w0, keys=WIN_KEYS):
        pw = jnp.exp2(sw - mw)
        ow = _dot(vtw_ref[0, 0, :, pl.ds(w0, keys)], pw.astype(BF16))
        o_win = ow[0:HEAD_DIM] * (1.0 / ow[HEAD_DIM:HEAD_DIM + 1])
        gt = gt_ref[0, :, pl.ds(tile_start(qi), Q_TILE)]
        ocw_s[qi] = ocw_s[qi] + gate_row(gt, 2) * o_win

    def win_scores(qi):
        slot = qi % IN_FLIGHT
        keys = min(qi + 1, win_tiles + 1) * Q_TILE
        w0 = (qi + 1) * Q_TILE - keys
        wq = jnp.concatenate([stack_heads(qr_ref, qi * Q_TILE), q_pad], axis=0)
        sw = _dot(kw_ref[0, 0, w0:w0 + keys, :], wq)
        parts = [jnp.where(causal_sq, sw[keys - Q_TILE:], MASKED)]
        if keys > Q_TILE:
            oldest = sw[:Q_TILE]
            if qi >= win_tiles:
                oldest = jnp.where(lower_sq, oldest, MASKED)
            parts = [oldest, sw[Q_TILE:keys - Q_TILE]] + parts if keys > 2 * Q_TILE else [oldest] + parts
        sw = jnp.concatenate(parts, axis=0)
        swb_s[slot, 0:keys, :] = sw
        wmax_s[slot] = jnp.broadcast_to(jnp.max(sw, axis=0, keepdims=True), (8, lanes))

    def win_tile(qi):
        slot = qi % IN_FLIGHT
        keys = min(qi + 1, win_tiles + 1) * Q_TILE
        win_finish(qi, swb_s[slot, 0:keys, :], wmax_s[slot][0:1], (qi + 1) * Q_TILE - keys, keys)

    scb_s[0, 0:cmp_rows(0), :] = cmp_scores(0)
    for u in range(min(AHEAD, n_q)):
        win_scores(u)
    for qi in range(n_q):
        rb = qi // tpc + 1
        if qi + 1 < n_q:
            scb_s[(qi + 1) & 1, 0:cmp_rows(qi + 1), :] = cmp_scores(qi + 1)
        if qi + AHEAD < n_q:
            win_scores(qi + AHEAD)
        select_tile(qi, rb, scb_s[qi & 1, 0:cmp_rows(qi), :])
        win_tile(qi)

    wq_pad = jnp.zeros((KV_WIDTH - HEAD_DIM - 2 * bpc, lanes), BF16)

    def scores(c, qi, diag):
        rows = KV_CHUNK if diag is None else (diag + 1) * Q_TILE
        k0 = pl.multiple_of(c * KV_CHUNK, KV_CHUNK)
        wq = jnp.concatenate([wq_s[qi], bias_s[qi, c], wq_pad], axis=0)
        s = _dot(ks_ref[0, 0, pl.ds(k0, rows), :], wq)
        if diag is not None:
            head = [s[:rows - Q_TILE]] if diag > 0 else []
            s = jnp.concatenate(head + [jnp.where(causal_sq, s[rows - Q_TILE:], MASKED)], axis=0)
        return s, jnp.broadcast_to(jnp.max(s, axis=0, keepdims=True), (8, lanes))

    def absorb(c, qi, s, s_max):
        rows = s.shape[0]
        k0 = pl.multiple_of(c * KV_CHUNK, KV_CHUNK)
        m_old = m_s[qi][0:1]
        m_new = jnp.maximum(m_old, s_max[0:1])
        alpha = jnp.exp2(m_old - m_new)
        p = jnp.exp2(s - m_new)
        m_s[qi] = jnp.broadcast_to(m_new, (8, lanes))
        acc_s[qi] = alpha * acc_s[qi] + _dot(vts_ref[0, 0, :, pl.ds(k0, rows)], p.astype(BF16))

    def full_scores(j, slot):
        sbuf_s[slot], smax_s[slot] = scores(jc_ref[j], jq_ref[j], None)

    def full_group(i, _):
        j0 = jobs_per_trip * i
        for u in range(jobs_per_trip):
            full_scores(j0 + u + AHEAD, (u + AHEAD) % IN_FLIGHT)
            absorb(jc_ref[j0 + u], jq_ref[j0 + u], sbuf_s[u % IN_FLIGHT], smax_s[u % IN_FLIGHT])
        return 0

    if n_jobs:
        for u in range(AHEAD):
            full_scores(u, u)
        lax.fori_loop(0, n_jobs // jobs_per_trip, full_group, 0)

    def diag_scores(c, r):
        dbuf_s[r, 0:(r + 1) * Q_TILE, :], dmax_s[r] = scores(c, c * tpc + r, r)

    def diag_chunks(i, _):
        for k in range(DIAG_CHUNKS_PER_TRIP):
            c = DIAG_CHUNKS_PER_TRIP * i + k
            nxt = jnp.minimum(c + 1, n_ck - 1)
            for r in range(tpc):
                ahead = r + AHEAD
                diag_scores(c if ahead < tpc else nxt, ahead % tpc)
                absorb(c, c * tpc + r, dbuf_s[r, 0:(r + 1) * Q_TILE, :], dmax_s[r])
        return 0

    for r in range(AHEAD):
        diag_scores(0, r)
    lax.fori_loop(0, n_ck // DIAG_CHUNKS_PER_TRIP, diag_chunks, 0)

    def finish(qi, _):
        s0 = tile_start(qi)
        gt = gt_ref[0, :, pl.ds(s0, Q_TILE)]
        acc = acc_s[qi]
        ot = ocw_s[qi] + gate_row(gt, 1) * acc[0:HEAD_DIM] * (1.0 / acc[HEAD_DIM:HEAD_DIM + 1])
        ot = (ot * stack_heads(szat_ref, s0).astype(F32)).astype(o_ref.dtype)
        for h in range(A_HPG):
            o_ref[0, h * HEAD_DIM:(h + 1) * HEAD_DIM, pl.ds(s0, Q_TILE)] = ot[:, h * Q_TILE:(h + 1) * Q_TILE]
        return 0

    lax.fori_loop(0, n_q, finish, 0)


def _nsa(qr, qn, gt, szat, kc, vct, ks, kw, vts, vtw, ovl):
    B, _, S = qr.shape
    NC = kc.shape[2]
    n_slc = S // SLC_BLOCK
    n_sel = min(SLC_TOP_N, n_slc)
    n_q = S // Q_TILE
    n_ck = S // KV_CHUNK
    tpc = KV_CHUNK // Q_TILE
    gw = A_HPG * HEAD_DIM
    lanes = A_HPG * Q_TILE
    jobs = [(c, q) for c in range(n_ck) for q in range((c + 1) * tpc, n_q)]
    n_jobs = len(jobs)
    jobs_per_trip = max(j for j in range(IN_FLIGHT, MAX_JOBS_PER_TRIP + 1, IN_FLIGHT) if n_jobs % j == 0)
    assert tpc == IN_FLIGHT and AHEAD < IN_FLIGHT and n_ck % DIAG_CHUNKS_PER_TRIP == 0
    jobs = jobs + [jobs[-1] if jobs else (0, 0)] * AHEAD
    jc = jnp.asarray([j[0] for j in jobs], jnp.int32)
    jq = jnp.asarray([j[1] for j in jobs], jnp.int32)
    kernel = functools.partial(_nsa_kernel, seq=S, n_sel=n_sel, n_jobs=n_jobs, jobs_per_trip=jobs_per_trip)
    per_group = lambda rows, cols: pl.BlockSpec((1, 1, rows, cols), lambda b, g, *_: (b, g, 0, 0))
    in_specs = [
        pl.BlockSpec((1, gw, S), lambda b, g, *_: (b, g, 0)),
        pl.BlockSpec((1, gw, S), lambda b, g, *_: (b, g, 0)),
        pl.BlockSpec((1, GATE_ROWS, S), lambda b, g, *_: (b, g, 0)),
        pl.BlockSpec((1, gw, S), lambda b, g, *_: (b, g, 0)),
        per_group(NC, HEAD_DIM), per_group(HEAD_DIM, NC),
        per_group(S, KV_WIDTH), per_group(S, KV_WIDTH),
        per_group(V_ROWS, S), per_group(V_ROWS, S),
        pl.BlockSpec((n_slc, NC), lambda b, g, *_: (0, 0)),
    ]
    scratch = [
        pltpu.VMEM((n_q, n_ck, 2 * (KV_CHUNK // SLC_BLOCK), lanes), BF16),
        pltpu.VMEM((n_q, HEAD_DIM, lanes), BF16),
        pltpu.VMEM((n_q, 8, lanes), F32),
        pltpu.VMEM((n_q, V_ROWS, lanes), F32),
        pltpu.VMEM((n_q, HEAD_DIM, lanes), F32),
        pltpu.VMEM((IN_FLIGHT, KV_CHUNK, lanes), F32),
        pltpu.VMEM((IN_FLIGHT, 8, lanes), F32),
        pltpu.VMEM((tpc, KV_CHUNK, lanes), F32),
        pltpu.VMEM((tpc, 8, lanes), F32),
        pltpu.VMEM((2, NC, lanes), F32),
        pltpu.VMEM((IN_FLIGHT, WIN_KEYS, lanes), F32),
        pltpu.VMEM((IN_FLIGHT, 8, lanes), F32),
    ]
    return pl.pallas_call(
        kernel,
        out_shape=jax.ShapeDtypeStruct((B, A_WIDTH, S), BF16),
        grid_spec=pltpu.PrefetchScalarGridSpec(
            num_scalar_prefetch=2,
            grid=(B, A_GROUPS),
            in_specs=in_specs,
            out_specs=pl.BlockSpec((1, gw, S), lambda b, g, *_: (b, g, 0)),
            scratch_shapes=scratch),
        compiler_params=pltpu.CompilerParams(
            dimension_semantics=("parallel", "parallel"),
            vmem_limit_bytes=V7X_VMEM_LIMIT_BYTES),
        name="nsa_attention",
    )(jc, jq, qr, qn, gt, szat, kc, vct, ks, kw, vts, vtw, ovl)


def _hgrn_stages(q_ref, f_ref, i_ref, sz_ref, lb, gain, tri, bd, causal, state_s, h, out):
    T = q_ref.shape[1]
    cpg = HG_ROWS // CHUNK
    lanes = slice(h * B_KDIM, (h + 1) * B_KDIM)
    groups = [slice(gi * HG_ROWS, (gi + 1) * HG_ROWS) for gi in range(T // HG_ROWS)]
    mid = {}

    def decay():
        mid["kk"], mid["b"] = [], []
        for rows in groups:
            fg = lb + (1.0 - lb) * _sigmoid(f_ref[0, rows, lanes])
            logf = jnp.log(fg)
            l_hi = logf.astype(BF16)
            l_lo = (logf - l_hi.astype(F32)).astype(BF16)
            bb = _dot(tri, jnp.concatenate([l_hi, l_lo], axis=1))
            mid["kk"].append(1.0 - fg)
            mid["b"].append(bb[:, :B_KDIM] + bb[:, B_KDIM:])

    def intra():
        mid["qe"], mid["oi"], mid["kv"], mid["dl"] = [], [], [], []
        for rows, kk, b in zip(groups, mid["kk"], mid["b"]):
            b_last = jnp.concatenate(
                [jnp.broadcast_to(b[(c + 1) * CHUNK - 1:(c + 1) * CHUNK, :], (CHUNK, B_KDIM))
                 for c in range(cpg)], axis=0)
            qe = (_silu(q_ref[0, rows, lanes].astype(F32)) * jnp.exp(b)).astype(BF16)
            ke = (kk * jnp.exp(-b)).astype(BF16)
            kd = (kk * jnp.exp(b_last - b)).astype(BF16)
            attn = jnp.where(causal, _dot_nt(qe, ke), 0.0).astype(BF16)
            v = i_ref[0, rows, lanes]
            mid["oi"].append(_dot(attn, v))
            mid["qe"].append(qe)
            kd_bd = jnp.concatenate([kd] * cpg, axis=1) * bd
            kv_all = lax.dot_general(v, kd_bd, (((0,), (0,)), ((), ())), preferred_element_type=F32)
            for c in range(cpg):
                mid["kv"].append(kv_all[:, c * B_KDIM:(c + 1) * B_KDIM])
                mid["dl"].append(jnp.exp(b_last[c * CHUNK:c * CHUNK + 1, :]))

    def inter():
        state = state_s[h]
        outs = []
        for n in range(T // CHUNK):
            gi, c = divmod(n, cpg)
            sl = slice(c * CHUNK, (c + 1) * CHUNK)
            rows = slice(n * CHUNK, (n + 1) * CHUNK)
            o = mid["oi"][gi][sl] + _dot_nt(mid["qe"][gi][sl], state.astype(BF16))
            o = o * lax.rsqrt(jnp.mean(o * o, axis=-1, keepdims=True) + NORM_EPS)
            outs.append((o * gain * sz_ref[0, rows, lanes].astype(F32)).astype(BF16))
            state = mid["dl"][n] * state + mid["kv"][n]
        state_s[h] = state
        out.append(jnp.concatenate(outs, axis=0))

    return [decay, intra, inter]


def _out_kernel(x_ref, oat_ref, q_ref, f_ref, i_ref, sz_ref, lbl_ref, g_ref, tri_ref, bd_ref,
                wgm_ref, bgm_ref, wa_ref, wb_ref, wo_ref, lng_ref, lnb_ref, o_ref, state_s, sg_s, ya_s,
                *, alpha, layer):
    @pl.when(pl.program_id(1) == 0)
    def _():
        state_s[...] = jnp.zeros_like(state_s)

    x = x_ref[0]
    xb = x.astype(BF16)

    def gate_chunk(j):
        def run():
            sg_s[:, j:j + GATE_COLS] = _sigmoid(_dot(xb, wgm_ref[:, j:j + GATE_COLS]) + bgm_ref[:, j:j + GATE_COLS])
        return run

    def ya_chunk(j):
        def run():
            ya_s[:, j:j + YA_COLS] = lax.dot_general(oat_ref[0], wa_ref[:, j:j + YA_COLS], (((0,), (0,)), ((), ())),
                                                     preferred_element_type=F32)
        return run

    fillers = [gate_chunk(j) for j in range(0, 2 * D_MODEL, GATE_COLS)] + \
              [ya_chunk(j) for j in range(0, D_MODEL, YA_COLS)]

    lg = lbl_ref[...]
    e = jnp.exp(lg - jnp.max(lg, axis=0, keepdims=True))
    lb_all = jnp.sum(e[0:layer + 1], axis=0, keepdims=True) / jnp.sum(e, axis=0, keepdims=True)
    tri = tri_ref[...]
    bd = bd_ref[...]
    ri = lax.broadcasted_iota(jnp.int32, (HG_ROWS, HG_ROWS), 0)
    ci = lax.broadcasted_iota(jnp.int32, (HG_ROWS, HG_ROWS), 1)
    causal = (ri // CHUNK == ci // CHUNK) & (ci <= ri)

    obs = []
    stages = []
    for h in range(B_HEADS):
        lanes = slice(h * B_KDIM, (h + 1) * B_KDIM)
        stages += _hgrn_stages(q_ref, f_ref, i_ref, sz_ref, lb_all[:, lanes], g_ref[:, lanes],
                               tri, bd, causal, state_s, h, obs)
    for k, stage in enumerate(stages):
        if k < len(fillers):
            fillers[k]()
        stage()
    for filler in fillers[len(stages):]:
        filler()
    ob = jnp.concatenate(obs, axis=1)

    sub = x.shape[0] // TAIL_SPLIT
    halves = [slice(i * sub, (i + 1) * sub) for i in range(TAIL_SPLIT)]
    ybs = [_dot(ob[r], wb_ref[...]) for r in halves]
    ys = [(sg_s[r, :D_MODEL] * ya_s[r, :] + sg_s[r, D_MODEL:] * yb).astype(BF16) for r, yb in zip(halves, ybs)]
    outs = [_dot(y, wo_ref[...]) for y in ys]
    for r, out in zip(halves, outs):
        res = alpha * x[r] + out
        mu = jnp.mean(res, axis=-1, keepdims=True)
        d = res - mu
        var = jnp.mean(d * d, axis=-1, keepdims=True)
        o_ref[0, r, :] = d * lax.rsqrt(var + NORM_EPS) * lng_ref[...] + lnb_ref[...]


def _merge_out(x, oat, qb, fb, ib, szb, lb_logits, gain, tri, bd, wgm, bgm, wa, wb, wo, lng, lnb, alpha, layer):
    B, S, D = x.shape
    T = OUT_ROWS
    full = lambda a: pl.BlockSpec(a.shape, lambda b, s: (0,) * a.ndim)
    rows = lambda w: pl.BlockSpec((1, T, w), lambda b, s: (b, s, 0))
    return pl.pallas_call(
        functools.partial(_out_kernel, alpha=alpha, layer=layer),
        out_shape=jax.ShapeDtypeStruct((B, S, D), x.dtype),
        grid=(B, S // T),
        in_specs=[rows(D), pl.BlockSpec((1, A_WIDTH, T), lambda b, s: (b, 0, s)),
                  rows(B_FWIDTH), rows(B_FWIDTH), rows(B_WIDTH), rows(B_WIDTH),
                  full(lb_logits), full(gain), full(tri), full(bd),
                  full(wgm), full(bgm), full(wa), full(wb), full(wo), full(lng), full(lnb)],
        out_specs=rows(D),
        scratch_shapes=[pltpu.VMEM((B_HEADS, B_VDIM, B_KDIM), F32),
                        pltpu.VMEM((T, 2 * D), F32),
                        pltpu.VMEM((T, D), F32)],
        compiler_params=pltpu.CompilerParams(
            dimension_semantics=("parallel", "arbitrary"),
            vmem_limit_bytes=V7X_VMEM_LIMIT_BYTES),
        name="hgrn_merge_out",
    )(x, oat, qb, fb, ib, szb, lb_logits, gain, tri, bd, wgm, bgm, wa, wb, wo, lng, lnb)


def _rope_tables(S):
    inv = ROPE_THETA ** (-np.arange(0, HEAD_DIM, 2, dtype=np.float64) / HEAD_DIM)
    ang = np.arange(S, dtype=np.float64)[:, None] * inv[None, :]
    cos = np.concatenate([np.cos(ang), np.cos(ang)], axis=-1)
    sin = np.concatenate([np.sin(ang), np.sin(ang)], axis=-1)
    first = (np.arange(HEAD_DIM) < HEAD_DIM // 2)[None, :]
    sina = np.where(first, -sin, 0.0)
    sinb = np.where(first, 0.0, sin)
    tile = lambda a: np.concatenate([a] * A_GROUPS, axis=-1)
    return tuple(jnp.asarray(np.ascontiguousarray(a), dtype=F32)
                 for a in (cos.T, sin.T, tile(cos), tile(sina), tile(sinb)))


def _overlap_t(S):
    n_cmp = S // CMP_STRIDE
    n_slc = S // SLC_BLOCK
    cs = np.arange(n_cmp)[None, :] * CMP_STRIDE
    ss = np.arange(n_slc)[:, None] * SLC_BLOCK
    ov = (cs < ss + SLC_BLOCK) & (cs + CMP_BLOCK > ss) & (np.arange(n_cmp)[None, :] < n_cmp - 1)
    return jnp.asarray(ov, dtype=BF16)


def _block_tri():
    r = np.arange(HG_ROWS)
    return jnp.asarray((r[:, None] // CHUNK == r[None, :] // CHUNK) & (r[None, :] <= r[:, None]), dtype=BF16)


def _block_diag():
    r = np.arange(HG_ROWS)[:, None] // CHUNK
    c = np.arange(HG_ROWS // CHUNK * B_KDIM)[None, :] // B_KDIM
    return jnp.asarray(r == c, dtype=BF16)


def _layer(x, l, w_in, b_in, pe_k, w_k1, w_k2, pe_v, w_v1, w_v2, lb_logits, norm_g,
           w_a, w_b, w_o, ln_g, ln_b):
    B, S, D = x.shape
    alpha = (2 * DEPTH) ** 0.25
    o = _OFF
    w_t = w_in.T
    wsl = lambda i: w_t[o[i]:o[i + 1]]
    bsl = lambda i: b_in[o[i]:o[i + 1]]
    kvw, kvb = wsl(1), bsl(1)
    kv_w = lambda j: kvw[j * KV_WIDTH:(j + 1) * KV_WIDTH]
    kv_b = lambda j: kvb[j * KV_WIDTH:(j + 1) * KV_WIDTH]
    gw, gb = wsl(2), bsl(2)
    gidx = np.zeros((A_GROUPS, GATE_ROWS), np.int32)
    gmask = np.zeros((A_GROUPS, GATE_ROWS), np.float32)
    for g in range(A_GROUPS):
        for br in range(3):
            for h in range(A_HPG):
                gidx[g, br * A_HPG + h] = (g * A_HPG + h) * 3 + br
                gmask[g, br * A_HPG + h] = 1.0
    gidx, gmask = gidx.reshape(-1), gmask.reshape(-1)
    gw_t = gw[gidx] * gmask[:, None]
    gb_t = gb[gidx] * gmask

    wt = jnp.concatenate([wsl(0), kv_w(3), kv_w(5), gw_t, wsl(3)], axis=0).astype(BF16)
    bt = jnp.concatenate([bsl(0), kv_b(3), kv_b(5), gb_t, bsl(3)])[:, None]
    wn = jnp.concatenate([kv_w(0), kv_w(1), kv_w(2), kv_w(4), wsl(7), wsl(4), wsl(6), wsl(5)],
                         axis=0).astype(BF16)
    bn = jnp.concatenate([kv_b(0), kv_b(1), kv_b(2), kv_b(4), bsl(7), bsl(4), bsl(6), bsl(5)])[None, :]
    cost, sint, cos, sina, sinb = _rope_tables(S)

    (qr, qn, vts, vtw, gt, szat, kcmp, vcmp, ks, kw, qb, fb, ib, szb) = _project(
        x, wn, bn, wt, bt, cost, sint, cos, sina, sinb)

    half = CMP_STRIDE * HEAD_DIM

    def w1_planes(w1):
        both = jnp.concatenate([w1[:half], w1[half:]], axis=1).reshape(CMP_STRIDE, HEAD_DIM, 2 * CMP_HIDDEN)
        z = jnp.zeros_like(both)
        planes = jnp.concatenate([jnp.concatenate([both, z], axis=2),
                                  jnp.concatenate([z, both], axis=2)], axis=1)
        return planes.reshape(CMP_STRIDE * KV_WIDTH, 2 * A_GROUPS * CMP_HIDDEN).astype(BF16)

    kc, vct = _compress(kcmp, vcmp,
                        w1_planes(w_k1), w_k1, pe_k.reshape(-1, 1), w_k2.astype(BF16),
                        w1_planes(w_v1), w_v1, pe_v.reshape(-1, 1), w_v2.T.astype(BF16))

    oat = _nsa(qr, qn, gt, szat, kc, vct, ks, kw, vts, vtw, _overlap_t(S))
    wgm = jnp.concatenate([wsl(8), wsl(9)], axis=0).astype(BF16).T
    bgm = jnp.concatenate([bsl(8), bsl(9)])[None, :]
    return _merge_out(x, oat, qb, fb, ib, szb, lb_logits, norm_g[None, :], _block_tri(), _block_diag(),
                      wgm, bgm, w_a.astype(BF16), w_b.astype(BF16), w_o.astype(BF16),
                      ln_g[None, :], ln_b[None, :], alpha, l)


@jax.jit
def kernel(x, w_in, b_in, pe_cmp_k, w_cmp_k1, w_cmp_k2, pe_cmp_v, w_cmp_v1, w_cmp_v2,
           hgrn_lb_logits, hgrn_norm_g, w_branch_a, w_branch_b, w_out, ln_g, ln_b):
    B, S, D = x.shape
    assert D == D_MODEL and S % KV_CHUNK == 0 and S % PROJ_ROWS == 0 and S >= WIN_KEYS
    assert S % OUT_ROWS == 0 and OUT_ROWS % HG_ROWS == 0 and (S // KV_CHUNK) * KV_CHUNK == S
    for l in range(DEPTH):
        x = _layer(x, l, w_in[l], b_in[l], pe_cmp_k[l], w_cmp_k1[l], w_cmp_k2[l],
                   pe_cmp_v[l], w_cmp_v1[l], w_cmp_v2[l], hgrn_lb_logits, hgrn_norm_g[l],
                   w_branch_a[l], w_branch_b[l], w_out[l], ln_g[l], ln_b[l])
    return x
```

```python
import functools
import math

import numpy as np
import jax
import jax.numpy as jnp
from jax import lax
from jax.experimental import pallas as pl
from jax.experimental.pallas import tpu as pltpu

D_MODEL = 1024
DEPTH = 1
A_HEADS = 8
A_GROUPS = 2
A_HPG = A_HEADS // A_GROUPS
HEAD_DIM = 64
A_WIDTH = A_HEADS * HEAD_DIM
KV_WIDTH = A_GROUPS * HEAD_DIM
CMP_BLOCK = 32
CMP_STRIDE = 16
CMP_HIDDEN = 128
SLC_BLOCK = 64
SLC_TOP_N = 16
WINDOW = 512
ROPE_THETA = 10000.0
FORCE_SCORE = 1e30
B_HEADS = 4
B_KDIM = 128
B_VDIM = 128
B_FWIDTH = B_HEADS * B_KDIM
B_WIDTH = B_HEADS * B_VDIM
CHUNK = 64
NORM_EPS = 1e-5

IN_SPLITS = (A_WIDTH, 6 * KV_WIDTH, A_HEADS * 3, A_WIDTH, B_FWIDTH, B_FWIDTH,
             B_WIDTH, B_WIDTH, D_MODEL, D_MODEL)
_OFF = tuple(int(v) for v in np.cumsum((0,) + IN_SPLITS))

V7X_VMEM_LIMIT_BYTES = 56 * 1024 * 1024
PROJ_ROWS = 512
Q_TILE = 128
KV_CHUNK = 512
IN_FLIGHT = 4
AHEAD = 2
DIAG_CHUNKS_PER_TRIP = 4
MAX_JOBS_PER_TRIP = 16
WIN_KEYS = WINDOW + Q_TILE
HG_ROWS = 256
OUT_ROWS = 512
GATE_COLS = 256
YA_COLS = 512
TAIL_SPLIT = 2
MASKED = -1e30
LOG2E = math.log2(math.e)
GATE_ROWS = 16
V_ROWS = HEAD_DIM + 16

F32 = jnp.float32
BF16 = jnp.bfloat16


def _dot(a, b):
    return jnp.dot(a, b, preferred_element_type=F32)


def _dot_nt(a, b):
    return lax.dot_general(a, b, (((1,), (1,)), ((), ())), preferred_element_type=F32)


def _sigmoid(x):
    return 1.0 / (1.0 + jnp.exp(-x))


def _silu(x):
    return x * _sigmoid(x)


def _proj_kernel(x_ref, wn_ref, bn_ref, wt_ref, bt_ref, cost_ref, sint_ref,
                 cos_ref, sina_ref, sinb_ref,
                 qr_ref, qn_ref, vts_ref, vtw_ref, gt_ref, szat_ref, kc_ref, vc_ref,
                 ks_ref, kw_ref, qb_ref, fb_ref, ib_ref, szb_ref, cmp_s):
    xb = x_ref[0].astype(BF16)
    scale = HEAD_DIM ** -0.5 * LOG2E

    ht = _dot_nt(wt_ref[...], xb) + bt_ref[...]
    cost = cost_ref[...]
    sint = sint_ref[...]
    half = HEAD_DIM // 2
    for h in range(A_HEADS):
        blk = ht[h * HEAD_DIM:(h + 1) * HEAD_DIM]
        rot = jnp.concatenate([-blk[half:], blk[:half]], axis=0)
        qr_ref[0, h * HEAD_DIM:(h + 1) * HEAD_DIM, :] = ((blk * cost + rot * sint) * scale).astype(BF16)
        qn_ref[0, h * HEAD_DIM:(h + 1) * HEAD_DIM, :] = (blk * scale).astype(BF16)
    o = A_WIDTH
    ones_rows = jnp.where(lax.broadcasted_iota(jnp.int32, (V_ROWS - HEAD_DIM, ht.shape[1]), 0) == 0, 1.0, 0.0)
    for g in range(A_GROUPS):
        for ref, base in ((vts_ref, o), (vtw_ref, o + KV_WIDTH)):
            rows_g = ht[base + g * HEAD_DIM:base + (g + 1) * HEAD_DIM]
            ref[0, g] = jnp.concatenate([rows_g, ones_rows], axis=0).astype(BF16)
    gt_ref[0] = _sigmoid(ht[o + 2 * KV_WIDTH:o + 2 * KV_WIDTH + 2 * GATE_ROWS])
    o += 2 * KV_WIDTH + 2 * GATE_ROWS
    szat_ref[0] = _silu(ht[o:o + A_WIDTH]).astype(BF16)

    def cols(lo, hi):
        return _dot(xb, wn_ref[:, lo:hi]) + bn_ref[:, lo:hi]

    kv = cols(0, 4 * KV_WIDTH)
    cmp_s[0] = kv[:, 0:KV_WIDTH]
    cmp_s[1] = kv[:, KV_WIDTH:2 * KV_WIDTH]
    pieces = cmp_s.shape[1] // CMP_STRIDE
    for t in range(CMP_STRIDE):
        lanes_t = slice(t * KV_WIDTH, (t + 1) * KV_WIDTH)
        kc_ref[0, :, lanes_t] = cmp_s[0, pl.ds(t, pieces, stride=CMP_STRIDE), :].astype(BF16)
        vc_ref[0, :, lanes_t] = cmp_s[1, pl.ds(t, pieces, stride=CMP_STRIDE), :].astype(BF16)
    cos = cos_ref[...]
    sina = sina_ref[...]
    sinb = sinb_ref[...]

    def rope_rows(k):
        return (k * cos + pltpu.roll(k, 128 - half, axis=1) * sina + pltpu.roll(k, half, axis=1) * sinb)

    ks = rope_rows(kv[:, 2 * KV_WIDTH:3 * KV_WIDTH])
    rows = ks.shape[0]
    lane = lax.broadcasted_iota(jnp.int32, (rows, KV_WIDTH), 1)
    pos = pl.program_id(1) * rows + lax.broadcasted_iota(jnp.int32, (rows, KV_WIDTH), 0)
    blocks_per_chunk = KV_CHUNK // SLC_BLOCK
    ind = jnp.where(lane == HEAD_DIM + (pos // SLC_BLOCK) % blocks_per_chunk, 1.0, 0.0)
    ks_ref[0, 0] = (jnp.where(lane < HEAD_DIM, ks, 0.0) + ind).astype(BF16)
    ks_ref[0, 1] = (jnp.where(lane < HEAD_DIM, pltpu.roll(ks, HEAD_DIM, axis=1), 0.0) + ind).astype(BF16)
    kw = rope_rows(kv[:, 3 * KV_WIDTH:4 * KV_WIDTH])
    kw_ref[0, 0] = jnp.where(lane < HEAD_DIM, kw, 0.0).astype(BF16)
    kw_ref[0, 1] = jnp.where(lane < HEAD_DIM, pltpu.roll(kw, HEAD_DIM, axis=1), 0.0).astype(BF16)
    o = 4 * KV_WIDTH
    szb_ref[0] = _silu(cols(o, o + B_WIDTH)).astype(BF16)
    o += B_WIDTH
    qb_ref[0] = cols(o, o + B_FWIDTH).astype(BF16)
    o += B_FWIDTH
    ib_ref[0] = cols(o, o + B_WIDTH).astype(BF16)
    o += B_WIDTH
    fb_ref[0] = cols(o, o + B_FWIDTH)


def _project(x, wn, bn, wt, bt, cost, sint, cos, sina, sinb):
    B, S, D = x.shape
    T = PROJ_ROWS
    n_t = wt.shape[0]
    n_n = wn.shape[1]
    full = lambda shape: pl.BlockSpec(shape, lambda b, s: (0,) * len(shape))
    row_out = lambda w: pl.BlockSpec((1, T, w), lambda b, s: (b, s, 0))
    col_out = lambda r: pl.BlockSpec((1, r, T), lambda b, s: (b, 0, s))
    sds = jax.ShapeDtypeStruct
    out_shape = (
        sds((B, A_WIDTH, S), BF16), sds((B, A_WIDTH, S), BF16),
        sds((B, A_GROUPS, V_ROWS, S), BF16), sds((B, A_GROUPS, V_ROWS, S), BF16),
        sds((B, 2 * GATE_ROWS, S), F32),
        sds((B, A_WIDTH, S), BF16),
        sds((B, S // CMP_STRIDE, CMP_STRIDE * KV_WIDTH), BF16),
        sds((B, S // CMP_STRIDE, CMP_STRIDE * KV_WIDTH), BF16),
        sds((B, A_GROUPS, S, KV_WIDTH), BF16), sds((B, A_GROUPS, S, KV_WIDTH), BF16),
        sds((B, S, B_FWIDTH), BF16), sds((B, S, B_FWIDTH), F32),
        sds((B, S, B_WIDTH), BF16), sds((B, S, B_WIDTH), BF16),
    )
    out_specs = (
        col_out(A_WIDTH), col_out(A_WIDTH),
        pl.BlockSpec((1, A_GROUPS, V_ROWS, T), lambda b, s: (b, 0, 0, s)),
        pl.BlockSpec((1, A_GROUPS, V_ROWS, T), lambda b, s: (b, 0, 0, s)),
        col_out(2 * GATE_ROWS), col_out(A_WIDTH),
        pl.BlockSpec((1, T // CMP_STRIDE, CMP_STRIDE * KV_WIDTH), lambda b, s: (b, s, 0)),
        pl.BlockSpec((1, T // CMP_STRIDE, CMP_STRIDE * KV_WIDTH), lambda b, s: (b, s, 0)),
        pl.BlockSpec((1, A_GROUPS, T, KV_WIDTH), lambda b, s: (b, 0, s, 0)),
        pl.BlockSpec((1, A_GROUPS, T, KV_WIDTH), lambda b, s: (b, 0, s, 0)),
        row_out(B_FWIDTH), row_out(B_FWIDTH), row_out(B_WIDTH), row_out(B_WIDTH),
    )
    in_specs = [
        pl.BlockSpec((1, T, D), lambda b, s: (b, s, 0)),
        full((D, n_n)), full((1, n_n)), full((n_t, D)), full((n_t, 1)),
        pl.BlockSpec((HEAD_DIM, T), lambda b, s: (0, s)),
        pl.BlockSpec((HEAD_DIM, T), lambda b, s: (0, s)),
        pl.BlockSpec((T, KV_WIDTH), lambda b, s: (s, 0)),
        pl.BlockSpec((T, KV_WIDTH), lambda b, s: (s, 0)),
        pl.BlockSpec((T, KV_WIDTH), lambda b, s: (s, 0)),
    ]
    return pl.pallas_call(
        _proj_kernel, out_shape=out_shape, grid=(B, S // T),
        in_specs=in_specs, out_specs=out_specs,
        scratch_shapes=[pltpu.VMEM((2, T, KV_WIDTH), F32)],
        compiler_params=pltpu.CompilerParams(
            dimension_semantics=("parallel", "parallel"),
            vmem_limit_bytes=V7X_VMEM_LIMIT_BYTES),
        name="in_proj",
    )(x, wn, bn, wt, bt, cost, sint, cos, sina, sinb)


def _compress_kernel(ck_ref, cv_ref, w1k_ref, w1kf_ref, pek_ref, w2k_ref,
                     w1v_ref, w1vf_ref, pev_ref, w2vt_ref, kc_ref, vct_ref):
    nc = ck_ref.shape[1]
    hid = CMP_HIDDEN

    def hidden(c_ref, w1_ref, w1f_ref, pe_ref):
        a = _dot(c_ref[0], w1_ref[...])
        pe_term = jnp.sum(w1f_ref[...] * pe_ref[...], axis=0, keepdims=True)
        out = []
        for g in range(A_GROUPS):
            lo = a[:, 2 * g * hid:(2 * g + 1) * hid]
            hi = pltpu.roll(a[:, (2 * g + 1) * hid:(2 * g + 2) * hid], nc - 1, axis=0)
            out.append(_silu(lo + hi + pe_term).astype(BF16))
        return out

    hk = hidden(ck_ref, w1k_ref, w1kf_ref, pek_ref)
    hv = hidden(cv_ref, w1v_ref, w1vf_ref, pev_ref)
    for g in range(A_GROUPS):
        kc_ref[0, g] = _dot(hk[g], w2k_ref[...]).astype(BF16)
        vct_ref[0, g] = _dot_nt(w2vt_ref[...], hv[g]).astype(BF16)


def _compress(ck, cv, w1k, w1kf, pek, w2k, w1v, w1vf, pev, w2vt):
    B, NC, W = ck.shape
    full = lambda a: pl.BlockSpec(a.shape, lambda b: (0,) * a.ndim)
    blk = pl.BlockSpec((1, NC, W), lambda b: (b, 0, 0))
    return pl.pallas_call(
        _compress_kernel,
        out_shape=(jax.ShapeDtypeStruct((B, A_GROUPS, NC, HEAD_DIM), BF16),
                   jax.ShapeDtypeStruct((B, A_GROUPS, HEAD_DIM, NC), BF16)),
        grid=(B,),
        in_specs=[blk, blk, full(w1k), full(w1kf), full(pek), full(w2k),
                  full(w1v), full(w1vf), full(pev), full(w2vt)],
        out_specs=(pl.BlockSpec((1, A_GROUPS, NC, HEAD_DIM), lambda b: (b, 0, 0, 0)),
                   pl.BlockSpec((1, A_GROUPS, HEAD_DIM, NC), lambda b: (b, 0, 0, 0))),
        compiler_params=pltpu.CompilerParams(
            dimension_semantics=("parallel",),
            vmem_limit_bytes=V7X_VMEM_LIMIT_BYTES),
        name="kv_compress",
    )(ck, cv, w1k, w1kf, pek, w2k, w1v, w1vf, pev, w2vt)


def _nsa_kernel(jc_ref, jq_ref, qr_ref, qn_ref, gt_ref, szat_ref, kc_ref, vct_ref, ks_ref, kw_ref,
                vts_ref, vtw_ref, ovl_ref, o_ref,
                bias_s, wq_s, m_s, acc_s, ocw_s, sbuf_s, smax_s, dbuf_s, dmax_s, scb_s, swb_s, wmax_s,
                *, seq, n_sel, n_jobs, jobs_per_trip):
    nc = kc_ref.shape[2]
    n_q = seq // Q_TILE
    n_ck = seq // KV_CHUNK
    lanes = A_HPG * Q_TILE
    bpc = KV_CHUNK // SLC_BLOCK
    tpc = KV_CHUNK // Q_TILE
    win_tiles = WINDOW // Q_TILE

    def stack_heads(ref, q0):
        return jnp.concatenate(
            [ref[0, h * HEAD_DIM:(h + 1) * HEAD_DIM, pl.ds(q0, Q_TILE)] for h in range(A_HPG)], axis=1)

    def per_head(a):
        return jnp.concatenate([a] * A_HPG, axis=1)

    def gate_row(gt, branch):
        return jnp.concatenate(
            [gt[branch * A_HPG + h:branch * A_HPG + h + 1, :] for h in range(A_HPG)], axis=1)

    def tile_start(qi):
        return pl.multiple_of(qi * Q_TILE, Q_TILE)

    lane_q = lax.broadcasted_iota(jnp.int32, (1, lanes), 1) & (Q_TILE - 1)
    r_sq = lax.broadcasted_iota(jnp.int32, (Q_TILE, lanes), 0)
    causal_sq = r_sq <= lane_q
    lower_sq = r_sq > lane_q
    q_pad = jnp.zeros((KV_WIDTH - HEAD_DIM, lanes), BF16)

    def cmp_keys(rb):
        return min(nc, -(-(rb * KV_CHUNK // CMP_STRIDE) // 128) * 128)

    def cmp_rows(qi):
        return min(nc, -(-((qi + 1) * Q_TILE // CMP_STRIDE) // 16) * 16)

    def cmp_scores(qi):
        return _dot(kc_ref[0, 0, 0:cmp_rows(qi), :], stack_heads(qn_ref, tile_start(qi)))

    def zero_rows(a, rows):
        if a.shape[0] == rows:
            return a
        return jnp.concatenate([a, jnp.zeros((rows - a.shape[0], a.shape[1]), a.dtype)], axis=0)

    def select_tile(qi, rb, sc):
        nk = cmp_keys(rb)
        n_causal = (qi + 1) * Q_TILE // SLC_BLOCK
        s0 = tile_start(qi)
        t_row = s0 + lax.broadcasted_iota(jnp.int32, (1, Q_TILE), 1)
        wq_s[qi] = stack_heads(qr_ref, s0)

        nv = cmp_rows(qi)
        lo = max(0, (qi * Q_TILE - (CMP_BLOCK - 1)) // CMP_STRIDE // 8 * 8)
        n_end = (lo + lax.broadcasted_iota(jnp.int32, (nv - lo, Q_TILE), 0)) * CMP_STRIDE + (CMP_BLOCK - 1)
        edge = sc[lo:] + per_head(jnp.where(n_end <= t_row, 0.0, MASKED))
        sc = jnp.concatenate([sc[:lo], edge], axis=0) if lo else edge
        mc = jnp.max(sc, axis=0, keepdims=True)
        pc = jnp.exp2(sc - mc)
        lc = jnp.sum(pc, axis=0, keepdims=True)
        pc = pc * jnp.where(s0 + lane_q >= CMP_BLOCK - 1, 1.0 / lc, 0.0)
        o_cmp = _dot(vct_ref[0, 0, :, 0:nk], zero_rows(pc.astype(BF16), nk))

        ps = pc[:, 0:Q_TILE]
        for h in range(1, A_HPG):
            ps = ps + pc[:, h * Q_TILE:(h + 1) * Q_TILE]
        p_hi = ps.astype(BF16)
        p_lo = (ps - p_hi.astype(F32)).astype(BF16)
        imp2 = _dot(ovl_ref[0:8 * rb, 0:nk],
                    zero_rows(jnp.concatenate([p_hi, p_lo], axis=1), nk))
        imp = imp2[:, :Q_TILE] + imp2[:, Q_TILE:]

        j_idx = lax.broadcasted_iota(jnp.int32, (8 * rb, Q_TILE), 0)
        cur = t_row // SLC_BLOCK
        sel = j_idx <= cur
        if 8 * rb > n_sel:
            forced = (j_idx == 0) | (j_idx == cur) | (j_idx == cur - 1)
            val = jnp.where(j_idx > cur, -1.0, jnp.where(forced, FORCE_SCORE, imp))
            rows = [val[8 * j:8 * j + 8] for j in range(rb)]
            cnt = [jnp.zeros((8, Q_TILE), F32) for _ in range(rb)]
            jl = lax.broadcasted_iota(jnp.int32, (8, Q_TILE), 0)
            for kb in range(rb):
                for kl in range(8):
                    if 8 * kb + kl >= n_causal:
                        continue
                    row = rows[kb][kl:kl + 1, :]
                    for jb in range(rb):
                        if jb < kb:
                            beats = row > rows[jb]
                        elif jb > kb:
                            beats = row >= rows[jb]
                        else:
                            beats = (row > rows[jb]) | ((row == rows[jb]) & (jl > kl))
                        cnt[jb] = cnt[jb] + jnp.where(beats, 1.0, 0.0)
            sel = sel & (jnp.concatenate(cnt, axis=0) < n_sel)
        bias = per_head(jnp.where(sel, 0.0, MASKED))
        pad = jnp.zeros((bpc, lanes), F32)
        for c in range(rb):
            bias_s[qi, c] = jnp.concatenate([bias[c * bpc:(c + 1) * bpc], pad], axis=0).astype(BF16)

        gt = gt_ref[0, :, pl.ds(s0, Q_TILE)]
        ocw_s[qi] = gate_row(gt, 0) * o_cmp
        m_s[qi] = jnp.full((8, lanes), MASKED, F32)
        acc_s[qi] = jnp.zeros((V_ROWS, lanes), F32)

    def win_finish(qi, sw, mw, w0, keys=WIN_KEYS):
        pw = jnp.exp2(sw - mw)
        ow = _dot(vtw_ref[0, 0, :, pl.ds(w0, keys)], pw.astype(BF16))
        o_win = ow[0:HEAD_DIM] * (1.0 / ow[HEAD_DIM:HEAD_DIM + 1])
        gt = gt_ref[0, :, pl.ds(tile_start(qi), Q_TILE)]
        ocw_s[qi] = ocw_s[qi] + gate_row(gt, 2) * o_win

    def win_scores(qi):
        slot = qi % IN_FLIGHT
        keys = min(qi + 1, win_tiles + 1) * Q_TILE
        w0 = (qi + 1) * Q_TILE - keys
        wq = jnp.concatenate([stack_heads(qr_ref, qi * Q_TILE), q_pad], axis=0)
        sw = _dot(kw_ref[0, 0, w0:w0 + keys, :], wq)
        parts = [jnp.where(causal_sq, sw[keys - Q_TILE:], MASKED)]
        if keys > Q_TILE:
            oldest = sw[:Q_TILE]
            if qi >= win_tiles:
                oldest = jnp.where(lower_sq, oldest, MASKED)
            parts = [oldest, sw[Q_TILE:keys - Q_TILE]] + parts if keys > 2 * Q_TILE else [oldest] + parts
        sw = jnp.concatenate(parts, axis=0)
        swb_s[slot, 0:keys, :] = sw
        wmax_s[slot] = jnp.broadcast_to(jnp.max(sw, axis=0, keepdims=True), (8, lanes))

    def win_tile(qi):
        slot = qi % IN_FLIGHT
        keys = min(qi + 1, win_tiles + 1) * Q_TILE
        win_finish(qi, swb_s[slot, 0:keys, :], wmax_s[slot][0:1], (qi + 1) * Q_TILE - keys, keys)

    scb_s[0, 0:cmp_rows(0), :] = cmp_scores(0)
    for u in range(min(AHEAD, n_q)):
        win_scores(u)
    for qi in range(n_q):
        rb = qi // tpc + 1
        if qi + 1 < n_q:
            scb_s[(qi + 1) & 1, 0:cmp_rows(qi + 1), :] = cmp_scores(qi + 1)
        if qi + AHEAD < n_q:
            win_scores(qi + AHEAD)
        select_tile(qi, rb, scb_s[qi & 1, 0:cmp_rows(qi), :])
        win_tile(qi)

    wq_pad = jnp.zeros((KV_WIDTH - HEAD_DIM - 2 * bpc, lanes), BF16)

    def scores(c, qi, diag):
        rows = KV_CHUNK if diag is None else (diag + 1) * Q_TILE
        k0 = pl.multiple_of(c * KV_CHUNK, KV_CHUNK)
        wq = jnp.concatenate([wq_s[qi], bias_s[qi, c], wq_pad], axis=0)
        s = _dot(ks_ref[0, 0, pl.ds(k0, rows), :], wq)
        if diag is not None:
            head = [s[:rows - Q_TILE]] if diag > 0 else []
            s = jnp.concatenate(head + [jnp.where(causal_sq, s[rows - Q_TILE:], MASKED)], axis=0)
        return s, jnp.broadcast_to(jnp.max(s, axis=0, keepdims=True), (8, lanes))

    def absorb(c, qi, s, s_max):
        rows = s.shape[0]
        k0 = pl.multiple_of(c * KV_CHUNK, KV_CHUNK)
        m_old = m_s[qi][0:1]
        m_new = jnp.maximum(m_old, s_max[0:1])
        alpha = jnp.exp2(m_old - m_new)
        p = jnp.exp2(s - m_new)
        m_s[qi] = jnp.broadcast_to(m_new, (8, lanes))
        acc_s[qi] = alpha * acc_s[qi] + _dot(vts_ref[0, 0, :, pl.ds(k0, rows)], p.astype(BF16))

    def full_scores(j, slot):
        sbuf_s[slot], smax_s[slot] = scores(jc_ref[j], jq_ref[j], None)

    def full_group(i, _):
        j0 = jobs_per_trip * i
        for u in range(jobs_per_trip):
            full_scores(j0 + u + AHEAD, (u + AHEAD) % IN_FLIGHT)
            absorb(jc_ref[j0 + u], jq_ref[j0 + u], sbuf_s[u % IN_FLIGHT], smax_s[u % IN_FLIGHT])
        return 0

    if n_jobs:
        for u in range(AHEAD):
            full_scores(u, u)
        lax.fori_loop(0, n_jobs // jobs_per_trip, full_group, 0)

    def diag_scores(c, r):
        dbuf_s[r, 0:(r + 1) * Q_TILE, :], dmax_s[r] = scores(c, c * tpc + r, r)

    def diag_chunks(i, _):
        for k in range(DIAG_CHUNKS_PER_TRIP):
            c = DIAG_CHUNKS_PER_TRIP * i + k
            nxt = jnp.minimum(c + 1, n_ck - 1)
            for r in range(tpc):
                ahead = r + AHEAD
                diag_scores(c if ahead < tpc else nxt, ahead % tpc)
                absorb(c, c * tpc + r, dbuf_s[r, 0:(r + 1) * Q_TILE, :], dmax_s[r])
        return 0

    for r in range(AHEAD):
        diag_scores(0, r)
    lax.fori_loop(0, n_ck // DIAG_CHUNKS_PER_TRIP, diag_chunks, 0)

    def finish(qi, _):
        s0 = tile_start(qi)
        gt = gt_ref[0, :, pl.ds(s0, Q_TILE)]
        acc = acc_s[qi]
        ot = ocw_s[qi] + gate_row(gt, 1) * acc[0:HEAD_DIM] * (1.0 / acc[HEAD_DIM:HEAD_DIM + 1])
        ot = (ot * stack_heads(szat_ref, s0).astype(F32)).astype(o_ref.dtype)
        for h in range(A_HPG):
            o_ref[0, h * HEAD_DIM:(h + 1) * HEAD_DIM, pl.ds(s0, Q_TILE)] = ot[:, h * Q_TILE:(h + 1) * Q_TILE]
        return 0

    lax.fori_loop(0, n_q, finish, 0)


def _nsa(qr, qn, gt, szat, kc, vct, ks, kw, vts, vtw, ovl):
    B, _, S = qr.shape
    NC = kc.shape[2]
    n_slc = S // SLC_BLOCK
    n_sel = min(SLC_TOP_N, n_slc)
    n_q = S // Q_TILE
    n_ck = S // KV_CHUNK
    tpc = KV_CHUNK // Q_TILE
    gw = A_HPG * HEAD_DIM
    lanes = A_HPG * Q_TILE
    jobs = [(c, q) for c in range(n_ck) for q in range((c + 1) * tpc, n_q)]
    n_jobs = len(jobs)
    jobs_per_trip = max(j for j in range(IN_FLIGHT, MAX_JOBS_PER_TRIP + 1, IN_FLIGHT) if n_jobs % j == 0)
    assert tpc == IN_FLIGHT and AHEAD < IN_FLIGHT and n_ck % DIAG_CHUNKS_PER_TRIP == 0
    jobs = jobs + [jobs[-1] if jobs else (0, 0)] * AHEAD
    jc = jnp.asarray([j[0] for j in jobs], jnp.int32)
    jq = jnp.asarray([j[1] for j in jobs], jnp.int32)
    kernel = functools.partial(_nsa_kernel, seq=S, n_sel=n_sel, n_jobs=n_jobs, jobs_per_trip=jobs_per_trip)
    per_group = lambda rows, cols: pl.BlockSpec((1, 1, rows, cols), lambda b, g, *_: (b, g, 0, 0))
    in_specs = [
        pl.BlockSpec((1, gw, S), lambda b, g, *_: (b, g, 0)),
        pl.BlockSpec((1, gw, S), lambda b, g, *_: (b, g, 0)),
        pl.BlockSpec((1, GATE_ROWS, S), lambda b, g, *_: (b, g, 0)),
        pl.BlockSpec((1, gw, S), lambda b, g, *_: (b, g, 0)),
        per_group(NC, HEAD_DIM), per_group(HEAD_DIM, NC),
        per_group(S, KV_WIDTH), per_group(S, KV_WIDTH),
        per_group(V_ROWS, S), per_group(V_ROWS, S),
        pl.BlockSpec((n_slc, NC), lambda b, g, *_: (0, 0)),
    ]
    scratch = [
        pltpu.VMEM((n_q, n_ck, 2 * (KV_CHUNK // SLC_BLOCK), lanes), BF16),
        pltpu.VMEM((n_q, HEAD_DIM, lanes), BF16),
        pltpu.VMEM((n_q, 8, lanes), F32),
        pltpu.VMEM((n_q, V_ROWS, lanes), F32),
        pltpu.VMEM((n_q, HEAD_DIM, lanes), F32),
        pltpu.VMEM((IN_FLIGHT, KV_CHUNK, lanes), F32),
        pltpu.VMEM((IN_FLIGHT, 8, lanes), F32),
        pltpu.VMEM((tpc, KV_CHUNK, lanes), F32),
        pltpu.VMEM((tpc, 8, lanes), F32),
        pltpu.VMEM((2, NC, lanes), F32),
        pltpu.VMEM((IN_FLIGHT, WIN_KEYS, lanes), F32),
        pltpu.VMEM((IN_FLIGHT, 8, lanes), F32),
    ]
    return pl.pallas_call(
        kernel,
        out_shape=jax.ShapeDtypeStruct((B, A_WIDTH, S), BF16),
        grid_spec=pltpu.PrefetchScalarGridSpec(
            num_scalar_prefetch=2,
            grid=(B, A_GROUPS),
            in_specs=in_specs,
            out_specs=pl.BlockSpec((1, gw, S), lambda b, g, *_: (b, g, 0)),
            scratch_shapes=scratch),
        compiler_params=pltpu.CompilerParams(
            dimension_semantics=("parallel", "parallel"),
            vmem_limit_bytes=V7X_VMEM_LIMIT_BYTES),
        name="nsa_attention",
    )(jc, jq, qr, qn, gt, szat, kc, vct, ks, kw, vts, vtw, ovl)


def _hgrn_stages(q_ref, f_ref, i_ref, sz_ref, lb, gain, tri, bd, causal, state_s, h, out):
    T = q_ref.shape[1]
    cpg = HG_ROWS // CHUNK
    lanes = slice(h * B_KDIM, (h + 1) * B_KDIM)
    groups = [slice(gi * HG_ROWS, (gi + 1) * HG_ROWS) for gi in range(T // HG_ROWS)]
    mid = {}

    def decay():
        mid["kk"], mid["b"] = [], []
        for rows in groups:
            fg = lb + (1.0 - lb) * _sigmoid(f_ref[0, rows, lanes])
            logf = jnp.log(fg)
            l_hi = logf.astype(BF16)
            l_lo = (logf - l_hi.astype(F32)).astype(BF16)
            bb = _dot(tri, jnp.concatenate([l_hi, l_lo], axis=1))
            mid["kk"].append(1.0 - fg)
            mid["b"].append(bb[:, :B_KDIM] + bb[:, B_KDIM:])

    def intra():
        mid["qe"], mid["oi"], mid["kv"], mid["dl"] = [], [], [], []
        for rows, kk, b in zip(groups, mid["kk"], mid["b"]):
            b_last = jnp.concatenate(
                [jnp.broadcast_to(b[(c + 1) * CHUNK - 1:(c + 1) * CHUNK, :], (CHUNK, B_KDIM))
                 for c in range(cpg)], axis=0)
            qe = (_silu(q_ref[0, rows, lanes].astype(F32)) * jnp.exp(b)).astype(BF16)
            ke = (kk * jnp.exp(-b)).astype(BF16)
            kd = (kk * jnp.exp(b_last - b)).astype(BF16)
            attn = jnp.where(causal, _dot_nt(qe, ke), 0.0).astype(BF16)
            v = i_ref[0, rows, lanes]
            mid["oi"].append(_dot(attn, v))
            mid["qe"].append(qe)
            kd_bd = jnp.concatenate([kd] * cpg, axis=1) * bd
            kv_all = lax.dot_general(v, kd_bd, (((0,), (0,)), ((), ())), preferred_element_type=F32)
            for c in range(cpg):
                mid["kv"].append(kv_all[:, c * B_KDIM:(c + 1) * B_KDIM])
                mid["dl"].append(jnp.exp(b_last[c * CHUNK:c * CHUNK + 1, :]))

    def inter():
        state = state_s[h]
        outs = []
        for n in range(T // CHUNK):
            gi, c = divmod(n, cpg)
            sl = slice(c * CHUNK, (c + 1) * CHUNK)
            rows = slice(n * CHUNK, (n + 1) * CHUNK)
            o = mid["oi"][gi][sl] + _dot_nt(mid["qe"][gi][sl], state.astype(BF16))
            o = o * lax.rsqrt(jnp.mean(o * o, axis=-1, keepdims=True) + NORM_EPS)
            outs.append((o * gain * sz_ref[0, rows, lanes].astype(F32)).astype(BF16))
            state = mid["dl"][n] * state + mid["kv"][n]
        state_s[h] = state
        out.append(jnp.concatenate(outs, axis=0))

    return [decay, intra, inter]


def _out_kernel(x_ref, oat_ref, q_ref, f_ref, i_ref, sz_ref, lbl_ref, g_ref, tri_ref, bd_ref,
                wgm_ref, bgm_ref, wa_ref, wb_ref, wo_ref, lng_ref, lnb_ref, o_ref, state_s, sg_s, ya_s,
                *, alpha, layer):
    @pl.when(pl.program_id(1) == 0)
    def _():
        state_s[...] = jnp.zeros_like(state_s)

    x = x_ref[0]
    xb = x.astype(BF16)

    def gate_chunk(j):
        def run():
            sg_s[:, j:j + GATE_COLS] = _sigmoid(_dot(xb, wgm_ref[:, j:j + GATE_COLS]) + bgm_ref[:, j:j + GATE_COLS])
        return run

    def ya_chunk(j):
        def run():
            ya_s[:, j:j + YA_COLS] = lax.dot_general(oat_ref[0], wa_ref[:, j:j + YA_COLS], (((0,), (0,)), ((), ())),
                                                     preferred_element_type=F32)
        return run

    fillers = [gate_chunk(j) for j in range(0, 2 * D_MODEL, GATE_COLS)] + \
              [ya_chunk(j) for j in range(0, D_MODEL, YA_COLS)]

    lg = lbl_ref[...]
    e = jnp.exp(lg - jnp.max(lg, axis=0, keepdims=True))
    lb_all = jnp.sum(e[0:layer + 1], axis=0, keepdims=True) / jnp.sum(e, axis=0, keepdims=True)
    tri = tri_ref[...]
    bd = bd_ref[...]
    ri = lax.broadcasted_iota(jnp.int32, (HG_ROWS, HG_ROWS), 0)
    ci = lax.broadcasted_iota(jnp.int32, (HG_ROWS, HG_ROWS), 1)
    causal = (ri // CHUNK == ci // CHUNK) & (ci <= ri)

    obs = []
    stages = []
    for h in range(B_HEADS):
        lanes = slice(h * B_KDIM, (h + 1) * B_KDIM)
        stages += _hgrn_stages(q_ref, f_ref, i_ref, sz_ref, lb_all[:, lanes], g_ref[:, lanes],
                               tri, bd, causal, state_s, h, obs)
    for k, stage in enumerate(stages):
        if k < len(fillers):
            fillers[k]()
        stage()
    for filler in fillers[len(stages):]:
        filler()
    ob = jnp.concatenate(obs, axis=1)

    sub = x.shape[0] // TAIL_SPLIT
    halves = [slice(i * sub, (i + 1) * sub) for i in range(TAIL_SPLIT)]
    ybs = [_dot(ob[r], wb_ref[...]) for r in halves]
    ys = [(sg_s[r, :D_MODEL] * ya_s[r, :] + sg_s[r, D_MODEL:] * yb).astype(BF16) for r, yb in zip(halves, ybs)]
    outs = [_dot(y, wo_ref[...]) for y in ys]
    for r, out in zip(halves, outs):
        res = alpha * x[r] + out
        mu = jnp.mean(res, axis=-1, keepdims=True)
        d = res - mu
        var = jnp.mean(d * d, axis=-1, keepdims=True)
        o_ref[0, r, :] = d * lax.rsqrt(var + NORM_EPS) * lng_ref[...] + lnb_ref[...]


def _merge_out(x, oat, qb, fb, ib, szb, lb_logits, gain, tri, bd, wgm, bgm, wa, wb, wo, lng, lnb, alpha, layer):
    B, S, D = x.shape
    T = OUT_ROWS
    full = lambda a: pl.BlockSpec(a.shape, lambda b, s: (0,) * a.ndim)
    rows = lambda w: pl.BlockSpec((1, T, w), lambda b, s: (b, s, 0))
    return pl.pallas_call(
        functools.partial(_out_kernel, alpha=alpha, layer=layer),
        out_shape=jax.ShapeDtypeStruct((B, S, D), x.dtype),
        grid=(B, S // T),
        in_specs=[rows(D), pl.BlockSpec((1, A_WIDTH, T), lambda b, s: (b, 0, s)),
                  rows(B_FWIDTH), rows(B_FWIDTH), rows(B_WIDTH), rows(B_WIDTH),
                  full(lb_logits), full(gain), full(tri), full(bd),
                  full(wgm), full(bgm), full(wa), full(wb), full(wo), full(lng), full(lnb)],
        out_specs=rows(D),
        scratch_shapes=[pltpu.VMEM((B_HEADS, B_VDIM, B_KDIM), F32),
                        pltpu.VMEM((T, 2 * D), F32),
                        pltpu.VMEM((T, D), F32)],
        compiler_params=pltpu.CompilerParams(
            dimension_semantics=("parallel", "arbitrary"),
            vmem_limit_bytes=V7X_VMEM_LIMIT_BYTES),
        name="hgrn_merge_out",
    )(x, oat, qb, fb, ib, szb, lb_logits, gain, tri, bd, wgm, bgm, wa, wb, wo, lng, lnb)


def _rope_tables(S):
    inv = ROPE_THETA ** (-np.arange(0, HEAD_DIM, 2, dtype=np.float64) / HEAD_DIM)
    ang = np.arange(S, dtype=np.float64)[:, None] * inv[None, :]
    cos = np.concatenate([np.cos(ang), np.cos(ang)], axis=-1)
    sin = np.concatenate([np.sin(ang), np.sin(ang)], axis=-1)
    first = (np.arange(HEAD_DIM) < HEAD_DIM // 2)[None, :]
    sina = np.where(first, -sin, 0.0)
    sinb = np.where(first, 0.0, sin)
    tile = lambda a: np.concatenate([a] * A_GROUPS, axis=-1)
    return tuple(jnp.asarray(np.ascontiguousarray(a), dtype=F32)
                 for a in (cos.T, sin.T, tile(cos), tile(sina), tile(sinb)))


def _overlap_t(S):
    n_cmp = S // CMP_STRIDE
    n_slc = S // SLC_BLOCK
    cs = np.arange(n_cmp)[None, :] * CMP_STRIDE
    ss = np.arange(n_slc)[:, None] * SLC_BLOCK
    ov = (cs < ss + SLC_BLOCK) & (cs + CMP_BLOCK > ss) & (np.arange(n_cmp)[None, :] < n_cmp - 1)
    return jnp.asarray(ov, dtype=BF16)


def _block_tri():
    r = np.arange(HG_ROWS)
    return jnp.asarray((r[:, None] // CHUNK == r[None, :] // CHUNK) & (r[None, :] <= r[:, None]), dtype=BF16)


def _block_diag():
    r = np.arange(HG_ROWS)[:, None] // CHUNK
    c = np.arange(HG_ROWS // CHUNK * B_KDIM)[None, :] // B_KDIM
    return jnp.asarray(r == c, dtype=BF16)


def _layer(x, l, w_in, b_in, pe_k, w_k1, w_k2, pe_v, w_v1, w_v2, lb_logits, norm_g,
           w_a, w_b, w_o, ln_g, ln_b):
    B, S, D = x.shape
    alpha = (2 * DEPTH) ** 0.25
    o = _OFF
    wsl = lambda i: w_in[:, o[i]:o[i + 1]]
    bsl = lambda i: b_in[o[i]:o[i + 1]]
    kvw, kvb = wsl(1), bsl(1)
    kv_w = lambda j: kvw[:, j * KV_WIDTH:(j + 1) * KV_WIDTH]
    kv_b = lambda j: kvb[j * KV_WIDTH:(j + 1) * KV_WIDTH]
    gw, gb = wsl(2), bsl(2)
    gidx = np.zeros((A_GROUPS, GATE_ROWS), np.int32)
    gmask = np.zeros((A_GROUPS, GATE_ROWS), np.float32)
    for g in range(A_GROUPS):
        for br in range(3):
            for h in range(A_HPG):
                gidx[g, br * A_HPG + h] = (g * A_HPG + h) * 3 + br
                gmask[g, br * A_HPG + h] = 1.0
    gidx, gmask = gidx.reshape(-1), gmask.reshape(-1)
    gw_t = gw[:, gidx] * gmask[None, :]
    gb_t = gb[gidx] * gmask

    wt = jnp.concatenate([wsl(0), kv_w(3), kv_w(5), gw_t, wsl(3)], axis=1).T.astype(BF16)
    bt = jnp.concatenate([bsl(0), kv_b(3), kv_b(5), gb_t, bsl(3)])[:, None]
    wn = jnp.concatenate([kv_w(0), kv_w(1), kv_w(2), kv_w(4), wsl(7), wsl(4), wsl(6), wsl(5)],
                         axis=1).astype(BF16)
    bn = jnp.concatenate([kv_b(0), kv_b(1), kv_b(2), kv_b(4), bsl(7), bsl(4), bsl(6), bsl(5)])[None, :]
    cost, sint, cos, sina, sinb = _rope_tables(S)

    (qr, qn, vts, vtw, gt, szat, kcmp, vcmp, ks, kw, qb, fb, ib, szb) = _project(
        x, wn, bn, wt, bt, cost, sint, cos, sina, sinb)

    half = CMP_STRIDE * HEAD_DIM

    def w1_planes(w1):
        both = jnp.concatenate([w1[:half], w1[half:]], axis=1).reshape(CMP_STRIDE, HEAD_DIM, 2 * CMP_HIDDEN)
        z = jnp.zeros_like(both)
        planes = jnp.concatenate([jnp.concatenate([both, z], axis=2),
                                  jnp.concatenate([z, both], axis=2)], axis=1)
        return planes.reshape(CMP_STRIDE * KV_WIDTH, 2 * A_GROUPS * CMP_HIDDEN).astype(BF16)

    kc, vct = _compress(kcmp, vcmp,
                        w1_planes(w_k1), w_k1, pe_k.reshape(-1, 1), w_k2.astype(BF16),
                        w1_planes(w_v1), w_v1, pe_v.reshape(-1, 1), w_v2.T.astype(BF16))

    oat = _nsa(qr, qn, gt, szat, kc, vct, ks, kw, vts, vtw, _overlap_t(S))
    wgm = jnp.concatenate([wsl(8), wsl(9)], axis=1).astype(BF16)
    bgm = jnp.concatenate([bsl(8), bsl(9)])[None, :]
    return _merge_out(x, oat, qb, fb, ib, szb, lb_logits, norm_g[None, :], _block_tri(), _block_diag(),
                      wgm, bgm, w_a.astype(BF16), w_b.astype(BF16), w_o.astype(BF16),
                      ln_g[None, :], ln_b[None, :], alpha, l)


@jax.jit
def kernel(x, w_in, b_in, pe_cmp_k, w_cmp_k1, w_cmp_k2, pe_cmp_v, w_cmp_v1, w_cmp_v2,
           hgrn_lb_logits, hgrn_norm_g, w_branch_a, w_branch_b, w_out, ln_g, ln_b):
    B, S, D = x.shape
    assert D == D_MODEL and S % KV_CHUNK == 0 and S % PROJ_ROWS == 0 and S >= WIN_KEYS
    assert S % OUT_ROWS == 0 and OUT_ROWS % HG_ROWS == 0 and (S // KV_CHUNK) * KV_CHUNK == S
    for l in range(DEPTH):
        x = _layer(x, l, w_in[l], b_in[l], pe_cmp_k[l], w_cmp_k1[l], w_cmp_k2[l],
                   pe_cmp_v[l], w_cmp_v1[l], w_cmp_v2[l], hgrn_lb_logits, hgrn_norm_g[l],
                   w_branch_a[l], w_branch_b[l], w_out[l], ln_g[l], ln_b[l])
    return x
```

```python
import functools
import math

import numpy as np
import jax
import jax.numpy as jnp
from jax import lax
from jax.experimental import pallas as pl
from jax.experimental.pallas import tpu as pltpu

D_MODEL = 1024
DEPTH = 1
A_HEADS = 8
A_GROUPS = 2
A_HPG = A_HEADS // A_GROUPS
HEAD_DIM = 64
A_WIDTH = A_HEADS * HEAD_DIM
KV_WIDTH = A_GROUPS * HEAD_DIM
CMP_BLOCK = 32
CMP_STRIDE = 16
CMP_HIDDEN = 128
SLC_BLOCK = 64
SLC_TOP_N = 16
WINDOW = 512
ROPE_THETA = 10000.0
FORCE_SCORE = 1e30
B_HEADS = 4
B_KDIM = 128
B_VDIM = 128
B_FWIDTH = B_HEADS * B_KDIM
B_WIDTH = B_HEADS * B_VDIM
CHUNK = 64
NORM_EPS = 1e-5

IN_SPLITS = (A_WIDTH, 6 * KV_WIDTH, A_HEADS * 3, A_WIDTH, B_FWIDTH, B_FWIDTH,
             B_WIDTH, B_WIDTH, D_MODEL, D_MODEL)
_OFF = tuple(int(v) for v in np.cumsum((0,) + IN_SPLITS))

V7X_VMEM_LIMIT_BYTES = 56 * 1024 * 1024
PROJ_ROWS = 512
Q_TILE = 128
KV_CHUNK = 512
IN_FLIGHT = 4
AHEAD = 2
DIAG_CHUNKS_PER_TRIP = 4
MAX_JOBS_PER_TRIP = 56
WIN_KEYS = WINDOW + Q_TILE
HG_ROWS = 256
OUT_ROWS = 512
GATE_COLS = 256
YA_COLS = 512
TAIL_SPLIT = 2
MASKED = -1e30
LOG2E = math.log2(math.e)
GATE_ROWS = 16
V_ROWS = HEAD_DIM + 16

F32 = jnp.float32
BF16 = jnp.bfloat16


def _dot(a, b):
    return jnp.dot(a, b, preferred_element_type=F32)


def _dot_nt(a, b):
    return lax.dot_general(a, b, (((1,), (1,)), ((), ())), preferred_element_type=F32)


def _sigmoid(x):
    return 1.0 / (1.0 + jnp.exp(-x))


def _silu(x):
    return x * _sigmoid(x)


def _proj_kernel(x_ref, wn_ref, bn_ref, wt_ref, bt_ref, cost_ref, sint_ref,
                 cos_ref, sina_ref, sinb_ref,
                 qr_ref, qn_ref, vts_ref, vtw_ref, gt_ref, szat_ref, kc_ref, vc_ref,
                 ks_ref, kw_ref, qb_ref, fb_ref, ib_ref, szb_ref, cmp_s):
    xb = x_ref[0].astype(BF16)
    scale = HEAD_DIM ** -0.5 * LOG2E

    ht = _dot_nt(wt_ref[...], xb) + bt_ref[...]
    cost = cost_ref[...]
    sint = sint_ref[...]
    half = HEAD_DIM // 2
    for h in range(A_HEADS):
        blk = ht[h * HEAD_DIM:(h + 1) * HEAD_DIM]
        rot = jnp.concatenate([-blk[half:], blk[:half]], axis=0)
        qr_ref[0, h * HEAD_DIM:(h + 1) * HEAD_DIM, :] = ((blk * cost + rot * sint) * scale).astype(BF16)
        qn_ref[0, h * HEAD_DIM:(h + 1) * HEAD_DIM, :] = (blk * scale).astype(BF16)
    o = A_WIDTH
    ones_rows = jnp.where(lax.broadcasted_iota(jnp.int32, (V_ROWS - HEAD_DIM, ht.shape[1]), 0) == 0, 1.0, 0.0)
    for g in range(A_GROUPS):
        for ref, base in ((vts_ref, o), (vtw_ref, o + KV_WIDTH)):
            rows_g = ht[base + g * HEAD_DIM:base + (g + 1) * HEAD_DIM]
            ref[0, g] = jnp.concatenate([rows_g, ones_rows], axis=0).astype(BF16)
    gt_ref[0] = _sigmoid(ht[o + 2 * KV_WIDTH:o + 2 * KV_WIDTH + 2 * GATE_ROWS])
    o += 2 * KV_WIDTH + 2 * GATE_ROWS
    szat_ref[0] = _silu(ht[o:o + A_WIDTH]).astype(BF16)

    def cols(lo, hi):
        return _dot(xb, wn_ref[:, lo:hi]) + bn_ref[:, lo:hi]

    kv = cols(0, 4 * KV_WIDTH)
    cmp_s[0] = kv[:, 0:KV_WIDTH]
    cmp_s[1] = kv[:, KV_WIDTH:2 * KV_WIDTH]
    pieces = cmp_s.shape[1] // CMP_STRIDE
    for t in range(CMP_STRIDE):
        lanes_t = slice(t * KV_WIDTH, (t + 1) * KV_WIDTH)
        kc_ref[0, :, lanes_t] = cmp_s[0, pl.ds(t, pieces, stride=CMP_STRIDE), :].astype(BF16)
        vc_ref[0, :, lanes_t] = cmp_s[1, pl.ds(t, pieces, stride=CMP_STRIDE), :].astype(BF16)
    cos = cos_ref[...]
    sina = sina_ref[...]
    sinb = sinb_ref[...]

    def rope_rows(k):
        return (k * cos + pltpu.roll(k, 128 - half, axis=1) * sina + pltpu.roll(k, half, axis=1) * sinb)

    ks = rope_rows(kv[:, 2 * KV_WIDTH:3 * KV_WIDTH])
    rows = ks.shape[0]
    lane = lax.broadcasted_iota(jnp.int32, (rows, KV_WIDTH), 1)
    pos = pl.program_id(1) * rows + lax.broadcasted_iota(jnp.int32, (rows, KV_WIDTH), 0)
    blocks_per_chunk = KV_CHUNK // SLC_BLOCK
    ind = jnp.where(lane == HEAD_DIM + (pos // SLC_BLOCK) % blocks_per_chunk, 1.0, 0.0)
    ks_ref[0, 0] = (jnp.where(lane < HEAD_DIM, ks, 0.0) + ind).astype(BF16)
    ks_ref[0, 1] = (jnp.where(lane < HEAD_DIM, pltpu.roll(ks, HEAD_DIM, axis=1), 0.0) + ind).astype(BF16)
    kw = rope_rows(kv[:, 3 * KV_WIDTH:4 * KV_WIDTH])
    kw_ref[0, 0] = jnp.where(lane < HEAD_DIM, kw, 0.0).astype(BF16)
    kw_ref[0, 1] = jnp.where(lane < HEAD_DIM, pltpu.roll(kw, HEAD_DIM, axis=1), 0.0).astype(BF16)
    o = 4 * KV_WIDTH
    szb_ref[0] = _silu(cols(o, o + B_WIDTH)).astype(BF16)
    o += B_WIDTH
    qb_ref[0] = cols(o, o + B_FWIDTH).astype(BF16)
    o += B_FWIDTH
    ib_ref[0] = cols(o, o + B_WIDTH).astype(BF16)
    o += B_WIDTH
    fb_ref[0] = cols(o, o + B_FWIDTH)


def _project(x, wn, bn, wt, bt, cost, sint, cos, sina, sinb):
    B, S, D = x.shape
    T = PROJ_ROWS
    n_t = wt.shape[0]
    n_n = wn.shape[1]
    full = lambda shape: pl.BlockSpec(shape, lambda b, s: (0,) * len(shape))
    row_out = lambda w: pl.BlockSpec((1, T, w), lambda b, s: (b, s, 0))
    col_out = lambda r: pl.BlockSpec((1, r, T), lambda b, s: (b, 0, s))
    sds = jax.ShapeDtypeStruct
    out_shape = (
        sds((B, A_WIDTH, S), BF16), sds((B, A_WIDTH, S), BF16),
        sds((B, A_GROUPS, V_ROWS, S), BF16), sds((B, A_GROUPS, V_ROWS, S), BF16),
        sds((B, 2 * GATE_ROWS, S), F32),
        sds((B, A_WIDTH, S), BF16),
        sds((B, S // CMP_STRIDE, CMP_STRIDE * KV_WIDTH), BF16),
        sds((B, S // CMP_STRIDE, CMP_STRIDE * KV_WIDTH), BF16),
        sds((B, A_GROUPS, S, KV_WIDTH), BF16), sds((B, A_GROUPS, S, KV_WIDTH), BF16),
        sds((B, S, B_FWIDTH), BF16), sds((B, S, B_FWIDTH), F32),
        sds((B, S, B_WIDTH), BF16), sds((B, S, B_WIDTH), BF16),
    )
    out_specs = (
        col_out(A_WIDTH), col_out(A_WIDTH),
        pl.BlockSpec((1, A_GROUPS, V_ROWS, T), lambda b, s: (b, 0, 0, s)),
        pl.BlockSpec((1, A_GROUPS, V_ROWS, T), lambda b, s: (b, 0, 0, s)),
        col_out(2 * GATE_ROWS), col_out(A_WIDTH),
        pl.BlockSpec((1, T // CMP_STRIDE, CMP_STRIDE * KV_WIDTH), lambda b, s: (b, s, 0)),
        pl.BlockSpec((1, T // CMP_STRIDE, CMP_STRIDE * KV_WIDTH), lambda b, s: (b, s, 0)),
        pl.BlockSpec((1, A_GROUPS, T, KV_WIDTH), lambda b, s: (b, 0, s, 0)),
        pl.BlockSpec((1, A_GROUPS, T, KV_WIDTH), lambda b, s: (b, 0, s, 0)),
        row_out(B_FWIDTH), row_out(B_FWIDTH), row_out(B_WIDTH), row_out(B_WIDTH),
    )
    in_specs = [
        pl.BlockSpec((1, T, D), lambda b, s: (b, s, 0)),
        full((D, n_n)), full((1, n_n)), full((n_t, D)), full((n_t, 1)),
        pl.BlockSpec((HEAD_DIM, T), lambda b, s: (0, s)),
        pl.BlockSpec((HEAD_DIM, T), lambda b, s: (0, s)),
        pl.BlockSpec((T, KV_WIDTH), lambda b, s: (s, 0)),
        pl.BlockSpec((T, KV_WIDTH), lambda b, s: (s, 0)),
        pl.BlockSpec((T, KV_WIDTH), lambda b, s: (s, 0)),
    ]
    return pl.pallas_call(
        _proj_kernel, out_shape=out_shape, grid=(B, S // T),
        in_specs=in_specs, out_specs=out_specs,
        scratch_shapes=[pltpu.VMEM((2, T, KV_WIDTH), F32)],
        compiler_params=pltpu.CompilerParams(
            dimension_semantics=("parallel", "parallel"),
            vmem_limit_bytes=V7X_VMEM_LIMIT_BYTES),
        name="in_proj",
    )(x, wn, bn, wt, bt, cost, sint, cos, sina, sinb)


def _compress_kernel(ck_ref, cv_ref, w1k_ref, w1kf_ref, pek_ref, w2k_ref,
                     w1v_ref, w1vf_ref, pev_ref, w2vt_ref, kc_ref, vct_ref):
    nc = ck_ref.shape[1]
    hid = CMP_HIDDEN

    def hidden(c_ref, w1_ref, w1f_ref, pe_ref):
        a = _dot(c_ref[0], w1_ref[...])
        pe_term = jnp.sum(w1f_ref[...] * pe_ref[...], axis=0, keepdims=True)
        out = []
        for g in range(A_GROUPS):
            lo = a[:, 2 * g * hid:(2 * g + 1) * hid]
            hi = pltpu.roll(a[:, (2 * g + 1) * hid:(2 * g + 2) * hid], nc - 1, axis=0)
            out.append(_silu(lo + hi + pe_term).astype(BF16))
        return out

    hk = hidden(ck_ref, w1k_ref, w1kf_ref, pek_ref)
    hv = hidden(cv_ref, w1v_ref, w1vf_ref, pev_ref)
    for g in range(A_GROUPS):
        kc_ref[0, g] = _dot(hk[g], w2k_ref[...]).astype(BF16)
        vct_ref[0, g] = _dot_nt(w2vt_ref[...], hv[g]).astype(BF16)


def _compress(ck, cv, w1k, w1kf, pek, w2k, w1v, w1vf, pev, w2vt):
    B, NC, W = ck.shape
    full = lambda a: pl.BlockSpec(a.shape, lambda b: (0,) * a.ndim)
    blk = pl.BlockSpec((1, NC, W), lambda b: (b, 0, 0))
    return pl.pallas_call(
        _compress_kernel,
        out_shape=(jax.ShapeDtypeStruct((B, A_GROUPS, NC, HEAD_DIM), BF16),
                   jax.ShapeDtypeStruct((B, A_GROUPS, HEAD_DIM, NC), BF16)),
        grid=(B,),
        in_specs=[blk, blk, full(w1k), full(w1kf), full(pek), full(w2k),
                  full(w1v), full(w1vf), full(pev), full(w2vt)],
        out_specs=(pl.BlockSpec((1, A_GROUPS, NC, HEAD_DIM), lambda b: (b, 0, 0, 0)),
                   pl.BlockSpec((1, A_GROUPS, HEAD_DIM, NC), lambda b: (b, 0, 0, 0))),
        compiler_params=pltpu.CompilerParams(
            dimension_semantics=("parallel",),
            vmem_limit_bytes=V7X_VMEM_LIMIT_BYTES),
        name="kv_compress",
    )(ck, cv, w1k, w1kf, pek, w2k, w1v, w1vf, pev, w2vt)


def _nsa_kernel(jc_ref, jq_ref, qr_ref, qn_ref, gt_ref, szat_ref, kc_ref, vct_ref, ks_ref, kw_ref,
                vts_ref, vtw_ref, ovl_ref, o_ref,
                bias_s, wq_s, m_s, acc_s, ocw_s, sbuf_s, smax_s, dbuf_s, dmax_s, scb_s, swb_s, wmax_s,
                *, seq, n_sel, n_jobs, jobs_per_trip):
    nc = kc_ref.shape[2]
    n_q = seq // Q_TILE
    n_ck = seq // KV_CHUNK
    lanes = A_HPG * Q_TILE
    bpc = KV_CHUNK // SLC_BLOCK
    tpc = KV_CHUNK // Q_TILE
    win_tiles = WINDOW // Q_TILE

    def stack_heads(ref, q0):
        return jnp.concatenate(
            [ref[0, h * HEAD_DIM:(h + 1) * HEAD_DIM, pl.ds(q0, Q_TILE)] for h in range(A_HPG)], axis=1)

    def per_head(a):
        return jnp.concatenate([a] * A_HPG, axis=1)

    def gate_row(gt, branch):
        return jnp.concatenate(
            [gt[branch * A_HPG + h:branch * A_HPG + h + 1, :] for h in range(A_HPG)], axis=1)

    def tile_start(qi):
        return pl.multiple_of(qi * Q_TILE, Q_TILE)

    lane_q = lax.broadcasted_iota(jnp.int32, (1, lanes), 1) & (Q_TILE - 1)
    r_sq = lax.broadcasted_iota(jnp.int32, (Q_TILE, lanes), 0)
    causal_sq = r_sq <= lane_q
    lower_sq = r_sq > lane_q
    q_pad = jnp.zeros((KV_WIDTH - HEAD_DIM, lanes), BF16)

    def cmp_keys(rb):
        return min(nc, -(-(rb * KV_CHUNK // CMP_STRIDE) // 128) * 128)

    def cmp_rows(qi):
        return min(nc, -(-((qi + 1) * Q_TILE // CMP_STRIDE) // 16) * 16)

    def cmp_scores(qi):
        return _dot(kc_ref[0, 0, 0:cmp_rows(qi), :], stack_heads(qn_ref, tile_start(qi)))

    def zero_rows(a, rows):
        if a.shape[0] == rows:
            return a
        return jnp.concatenate([a, jnp.zeros((rows - a.shape[0], a.shape[1]), a.dtype)], axis=0)

    def select_tile(qi, rb, sc):
        nk = cmp_keys(rb)
        n_causal = (qi + 1) * Q_TILE // SLC_BLOCK
        s0 = tile_start(qi)
        t_row = s0 + lax.broadcasted_iota(jnp.int32, (1, Q_TILE), 1)
        wq_s[qi] = stack_heads(qr_ref, s0)

        nv = cmp_rows(qi)
        lo = max(0, (qi * Q_TILE - (CMP_BLOCK - 1)) // CMP_STRIDE // 8 * 8)
        n_end = (lo + lax.broadcasted_iota(jnp.int32, (nv - lo, Q_TILE), 0)) * CMP_STRIDE + (CMP_BLOCK - 1)
        edge = sc[lo:] + per_head(jnp.where(n_end <= t_row, 0.0, MASKED))
        sc = jnp.concatenate([sc[:lo], edge], axis=0) if lo else edge
        mc = jnp.max(sc, axis=0, keepdims=True)
        pc = jnp.exp2(sc - mc)
        lc = jnp.sum(pc, axis=0, keepdims=True)
        pc = pc * jnp.where(s0 + lane_q >= CMP_BLOCK - 1, 1.0 / lc, 0.0)
        o_cmp = _dot(vct_ref[0, 0, :, 0:nk], zero_rows(pc.astype(BF16), nk))

        ps = pc[:, 0:Q_TILE]
        for h in range(1, A_HPG):
            ps = ps + pc[:, h * Q_TILE:(h + 1) * Q_TILE]
        p_hi = ps.astype(BF16)
        p_lo = (ps - p_hi.astype(F32)).astype(BF16)
        imp2 = _dot(ovl_ref[0:8 * rb, 0:nk],
                    zero_rows(jnp.concatenate([p_hi, p_lo], axis=1), nk))
        imp = imp2[:, :Q_TILE] + imp2[:, Q_TILE:]

        j_idx = lax.broadcasted_iota(jnp.int32, (8 * rb, Q_TILE), 0)
        cur = t_row // SLC_BLOCK
        sel = j_idx <= cur
        if 8 * rb > n_sel:
            forced = (j_idx == 0) | (j_idx == cur) | (j_idx == cur - 1)
            val = jnp.where(j_idx > cur, -1.0, jnp.where(forced, FORCE_SCORE, imp))
            rows = [val[8 * j:8 * j + 8] for j in range(rb)]
            cnt = [jnp.zeros((8, Q_TILE), F32) for _ in range(rb)]
            jl = lax.broadcasted_iota(jnp.int32, (8, Q_TILE), 0)
            for kb in range(rb):
                for kl in range(8):
                    if 8 * kb + kl >= n_causal:
                        continue
                    row = rows[kb][kl:kl + 1, :]
                    for jb in range(rb):
                        if jb < kb:
                            beats = row > rows[jb]
                        elif jb > kb:
                            beats = row >= rows[jb]
                        else:
                            beats = (row > rows[jb]) | ((row == rows[jb]) & (jl > kl))
                        cnt[jb] = cnt[jb] + jnp.where(beats, 1.0, 0.0)
            sel = sel & (jnp.concatenate(cnt, axis=0) < n_sel)
        bias = per_head(jnp.where(sel, 0.0, MASKED))
        pad = jnp.zeros((bpc, lanes), F32)
        for c in range(rb):
            bias_s[qi, c] = jnp.concatenate([bias[c * bpc:(c + 1) * bpc], pad], axis=0).astype(BF16)

        gt = gt_ref[0, :, pl.ds(s0, Q_TILE)]
        ocw_s[qi] = gate_row(gt, 0) * o_cmp
        m_s[qi] = jnp.full((8, lanes), MASKED, F32)
        acc_s[qi] = jnp.zeros((V_ROWS, lanes), F32)

    def win_finish(qi, sw, mw, w0, keys=WIN_KEYS):
        pw = jnp.exp2(sw - mw)
        ow = _dot(vtw_ref[0, 0, :, pl.ds(w0, keys)], pw.astype(BF16))
        o_win = ow[0:HEAD_DIM] * (1.0 / ow[HEAD_DIM:HEAD_DIM + 1])
        gt = gt_ref[0, :, pl.ds(tile_start(qi), Q_TILE)]
        ocw_s[qi] = ocw_s[qi] + gate_row(gt, 2) * o_win

    def win_scores(qi):
        slot = qi % IN_FLIGHT
        keys = min(qi + 1, win_tiles + 1) * Q_TILE
        w0 = (qi + 1) * Q_TILE - keys
        wq = jnp.concatenate([stack_heads(qr_ref, qi * Q_TILE), q_pad], axis=0)
        sw = _dot(kw_ref[0, 0, w0:w0 + keys, :], wq)
        parts = [jnp.where(causal_sq, sw[keys - Q_TILE:], MASKED)]
        if keys > Q_TILE:
            oldest = sw[:Q_TILE]
            if qi >= win_tiles:
                oldest = jnp.where(lower_sq, oldest, MASKED)
            parts = [oldest, sw[Q_TILE:keys - Q_TILE]] + parts if keys > 2 * Q_TILE else [oldest] + parts
        sw = jnp.concatenate(parts, axis=0)
        swb_s[slot, 0:keys, :] = sw
        wmax_s[slot] = jnp.broadcast_to(jnp.max(sw, axis=0, keepdims=True), (8, lanes))

    def win_tile(qi):
        slot = qi % IN_FLIGHT
        keys = min(qi + 1, win_tiles + 1) * Q_TILE
        win_finish(qi, swb_s[slot, 0:keys, :], wmax_s[slot][0:1], (qi + 1) * Q_TILE - keys, keys)

    scb_s[0, 0:cmp_rows(0), :] = cmp_scores(0)
    for u in range(min(AHEAD, n_q)):
        win_scores(u)
    for qi in range(n_q):
        rb = qi // tpc + 1
        if qi + 1 < n_q:
            scb_s[(qi + 1) & 1, 0:cmp_rows(qi + 1), :] = cmp_scores(qi + 1)
        if qi + AHEAD < n_q:
            win_scores(qi + AHEAD)
        select_tile(qi, rb, scb_s[qi & 1, 0:cmp_rows(qi), :])
        win_tile(qi)

    wq_pad = jnp.zeros((KV_WIDTH - HEAD_DIM - 2 * bpc, lanes), BF16)

    def scores(c, qi, diag):
        rows = KV_CHUNK if diag is None else (diag + 1) * Q_TILE
        k0 = pl.multiple_of(c * KV_CHUNK, KV_CHUNK)
        wq = jnp.concatenate([wq_s[qi], bias_s[qi, c], wq_pad], axis=0)
        s = _dot(ks_ref[0, 0, pl.ds(k0, rows), :], wq)
        if diag is not None:
            head = [s[:rows - Q_TILE]] if diag > 0 else []
            s = jnp.concatenate(head + [jnp.where(causal_sq, s[rows - Q_TILE:], MASKED)], axis=0)
        return s, jnp.broadcast_to(jnp.max(s, axis=0, keepdims=True), (8, lanes))

    def absorb(c, qi, s, s_max):
        rows = s.shape[0]
        k0 = pl.multiple_of(c * KV_CHUNK, KV_CHUNK)
        m_old = m_s[qi][0:1]
        m_new = jnp.maximum(m_old, s_max[0:1])
        alpha = jnp.exp2(m_old - m_new)
        p = jnp.exp2(s - m_new)
        m_s[qi] = jnp.broadcast_to(m_new, (8, lanes))
        acc_s[qi] = alpha * acc_s[qi] + _dot(vts_ref[0, 0, :, pl.ds(k0, rows)], p.astype(BF16))

    def full_scores(j, slot):
        sbuf_s[slot], smax_s[slot] = scores(jc_ref[j], jq_ref[j], None)

    def full_group(i, _):
        j0 = jobs_per_trip * i
        for u in range(jobs_per_trip):
            full_scores(j0 + u + AHEAD, (u + AHEAD) % IN_FLIGHT)
            absorb(jc_ref[j0 + u], jq_ref[j0 + u], sbuf_s[u % IN_FLIGHT], smax_s[u % IN_FLIGHT])
        return 0

    if n_jobs:
        for u in range(AHEAD):
            full_scores(u, u)
        lax.fori_loop(0, n_jobs // jobs_per_trip, full_group, 0)

    def diag_scores(c, r):
        dbuf_s[r, 0:(r + 1) * Q_TILE, :], dmax_s[r] = scores(c, c * tpc + r, r)

    def diag_chunks(i, _):
        for k in range(DIAG_CHUNKS_PER_TRIP):
            c = DIAG_CHUNKS_PER_TRIP * i + k
            nxt = jnp.minimum(c + 1, n_ck - 1)
            for r in range(tpc):
                ahead = r + AHEAD
                diag_scores(c if ahead < tpc else nxt, ahead % tpc)
                absorb(c, c * tpc + r, dbuf_s[r, 0:(r + 1) * Q_TILE, :], dmax_s[r])
        return 0

    for r in range(AHEAD):
        diag_scores(0, r)
    lax.fori_loop(0, n_ck // DIAG_CHUNKS_PER_TRIP, diag_chunks, 0)

    def finish(qi, _):
        s0 = tile_start(qi)
        gt = gt_ref[0, :, pl.ds(s0, Q_TILE)]
        acc = acc_s[qi]
        ot = ocw_s[qi] + gate_row(gt, 1) * acc[0:HEAD_DIM] * (1.0 / acc[HEAD_DIM:HEAD_DIM + 1])
        ot = (ot * stack_heads(szat_ref, s0).astype(F32)).astype(o_ref.dtype)
        for h in range(A_HPG):
            o_ref[0, h * HEAD_DIM:(h + 1) * HEAD_DIM, pl.ds(s0, Q_TILE)] = ot[:, h * Q_TILE:(h + 1) * Q_TILE]
        return 0

    lax.fori_loop(0, n_q, finish, 0)


def _nsa(qr, qn, gt, szat, kc, vct, ks, kw, vts, vtw, ovl):
    B, _, S = qr.shape
    NC = kc.shape[2]
    n_slc = S // SLC_BLOCK
    n_sel = min(SLC_TOP_N, n_slc)
    n_q = S // Q_TILE
    n_ck = S // KV_CHUNK
    tpc = KV_CHUNK // Q_TILE
    gw = A_HPG * HEAD_DIM
    lanes = A_HPG * Q_TILE
    jobs = [(c, q) for c in range(n_ck) for q in range((c + 1) * tpc, n_q)]
    n_jobs = len(jobs)
    jobs_per_trip = max(j for j in range(IN_FLIGHT, MAX_JOBS_PER_TRIP + 1, IN_FLIGHT) if n_jobs % j == 0)
    assert tpc == IN_FLIGHT and AHEAD < IN_FLIGHT and n_ck % DIAG_CHUNKS_PER_TRIP == 0
    jobs = jobs + [jobs[-1] if jobs else (0, 0)] * AHEAD
    jc = jnp.asarray([j[0] for j in jobs], jnp.int32)
    jq = jnp.asarray([j[1] for j in jobs], jnp.int32)
    kernel = functools.partial(_nsa_kernel, seq=S, n_sel=n_sel, n_jobs=n_jobs, jobs_per_trip=jobs_per_trip)
    per_group = lambda rows, cols: pl.BlockSpec((1, 1, rows, cols), lambda b, g, *_: (b, g, 0, 0))
    in_specs = [
        pl.BlockSpec((1, gw, S), lambda b, g, *_: (b, g, 0)),
        pl.BlockSpec((1, gw, S), lambda b, g, *_: (b, g, 0)),
        pl.BlockSpec((1, GATE_ROWS, S), lambda b, g, *_: (b, g, 0)),
        pl.BlockSpec((1, gw, S), lambda b, g, *_: (b, g, 0)),
        per_group(NC, HEAD_DIM), per_group(HEAD_DIM, NC),
        per_group(S, KV_WIDTH), per_group(S, KV_WIDTH),
        per_group(V_ROWS, S), per_group(V_ROWS, S),
        pl.BlockSpec((n_slc, NC), lambda b, g, *_: (0, 0)),
    ]
    scratch = [
        pltpu.VMEM((n_q, n_ck, 2 * (KV_CHUNK // SLC_BLOCK), lanes), BF16),
        pltpu.VMEM((n_q, HEAD_DIM, lanes), BF16),
        pltpu.VMEM((n_q, 8, lanes), F32),
        pltpu.VMEM((n_q, V_ROWS, lanes), F32),
        pltpu.VMEM((n_q, HEAD_DIM, lanes), F32),
        pltpu.VMEM((IN_FLIGHT, KV_CHUNK, lanes), F32),
        pltpu.VMEM((IN_FLIGHT, 8, lanes), F32),
        pltpu.VMEM((tpc, KV_CHUNK, lanes), F32),
        pltpu.VMEM((tpc, 8, lanes), F32),
        pltpu.VMEM((2, NC, lanes), F32),
        pltpu.VMEM((IN_FLIGHT, WIN_KEYS, lanes), F32),
        pltpu.VMEM((IN_FLIGHT, 8, lanes), F32),
    ]
    return pl.pallas_call(
        kernel,
        out_shape=jax.ShapeDtypeStruct((B, A_WIDTH, S), BF16),
        grid_spec=pltpu.PrefetchScalarGridSpec(
            num_scalar_prefetch=2,
            grid=(B, A_GROUPS),
            in_specs=in_specs,
            out_specs=pl.BlockSpec((1, gw, S), lambda b, g, *_: (b, g, 0)),
            scratch_shapes=scratch),
        compiler_params=pltpu.CompilerParams(
            dimension_semantics=("parallel", "parallel"),
            vmem_limit_bytes=V7X_VMEM_LIMIT_BYTES),
        name="nsa_attention",
    )(jc, jq, qr, qn, gt, szat, kc, vct, ks, kw, vts, vtw, ovl)


def _hgrn_stages(q_ref, f_ref, i_ref, sz_ref, lb, gain, tri, bd, causal, state_s, h, out):
    T = q_ref.shape[1]
    cpg = HG_ROWS // CHUNK
    lanes = slice(h * B_KDIM, (h + 1) * B_KDIM)
    groups = [slice(gi * HG_ROWS, (gi + 1) * HG_ROWS) for gi in range(T // HG_ROWS)]
    mid = {}

    def decay():
        mid["kk"], mid["b"] = [], []
        for rows in groups:
            fg = lb + (1.0 - lb) * _sigmoid(f_ref[0, rows, lanes])
            logf = jnp.log(fg)
            l_hi = logf.astype(BF16)
            l_lo = (logf - l_hi.astype(F32)).astype(BF16)
            bb = _dot(tri, jnp.concatenate([l_hi, l_lo], axis=1))
            mid["kk"].append(1.0 - fg)
            mid["b"].append(bb[:, :B_KDIM] + bb[:, B_KDIM:])

    def intra():
        mid["qe"], mid["oi"], mid["kv"], mid["dl"] = [], [], [], []
        for rows, kk, b in zip(groups, mid["kk"], mid["b"]):
            b_last = jnp.concatenate(
                [jnp.broadcast_to(b[(c + 1) * CHUNK - 1:(c + 1) * CHUNK, :], (CHUNK, B_KDIM))
                 for c in range(cpg)], axis=0)
            qe = (_silu(q_ref[0, rows, lanes].astype(F32)) * jnp.exp(b)).astype(BF16)
            ke = (kk * jnp.exp(-b)).astype(BF16)
            kd = (kk * jnp.exp(b_last - b)).astype(BF16)
            attn = jnp.where(causal, _dot_nt(qe, ke), 0.0).astype(BF16)
            v = i_ref[0, rows, lanes]
            mid["oi"].append(_dot(attn, v))
            mid["qe"].append(qe)
            kd_bd = jnp.concatenate([kd] * cpg, axis=1) * bd
            kv_all = lax.dot_general(v, kd_bd, (((0,), (0,)), ((), ())), preferred_element_type=F32)
            for c in range(cpg):
                mid["kv"].append(kv_all[:, c * B_KDIM:(c + 1) * B_KDIM])
                mid["dl"].append(jnp.exp(b_last[c * CHUNK:c * CHUNK + 1, :]))

    def inter():
        state = state_s[h]
        outs = []
        for n in range(T // CHUNK):
            gi, c = divmod(n, cpg)
            sl = slice(c * CHUNK, (c + 1) * CHUNK)
            rows = slice(n * CHUNK, (n + 1) * CHUNK)
            o = mid["oi"][gi][sl] + _dot_nt(mid["qe"][gi][sl], state.astype(BF16))
            o = o * lax.rsqrt(jnp.mean(o * o, axis=-1, keepdims=True) + NORM_EPS)
            outs.append((o * gain * sz_ref[0, rows, lanes].astype(F32)).astype(BF16))
            state = mid["dl"][n] * state + mid["kv"][n]
        state_s[h] = state
        out.append(jnp.concatenate(outs, axis=0))

    return [decay, intra, inter]


def _out_kernel(x_ref, oat_ref, q_ref, f_ref, i_ref, sz_ref, lbl_ref, g_ref, tri_ref, bd_ref,
                wgm_ref, bgm_ref, wa_ref, wb_ref, wo_ref, lng_ref, lnb_ref, o_ref, state_s, sg_s, ya_s,
                *, alpha, layer):
    @pl.when(pl.program_id(1) == 0)
    def _():
        state_s[...] = jnp.zeros_like(state_s)

    x = x_ref[0]
    xb = x.astype(BF16)

    def gate_chunk(j):
        def run():
            sg_s[:, j:j + GATE_COLS] = _sigmoid(_dot(xb, wgm_ref[:, j:j + GATE_COLS]) + bgm_ref[:, j:j + GATE_COLS])
        return run

    def ya_chunk(j):
        def run():
            ya_s[:, j:j + YA_COLS] = lax.dot_general(oat_ref[0], wa_ref[:, j:j + YA_COLS], (((0,), (0,)), ((), ())),
                                                     preferred_element_type=F32)
        return run

    fillers = [gate_chunk(j) for j in range(0, 2 * D_MODEL, GATE_COLS)] + \
              [ya_chunk(j) for j in range(0, D_MODEL, YA_COLS)]

    lg = lbl_ref[...]
    e = jnp.exp(lg - jnp.max(lg, axis=0, keepdims=True))
    lb_all = jnp.sum(e[0:layer + 1], axis=0, keepdims=True) / jnp.sum(e, axis=0, keepdims=True)
    tri = tri_ref[...]
    bd = bd_ref[...]
    ri = lax.broadcasted_iota(jnp.int32, (HG_ROWS, HG_ROWS), 0)
    ci = lax.broadcasted_iota(jnp.int32, (HG_ROWS, HG_ROWS), 1)
    causal = (ri // CHUNK == ci // CHUNK) & (ci <= ri)

    obs = []
    stages = []
    for h in range(B_HEADS):
        lanes = slice(h * B_KDIM, (h + 1) * B_KDIM)
        stages += _hgrn_stages(q_ref, f_ref, i_ref, sz_ref, lb_all[:, lanes], g_ref[:, lanes],
                               tri, bd, causal, state_s, h, obs)
    for k, stage in enumerate(stages):
        if k < len(fillers):
            fillers[k]()
        stage()
    for filler in fillers[len(stages):]:
        filler()
    ob = jnp.concatenate(obs, axis=1)

    sub = x.shape[0] // TAIL_SPLIT
    halves = [slice(i * sub, (i + 1) * sub) for i in range(TAIL_SPLIT)]
    ybs = [_dot(ob[r], wb_ref[...]) for r in halves]
    ys = [(sg_s[r, :D_MODEL] * ya_s[r, :] + sg_s[r, D_MODEL:] * yb).astype(BF16) for r, yb in zip(halves, ybs)]
    outs = [_dot(y, wo_ref[...]) for y in ys]
    for r, out in zip(halves, outs):
        res = alpha * x[r] + out
        mu = jnp.mean(res, axis=-1, keepdims=True)
        d = res - mu
        var = jnp.mean(d * d, axis=-1, keepdims=True)
        o_ref[0, r, :] = d * lax.rsqrt(var + NORM_EPS) * lng_ref[...] + lnb_ref[...]


def _merge_out(x, oat, qb, fb, ib, szb, lb_logits, gain, tri, bd, wgm, bgm, wa, wb, wo, lng, lnb, alpha, layer):
    B, S, D = x.shape
    T = OUT_ROWS
    full = lambda a: pl.BlockSpec(a.shape, lambda b, s: (0,) * a.ndim)
    rows = lambda w: pl.BlockSpec((1, T, w), lambda b, s: (b, s, 0))
    return pl.pallas_call(
        functools.partial(_out_kernel, alpha=alpha, layer=layer),
        out_shape=jax.ShapeDtypeStruct((B, S, D), x.dtype),
        grid=(B, S // T),
        in_specs=[rows(D), pl.BlockSpec((1, A_WIDTH, T), lambda b, s: (b, 0, s)),
                  rows(B_FWIDTH), rows(B_FWIDTH), rows(B_WIDTH), rows(B_WIDTH),
                  full(lb_logits), full(gain), full(tri), full(bd),
                  full(wgm), full(bgm), full(wa), full(wb), full(wo), full(lng), full(lnb)],
        out_specs=rows(D),
        scratch_shapes=[pltpu.VMEM((B_HEADS, B_VDIM, B_KDIM), F32),
                        pltpu.VMEM((T, 2 * D), F32),
                        pltpu.VMEM((T, D), F32)],
        compiler_params=pltpu.CompilerParams(
            dimension_semantics=("parallel", "arbitrary"),
            vmem_limit_bytes=V7X_VMEM_LIMIT_BYTES),
        name="hgrn_merge_out",
    )(x, oat, qb, fb, ib, szb, lb_logits, gain, tri, bd, wgm, bgm, wa, wb, wo, lng, lnb)


def _rope_tables(S):
    inv = ROPE_THETA ** (-np.arange(0, HEAD_DIM, 2, dtype=np.float64) / HEAD_DIM)
    ang = np.arange(S, dtype=np.float64)[:, None] * inv[None, :]
    cos = np.concatenate([np.cos(ang), np.cos(ang)], axis=-1)
    sin = np.concatenate([np.sin(ang), np.sin(ang)], axis=-1)
    first = (np.arange(HEAD_DIM) < HEAD_DIM // 2)[None, :]
    sina = np.where(first, -sin, 0.0)
    sinb = np.where(first, 0.0, sin)
    tile = lambda a: np.concatenate([a] * A_GROUPS, axis=-1)
    return tuple(jnp.asarray(np.ascontiguousarray(a), dtype=F32)
                 for a in (cos.T, sin.T, tile(cos), tile(sina), tile(sinb)))


def _overlap_t(S):
    n_cmp = S // CMP_STRIDE
    n_slc = S // SLC_BLOCK
    cs = np.arange(n_cmp)[None, :] * CMP_STRIDE
    ss = np.arange(n_slc)[:, None] * SLC_BLOCK
    ov = (cs < ss + SLC_BLOCK) & (cs + CMP_BLOCK > ss) & (np.arange(n_cmp)[None, :] < n_cmp - 1)
    return jnp.asarray(ov, dtype=BF16)


def _block_tri():
    r = np.arange(HG_ROWS)
    return jnp.asarray((r[:, None] // CHUNK == r[None, :] // CHUNK) & (r[None, :] <= r[:, None]), dtype=BF16)


def _block_diag():
    r = np.arange(HG_ROWS)[:, None] // CHUNK
    c = np.arange(HG_ROWS // CHUNK * B_KDIM)[None, :] // B_KDIM
    return jnp.asarray(r == c, dtype=BF16)


def _layer(x, l, w_in, b_in, pe_k, w_k1, w_k2, pe_v, w_v1, w_v2, lb_logits, norm_g,
           w_a, w_b, w_o, ln_g, ln_b):
    B, S, D = x.shape
    alpha = (2 * DEPTH) ** 0.25
    o = _OFF
    wsl = lambda i: w_in[:, o[i]:o[i + 1]]
    bsl = lambda i: b_in[o[i]:o[i + 1]]
    kvw, kvb = wsl(1), bsl(1)
    kv_w = lambda j: kvw[:, j * KV_WIDTH:(j + 1) * KV_WIDTH]
    kv_b = lambda j: kvb[j * KV_WIDTH:(j + 1) * KV_WIDTH]
    gw, gb = wsl(2), bsl(2)
    gidx = np.zeros((A_GROUPS, GATE_ROWS), np.int32)
    gmask = np.zeros((A_GROUPS, GATE_ROWS), np.float32)
    for g in range(A_GROUPS):
        for br in range(3):
            for h in range(A_HPG):
                gidx[g, br * A_HPG + h] = (g * A_HPG + h) * 3 + br
                gmask[g, br * A_HPG + h] = 1.0
    gidx, gmask = gidx.reshape(-1), gmask.reshape(-1)
    gw_t = gw[:, gidx] * gmask[None, :]
    gb_t = gb[gidx] * gmask

    wt = jnp.concatenate([wsl(0), kv_w(3), kv_w(5), gw_t, wsl(3)], axis=1).T.astype(BF16)
    bt = jnp.concatenate([bsl(0), kv_b(3), kv_b(5), gb_t, bsl(3)])[:, None]
    wn = jnp.concatenate([kv_w(0), kv_w(1), kv_w(2), kv_w(4), wsl(7), wsl(4), wsl(6), wsl(5)],
                         axis=1).astype(BF16)
    bn = jnp.concatenate([kv_b(0), kv_b(1), kv_b(2), kv_b(4), bsl(7), bsl(4), bsl(6), bsl(5)])[None, :]
    cost, sint, cos, sina, sinb = _rope_tables(S)

    (qr, qn, vts, vtw, gt, szat, kcmp, vcmp, ks, kw, qb, fb, ib, szb) = _project(
        x, wn, bn, wt, bt, cost, sint, cos, sina, sinb)

    half = CMP_STRIDE * HEAD_DIM

    def w1_planes(w1):
        both = jnp.concatenate([w1[:half], w1[half:]], axis=1).reshape(CMP_STRIDE, HEAD_DIM, 2 * CMP_HIDDEN)
        z = jnp.zeros_like(both)
        planes = jnp.concatenate([jnp.concatenate([both, z], axis=2),
                                  jnp.concatenate([z, both], axis=2)], axis=1)
        return planes.reshape(CMP_STRIDE * KV_WIDTH, 2 * A_GROUPS * CMP_HIDDEN).astype(BF16)

    kc, vct = _compress(kcmp, vcmp,
                        w1_planes(w_k1), w_k1, pe_k.reshape(-1, 1), w_k2.astype(BF16),
                        w1_planes(w_v1), w_v1, pe_v.reshape(-1, 1), w_v2.T.astype(BF16))

    oat = _nsa(qr, qn, gt, szat, kc, vct, ks, kw, vts, vtw, _overlap_t(S))
    wgm = jnp.concatenate([wsl(8), wsl(9)], axis=1).astype(BF16)
    bgm = jnp.concatenate([bsl(8), bsl(9)])[None, :]
    return _merge_out(x, oat, qb, fb, ib, szb, lb_logits, norm_g[None, :], _block_tri(), _block_diag(),
                      wgm, bgm, w_a.astype(BF16), w_b.astype(BF16), w_o.astype(BF16),
                      ln_g[None, :], ln_b[None, :], alpha, l)


@jax.jit
def kernel(x, w_in, b_in, pe_cmp_k, w_cmp_k1, w_cmp_k2, pe_cmp_v, w_cmp_v1, w_cmp_v2,
           hgrn_lb_logits, hgrn_norm_g, w_branch_a, w_branch_b, w_out, ln_g, ln_b):
    B, S, D = x.shape
    assert D == D_MODEL and S % KV_CHUNK == 0 and S % PROJ_ROWS == 0 and S >= WIN_KEYS
    assert S % OUT_ROWS == 0 and OUT_ROWS % HG_ROWS == 0 and (S // KV_CHUNK) * KV_CHUNK == S
    for l in range(DEPTH):
        x = _layer(x, l, w_in[l], b_in[l], pe_cmp_k[l], w_cmp_k1[l], w_cmp_k2[l],
                   pe_cmp_v[l], w_cmp_v1[l], w_cmp_v2[l], hgrn_lb_logits, hgrn_norm_g[l],
                   w_branch_a[l], w_branch_b[l], w_out[l], ln_g[l], ln_b[l])
    return x
```

```python
import functools
import math

import numpy as np
import jax
import jax.numpy as jnp
from jax import lax
from jax.experimental import pallas as pl
from jax.experimental.pallas import tpu as pltpu

D_MODEL = 1024
DEPTH = 1
A_HEADS = 8
A_GROUPS = 2
A_HPG = A_HEADS // A_GROUPS
HEAD_DIM = 64
A_WIDTH = A_HEADS * HEAD_DIM
KV_WIDTH = A_GROUPS * HEAD_DIM
CMP_BLOCK = 32
CMP_STRIDE = 16
CMP_HIDDEN = 128
SLC_BLOCK = 64
SLC_TOP_N = 16
WINDOW = 512
ROPE_THETA = 10000.0
FORCE_SCORE = 1e30
B_HEADS = 4
B_KDIM = 128
B_VDIM = 128
B_FWIDTH = B_HEADS * B_KDIM
B_WIDTH = B_HEADS * B_VDIM
CHUNK = 64
NORM_EPS = 1e-5

IN_SPLITS = (A_WIDTH, 6 * KV_WIDTH, A_HEADS * 3, A_WIDTH, B_FWIDTH, B_FWIDTH,
             B_WIDTH, B_WIDTH, D_MODEL, D_MODEL)
_OFF = tuple(int(v) for v in np.cumsum((0,) + IN_SPLITS))

V7X_VMEM_LIMIT_BYTES = 56 * 1024 * 1024
PROJ_ROWS = 512
Q_TILE = 128
KV_CHUNK = 512
IN_FLIGHT = 4
AHEAD = 2
DIAG_CHUNKS_PER_TRIP = 4
MAX_JOBS_PER_TRIP = 16
WIN_KEYS = WINDOW + Q_TILE
WIN_K_CHUNK = 256
HG_ROWS = 256
OUT_ROWS = 512
GATE_COLS = 256
YA_COLS = 512
TAIL_SPLIT = 2
MASKED = -1e30
LOG2E = math.log2(math.e)
GATE_ROWS = 16
V_ROWS = HEAD_DIM + 16

F32 = jnp.float32
BF16 = jnp.bfloat16


def _dot(a, b):
    return jnp.dot(a, b, preferred_element_type=F32)


def _dot_nt(a, b):
    return lax.dot_general(a, b, (((1,), (1,)), ((), ())), preferred_element_type=F32)


def _sigmoid(x):
    return 1.0 / (1.0 + jnp.exp(-x))


def _silu(x):
    return x * _sigmoid(x)


def _proj_kernel(x_ref, wn_ref, bn_ref, wt_ref, bt_ref, cost_ref, sint_ref,
                 cos_ref, sina_ref, sinb_ref,
                 qr_ref, qn_ref, vts_ref, vtw_ref, gt_ref, szat_ref, kc_ref, vc_ref,
                 ks_ref, kw_ref, qb_ref, fb_ref, ib_ref, szb_ref, cmp_s):
    xb = x_ref[0].astype(BF16)
    scale = HEAD_DIM ** -0.5 * LOG2E

    ht = _dot_nt(wt_ref[...], xb) + bt_ref[...]
    cost = cost_ref[...]
    sint = sint_ref[...]
    half = HEAD_DIM // 2
    for h in range(A_HEADS):
        blk = ht[h * HEAD_DIM:(h + 1) * HEAD_DIM]
        rot = jnp.concatenate([-blk[half:], blk[:half]], axis=0)
        qr_ref[0, h * HEAD_DIM:(h + 1) * HEAD_DIM, :] = ((blk * cost + rot * sint) * scale).astype(BF16)
        qn_ref[0, h * HEAD_DIM:(h + 1) * HEAD_DIM, :] = (blk * scale).astype(BF16)
    o = A_WIDTH
    ones_rows = jnp.where(lax.broadcasted_iota(jnp.int32, (V_ROWS - HEAD_DIM, ht.shape[1]), 0) == 0, 1.0, 0.0)
    for g in range(A_GROUPS):
        for ref, base in ((vts_ref, o), (vtw_ref, o + KV_WIDTH)):
            rows_g = ht[base + g * HEAD_DIM:base + (g + 1) * HEAD_DIM]
            ref[0, g] = jnp.concatenate([rows_g, ones_rows], axis=0).astype(BF16)
    gt_ref[0] = _sigmoid(ht[o + 2 * KV_WIDTH:o + 2 * KV_WIDTH + 2 * GATE_ROWS])
    o += 2 * KV_WIDTH + 2 * GATE_ROWS
    szat_ref[0] = _silu(ht[o:o + A_WIDTH]).astype(BF16)

    def cols(lo, hi):
        return _dot(xb, wn_ref[:, lo:hi]) + bn_ref[:, lo:hi]

    kv = cols(0, 4 * KV_WIDTH)
    cmp_s[0] = kv[:, 0:KV_WIDTH]
    cmp_s[1] = kv[:, KV_WIDTH:2 * KV_WIDTH]
    pieces = cmp_s.shape[1] // CMP_STRIDE
    for t in range(CMP_STRIDE):
        lanes_t = slice(t * KV_WIDTH, (t + 1) * KV_WIDTH)
        kc_ref[0, :, lanes_t] = cmp_s[0, pl.ds(t, pieces, stride=CMP_STRIDE), :].astype(BF16)
        vc_ref[0, :, lanes_t] = cmp_s[1, pl.ds(t, pieces, stride=CMP_STRIDE), :].astype(BF16)
    cos = cos_ref[...]
    sina = sina_ref[...]
    sinb = sinb_ref[...]

    def rope_rows(k):
        return (k * cos + pltpu.roll(k, 128 - half, axis=1) * sina + pltpu.roll(k, half, axis=1) * sinb)

    ks = rope_rows(kv[:, 2 * KV_WIDTH:3 * KV_WIDTH])
    rows = ks.shape[0]
    lane = lax.broadcasted_iota(jnp.int32, (rows, KV_WIDTH), 1)
    pos = pl.program_id(1) * rows + lax.broadcasted_iota(jnp.int32, (rows, KV_WIDTH), 0)
    blocks_per_chunk = KV_CHUNK // SLC_BLOCK
    ind = jnp.where(lane == HEAD_DIM + (pos // SLC_BLOCK) % blocks_per_chunk, 1.0, 0.0)
    ks_ref[0, 0] = (jnp.where(lane < HEAD_DIM, ks, 0.0) + ind).astype(BF16)
    ks_ref[0, 1] = (jnp.where(lane < HEAD_DIM, pltpu.roll(ks, HEAD_DIM, axis=1), 0.0) + ind).astype(BF16)
    kw = rope_rows(kv[:, 3 * KV_WIDTH:4 * KV_WIDTH])
    kw_ref[0, 0] = jnp.where(lane < HEAD_DIM, kw, 0.0).astype(BF16)
    kw_ref[0, 1] = jnp.where(lane < HEAD_DIM, pltpu.roll(kw, HEAD_DIM, axis=1), 0.0).astype(BF16)
    o = 4 * KV_WIDTH
    szb_ref[0] = _silu(cols(o, o + B_WIDTH)).astype(BF16)
    o += B_WIDTH
    qb_ref[0] = cols(o, o + B_FWIDTH).astype(BF16)
    o += B_FWIDTH
    ib_ref[0] = cols(o, o + B_WIDTH).astype(BF16)
    o += B_WIDTH
    fb_ref[0] = cols(o, o + B_FWIDTH)


def _project(x, wn, bn, wt, bt, cost, sint, cos, sina, sinb):
    B, S, D = x.shape
    T = PROJ_ROWS
    n_t = wt.shape[0]
    n_n = wn.shape[1]
    full = lambda shape: pl.BlockSpec(shape, lambda b, s: (0,) * len(shape))
    row_out = lambda w: pl.BlockSpec((1, T, w), lambda b, s: (b, s, 0))
    col_out = lambda r: pl.BlockSpec((1, r, T), lambda b, s: (b, 0, s))
    sds = jax.ShapeDtypeStruct
    out_shape = (
        sds((B, A_WIDTH, S), BF16), sds((B, A_WIDTH, S), BF16),
        sds((B, A_GROUPS, V_ROWS, S), BF16), sds((B, A_GROUPS, V_ROWS, S), BF16),
        sds((B, 2 * GATE_ROWS, S), F32),
        sds((B, A_WIDTH, S), BF16),
        sds((B, S // CMP_STRIDE, CMP_STRIDE * KV_WIDTH), BF16),
        sds((B, S // CMP_STRIDE, CMP_STRIDE * KV_WIDTH), BF16),
        sds((B, A_GROUPS, S, KV_WIDTH), BF16), sds((B, A_GROUPS, S, KV_WIDTH), BF16),
        sds((B, S, B_FWIDTH), BF16), sds((B, S, B_FWIDTH), F32),
        sds((B, S, B_WIDTH), BF16), sds((B, S, B_WIDTH), BF16),
    )
    out_specs = (
        col_out(A_WIDTH), col_out(A_WIDTH),
        pl.BlockSpec((1, A_GROUPS, V_ROWS, T), lambda b, s: (b, 0, 0, s)),
        pl.BlockSpec((1, A_GROUPS, V_ROWS, T), lambda b, s: (b, 0, 0, s)),
        col_out(2 * GATE_ROWS), col_out(A_WIDTH),
        pl.BlockSpec((1, T // CMP_STRIDE, CMP_STRIDE * KV_WIDTH), lambda b, s: (b, s, 0)),
        pl.BlockSpec((1, T // CMP_STRIDE, CMP_STRIDE * KV_WIDTH), lambda b, s: (b, s, 0)),
        pl.BlockSpec((1, A_GROUPS, T, KV_WIDTH), lambda b, s: (b, 0, s, 0)),
        pl.BlockSpec((1, A_GROUPS, T, KV_WIDTH), lambda b, s: (b, 0, s, 0)),
        row_out(B_FWIDTH), row_out(B_FWIDTH), row_out(B_WIDTH), row_out(B_WIDTH),
    )
    in_specs = [
        pl.BlockSpec((1, T, D), lambda b, s: (b, s, 0)),
        full((D, n_n)), full((1, n_n)), full((n_t, D)), full((n_t, 1)),
        pl.BlockSpec((HEAD_DIM, T), lambda b, s: (0, s)),
        pl.BlockSpec((HEAD_DIM, T), lambda b, s: (0, s)),
        pl.BlockSpec((T, KV_WIDTH), lambda b, s: (s, 0)),
        pl.BlockSpec((T, KV_WIDTH), lambda b, s: (s, 0)),
        pl.BlockSpec((T, KV_WIDTH), lambda b, s: (s, 0)),
    ]
    return pl.pallas_call(
        _proj_kernel, out_shape=out_shape, grid=(B, S // T),
        in_specs=in_specs, out_specs=out_specs,
        scratch_shapes=[pltpu.VMEM((2, T, KV_WIDTH), F32)],
        compiler_params=pltpu.CompilerParams(
            dimension_semantics=("parallel", "parallel"),
            vmem_limit_bytes=V7X_VMEM_LIMIT_BYTES),
        name="in_proj",
    )(x, wn, bn, wt, bt, cost, sint, cos, sina, sinb)


def _compress_kernel(ck_ref, cv_ref, w1k_ref, w1kf_ref, pek_ref, w2k_ref,
                     w1v_ref, w1vf_ref, pev_ref, w2vt_ref, kc_ref, vct_ref):
    nc = ck_ref.shape[1]
    hid = CMP_HIDDEN

    def hidden(c_ref, w1_ref, w1f_ref, pe_ref):
        a = _dot(c_ref[0], w1_ref[...])
        pe_term = jnp.sum(w1f_ref[...] * pe_ref[...], axis=0, keepdims=True)
        out = []
        for g in range(A_GROUPS):
            lo = a[:, 2 * g * hid:(2 * g + 1) * hid]
            hi = pltpu.roll(a[:, (2 * g + 1) * hid:(2 * g + 2) * hid], nc - 1, axis=0)
            out.append(_silu(lo + hi + pe_term).astype(BF16))
        return out

    hk = hidden(ck_ref, w1k_ref, w1kf_ref, pek_ref)
    hv = hidden(cv_ref, w1v_ref, w1vf_ref, pev_ref)
    for g in range(A_GROUPS):
        kc_ref[0, g] = _dot(hk[g], w2k_ref[...]).astype(BF16)
        vct_ref[0, g] = _dot_nt(w2vt_ref[...], hv[g]).astype(BF16)


def _compress(ck, cv, w1k, w1kf, pek, w2k, w1v, w1vf, pev, w2vt):
    B, NC, W = ck.shape
    full = lambda a: pl.BlockSpec(a.shape, lambda b: (0,) * a.ndim)
    blk = pl.BlockSpec((1, NC, W), lambda b: (b, 0, 0))
    return pl.pallas_call(
        _compress_kernel,
        out_shape=(jax.ShapeDtypeStruct((B, A_GROUPS, NC, HEAD_DIM), BF16),
                   jax.ShapeDtypeStruct((B, A_GROUPS, HEAD_DIM, NC), BF16)),
        grid=(B,),
        in_specs=[blk, blk, full(w1k), full(w1kf), full(pek), full(w2k),
                  full(w1v), full(w1vf), full(pev), full(w2vt)],
        out_specs=(pl.BlockSpec((1, A_GROUPS, NC, HEAD_DIM), lambda b: (b, 0, 0, 0)),
                   pl.BlockSpec((1, A_GROUPS, HEAD_DIM, NC), lambda b: (b, 0, 0, 0))),
        compiler_params=pltpu.CompilerParams(
            dimension_semantics=("parallel",),
            vmem_limit_bytes=V7X_VMEM_LIMIT_BYTES),
        name="kv_compress",
    )(ck, cv, w1k, w1kf, pek, w2k, w1v, w1vf, pev, w2vt)


def _nsa_kernel(jc_ref, jq_ref, qr_ref, qn_ref, gt_ref, szat_ref, kc_ref, vct_ref, ks_ref, kw_ref,
                vts_ref, vtw_ref, ovl_ref, o_ref,
                bias_s, wq_s, m_s, acc_s, ocw_s, sbuf_s, smax_s, dbuf_s, dmax_s, scb_s, swb_s, wmax_s,
                *, seq, n_sel, n_jobs, jobs_per_trip):
    nc = kc_ref.shape[2]
    n_q = seq // Q_TILE
    n_ck = seq // KV_CHUNK
    lanes = A_HPG * Q_TILE
    bpc = KV_CHUNK // SLC_BLOCK
    tpc = KV_CHUNK // Q_TILE
    win_tiles = WINDOW // Q_TILE

    def stack_heads(ref, q0):
        return jnp.concatenate(
            [ref[0, h * HEAD_DIM:(h + 1) * HEAD_DIM, pl.ds(q0, Q_TILE)] for h in range(A_HPG)], axis=1)

    def per_head(a):
        return jnp.concatenate([a] * A_HPG, axis=1)

    def gate_row(gt, branch):
        return jnp.concatenate(
            [gt[branch * A_HPG + h:branch * A_HPG + h + 1, :] for h in range(A_HPG)], axis=1)

    def tile_start(qi):
        return pl.multiple_of(qi * Q_TILE, Q_TILE)

    lane_q = lax.broadcasted_iota(jnp.int32, (1, lanes), 1) & (Q_TILE - 1)
    r_sq = lax.broadcasted_iota(jnp.int32, (Q_TILE, lanes), 0)
    causal_sq = r_sq <= lane_q
    lower_sq = r_sq > lane_q
    q_pad = jnp.zeros((KV_WIDTH - HEAD_DIM, lanes), BF16)

    def cmp_keys(rb):
        return min(nc, -(-(rb * KV_CHUNK // CMP_STRIDE) // 128) * 128)

    def cmp_rows(qi):
        return min(nc, -(-((qi + 1) * Q_TILE // CMP_STRIDE) // 16) * 16)

    def cmp_scores(qi):
        return _dot(kc_ref[0, 0, 0:cmp_rows(qi), :], stack_heads(qn_ref, tile_start(qi)))

    def zero_rows(a, rows):
        if a.shape[0] == rows:
            return a
        return jnp.concatenate([a, jnp.zeros((rows - a.shape[0], a.shape[1]), a.dtype)], axis=0)

    def select_tile(qi, rb, sc):
        nk = cmp_keys(rb)
        n_causal = (qi + 1) * Q_TILE // SLC_BLOCK
        s0 = tile_start(qi)
        t_row = s0 + lax.broadcasted_iota(jnp.int32, (1, Q_TILE), 1)
        wq_s[qi] = stack_heads(qr_ref, s0)

        nv = cmp_rows(qi)
        lo = max(0, (qi * Q_TILE - (CMP_BLOCK - 1)) // CMP_STRIDE // 8 * 8)
        n_end = (lo + lax.broadcasted_iota(jnp.int32, (nv - lo, Q_TILE), 0)) * CMP_STRIDE + (CMP_BLOCK - 1)
        edge = sc[lo:] + per_head(jnp.where(n_end <= t_row, 0.0, MASKED))
        sc = jnp.concatenate([sc[:lo], edge], axis=0) if lo else edge
        mc = jnp.max(sc, axis=0, keepdims=True)
        pc = jnp.exp2(sc - mc)
        lc = jnp.sum(pc, axis=0, keepdims=True)
        pc = pc * jnp.where(s0 + lane_q >= CMP_BLOCK - 1, 1.0 / lc, 0.0)
        o_cmp = _dot(vct_ref[0, 0, :, 0:nk], zero_rows(pc.astype(BF16), nk))

        ps = pc[:, 0:Q_TILE]
        for h in range(1, A_HPG):
            ps = ps + pc[:, h * Q_TILE:(h + 1) * Q_TILE]
        p_hi = ps.astype(BF16)
        p_lo = (ps - p_hi.astype(F32)).astype(BF16)
        imp2 = _dot(ovl_ref[0:8 * rb, 0:nk],
                    zero_rows(jnp.concatenate([p_hi, p_lo], axis=1), nk))
        imp = imp2[:, :Q_TILE] + imp2[:, Q_TILE:]

        j_idx = lax.broadcasted_iota(jnp.int32, (8 * rb, Q_TILE), 0)
        cur = t_row // SLC_BLOCK
        sel = j_idx <= cur
        if 8 * rb > n_sel:
            forced = (j_idx == 0) | (j_idx == cur) | (j_idx == cur - 1)
            val = jnp.where(j_idx > cur, -1.0, jnp.where(forced, FORCE_SCORE, imp))
            rows = [val[8 * j:8 * j + 8] for j in range(rb)]
            cnt = [jnp.zeros((8, Q_TILE), F32) for _ in range(rb)]
            jl = lax.broadcasted_iota(jnp.int32, (8, Q_TILE), 0)
            for kb in range(rb):
                for kl in range(8):
                    if 8 * kb + kl >= n_causal:
                        continue
                    row = rows[kb][kl:kl + 1, :]
                    for jb in range(rb):
                        if jb < kb:
                            beats = row > rows[jb]
                        elif jb > kb:
                            beats = row >= rows[jb]
                        else:
                            beats = (row > rows[jb]) | ((row == rows[jb]) & (jl > kl))
                        cnt[jb] = cnt[jb] + jnp.where(beats, 1.0, 0.0)
            sel = sel & (jnp.concatenate(cnt, axis=0) < n_sel)
        bias = per_head(jnp.where(sel, 0.0, MASKED))
        pad = jnp.zeros((bpc, lanes), F32)
        for c in range(rb):
            bias_s[qi, c] = jnp.concatenate([bias[c * bpc:(c + 1) * bpc], pad], axis=0).astype(BF16)

        gt = gt_ref[0, :, pl.ds(s0, Q_TILE)]
        ocw_s[qi] = gate_row(gt, 0) * o_cmp
        m_s[qi] = jnp.full((8, lanes), MASKED, F32)
        acc_s[qi] = jnp.zeros((V_ROWS, lanes), F32)

    def win_finish(qi, slot, w0, keys):
        mw = wmax_s[slot][0:1]
        ow = None
        for k0 in range(0, keys, WIN_K_CHUNK):
            kc = min(WIN_K_CHUNK, keys - k0)
            pw = jnp.exp2(swb_s[slot, k0:k0 + kc, :] - mw)
            part = _dot(vtw_ref[0, 0, :, w0 + k0:w0 + k0 + kc], pw.astype(BF16))
            ow = part if ow is None else ow + part
        o_win = ow[0:HEAD_DIM] * (1.0 / ow[HEAD_DIM:HEAD_DIM + 1])
        gt = gt_ref[0, :, pl.ds(tile_start(qi), Q_TILE)]
        ocw_s[qi] = ocw_s[qi] + gate_row(gt, 2) * o_win

    def win_scores(qi):
        slot = qi % IN_FLIGHT
        keys = min(qi + 1, win_tiles + 1) * Q_TILE
        w0 = (qi + 1) * Q_TILE - keys
        wq = jnp.concatenate([stack_heads(qr_ref, qi * Q_TILE), q_pad], axis=0)
        sw = _dot(kw_ref[0, 0, w0:w0 + keys, :], wq)
        parts = [jnp.where(causal_sq, sw[keys - Q_TILE:], MASKED)]
        if keys > Q_TILE:
            oldest = sw[:Q_TILE]
            if qi >= win_tiles:
                oldest = jnp.where(lower_sq, oldest, MASKED)
            parts = [oldest, sw[Q_TILE:keys - Q_TILE]] + parts if keys > 2 * Q_TILE else [oldest] + parts
        sw = jnp.concatenate(parts, axis=0)
        swb_s[slot, 0:keys, :] = sw
        wmax_s[slot] = jnp.broadcast_to(jnp.max(sw, axis=0, keepdims=True), (8, lanes))

    def win_tile(qi):
        slot = qi % IN_FLIGHT
        keys = min(qi + 1, win_tiles + 1) * Q_TILE
        win_finish(qi, slot, (qi + 1) * Q_TILE - keys, keys)

    scb_s[0, 0:cmp_rows(0), :] = cmp_scores(0)
    for u in range(min(AHEAD, n_q)):
        win_scores(u)
    for qi in range(n_q):
        rb = qi // tpc + 1
        if qi + 1 < n_q:
            scb_s[(qi + 1) & 1, 0:cmp_rows(qi + 1), :] = cmp_scores(qi + 1)
        if qi + AHEAD < n_q:
            win_scores(qi + AHEAD)
        select_tile(qi, rb, scb_s[qi & 1, 0:cmp_rows(qi), :])
        win_tile(qi)

    wq_pad = jnp.zeros((KV_WIDTH - HEAD_DIM - 2 * bpc, lanes), BF16)

    def scores(c, qi, diag):
        rows = KV_CHUNK if diag is None else (diag + 1) * Q_TILE
        k0 = pl.multiple_of(c * KV_CHUNK, KV_CHUNK)
        wq = jnp.concatenate([wq_s[qi], bias_s[qi, c], wq_pad], axis=0)
        s = _dot(ks_ref[0, 0, pl.ds(k0, rows), :], wq)
        if diag is not None:
            head = [s[:rows - Q_TILE]] if diag > 0 else []
            s = jnp.concatenate(head + [jnp.where(causal_sq, s[rows - Q_TILE:], MASKED)], axis=0)
        return s, jnp.broadcast_to(jnp.max(s, axis=0, keepdims=True), (8, lanes))

    def absorb(c, qi, s, s_max):
        rows = s.shape[0]
        k0 = pl.multiple_of(c * KV_CHUNK, KV_CHUNK)
        m_old = m_s[qi][0:1]
        m_new = jnp.maximum(m_old, s_max[0:1])
        alpha = jnp.exp2(m_old - m_new)
        p = jnp.exp2(s - m_new)
        m_s[qi] = jnp.broadcast_to(m_new, (8, lanes))
        acc_s[qi] = alpha * acc_s[qi] + _dot(vts_ref[0, 0, :, pl.ds(k0, rows)], p.astype(BF16))

    def full_scores(j, slot):
        sbuf_s[slot], smax_s[slot] = scores(jc_ref[j], jq_ref[j], None)

    def full_group(i, _):
        j0 = jobs_per_trip * i
        for u in range(jobs_per_trip):
            full_scores(j0 + u + AHEAD, (u + AHEAD) % IN_FLIGHT)
            absorb(jc_ref[j0 + u], jq_ref[j0 + u], sbuf_s[u % IN_FLIGHT], smax_s[u % IN_FLIGHT])
        return 0

    if n_jobs:
        for u in range(AHEAD):
            full_scores(u, u)
        lax.fori_loop(0, n_jobs // jobs_per_trip, full_group, 0)

    def diag_scores(c, r):
        dbuf_s[r, 0:(r + 1) * Q_TILE, :], dmax_s[r] = scores(c, c * tpc + r, r)

    def diag_chunks(i, _):
        for k in range(DIAG_CHUNKS_PER_TRIP):
            c = DIAG_CHUNKS_PER_TRIP * i + k
            nxt = jnp.minimum(c + 1, n_ck - 1)
            for r in range(tpc):
                ahead = r + AHEAD
                diag_scores(c if ahead < tpc else nxt, ahead % tpc)
                absorb(c, c * tpc + r, dbuf_s[r, 0:(r + 1) * Q_TILE, :], dmax_s[r])
        return 0

    for r in range(AHEAD):
        diag_scores(0, r)
    lax.fori_loop(0, n_ck // DIAG_CHUNKS_PER_TRIP, diag_chunks, 0)

    def finish(qi, _):
        s0 = tile_start(qi)
        gt = gt_ref[0, :, pl.ds(s0, Q_TILE)]
        acc = acc_s[qi]
        ot = ocw_s[qi] + gate_row(gt, 1) * acc[0:HEAD_DIM] * (1.0 / acc[HEAD_DIM:HEAD_DIM + 1])
        ot = (ot * stack_heads(szat_ref, s0).astype(F32)).astype(o_ref.dtype)
        for h in range(A_HPG):
            o_ref[0, h * HEAD_DIM:(h + 1) * HEAD_DIM, pl.ds(s0, Q_TILE)] = ot[:, h * Q_TILE:(h + 1) * Q_TILE]
        return 0

    lax.fori_loop(0, n_q, finish, 0)


def _nsa(qr, qn, gt, szat, kc, vct, ks, kw, vts, vtw, ovl):
    B, _, S = qr.shape
    NC = kc.shape[2]
    n_slc = S // SLC_BLOCK
    n_sel = min(SLC_TOP_N, n_slc)
    n_q = S // Q_TILE
    n_ck = S // KV_CHUNK
    tpc = KV_CHUNK // Q_TILE
    gw = A_HPG * HEAD_DIM
    lanes = A_HPG * Q_TILE
    jobs = [(c, q) for c in range(n_ck) for q in range((c + 1) * tpc, n_q)]
    n_jobs = len(jobs)
    jobs_per_trip = max(j for j in range(IN_FLIGHT, MAX_JOBS_PER_TRIP + 1, IN_FLIGHT) if n_jobs % j == 0)
    assert tpc == IN_FLIGHT and AHEAD < IN_FLIGHT and n_ck % DIAG_CHUNKS_PER_TRIP == 0
    jobs = jobs + [jobs[-1] if jobs else (0, 0)] * AHEAD
    jc = jnp.asarray([j[0] for j in jobs], jnp.int32)
    jq = jnp.asarray([j[1] for j in jobs], jnp.int32)
    kernel = functools.partial(_nsa_kernel, seq=S, n_sel=n_sel, n_jobs=n_jobs, jobs_per_trip=jobs_per_trip)
    per_group = lambda rows, cols: pl.BlockSpec((1, 1, rows, cols), lambda b, g, *_: (b, g, 0, 0))
    in_specs = [
        pl.BlockSpec((1, gw, S), lambda b, g, *_: (b, g, 0)),
        pl.BlockSpec((1, gw, S), lambda b, g, *_: (b, g, 0)),
        pl.BlockSpec((1, GATE_ROWS, S), lambda b, g, *_: (b, g, 0)),
        pl.BlockSpec((1, gw, S), lambda b, g, *_: (b, g, 0)),
        per_group(NC, HEAD_DIM), per_group(HEAD_DIM, NC),
        per_group(S, KV_WIDTH), per_group(S, KV_WIDTH),
        per_group(V_ROWS, S), per_group(V_ROWS, S),
        pl.BlockSpec((n_slc, NC), lambda b, g, *_: (0, 0)),
    ]
    scratch = [
        pltpu.VMEM((n_q, n_ck, 2 * (KV_CHUNK // SLC_BLOCK), lanes), BF16),
        pltpu.VMEM((n_q, HEAD_DIM, lanes), BF16),
        pltpu.VMEM((n_q, 8, lanes), F32),
        pltpu.VMEM((n_q, V_ROWS, lanes), F32),
        pltpu.VMEM((n_q, HEAD_DIM, lanes), F32),
        pltpu.VMEM((IN_FLIGHT, KV_CHUNK, lanes), F32),
        pltpu.VMEM((IN_FLIGHT, 8, lanes), F32),
        pltpu.VMEM((tpc, KV_CHUNK, lanes), F32),
        pltpu.VMEM((tpc, 8, lanes), F32),
        pltpu.VMEM((2, NC, lanes), F32),
        pltpu.VMEM((IN_FLIGHT, WIN_KEYS, lanes), F32),
        pltpu.VMEM((IN_FLIGHT, 8, lanes), F32),
    ]
    return pl.pallas_call(
        kernel,
        out_shape=jax.ShapeDtypeStruct((B, A_WIDTH, S), BF16),
        grid_spec=pltpu.PrefetchScalarGridSpec(
            num_scalar_prefetch=2,
            grid=(B, A_GROUPS),
            in_specs=in_specs,
            out_specs=pl.BlockSpec((1, gw, S), lambda b, g, *_: (b, g, 0)),
            scratch_shapes=scratch),
        compiler_params=pltpu.CompilerParams(
            dimension_semantics=("parallel", "parallel"),
            vmem_limit_bytes=V7X_VMEM_LIMIT_BYTES),
        name="nsa_attention",
    )(jc, jq, qr, qn, gt, szat, kc, vct, ks, kw, vts, vtw, ovl)


def _hgrn_stages(q_ref, f_ref, i_ref, sz_ref, lb, gain, tri, bd, causal, state_s, h, out):
    T = q_ref.shape[1]
    cpg = HG_ROWS // CHUNK
    lanes = slice(h * B_KDIM, (h + 1) * B_KDIM)
    groups = [slice(gi * HG_ROWS, (gi + 1) * HG_ROWS) for gi in range(T // HG_ROWS)]
    mid = {}

    def decay():
        mid["kk"], mid["b"] = [], []
        for rows in groups:
            fg = lb + (1.0 - lb) * _sigmoid(f_ref[0, rows, lanes])
            logf = jnp.log(fg)
            l_hi = logf.astype(BF16)
            l_lo = (logf - l_hi.astype(F32)).astype(BF16)
            bb = _dot(tri, jnp.concatenate([l_hi, l_lo], axis=1))
            mid["kk"].append(1.0 - fg)
            mid["b"].append(bb[:, :B_KDIM] + bb[:, B_KDIM:])

    def intra():
        mid["qe"], mid["oi"], mid["kv"], mid["dl"] = [], [], [], []
        for rows, kk, b in zip(groups, mid["kk"], mid["b"]):
            b_last = jnp.concatenate(
                [jnp.broadcast_to(b[(c + 1) * CHUNK - 1:(c + 1) * CHUNK, :], (CHUNK, B_KDIM))
                 for c in range(cpg)], axis=0)
            qe = (_silu(q_ref[0, rows, lanes].astype(F32)) * jnp.exp(b)).astype(BF16)
            ke = (kk * jnp.exp(-b)).astype(BF16)
            kd = (kk * jnp.exp(b_last - b)).astype(BF16)
            attn = jnp.where(causal, _dot_nt(qe, ke), 0.0).astype(BF16)
            v = i_ref[0, rows, lanes]
            mid["oi"].append(_dot(attn, v))
            mid["qe"].append(qe)
            kd_bd = jnp.concatenate([kd] * cpg, axis=1) * bd
            kv_all = lax.dot_general(v, kd_bd, (((0,), (0,)), ((), ())), preferred_element_type=F32)
            for c in range(cpg):
                mid["kv"].append(kv_all[:, c * B_KDIM:(c + 1) * B_KDIM])
                mid["dl"].append(jnp.exp(b_last[c * CHUNK:c * CHUNK + 1, :]))

    def inter():
        state = state_s[h]
        outs = []
        for n in range(T // CHUNK):
            gi, c = divmod(n, cpg)
            sl = slice(c * CHUNK, (c + 1) * CHUNK)
            rows = slice(n * CHUNK, (n + 1) * CHUNK)
            o = mid["oi"][gi][sl] + _dot_nt(mid["qe"][gi][sl], state.astype(BF16))
            o = o * lax.rsqrt(jnp.mean(o * o, axis=-1, keepdims=True) + NORM_EPS)
            outs.append((o * gain * sz_ref[0, rows, lanes].astype(F32)).astype(BF16))
            state = mid["dl"][n] * state + mid["kv"][n]
        state_s[h] = state
        out.append(jnp.concatenate(outs, axis=0))

    return [decay, intra, inter]


def _out_kernel(x_ref, oat_ref, q_ref, f_ref, i_ref, sz_ref, lbl_ref, g_ref, tri_ref, bd_ref,
                wgm_ref, bgm_ref, wa_ref, wb_ref, wo_ref, lng_ref, lnb_ref, o_ref, state_s, sg_s, ya_s,
                *, alpha, layer):
    @pl.when(pl.program_id(1) == 0)
    def _():
        state_s[...] = jnp.zeros_like(state_s)

    x = x_ref[0]
    xb = x.astype(BF16)

    def gate_chunk(j):
        def run():
            sg_s[:, j:j + GATE_COLS] = _sigmoid(_dot(xb, wgm_ref[:, j:j + GATE_COLS]) + bgm_ref[:, j:j + GATE_COLS])
        return run

    def ya_chunk(j):
        def run():
            ya_s[:, j:j + YA_COLS] = lax.dot_general(oat_ref[0], wa_ref[:, j:j + YA_COLS], (((0,), (0,)), ((), ())),
                                                     preferred_element_type=F32)
        return run

    fillers = [gate_chunk(j) for j in range(0, 2 * D_MODEL, GATE_COLS)] + \
              [ya_chunk(j) for j in range(0, D_MODEL, YA_COLS)]

    lg = lbl_ref[...]
    e = jnp.exp(lg - jnp.max(lg, axis=0, keepdims=True))
    lb_all = jnp.sum(e[0:layer + 1], axis=0, keepdims=True) / jnp.sum(e, axis=0, keepdims=True)
    tri = tri_ref[...]
    bd = bd_ref[...]
    ri = lax.broadcasted_iota(jnp.int32, (HG_ROWS, HG_ROWS), 0)
    ci = lax.broadcasted_iota(jnp.int32, (HG_ROWS, HG_ROWS), 1)
    causal = (ri // CHUNK == ci // CHUNK) & (ci <= ri)

    obs = []
    stages = []
    for h in range(B_HEADS):
        lanes = slice(h * B_KDIM, (h + 1) * B_KDIM)
        stages += _hgrn_stages(q_ref, f_ref, i_ref, sz_ref, lb_all[:, lanes], g_ref[:, lanes],
                               tri, bd, causal, state_s, h, obs)
    for k, stage in enumerate(stages):
        if k < len(fillers):
            fillers[k]()
        stage()
    for filler in fillers[len(stages):]:
        filler()
    ob = jnp.concatenate(obs, axis=1)

    sub = x.shape[0] // TAIL_SPLIT
    halves = [slice(i * sub, (i + 1) * sub) for i in range(TAIL_SPLIT)]
    ybs = [_dot(ob[r], wb_ref[...]) for r in halves]
    ys = [(sg_s[r, :D_MODEL] * ya_s[r, :] + sg_s[r, D_MODEL:] * yb).astype(BF16) for r, yb in zip(halves, ybs)]
    outs = [_dot(y, wo_ref[...]) for y in ys]
    for r, out in zip(halves, outs):
        res = alpha * x[r] + out
        mu = jnp.mean(res, axis=-1, keepdims=True)
        d = res - mu
        var = jnp.mean(d * d, axis=-1, keepdims=True)
        o_ref[0, r, :] = d * lax.rsqrt(var + NORM_EPS) * lng_ref[...] + lnb_ref[...]


def _merge_out(x, oat, qb, fb, ib, szb, lb_logits, gain, tri, bd, wgm, bgm, wa, wb, wo, lng, lnb, alpha, layer):
    B, S, D = x.shape
    T = OUT_ROWS
    full = lambda a: pl.BlockSpec(a.shape, lambda b, s: (0,) * a.ndim)
    rows = lambda w: pl.BlockSpec((1, T, w), lambda b, s: (b, s, 0))
    return pl.pallas_call(
        functools.partial(_out_kernel, alpha=alpha, layer=layer),
        out_shape=jax.ShapeDtypeStruct((B, S, D), x.dtype),
        grid=(B, S // T),
        in_specs=[rows(D), pl.BlockSpec((1, A_WIDTH, T), lambda b, s: (b, 0, s)),
                  rows(B_FWIDTH), rows(B_FWIDTH), rows(B_WIDTH), rows(B_WIDTH),
                  full(lb_logits), full(gain), full(tri), full(bd),
                  full(wgm), full(bgm), full(wa), full(wb), full(wo), full(lng), full(lnb)],
        out_specs=rows(D),
        scratch_shapes=[pltpu.VMEM((B_HEADS, B_VDIM, B_KDIM), F32),
                        pltpu.VMEM((T, 2 * D), F32),
                        pltpu.VMEM((T, D), F32)],
        compiler_params=pltpu.CompilerParams(
            dimension_semantics=("parallel", "arbitrary"),
            vmem_limit_bytes=V7X_VMEM_LIMIT_BYTES),
        name="hgrn_merge_out",
    )(x, oat, qb, fb, ib, szb, lb_logits, gain, tri, bd, wgm, bgm, wa, wb, wo, lng, lnb)


def _rope_tables(S):
    inv = ROPE_THETA ** (-np.arange(0, HEAD_DIM, 2, dtype=np.float64) / HEAD_DIM)
    ang = np.arange(S, dtype=np.float64)[:, None] * inv[None, :]
    cos = np.concatenate([np.cos(ang), np.cos(ang)], axis=-1)
    sin = np.concatenate([np.sin(ang), np.sin(ang)], axis=-1)
    first = (np.arange(HEAD_DIM) < HEAD_DIM // 2)[None, :]
    sina = np.where(first, -sin, 0.0)
    sinb = np.where(first, 0.0, sin)
    tile = lambda a: np.concatenate([a] * A_GROUPS, axis=-1)
    return tuple(jnp.asarray(np.ascontiguousarray(a), dtype=F32)
                 for a in (cos.T, sin.T, tile(cos), tile(sina), tile(sinb)))


def _overlap_t(S):
    n_cmp = S // CMP_STRIDE
    n_slc = S // SLC_BLOCK
    cs = np.arange(n_cmp)[None, :] * CMP_STRIDE
    ss = np.arange(n_slc)[:, None] * SLC_BLOCK
    ov = (cs < ss + SLC_BLOCK) & (cs + CMP_BLOCK > ss) & (np.arange(n_cmp)[None, :] < n_cmp - 1)
    return jnp.asarray(ov, dtype=BF16)


def _block_tri():
    r = np.arange(HG_ROWS)
    return jnp.asarray((r[:, None] // CHUNK == r[None, :] // CHUNK) & (r[None, :] <= r[:, None]), dtype=BF16)


def _block_diag():
    r = np.arange(HG_ROWS)[:, None] // CHUNK
    c = np.arange(HG_ROWS // CHUNK * B_KDIM)[None, :] // B_KDIM
    return jnp.asarray(r == c, dtype=BF16)


def _layer(x, l, w_in, b_in, pe_k, w_k1, w_k2, pe_v, w_v1, w_v2, lb_logits, norm_g,
           w_a, w_b, w_o, ln_g, ln_b):
    B, S, D = x.shape
    alpha = (2 * DEPTH) ** 0.25
    o = _OFF
    wsl = lambda i: w_in[:, o[i]:o[i + 1]]
    bsl = lambda i: b_in[o[i]:o[i + 1]]
    kvw, kvb = wsl(1), bsl(1)
    kv_w = lambda j: kvw[:, j * KV_WIDTH:(j + 1) * KV_WIDTH]
    kv_b = lambda j: kvb[j * KV_WIDTH:(j + 1) * KV_WIDTH]
    gw, gb = wsl(2), bsl(2)
    gidx = np.zeros((A_GROUPS, GATE_ROWS), np.int32)
    gmask = np.zeros((A_GROUPS, GATE_ROWS), np.float32)
    for g in range(A_GROUPS):
        for br in range(3):
            for h in range(A_HPG):
                gidx[g, br * A_HPG + h] = (g * A_HPG + h) * 3 + br
                gmask[g, br * A_HPG + h] = 1.0
    gidx, gmask = gidx.reshape(-1), gmask.reshape(-1)
    gw_t = gw[:, gidx] * gmask[None, :]
    gb_t = gb[gidx] * gmask

    wt = jnp.concatenate([wsl(0), kv_w(3), kv_w(5), gw_t, wsl(3)], axis=1).T.astype(BF16)
    bt = jnp.concatenate([bsl(0), kv_b(3), kv_b(5), gb_t, bsl(3)])[:, None]
    wn = jnp.concatenate([kv_w(0), kv_w(1), kv_w(2), kv_w(4), wsl(7), wsl(4), wsl(6), wsl(5)],
                         axis=1).astype(BF16)
    bn = jnp.concatenate([kv_b(0), kv_b(1), kv_b(2), kv_b(4), bsl(7), bsl(4), bsl(6), bsl(5)])[None, :]
    cost, sint, cos, sina, sinb = _rope_tables(S)

    (qr, qn, vts, vtw, gt, szat, kcmp, vcmp, ks, kw, qb, fb, ib, szb) = _project(
        x, wn, bn, wt, bt, cost, sint, cos, sina, sinb)

    half = CMP_STRIDE * HEAD_DIM

    def w1_planes(w1):
        both = jnp.concatenate([w1[:half], w1[half:]], axis=1).reshape(CMP_STRIDE, HEAD_DIM, 2 * CMP_HIDDEN)
        z = jnp.zeros_like(both)
        planes = jnp.concatenate([jnp.concatenate([both, z], axis=2),
                                  jnp.concatenate([z, both], axis=2)], axis=1)
        return planes.reshape(CMP_STRIDE * KV_WIDTH, 2 * A_GROUPS * CMP_HIDDEN).astype(BF16)

    kc, vct = _compress(kcmp, vcmp,
                        w1_planes(w_k1), w_k1, pe_k.reshape(-1, 1), w_k2.astype(BF16),
                        w1_planes(w_v1), w_v1, pe_v.reshape(-1, 1), w_v2.T.astype(BF16))

    oat = _nsa(qr, qn, gt, szat, kc, vct, ks, kw, vts, vtw, _overlap_t(S))
    wgm = jnp.concatenate([wsl(8), wsl(9)], axis=1).astype(BF16)
    bgm = jnp.concatenate([bsl(8), bsl(9)])[None, :]
    return _merge_out(x, oat, qb, fb, ib, szb, lb_logits, norm_g[None, :], _block_tri(), _block_diag(),
                      wgm, bgm, w_a.astype(BF16), w_b.astype(BF16), w_o.astype(BF16),
                      ln_g[None, :], ln_b[None, :], alpha, l)


@jax.jit
def kernel(x, w_in, b_in, pe_cmp_k, w_cmp_k1, w_cmp_k2, pe_cmp_v, w_cmp_v1, w_cmp_v2,
           hgrn_lb_logits, hgrn_norm_g, w_branch_a, w_branch_b, w_out, ln_g, ln_b):
    B, S, D = x.shape
    assert D == D_MODEL and S % KV_CHUNK == 0 and S % PROJ_ROWS == 0 and S >= WIN_KEYS
    assert S % OUT_ROWS == 0 and OUT_ROWS % HG_ROWS == 0 and (S // KV_CHUNK) * KV_CHUNK == S
    for l in range(DEPTH):
        x = _layer(x, l, w_in[l], b_in[l], pe_cmp_k[l], w_cmp_k1[l], w_cmp_k2[l],
                   pe_cmp_v[l], w_cmp_v1[l], w_cmp_v2[l], hgrn_lb_logits, hgrn_norm_g[l],
                   w_branch_a[l], w_branch_b[l], w_out[l], ln_g[l], ln_b[l])
    return x
```
